```python
import math
import jax
import jax.numpy as jnp
from jax import lax
import numpy as np

D_MODEL = 1024
BATCH = 4
SEQ = 4096
DEPTH = 1
DEC_BATCH = 32
DEC_SEQ = 8
PAST_LEN = 16384
PAGE_SIZE = 128

SSM_WIDTH = D_MODEL // 2
SSM_GROUP = 16
SSM_GROUPS = SSM_WIDTH // SSM_GROUP
SSM_STATE = 64
ATTN_WIDTH = D_MODEL - SSM_WIDTH
HEAD_DIM = 64
N_HEADS = ATTN_WIDTH // HEAD_DIM
IN_WIDTH = SSM_WIDTH + 3 * ATTN_WIDTH
DILATED_PAIRS = ((128, 1), (512, 4), (2048, 16))
MAX_WINDOW = 2048
ATTN_BLOCK = 128
N_EXPERTS = 32
TOP_K = 4
EXPERT_FF = D_MODEL
SWIGLU_LIMIT = 7.0
SWIGLU_ALPHA = 1.702
PLE_DIM = 256
EPS = 1e-6
MASK_VALUE = -1e30
DT_MIN = 1e-3
DT_MAX = 1e-1

kernel_name = "hymba_s5_dilated_alibi_moe_step"


def rms_norm(x, g):
    xf = x.astype(jnp.float32)
    y = xf * lax.rsqrt(jnp.mean(xf * xf, axis=-1, keepdims=True) + EPS)
    return (y * g.astype(jnp.float32)).astype(x.dtype)


def alibi_slopes(n):
    return jnp.asarray(2.0 ** (-8.0 * np.arange(1, n + 1) / n), dtype=jnp.float32)


def _cmul(ar, ai, br, bi):
    return ar * br - ai * bi, ar * bi + ai * br


def _ssm_combine(e1, e2):
    a1r, a1i, b1r, b1i = e1
    a2r, a2i, b2r, b2i = e2
    ar, ai = _cmul(a2r, a2i, a1r, a1i)
    br, bi = _cmul(a2r, a2i, b1r, b1i)
    return ar, ai, br + b2r, bi + b2i


def s5_discretise(a_re, a_im, b_re, b_im, log_dt):
    a_re = a_re.astype(jnp.float32)
    a_im = a_im.astype(jnp.float32)
    b_re = b_re.astype(jnp.float32)
    b_im = b_im.astype(jnp.float32)
    dt = jnp.exp(log_dt.astype(jnp.float32))[:, None]
    mag = jnp.exp(dt * a_re)
    ang = dt * a_im
    ab_re, ab_im = mag * jnp.cos(ang), mag * jnp.sin(ang)
    den = a_re * a_re + a_im * a_im
    nr, ni = ab_re - 1.0, ab_im
    f_re = (nr * a_re + ni * a_im) / den
    f_im = (ni * a_re - nr * a_im) / den
    bb_re = f_re[..., None] * b_re - f_im[..., None] * b_im
    bb_im = f_re[..., None] * b_im + f_im[..., None] * b_re
    return ab_re, ab_im, bb_re, bb_im


def s5_mixer(u, h0_re, h0_im, a_re, a_im, b_re, b_im, c_re, c_im, d_skip, log_dt, w_glu, b_glu):
    bsz, t_len, _ = u.shape
    uf = u.astype(jnp.float32).reshape(bsz, t_len, SSM_GROUPS, SSM_GROUP)
    ab_re, ab_im, bb_re, bb_im = s5_discretise(a_re, a_im, b_re, b_im, log_dt)
    bu_re = jnp.einsum('btgc,gnc->btgn', uf, bb_re)
    bu_im = jnp.einsum('btgc,gnc->btgn', uf, bb_im)
    a_r = jnp.broadcast_to(ab_re, bu_re.shape)
    a_i = jnp.broadcast_to(ab_im, bu_im.shape)
    pa_re, pa_im, h_re, h_im = lax.associative_scan(_ssm_combine, (a_r, a_i, bu_re, bu_im), axis=1)
    cr, ci = _cmul(pa_re, pa_im, h0_re.astype(jnp.float32)[:, None], h0_im.astype(jnp.float32)[:, None])
    h_re = h_re + cr
    h_im = h_im + ci
    y = (jnp.einsum('btgn,gcn->btgc', h_re, c_re.astype(jnp.float32))
         - jnp.einsum('btgn,gcn->btgc', h_im, c_im.astype(jnp.float32))
         + d_skip.astype(jnp.float32).reshape(SSM_GROUPS, SSM_GROUP) * uf)
    z = jax.nn.gelu(y.reshape(bsz, t_len, SSM_WIDTH))
    out = z * jax.nn.sigmoid(z @ w_glu.astype(jnp.float32) + b_glu.astype(jnp.float32))
    return out.astype(u.dtype), h_re[:, -1], h_im[:, -1]


def dilated_branch_prompt(q, k, v, window, dilation, slopes):
    bsz, s_len, n_h, d_h = q.shape
    blk = ATTN_BLOCK
    sub_len = s_len // dilation
    wk = window // dilation
    nb = -(-sub_len // blk)
    sub_pad = nb * blk

    def strided(t):
        t = t.astype(jnp.float32).reshape(bsz, sub_len, dilation, n_h, d_h).transpose(0, 2, 1, 3, 4)
        return jnp.pad(t, ((0, 0), (0, 0), (0, sub_pad - sub_len), (0, 0), (0, 0)))

    def band(t):
        t = jnp.pad(t, ((0, 0), (0, 0), (blk, 0), (0, 0), (0, 0))).reshape(bsz, dilation, nb + 1, blk, n_h, d_h)
        return jnp.concatenate([t[:, :, :-1], t[:, :, 1:]], axis=3)

    qb = strided(q).reshape(bsz, dilation, nb, blk, n_h, d_h)
    kb = band(strided(k))
    vb = band(strided(v))
    s = jnp.einsum('brnihd,brnjhd->brnhij', qb, kb) * (HEAD_DIM ** -0.5)
    i_idx = jnp.arange(blk)[:, None]
    j_idx = jnp.arange(2 * blk)[None, :]
    delta = i_idx - j_idx + blk
    key_sub = jnp.arange(nb)[:, None, None] * blk + j_idx - blk
    valid = (delta >= 0) & (delta <= wk) & (key_sub >= 0)
    bias = -slopes[:, None, None] * (delta * dilation).astype(jnp.float32)
    s = jnp.where(valid[None, None, :, None], s + bias[None, None, None], MASK_VALUE)
    lse = jax.nn.logsumexp(s, axis=-1)
    p = jnp.exp(s - lse[..., None])
    o = jnp.einsum('brnhij,brnjhd->brnihd', p, vb)
    o = o.reshape(bsz, dilation, sub_pad, n_h, d_h)[:, :, :sub_len]
    o = o.transpose(0, 2, 1, 3, 4).reshape(bsz, s_len, n_h, d_h)
    lse = lse.transpose(0, 1, 2, 4, 3).reshape(bsz, dilation, sub_pad, n_h)[:, :, :sub_len]
    lse = lse.transpose(0, 2, 1, 3).reshape(bsz, s_len, n_h)
    return o, lse


def dilated_branch_sample(q, k_all, v_all, wb, window, dilation, slopes):
    t_len = q.shape[1]
    wk = window // dilation
    t_idx = jnp.arange(t_len)[:, None]
    j_idx = jnp.arange(wk + 1)[None, :]
    idx = wb + t_idx - j_idx * dilation
    valid = idx >= 0
    idx = jnp.maximum(idx, 0)
    kg = k_all[:, idx].astype(jnp.float32)
    vg = v_all[:, idx].astype(jnp.float32)
    s = jnp.einsum('bthd,btjhd->bhtj', q.astype(jnp.float32), kg) * (HEAD_DIM ** -0.5)
    s = s - slopes[:, None, None] * (j_idx * dilation).astype(jnp.float32)
    s = jnp.where(valid[None, None], s, MASK_VALUE)
    lse = jax.nn.logsumexp(s, axis=-1)
    p = jnp.exp(s - lse[..., None])
    o = jnp.einsum('bhtj,btjhd->bthd', p, vg)
    return o, lse.transpose(0, 2, 1)


def dilated_attention(q, k, v, past_k, past_v, slopes):
    if past_k is None:
        branches = [dilated_branch_prompt(q, k, v, w, d, slopes) for (w, d) in DILATED_PAIRS]
    else:
        wb = past_k.shape[1]
        k_all = jnp.concatenate([past_k.astype(k.dtype), k], axis=1)
        v_all = jnp.concatenate([past_v.astype(v.dtype), v], axis=1)
        branches = [dilated_branch_sample(q, k_all, v_all, wb, w, d, slopes) for (w, d) in DILATED_PAIRS]
    outs = jnp.stack([b[0] for b in branches], axis=0)
    lses = jnp.stack([b[1] for b in branches], axis=0)
    w = jax.nn.softmax(lses, axis=0)
    return jnp.einsum('kbth,kbthd->bthd', w, outs).astype(q.dtype)


def moe_ffn(h, w_router, b_router, w_up, b_up, w_down, b_down):
    bsz, t_len, dim = h.shape
    tok = h.reshape(-1, dim)
    logits = (tok @ w_router + b_router).astype(jnp.float32)
    top_val, top_idx = lax.top_k(logits, TOP_K)
    top_w = jax.nn.softmax(top_val, axis=-1)
    gates = jnp.sum(jax.nn.one_hot(top_idx, N_EXPERTS, dtype=jnp.float32) * top_w[..., None], axis=1)
    out = jnp.zeros((tok.shape[0], dim), jnp.float32)
    for e in range(N_EXPERTS):
        a = tok @ w_up[e] + b_up[e]
        g = jnp.minimum(a[:, :EXPERT_FF], SWIGLU_LIMIT)
        lin = jnp.clip(a[:, EXPERT_FF:], -SWIGLU_LIMIT, SWIGLU_LIMIT)
        y = ((lin + 1.0) * (g * jax.nn.sigmoid(SWIGLU_ALPHA * g))) @ w_down[e] + b_down[e]
        out = out + gates[:, e:e + 1] * y.astype(jnp.float32)
    return out.reshape(bsz, t_len, dim).astype(h.dtype)


def decoder_layer(x, pe, past_k, past_v, h0_re, h0_im, slopes,
                  ln_mix, w_in, ssm_a_re, ssm_a_im, ssm_b_re, ssm_b_im, ssm_c_re, ssm_c_im,
                  ssm_d, ssm_log_dt, w_glu, b_glu, ln_ssm_out, ln_attn_out, w_out,
                  ln_moe, w_router, b_router, w_up, b_up, w_down, b_down,
                  ln_ple, w_ple_gate, b_ple_gate, w_ple_proj):
    bsz, t_len, _ = x.shape
    h = rms_norm(x, ln_mix)
    proj = h @ w_in
    u, q, k, v = jnp.split(proj, [SSM_WIDTH, SSM_WIDTH + ATTN_WIDTH, SSM_WIDTH + 2 * ATTN_WIDTH], axis=-1)
    q = q.reshape(bsz, t_len, N_HEADS, HEAD_DIM)
    k = k.reshape(bsz, t_len, N_HEADS, HEAD_DIM)
    v = v.reshape(bsz, t_len, N_HEADS, HEAD_DIM)
    ssm_out, hT_re, hT_im = s5_mixer(u, h0_re, h0_im, ssm_a_re, ssm_a_im, ssm_b_re, ssm_b_im,
                                     ssm_c_re, ssm_c_im, ssm_d, ssm_log_dt, w_glu, b_glu)
    attn_out = dilated_attention(q, k, v, past_k, past_v, slopes).reshape(bsz, t_len, ATTN_WIDTH)
    mixed = jnp.concatenate([rms_norm(ssm_out, ln_ssm_out), rms_norm(attn_out, ln_attn_out)], axis=-1)
    x = x + mixed @ w_out
    x = x + moe_ffn(rms_norm(x, ln_moe), w_router, b_router, w_up, b_up, w_down, b_down)
    gate = jax.nn.sigmoid(rms_norm(x, ln_ple) @ w_ple_gate + b_ple_gate)
    x = x + gate * (pe.astype(x.dtype) @ w_ple_proj)
    return x, k, v, hT_re.astype(x.dtype), hT_im.astype(x.dtype)


def setup_inputs(seed: int = 0) -> dict:
    key = jax.random.key(seed)
    keys = iter(jax.random.split(key, 48))
    f32 = jnp.float32

    def nrm(shape, scale):
        return scale * jax.random.normal(next(keys), shape, f32)

    def gain(shape):
        return 1.0 + 0.05 * jax.random.normal(next(keys), shape, f32)

    wb = min(MAX_WINDOW, PAST_LEN)
    g, n, c = SSM_GROUPS, SSM_STATE, SSM_GROUP
    n_idx = jnp.arange(n, dtype=f32)
    return {
        "x_prompt": nrm((BATCH, SEQ, D_MODEL), 1.0),
        "x_sample": nrm((DEC_BATCH, DEC_SEQ, D_MODEL), 1.0),
        "cache_attn_k": nrm((DEPTH, DEC_BATCH, wb, N_HEADS, HEAD_DIM), 1.0),
        "cache_attn_v": nrm((DEPTH, DEC_BATCH, wb, N_HEADS, HEAD_DIM), 1.0),
        "state_ssm_re": nrm((DEPTH, DEC_BATCH, g, n), 0.1),
        "state_ssm_im": nrm((DEPTH, DEC_BATCH, g, n), 0.1),
        "p_prompt": nrm((DEPTH, BATCH, SEQ, PLE_DIM), 1.0),
        "p_sample": nrm((DEPTH, DEC_BATCH, DEC_SEQ, PLE_DIM), 1.0),
        "ln_mix": gain((DEPTH, D_MODEL)),
        "w_in": nrm((DEPTH, D_MODEL, IN_WIDTH), D_MODEL ** -0.5),
        "ssm_a_re": -0.5 + nrm((DEPTH, g, n), 0.01),
        "ssm_a_im": math.pi * n_idx + nrm((DEPTH, g, n), 0.01),
        "ssm_b_re": nrm((DEPTH, g, n, c), (2 * c) ** -0.5),
        "ssm_b_im": nrm((DEPTH, g, n, c), (2 * c) ** -0.5),
        "ssm_c_re": nrm((DEPTH, g, c, n), (2 * n) ** -0.5),
        "ssm_c_im": nrm((DEPTH, g, c, n), (2 * n) ** -0.5),
        "ssm_d": nrm((DEPTH, SSM_WIDTH), 1.0),
        "ssm_log_dt": jax.random.uniform(next(keys), (DEPTH, g), f32, math.log(DT_MIN), math.log(DT_MAX)),
        "w_glu": nrm((DEPTH, SSM_WIDTH, SSM_WIDTH), SSM_WIDTH ** -0.5),
        "b_glu": nrm((DEPTH, SSM_WIDTH), 0.01),
        "ln_ssm_out": gain((DEPTH, SSM_WIDTH)),
        "ln_attn_out": gain((DEPTH, ATTN_WIDTH)),
        "w_out": nrm((DEPTH, D_MODEL, D_MODEL), D_MODEL ** -0.5),
        "ln_moe": gain((DEPTH, D_MODEL)),
        "w_router": nrm((DEPTH, D_MODEL, N_EXPERTS), D_MODEL ** -0.5),
        "b_router": nrm((DEPTH, N_EXPERTS), 0.01),
        "w_up": nrm((DEPTH, N_EXPERTS, D_MODEL, 2 * EXPERT_FF), D_MODEL ** -0.5),
        "b_up": nrm((DEPTH, N_EXPERTS, 2 * EXPERT_FF), 0.01),
        "w_down": nrm((DEPTH, N_EXPERTS, EXPERT_FF, D_MODEL), EXPERT_FF ** -0.5),
        "b_down": nrm((DEPTH, N_EXPERTS, D_MODEL), 0.01),
        "ln_ple": gain((DEPTH, D_MODEL)),
        "w_ple_gate": nrm((DEPTH, D_MODEL, D_MODEL), D_MODEL ** -0.5),
        "b_ple_gate": nrm((DEPTH, D_MODEL), 0.01),
        "w_ple_proj": nrm((DEPTH, PLE_DIM, D_MODEL), PLE_DIM ** -0.5),
        "ln_final": gain((D_MODEL,)),
    }


def reference(x_prompt, x_sample, cache_attn_k, cache_attn_v, state_ssm_re, state_ssm_im,
              p_prompt, p_sample, ln_mix, w_in, ssm_a_re, ssm_a_im, ssm_b_re, ssm_b_im,
              ssm_c_re, ssm_c_im, ssm_d, ssm_log_dt, w_glu, b_glu, ln_ssm_out, ln_attn_out,
              w_out, ln_moe, w_router, b_router, w_up, b_up, w_down, b_down,
              ln_ple, w_ple_gate, b_ple_gate, w_ple_proj, ln_final):
    slopes = alibi_slopes(N_HEADS)
    xp, xs = x_prompt, x_sample
    kp_l, vp_l, hrp_l, hip_l = [], [], [], []
    ks_l, vs_l, hrs_l, his_l = [], [], [], []
    zeros_state = jnp.zeros((x_prompt.shape[0], SSM_GROUPS, SSM_STATE), jnp.float32)
    wb_prompt = min(MAX_WINDOW, x_prompt.shape[1])
    for i in range(DEPTH):
        lw = (ln_mix[i], w_in[i], ssm_a_re[i], ssm_a_im[i], ssm_b_re[i], ssm_b_im[i],
              ssm_c_re[i], ssm_c_im[i], ssm_d[i], ssm_log_dt[i], w_glu[i], b_glu[i],
              ln_ssm_out[i], ln_attn_out[i], w_out[i], ln_moe[i], w_router[i], b_router[i],
              w_up[i], b_up[i], w_down[i], b_down[i], ln_ple[i], w_ple_gate[i],
              b_ple_gate[i], w_ple_proj[i])
        xp, kp, vp, hrp, hip = decoder_layer(xp, p_prompt[i], None, None, zeros_state, zeros_state, slopes, *lw)
        xs, ks, vs, hrs, his = decoder_layer(xs, p_sample[i], cache_attn_k[i], cache_attn_v[i],
                                             state_ssm_re[i], state_ssm_im[i], slopes, *lw)
        kp_l.append(kp[:, -wb_prompt:])
        vp_l.append(vp[:, -wb_prompt:])
        hrp_l.append(hrp)
        hip_l.append(hip)
        ks_l.append(ks)
        vs_l.append(vs)
        hrs_l.append(hrs)
        his_l.append(his)
    y_prompt = rms_norm(xp, ln_final)
    y_sample = rms_norm(xs, ln_final)
    new_k_prompt = jnp.stack(kp_l, 0)
    new_v_prompt = jnp.stack(vp_l, 0)
    new_ssm_re_prompt = jnp.stack(hrp_l, 0)
    new_ssm_im_prompt = jnp.stack(hip_l, 0)
    new_k_sample = jnp.stack(ks_l, 0)
    new_v_sample = jnp.stack(vs_l, 0)
    new_ssm_re_sample = jnp.stack(hrs_l, 0)
    new_ssm_im_sample = jnp.stack(his_l, 0)
    return (y_prompt, y_sample, new_k_prompt, new_v_prompt, new_ssm_re_prompt, new_ssm_im_prompt,
            new_k_sample, new_v_sample, new_ssm_re_sample, new_ssm_im_sample)
```

```python
import functools
import math

import jax
import jax.numpy as jnp
from jax import lax
from jax.experimental import pallas as pl
from jax.experimental.pallas import tpu as pltpu

F32 = jnp.float32
BF16 = jnp.bfloat16
I32 = jnp.int32

D_MODEL = 1024
SSM_WIDTH = 512
SSM_GROUP = 16
SSM_GROUPS = 32
SSM_STATE = 64
ATTN_WIDTH = 512
HEAD_DIM = 64
N_HEADS = 8
IN_WIDTH = SSM_WIDTH + 3 * ATTN_WIDTH
DILATIONS = (1, 4, 16)
WINDOWS = (128, 512, 2048)
ATTN_BLOCK = 128
N_EXPERTS = 32
TOP_K = 4
EXPERT_FF = D_MODEL
SWIGLU_LIMIT = 7.0
SWIGLU_ALPHA = 1.702
PLE_DIM = 256
EPS = 1e-6
MASK_VALUE = -1e30
NEG_BIG = -3.0e38

LANES = 128
SUBLANES = 8
ROW_TILES = D_MODEL // LANES
TOKEN_TILE = 256
MOE_TILE = 256
SSM_HALF = SSM_WIDTH // 2
STATE_HALF = SSM_GROUPS * SSM_STATE // 2
ALIBI_SLOPES = tuple(2.0 ** (-8.0 * (h + 1) / N_HEADS) for h in range(N_HEADS))
VMEM_LIMIT = 56 * 1024 * 1024


def _rms(x, g):
    return x * lax.rsqrt(jnp.mean(x * x, axis=-1, keepdims=True) + EPS) * g


def _sigmoid(x):
    return 1.0 / (1.0 + jnp.exp(-x))


def _full(shape):
    n = len(shape)
    return pl.BlockSpec(shape, lambda *_: (0,) * n)


def _in_kernel(x_ref, g_ref, w_ref, u_ref, k_ref, v_ref, qb_ref, kb_ref, vb_ref):
    h = _rms(x_ref[...], g_ref[...]).astype(BF16)
    p = jnp.dot(h, w_ref[...], preferred_element_type=F32)
    u_ref[...] = p[:, :SSM_WIDTH]
    k = p[:, SSM_WIDTH + ATTN_WIDTH:SSM_WIDTH + 2 * ATTN_WIDTH]
    v = p[:, SSM_WIDTH + 2 * ATTN_WIDTH:]
    k_ref[...] = k
    v_ref[...] = v
    qb_ref[...] = (p[:, SSM_WIDTH:SSM_WIDTH + ATTN_WIDTH] * (HEAD_DIM ** -0.5)).astype(BF16)
    kb_ref[...] = k.astype(BF16)
    vb_ref[...] = v.astype(BF16)


def _in_proj(x2d, ln_mix, w_in_bf16):
    n = x2d.shape[0]
    tm = min(n, 512)
    row = lambda w: pl.BlockSpec((tm, w), lambda i: (i, 0))
    return pl.pallas_call(
        _in_kernel,
        grid=(n // tm,),
        in_specs=[row(D_MODEL), _full((1, D_MODEL)), _full((D_MODEL, IN_WIDTH))],
        out_specs=[row(SSM_WIDTH)] * 6,
        out_shape=[jax.ShapeDtypeStruct((n, SSM_WIDTH), F32)] * 3
        + [jax.ShapeDtypeStruct((n, ATTN_WIDTH), BF16)] * 3,
        compiler_params=pltpu.CompilerParams(dimension_semantics=("arbitrary",),
                                             vmem_limit_bytes=VMEM_LIMIT),
        name="in_proj",
    )(x2d, ln_mix.reshape(1, D_MODEL), w_in_bf16)


def _s5_kernel(u_ref, bmat_ref, cmat_ref, are_ref, aim_ref, h0_ref, d_ref,
               y_ref, ht_ref, buf, hc, tmp, *, nb, tt, batched):
    rows = 2 * nb
    ntile = 2 * STATE_HALF // LANES
    half_tiles = ntile // 2

    def lane_tile(c):
        return slice(c * LANES, (c + 1) * LANES)

    @pl.when(pl.program_id(0) == 0)
    def _():
        hc[...] = h0_ref[...]

    if batched:
        u_all = u_ref[...].reshape(nb * tt, SSM_WIDTH)
        ub_all = u_all.astype(BF16)
        for hf in range(2):
            bu = jnp.dot(ub_all[:, hf * SSM_HALF:(hf + 1) * SSM_HALF], bmat_ref[hf],
                         preferred_element_type=F32)
            for c in range(ntile):
                tmp[c] = bu[:, lane_tile(c)]
            for c in range(ntile):
                for t in range(tt):
                    buf[c, t * rows + hf * nb:t * rows + (hf + 1) * nb, :] = tmp[c, pl.ds(t, nb, stride=tt), :]
    else:
        for b in range(nb):
            ub = u_ref[b].astype(BF16)
            for hf in range(2):
                bu = jnp.dot(ub[:, hf * SSM_HALF:(hf + 1) * SSM_HALF], bmat_ref[hf],
                             preferred_element_type=F32)
                for c in range(ntile):
                    buf[c, pl.ds(hf * nb + b, tt, stride=rows), :] = bu[:, lane_tile(c)]

    group = 4
    for s in range(rows // SUBLANES):
        r0 = s * SUBLANES
        for c0 in range(0, half_tiles, group):
            ar = [are_ref[r0:r0 + SUBLANES, lane_tile(c0 + k)] for k in range(group)]
            ai = [aim_ref[r0:r0 + SUBLANES, lane_tile(c0 + k)] for k in range(group)]
            init = tuple(hc[r0:r0 + SUBLANES, lane_tile(c0 + k)] for k in range(group)) + tuple(
                hc[r0:r0 + SUBLANES, lane_tile(half_tiles + c0 + k)] for k in range(group))

            def step(t, carry, r0=r0, c0=c0, ar=ar, ai=ai):
                row = pl.multiple_of(t * rows + r0, SUBLANES)
                out_r, out_i = [], []
                for k in range(group):
                    hr, hi = carry[k], carry[group + k]
                    xr = buf[c0 + k, pl.ds(row, SUBLANES), :]
                    xi = buf[half_tiles + c0 + k, pl.ds(row, SUBLANES), :]
                    nr = ar[k] * hr - ai[k] * hi + xr
                    ni = ar[k] * hi + ai[k] * hr + xi
                    buf[c0 + k, pl.ds(row, SUBLANES), :] = nr
                    buf[half_tiles + c0 + k, pl.ds(row, SUBLANES), :] = ni
                    out_r.append(nr)
                    out_i.append(ni)
                return tuple(out_r) + tuple(out_i)

            fin = lax.fori_loop(0, tt, step, init, unroll=min(tt, 8))
            for k in range(group):
                hc[r0:r0 + SUBLANES, lane_tile(c0 + k)] = fin[k]
                hc[r0:r0 + SUBLANES, lane_tile(half_tiles + c0 + k)] = fin[group + k]

    if batched:
        parts = []
        for hf in range(2):
            for c in range(ntile):
                for t in range(tt):
                    tmp[c, pl.ds(t, nb, stride=tt), :] = buf[c, t * rows + hf * nb:t * rows + (hf + 1) * nb, :]
            hs = jnp.concatenate([tmp[c] for c in range(ntile)], axis=1).astype(BF16)
            parts.append(jnp.dot(hs, cmat_ref[hf], preferred_element_type=F32))
        y_all = jnp.concatenate(parts, axis=1) + d_ref[...] * u_all
        y_ref[...] = y_all.reshape(nb, tt, SSM_WIDTH)
    else:
        for b in range(nb):
            parts = []
            for hf in range(2):
                hs = jnp.concatenate(
                    [buf[c, pl.ds(hf * nb + b, tt, stride=rows), :] for c in range(ntile)],
                    axis=1).astype(BF16)
                parts.append(jnp.dot(hs, cmat_ref[hf], preferred_element_type=F32))
            y_ref[b] = jnp.concatenate(parts, axis=1) + d_ref[...] * u_ref[b]

    ht_ref[...] = hc[...]


def _s5(u3, h0, bmat, cmat, a_re, a_im, d_skip):
    nb, t_len, _ = u3.shape
    tt = min(t_len, 256)
    rows = 2 * nb
    batched = tt < 16
    kern = functools.partial(_s5_kernel, nb=nb, tt=tt, batched=batched)
    ntile = 2 * STATE_HALF // LANES
    tmp_shape = (ntile, nb * tt, LANES) if batched else (1, SUBLANES, LANES)
    return pl.pallas_call(
        kern,
        grid=(t_len // tt,),
        in_specs=[pl.BlockSpec((nb, tt, SSM_WIDTH), lambda i: (0, i, 0)),
                  _full((2, SSM_HALF, 2 * STATE_HALF)),
                  _full((2, 2 * STATE_HALF, SSM_HALF)),
                  _full((rows, STATE_HALF)), _full((rows, STATE_HALF)),
                  _full((rows, 2 * STATE_HALF)), _full((1, SSM_WIDTH))],
        out_specs=[pl.BlockSpec((nb, tt, SSM_WIDTH), lambda i: (0, i, 0)),
                   _full((rows, 2 * STATE_HALF))],
        out_shape=[jax.ShapeDtypeStruct((nb, t_len, SSM_WIDTH), F32),
                   jax.ShapeDtypeStruct((rows, 2 * STATE_HALF), F32)],
        scratch_shapes=[pltpu.VMEM((ntile, tt * rows, LANES), F32),
                        pltpu.VMEM((rows, 2 * STATE_HALF), F32),
                        pltpu.VMEM(tmp_shape, F32)],
        compiler_params=pltpu.CompilerParams(dimension_semantics=("arbitrary",),
                                             vmem_limit_bytes=VMEM_LIMIT),
        name="s5_scan",
    )(u3, bmat, cmat, a_re, a_im, h0, d_skip.reshape(1, SSM_WIDTH))


def _s5_params(a_re, a_im, b_re, b_im, c_re, c_im, log_dt):
    dt = jnp.exp(log_dt)[:, None]
    mag = jnp.exp(dt * a_re)
    ang = dt * a_im
    ab_re, ab_im = mag * jnp.cos(ang), mag * jnp.sin(ang)
    den = a_re * a_re + a_im * a_im
    nr, ni = ab_re - 1.0, ab_im
    f_re = (nr * a_re + ni * a_im) / den
    f_im = (ni * a_re - nr * a_im) / den
    bb_re = f_re[..., None] * b_re - f_im[..., None] * b_im
    bb_im = f_re[..., None] * b_im + f_im[..., None] * b_re
    gh = SSM_GROUPS // 2
    eye = jnp.eye(gh, dtype=F32)

    def b_half(w):
        return jnp.einsum('gnc,gh->gchn', w, eye).reshape(gh * SSM_GROUP, gh * SSM_STATE)

    def c_half(w):
        return jnp.einsum('gcn,gh->gnhc', w, eye).reshape(gh * SSM_STATE, gh * SSM_GROUP)

    bmat = jnp.stack([jnp.concatenate([b_half(bb_re[h * gh:(h + 1) * gh]),
                                       b_half(bb_im[h * gh:(h + 1) * gh])], axis=1)
                      for h in range(2)]).astype(BF16)
    cmat = jnp.stack([jnp.concatenate([c_half(c_re[h * gh:(h + 1) * gh]),
                                       -c_half(c_im[h * gh:(h + 1) * gh])], axis=0)
                      for h in range(2)]).astype(BF16)
    return bmat, cmat, ab_re.reshape(2, 1, STATE_HALF), ab_im.reshape(2, 1, STATE_HALF)


def _state_to_rows(h_re, h_im):
    nb = h_re.shape[0]
    f = lambda h: h.reshape(nb, 2, STATE_HALF).transpose(1, 0, 2).reshape(2 * nb, STATE_HALF)
    return jnp.concatenate([f(h_re), f(h_im)], axis=1)


def _rows_to_state(ht, nb):
    f = lambda h: h.reshape(2, nb, STATE_HALF).transpose(1, 0, 2).reshape(nb, SSM_GROUPS, SSM_STATE)
    return f(ht[:, :STATE_HALF]), f(ht[:, STATE_HALF:])


def _attn_prompt_kernel(q_ref, kp_ref, kc_ref, vp_ref, vc_ref, o_ref, l_ref, *, dil):
    blk = ATTN_BLOCK
    n = pl.program_id(2)
    i_idx = lax.broadcasted_iota(I32, (blk, 2 * blk), 0)
    j_idx = lax.broadcasted_iota(I32, (blk, 2 * blk), 1)
    delta = i_idx - j_idx + blk
    valid = (delta >= 0) & (delta <= blk) & ((j_idx >= blk) | (n > 0))
    dist = (delta * dil).astype(F32)
    lane = lax.broadcasted_iota(I32, (blk, LANES), 1)
    lse_all = jnp.zeros((blk, LANES), F32)
    for hp in range(N_HEADS // 2):
        cols = slice(hp * LANES, (hp + 1) * LANES)
        q2 = q_ref[:, cols]
        kk = jnp.concatenate([kp_ref[:, cols], kc_ref[:, cols]], axis=0)
        vv = jnp.concatenate([vp_ref[:, cols], vc_ref[:, cols]], axis=0)
        outs = []
        for half in range(2):
            h = 2 * hp + half
            in_head = (lane >= half * HEAD_DIM) & (lane < (half + 1) * HEAD_DIM)
            qm = jnp.where(in_head, q2, jnp.zeros_like(q2))
            s = lax.dot_general(qm, kk, (((1,), (1,)), ((), ())), preferred_element_type=F32)
            s = jnp.where(valid, s - ALIBI_SLOPES[h] * dist, MASK_VALUE)
            m = jnp.max(s, axis=1, keepdims=True)
            p = jnp.exp(s - m)
            l = jnp.sum(p, axis=1, keepdims=True)
            outs.append(jnp.dot(p.astype(BF16), vv, preferred_element_type=F32) / l)
            lse_all = jnp.where(lane == h, m + jnp.log(l), lse_all)
        o_ref[:, cols] = jnp.where(lane < HEAD_DIM, outs[0], outs[1])
    l_ref[...] = lse_all


def _attn_prompt_branch(qb, kb, vb, bsz, s_len, dil):
    sub = s_len // dil
    nblk = sub // ATTN_BLOCK
    view = lambda t: t.reshape(bsz, sub, dil * ATTN_WIDTH)
    cur = pl.BlockSpec((None, ATTN_BLOCK, ATTN_WIDTH), lambda b, r, n: (b, n, r))
    prev = pl.BlockSpec((None, ATTN_BLOCK, ATTN_WIDTH), lambda b, r, n: (b, jnp.maximum(n - 1, 0), r))
    o, lse = pl.pallas_call(
        functools.partial(_attn_prompt_kernel, dil=dil),
        grid=(bsz, dil, nblk),
        in_specs=[cur, prev, cur, prev, cur],
        out_specs=[cur, pl.BlockSpec((None, ATTN_BLOCK, LANES), lambda b, r, n: (b, n, r))],
        out_shape=[jax.ShapeDtypeStruct((bsz, sub, dil * ATTN_WIDTH), F32),
                   jax.ShapeDtypeStruct((bsz, sub, dil * LANES), F32)],
        compiler_params=pltpu.CompilerParams(
            dimension_semantics=("arbitrary", "arbitrary", "arbitrary"),
            vmem_limit_bytes=VMEM_LIMIT),
        name=f"attn_prompt_d{dil}",
    )(view(qb), view(kb), view(kb), view(vb), view(vb))
    return o.reshape(bsz * s_len, ATTN_WIDTH), lse.reshape(bsz * s_len, LANES)


def _attn_sample_kernel(q_ref, kn_ref, vn_ref, kc_ref, vc_ref, o_ref, *, t_len, wb):
    nrow = N_HEADS * t_len
    t_shift = t_len.bit_length() - 1
    d_shift = HEAD_DIM.bit_length() - 1
    q = q_ref[...].astype(F32)
    qt = jnp.concatenate([q] * N_HEADS, axis=0)
    row_w = lax.broadcasted_iota(I32, (nrow, ATTN_WIDTH), 0)
    lane_w = lax.broadcasted_iota(I32, (nrow, ATTN_WIDTH), 1)
    qm = jnp.where((lane_w >> d_shift) == (row_w >> t_shift), qt, 0.0).astype(BF16)
    nt = (((1,), (1,)), ((), ()))
    head_col = lax.broadcasted_iota(I32, (nrow, 1), 0) >> t_shift
    slope = jnp.zeros((nrow, 1), F32)
    for h in range(N_HEADS):
        slope = jnp.where(head_col == h, ALIBI_SLOPES[h], slope)

    def scores(keys, dist):
        s = lax.dot_general(qm, keys, nt, preferred_element_type=F32)
        mult = jnp.zeros(dist.shape, F32)
        for win, dil in zip(WINDOWS, DILATIONS):
            hit = (dist >= 0) & (dist <= win) & ((dist & (dil - 1)) == 0)
            mult = mult + jnp.where(hit, 1.0, 0.0)
        s = jnp.where(mult > 0.0, s - slope * dist.astype(F32), MASK_VALUE)
        return s, mult

    tq_c = lax.broadcasted_iota(I32, (nrow, wb), 0) & (t_len - 1)
    dist_c = wb + tq_c - lax.broadcasted_iota(I32, (nrow, wb), 1)
    s_c, mult_c = scores(kc_ref[...].astype(BF16), dist_c)

    pad = LANES - t_len
    kn = jnp.concatenate([kn_ref[...].astype(F32), jnp.zeros((pad, ATTN_WIDTH), F32)], axis=0).astype(BF16)
    vn = jnp.concatenate([vn_ref[...].astype(F32), jnp.zeros((pad, ATTN_WIDTH), F32)], axis=0).astype(BF16)
    tq_n = lax.broadcasted_iota(I32, (nrow, LANES), 0) & (t_len - 1)
    tk_n = lax.broadcasted_iota(I32, (nrow, LANES), 1)
    dist_n = jnp.where(tk_n < t_len, tq_n - tk_n, -1)
    s_n, mult_n = scores(kn, dist_n)

    m = jnp.maximum(jnp.max(s_c, axis=1, keepdims=True), jnp.max(s_n, axis=1, keepdims=True))
    p_c = jnp.exp(s_c - m) * mult_c
    p_n = jnp.exp(s_n - m) * mult_n
    l = jnp.sum(p_c, axis=1, keepdims=True) + jnp.sum(p_n, axis=1, keepdims=True)
    o = (jnp.dot(p_c.astype(BF16), vc_ref[...].astype(BF16), preferred_element_type=F32)
         + jnp.dot(p_n.astype(BF16), vn, preferred_element_type=F32)) / l
    lane_o = lax.broadcasted_iota(I32, (t_len, ATTN_WIDTH), 1) >> d_shift
    out = jnp.zeros((t_len, ATTN_WIDTH), F32)
    for h in range(N_HEADS):
        out = jnp.where(lane_o == h, o[h * t_len:(h + 1) * t_len], out)
    o_ref[...] = out


def _attn_sample(qb, kb, vb, cache_k, cache_v):
    bsz, t_len, _ = qb.shape
    wb = cache_k.shape[1]
    new = pl.BlockSpec((None, t_len, ATTN_WIDTH), lambda b: (b, 0, 0))
    old = pl.BlockSpec((None, wb, ATTN_WIDTH), lambda b: (b, 0, 0))
    return pl.pallas_call(
        functools.partial(_attn_sample_kernel, t_len=t_len, wb=wb),
        grid=(bsz,),
        in_specs=[new, new, new, old, old],
        out_specs=new,
        out_shape=jax.ShapeDtypeStruct((bsz, t_len, ATTN_WIDTH), F32),
        compiler_params=pltpu.CompilerParams(dimension_semantics=("arbitrary",),
                                             vmem_limit_bytes=VMEM_LIMIT),
        name="attn_sample",
    )(qb, kb, vb, cache_k, cache_v)


def _mid_kernel(*refs, n_branch):
    x_ref, y_ref = refs[0], refs[1]
    o_refs = refs[2:2 + n_branch]
    pos = 2 + n_branch
    l_refs = refs[pos:pos + (n_branch if n_branch > 1 else 0)]
    pos += len(l_refs)
    (wglu_ref, bglu_ref, lns_ref, lna_ref, wout_ref, lnm_ref, wrh_ref, wrl_ref, br_ref,
     expand_ref, tri_ref,
     x1_ref, hrow_ref, eidx_ref, epos_ref, egate_ref, cnt_ref, carry) = refs[pos:]
    tm = x_ref.shape[0]

    @pl.when(pl.program_id(0) == 0)
    def _():
        carry[...] = jnp.zeros_like(carry)

    y = y_ref[...]
    z = y * (0.5 * (1.0 + jnp.tanh(math.sqrt(2.0 / math.pi) * (y + 0.044715 * (y * y * y)))))
    glu = z * _sigmoid(jnp.dot(z.astype(BF16), wglu_ref[...], preferred_element_type=F32) + bglu_ref[...])
    n_ssm = _rms(glu, lns_ref[...])

    if n_branch == 1:
        attn = o_refs[0][...]
    else:
        lses = [r[...] for r in l_refs]
        mx = functools.reduce(jnp.maximum, lses)
        es = [jnp.exp(l - mx) for l in lses]
        inv = 1.0 / functools.reduce(lambda a, b: a + b, es)
        attn = jnp.zeros((tm, ATTN_WIDTH), F32)
        for e, o_ref in zip(es, o_refs):
            w = e * inv
            w_hi = w.astype(BF16)
            w_lo = (w - w_hi.astype(F32)).astype(BF16)
            wide = (jnp.dot(w_hi, expand_ref[...], preferred_element_type=F32)
                    + jnp.dot(w_lo, expand_ref[...], preferred_element_type=F32))
            attn = attn + wide * o_ref[...]
    n_attn = _rms(attn, lna_ref[...])

    x1 = (x_ref[...]
          + jnp.dot(n_ssm.astype(BF16), wout_ref[:SSM_WIDTH, :], preferred_element_type=F32)
          + jnp.dot(n_attn.astype(BF16), wout_ref[SSM_WIDTH:, :], preferred_element_type=F32))
    x1_ref[...] = x1
    hm = _rms(x1, lnm_ref[...])
    for s in range(ROW_TILES):
        hrow_ref[pl.ds(s, tm, stride=ROW_TILES), :] = hm[:, s * LANES:(s + 1) * LANES]

    h_hi = hm.astype(BF16)
    h_lo = (hm - h_hi.astype(F32)).astype(BF16)
    logits = (jnp.dot(h_hi, wrh_ref[...], preferred_element_type=F32)
              + jnp.dot(h_lo, wrh_ref[...], preferred_element_type=F32)
              + jnp.dot(h_hi, wrl_ref[...], preferred_element_type=F32)
              + br_ref[...])
    lane = lax.broadcasted_iota(I32, (tm, LANES), 1)
    lane_f = lane.astype(F32)
    work = jnp.where(lane < N_EXPERTS, logits, NEG_BIG)
    vals, idxs, hots = [], [], []
    for _ in range(TOP_K):
        m = jnp.max(work, axis=1, keepdims=True)
        idx = jnp.min(jnp.where(work == m, lane_f, float(LANES)), axis=1, keepdims=True)
        hot = lane_f == idx
        vals.append(m)
        idxs.append(idx)
        hots.append(hot)
        work = jnp.where(hot, NEG_BIG, work)
    exps = [jnp.exp(v - vals[0]) for v in vals]
    inv = 1.0 / functools.reduce(lambda a, b: a + b, exps)

    sel = functools.reduce(lambda a, b: a + b, [h.astype(F32) for h in hots])
    before = jnp.dot(tri_ref[...], sel.astype(BF16), preferred_element_type=F32) + carry[0:1, :]
    eidx = jnp.zeros((tm, LANES), I32)
    epos = jnp.zeros((tm, LANES), I32)
    egate = jnp.zeros((tm, LANES), F32)
    for k in range(TOP_K):
        pk = jnp.sum(jnp.where(hots[k], before, 0.0), axis=1, keepdims=True)
        eidx = jnp.where(lane == k, idxs[k].astype(I32), eidx)
        epos = jnp.where(lane == k, pk.astype(I32), epos)
        egate = jnp.where(lane == k, exps[k] * inv, egate)
    eidx_ref[...] = eidx
    epos_ref[...] = epos
    egate_ref[...] = egate
    total = carry[0:1, :] + jnp.sum(sel, axis=0, keepdims=True)
    carry[...] = jnp.broadcast_to(total, carry.shape)
    cnt_ref[...] = jnp.broadcast_to(total, cnt_ref.shape).astype(I32)


def _mid(x2d, y2d, attn_o, attn_lse, w):
    n = x2d.shape[0]
    tm = TOKEN_TILE
    n_branch = len(attn_o)
    row = lambda width: pl.BlockSpec((tm, width), lambda i: (i, 0))
    in_specs = ([row(D_MODEL), row(SSM_WIDTH)] + [row(ATTN_WIDTH)] * n_branch
                + [row(LANES)] * len(attn_lse)
                + [_full((SSM_WIDTH, SSM_WIDTH)), _full((1, SSM_WIDTH)), _full((1, SSM_WIDTH)),
                   _full((1, ATTN_WIDTH)), _full((D_MODEL, D_MODEL)), _full((1, D_MODEL)),
                   _full((D_MODEL, LANES)), _full((D_MODEL, LANES)), _full((1, LANES)),
                   _full((LANES, ATTN_WIDTH)), _full((tm, tm))])
    out_specs = [row(D_MODEL), pl.BlockSpec((tm * ROW_TILES, LANES), lambda i: (i, 0)),
                 row(LANES), row(LANES), row(LANES), _full((SUBLANES, LANES))]
    out_shape = [jax.ShapeDtypeStruct((n, D_MODEL), F32),
                 jax.ShapeDtypeStruct((n * ROW_TILES, LANES), F32),
                 jax.ShapeDtypeStruct((n, LANES), I32),
                 jax.ShapeDtypeStruct((n, LANES), I32),
                 jax.ShapeDtypeStruct((n, LANES), F32),
                 jax.ShapeDtypeStruct((SUBLANES, LANES), I32)]
    return pl.pallas_call(
        functools.partial(_mid_kernel, n_branch=n_branch),
        grid=(n // tm,),
        in_specs=in_specs,
        out_specs=out_specs,
        out_shape=out_shape,
        scratch_shapes=[pltpu.VMEM((SUBLANES, LANES), F32)],
        compiler_params=pltpu.CompilerParams(dimension_semantics=("arbitrary",),
                                             vmem_limit_bytes=VMEM_LIMIT),
        name="mid",
    )(x2d, y2d, *attn_o, *attn_lse, w["w_glu"], w["b_glu"], w["ln_ssm_out"], w["ln_attn_out"],
      w["w_out"], w["ln_moe"], w["wr_hi"], w["wr_lo"], w["b_router"], w["expand"], w["tri"])


def _dispatch_kernel(dest_ref, hp_hbm, hs_hbm, xs_hbm, sem, *, n_prompt_tiles):
    i = pl.program_id(0)
    npair = TOKEN_TILE * TOP_K

    def copy(src_hbm, tok, d):
        return pltpu.make_async_copy(
            src_hbm.at[pl.ds(pl.multiple_of(tok * ROW_TILES, ROW_TILES), ROW_TILES), :],
            xs_hbm.at[pl.ds(pl.multiple_of(d * ROW_TILES, ROW_TILES), ROW_TILES), :], sem)

    def run(src_hbm, base):
        def issue(j, c):
            copy(src_hbm, base + j // TOP_K, dest_ref[0, j]).start()
            return c

        def drain(j, c):
            copy(src_hbm, 0, 0).wait()
            return c

        lax.fori_loop(0, npair, issue, 0)
        lax.fori_loop(0, npair, drain, 0)

    @pl.when(i < n_prompt_tiles)
    def _():
        run(hp_hbm, i * TOKEN_TILE)

    @pl.when(i >= n_prompt_tiles)
    def _():
        run(hs_hbm, (i - n_prompt_tiles) * TOKEN_TILE)


def _dispatch(dest, hrow_p, hrow_s):
    n_p = hrow_p.shape[0] // ROW_TILES
    n_s = hrow_s.shape[0] // ROW_TILES
    ntile = (n_p + n_s) // TOKEN_TILE
    npair = TOKEN_TILE * TOP_K
    return pl.pallas_call(
        functools.partial(_dispatch_kernel, n_prompt_tiles=n_p // TOKEN_TILE),
        grid=(ntile,),
        in_specs=[pl.BlockSpec((None, 1, npair), lambda i: (i, 0, 0), memory_space=pltpu.SMEM),
                  pl.BlockSpec(memory_space=pl.ANY), pl.BlockSpec(memory_space=pl.ANY)],
        out_specs=pl.BlockSpec(memory_space=pl.ANY),
        out_shape=jax.ShapeDtypeStruct(((n_p + n_s) * TOP_K * ROW_TILES, LANES), F32),
        scratch_shapes=[pltpu.SemaphoreType.DMA(())],
        compiler_params=pltpu.CompilerParams(dimension_semantics=("arbitrary",)),
        name="moe_dispatch",
    )(dest.reshape(ntile, 1, npair), hrow_p, hrow_s)


def _expert_kernel(vt_ref, ve_ref, vok_ref, gs_ref, xs_ref, wu_ref, bu_ref, wd_ref, bd_ref,
                   out_ref, wu_s, wd_s, x_s):
    v = pl.program_id(0)
    e = ve_ref[v]
    j = vt_ref[v]
    vprev = jnp.maximum(v - 1, 0)
    new_e = (v == 0) | (e != ve_ref[vprev])
    new_j = (v == 0) | (j != vt_ref[vprev])
    tm = MOE_TILE

    @pl.when(new_e)
    def _():
        wu_s[...] = wu_ref[...].astype(BF16)
        wd_s[...] = wd_ref[...].astype(BF16)

    @pl.when(new_j)
    def _():
        out_ref[...] = jnp.zeros_like(out_ref)

    @pl.when(vok_ref[v] == 1)
    def _():
        for s in range(ROW_TILES):
            x_s[:, s * LANES:(s + 1) * LANES] = xs_ref[pl.ds(s, tm, stride=ROW_TILES), :].astype(BF16)
        a = jnp.dot(x_s[...], wu_s[...], preferred_element_type=F32) + bu_ref[...]
        g = jnp.minimum(a[:, :EXPERT_FF], SWIGLU_LIMIT)
        lin = jnp.clip(a[:, EXPERT_FF:], -SWIGLU_LIMIT, SWIGLU_LIMIT)
        act = (lin + 1.0) * (g * _sigmoid(SWIGLU_ALPHA * g))
        y = jnp.dot(act.astype(BF16), wd_s[...], preferred_element_type=F32) + bd_ref[...]
        rows = j * tm + lax.broadcasted_iota(I32, (tm, 1), 0)
        mine = (rows >= gs_ref[e]) & (rows < gs_ref[e + 1])
        for s in range(ROW_TILES):
            cur = out_ref[pl.ds(s, tm, stride=ROW_TILES), :]
            out_ref[pl.ds(s, tm, stride=ROW_TILES), :] = jnp.where(
                mine, y[:, s * LANES:(s + 1) * LANES], cur)


def _experts(xs, vt, ve, vok, gstart, w_up, b_up, w_down, b_down):
    tm = MOE_TILE
    nvisit = vt.shape[0]
    rows = pl.BlockSpec((tm * ROW_TILES, LANES), lambda v, vt, ve, vok, gs: (vt[v], 0))
    per_e = lambda a, b: pl.BlockSpec((None, a, b), lambda v, vt, ve, vok, gs: (ve[v], 0, 0))
    grid_spec = pltpu.PrefetchScalarGridSpec(
        num_scalar_prefetch=4,
        grid=(nvisit,),
        in_specs=[rows, per_e(D_MODEL, 2 * EXPERT_FF), per_e(1, 2 * EXPERT_FF),
                  per_e(EXPERT_FF, D_MODEL), per_e(1, D_MODEL)],
        out_specs=rows,
        scratch_shapes=[pltpu.VMEM((D_MODEL, 2 * EXPERT_FF), BF16),
                        pltpu.VMEM((EXPERT_FF, D_MODEL), BF16),
                        pltpu.VMEM((tm, D_MODEL), BF16)],
    )
    return pl.pallas_call(
        _expert_kernel,
        grid_spec=grid_spec,
        out_shape=jax.ShapeDtypeStruct(xs.shape, F32),
        compiler_params=pltpu.CompilerParams(dimension_semantics=("arbitrary",),
                                             vmem_limit_bytes=VMEM_LIMIT),
        name="moe_experts",
    )(vt, ve, vok, gstart, xs, w_up, b_up.reshape(N_EXPERTS, 1, 2 * EXPERT_FF),
      w_down, b_down.reshape(N_EXPERTS, 1, D_MODEL))


def _routing(eidx_p, epos_p, cnt_p, eidx_s, epos_s, cnt_s, n_rows):
    cnt_p = cnt_p[0, :N_EXPERTS]
    cnt_s = cnt_s[0, :N_EXPERTS]
    cnt = cnt_p + cnt_s
    gend = jnp.cumsum(cnt)
    gstart = gend - cnt
    ep = eidx_p[:, :TOP_K]
    es = eidx_s[:, :TOP_K]
    dest_p = gstart[ep] + epos_p[:, :TOP_K]
    dest_s = gstart[es] + cnt_p[es] + epos_s[:, :TOP_K]
    ntile = n_rows // MOE_TILE
    nvisit = ntile + N_EXPERTS
    first = gstart // MOE_TILE
    last = jnp.maximum(gend - 1, 0) // MOE_TILE
    nv = jnp.where(cnt > 0, last - first + 1, 0)
    vend = jnp.cumsum(nv)
    vstart = vend - nv
    total = vend[-1]
    v = jnp.arange(nvisit, dtype=I32)
    vc = jnp.minimum(v, total - 1)
    ve = jnp.searchsorted(vend, vc, side='right').astype(I32)
    vt = (first[ve] + vc - vstart[ve]).astype(I32)
    vok = (v < total).astype(I32)
    gs = jnp.concatenate([gstart, gend[-1:]]).astype(I32)
    return dest_p.astype(I32), dest_s.astype(I32), vt, ve, vok, gs


def _out_kernel(dest_ref, y_hbm, x1_ref, gate_ref, pe_ref, lnp_ref, wg_ref, bg_ref, wp_ref, lnf_ref,
                o_ref, buf, sem):
    tm = TOKEN_TILE
    npair = tm * TOP_K

    def copy(d, slot):
        return pltpu.make_async_copy(
            y_hbm.at[pl.ds(pl.multiple_of(d * ROW_TILES, ROW_TILES), ROW_TILES), :],
            buf.at[pl.ds(pl.multiple_of(slot * ROW_TILES, ROW_TILES), ROW_TILES), :], sem)

    def issue(j, c):
        copy(dest_ref[0, j], j).start()
        return c

    def drain(j, c):
        copy(0, 0).wait()
        return c

    lax.fori_loop(0, npair, issue, 0)
    lax.fori_loop(0, npair, drain, 0)

    gates = gate_ref[...]
    parts = []
    for s in range(ROW_TILES):
        acc = jnp.zeros((tm, LANES), F32)
        for k in range(TOP_K):
            rows = buf[pl.ds(k * ROW_TILES + s, tm, stride=TOP_K * ROW_TILES), :]
            acc = acc + gates[:, k:k + 1] * rows
        parts.append(acc)
    x2 = x1_ref[...] + jnp.concatenate(parts, axis=1)
    gate = _sigmoid(jnp.dot(_rms(x2, lnp_ref[...]).astype(BF16), wg_ref[...],
                            preferred_element_type=F32) + bg_ref[...])
    x3 = x2 + gate * jnp.dot(pe_ref[...].astype(BF16), wp_ref[...], preferred_element_type=F32)
    o_ref[...] = _rms(x3, lnf_ref[...])


def _combine(dest, y_rows, x1, egate, pe, w):
    n = x1.shape[0]
    tm = TOKEN_TILE
    npair = tm * TOP_K
    row = lambda width: pl.BlockSpec((tm, width), lambda i: (i, 0))
    return pl.pallas_call(
        _out_kernel,
        grid=(n // tm,),
        in_specs=[pl.BlockSpec((None, 1, npair), lambda i: (i, 0, 0), memory_space=pltpu.SMEM),
                  pl.BlockSpec(memory_space=pl.ANY),
                  row(D_MODEL), row(LANES), row(PLE_DIM),
                  _full((1, D_MODEL)), _full((D_MODEL, D_MODEL)), _full((1, D_MODEL)),
                  _full((PLE_DIM, D_MODEL)), _full((1, D_MODEL))],
        out_specs=row(D_MODEL),
        out_shape=jax.ShapeDtypeStruct((n, D_MODEL), F32),
        scratch_shapes=[pltpu.VMEM((npair * ROW_TILES, LANES), F32), pltpu.SemaphoreType.DMA(())],
        compiler_params=pltpu.CompilerParams(dimension_semantics=("arbitrary",),
                                             vmem_limit_bytes=VMEM_LIMIT),
        name="combine_out",
    )(dest.reshape(n // tm, 1, npair), y_rows, x1, egate, pe, w["ln_ple"], w["w_ple_gate"],
      w["b_ple_gate"], w["w_ple_proj"], w["ln_final"])


def kernel(x_prompt, x_sample, cache_attn_k, cache_attn_v, state_ssm_re, state_ssm_im, p_prompt, p_sample, ln_mix, w_in, ssm_a_re, ssm_a_im, ssm_b_re, ssm_b_im, ssm_c_re, ssm_c_im, ssm_d, ssm_log_dt, w_glu, b_glu, ln_ssm_out, ln_attn_out, w_out, ln_moe, w_router, b_router, w_up, b_up, w_down, b_down, ln_ple, w_ple_gate, b_ple_gate, w_ple_proj, ln_final):
    bsz, s_len, _ = x_prompt.shape
    dbsz, dt_len, _ = x_sample.shape
    n_p, n_s = bsz * s_len, dbsz * dt_len
    wb = cache_attn_k.shape[2]
    wb_prompt = min(WINDOWS[-1], s_len)

    wr = jnp.pad(w_router[0], ((0, 0), (0, LANES - N_EXPERTS)))
    wr_hi = wr.astype(BF16)
    ti = jnp.arange(TOKEN_TILE)
    w = {
        "w_glu": w_glu[0].astype(BF16), "b_glu": b_glu[0].reshape(1, -1),
        "ln_ssm_out": ln_ssm_out[0].reshape(1, -1), "ln_attn_out": ln_attn_out[0].reshape(1, -1),
        "w_out": w_out[0].astype(BF16), "ln_moe": ln_moe[0].reshape(1, -1),
        "wr_hi": wr_hi, "wr_lo": (wr - wr_hi.astype(F32)).astype(BF16),
        "b_router": jnp.pad(b_router[0], (0, LANES - N_EXPERTS)).reshape(1, -1),
        "expand": (jnp.arange(LANES)[:, None] == jnp.arange(ATTN_WIDTH)[None, :] // HEAD_DIM).astype(BF16),
        "tri": (ti[:, None] > ti[None, :]).astype(BF16),
        "ln_ple": ln_ple[0].reshape(1, -1), "w_ple_gate": w_ple_gate[0].astype(BF16),
        "b_ple_gate": b_ple_gate[0].reshape(1, -1), "w_ple_proj": w_ple_proj[0].astype(BF16),
        "ln_final": ln_final.reshape(1, -1),
    }
    w_in_b = w_in[0].astype(BF16)
    bmat, cmat, ab_re, ab_im = _s5_params(ssm_a_re[0], ssm_a_im[0], ssm_b_re[0], ssm_b_im[0],
                                          ssm_c_re[0], ssm_c_im[0], ssm_log_dt[0])

    def coeff(a, nb):
        return jnp.broadcast_to(a, (2, nb, STATE_HALF)).reshape(2 * nb, STATE_HALF)

    u_p, k_p, v_p, qb_p, kb_p, vb_p = _in_proj(x_prompt.reshape(n_p, D_MODEL), ln_mix[0], w_in_b)
    zeros_state = jnp.zeros((bsz, SSM_GROUPS, SSM_STATE), F32)
    y_p, ht_p = _s5(u_p.reshape(bsz, s_len, SSM_WIDTH), _state_to_rows(zeros_state, zeros_state),
                    bmat, cmat, coeff(ab_re, bsz), coeff(ab_im, bsz), ssm_d[0])
    branches = [_attn_prompt_branch(qb_p, kb_p, vb_p, bsz, s_len, d) for d in DILATIONS]
    x1_p, hrow_p, eidx_p, epos_p, egate_p, cnt_p = _mid(
        x_prompt.reshape(n_p, D_MODEL), y_p.reshape(n_p, SSM_WIDTH),
        [b[0] for b in branches], [b[1] for b in branches], w)

    u_s, k_s, v_s, qb_s, kb_s, vb_s = _in_proj(x_sample.reshape(n_s, D_MODEL), ln_mix[0], w_in_b)
    y_s, ht_s = _s5(u_s.reshape(dbsz, dt_len, SSM_WIDTH), _state_to_rows(state_ssm_re[0], state_ssm_im[0]),
                    bmat, cmat, coeff(ab_re, dbsz), coeff(ab_im, dbsz), ssm_d[0])
    as3 = lambda t: t.reshape(dbsz, dt_len, ATTN_WIDTH)
    attn_s = _attn_sample(as3(qb_s), as3(kb_s), as3(vb_s),
                          cache_attn_k[0].reshape(dbsz, wb, ATTN_WIDTH),
                          cache_attn_v[0].reshape(dbsz, wb, ATTN_WIDTH))
    x1_s, hrow_s, eidx_s, epos_s, egate_s, cnt_s = _mid(
        x_sample.reshape(n_s, D_MODEL), y_s.reshape(n_s, SSM_WIDTH),
        [attn_s.reshape(n_s, ATTN_WIDTH)], [], w)

    n_rows = (n_p + n_s) * TOP_K
    dest_p, dest_s, vt, ve, vok, gs = _routing(eidx_p, epos_p, cnt_p, eidx_s, epos_s, cnt_s, n_rows)
    xs = _dispatch(jnp.concatenate([dest_p.reshape(-1), dest_s.reshape(-1)]), hrow_p, hrow_s)
    y_rows = _experts(xs, vt, ve, vok, gs, w_up[0], b_up[0], w_down[0], b_down[0])

    out_p = _combine(dest_p, y_rows, x1_p, egate_p, p_prompt[0].reshape(n_p, PLE_DIM), w)
    out_s = _combine(dest_s, y_rows, x1_s, egate_s, p_sample[0].reshape(n_s, PLE_DIM), w)

    hr_p, hi_p = _rows_to_state(ht_p, bsz)
    hr_s, hi_s = _rows_to_state(ht_s, dbsz)
    kv_p = lambda t: t.reshape(bsz, s_len, N_HEADS, HEAD_DIM)[:, -wb_prompt:][None]
    kv_s = lambda t: t.reshape(dbsz, dt_len, N_HEADS, HEAD_DIM)[None]
    return (out_p.reshape(bsz, s_len, D_MODEL), out_s.reshape(dbsz, dt_len, D_MODEL),
            kv_p(k_p), kv_p(v_p), hr_p[None], hi_p[None],
            kv_s(k_s), kv_s(v_s), hr_s[None], hi_s[None])
```

```python
import functools
import math

import jax
import jax.numpy as jnp
from jax import lax
from jax.experimental import pallas as pl
from jax.experimental.pallas import tpu as pltpu

F32 = jnp.float32
BF16 = jnp.bfloat16
I32 = jnp.int32

D_MODEL = 1024
SSM_WIDTH = 512
SSM_GROUP = 16
SSM_GROUPS = 32
SSM_STATE = 64
ATTN_WIDTH = 512
HEAD_DIM = 64
N_HEADS = 8
IN_WIDTH = SSM_WIDTH + 3 * ATTN_WIDTH
DILATIONS = (1, 4, 16)
WINDOWS = (128, 512, 2048)
ATTN_BLOCK = 128
N_EXPERTS = 32
TOP_K = 4
EXPERT_FF = D_MODEL
SWIGLU_LIMIT = 7.0
SWIGLU_ALPHA = 1.702
PLE_DIM = 256
EPS = 1e-6
MASK_VALUE = -1e30
NEG_BIG = -3.0e38

LANES = 128
SUBLANES = 8
ROW_TILES = D_MODEL // LANES
TOKEN_TILE = 256
MOE_TILE = 256
SSM_HALF = SSM_WIDTH // 2
STATE_HALF = SSM_GROUPS * SSM_STATE // 2
ALIBI_SLOPES = tuple(2.0 ** (-8.0 * (h + 1) / N_HEADS) for h in range(N_HEADS))
VMEM_LIMIT = 56 * 1024 * 1024


def _rms(x, g):
    return x * lax.rsqrt(jnp.mean(x * x, axis=-1, keepdims=True) + EPS) * g


def _sigmoid(x):
    return 1.0 / (1.0 + jnp.exp(-x))


def _full(shape):
    n = len(shape)
    return pl.BlockSpec(shape, lambda *_: (0,) * n)


def _in_kernel(x_ref, g_ref, w_ref, u_ref, k_ref, v_ref, qb_ref, kb_ref, vb_ref):
    h = _rms(x_ref[...], g_ref[...]).astype(BF16)
    p = jnp.dot(h, w_ref[...], preferred_element_type=F32)
    u_ref[...] = p[:, :SSM_WIDTH]
    k = p[:, SSM_WIDTH + ATTN_WIDTH:SSM_WIDTH + 2 * ATTN_WIDTH]
    v = p[:, SSM_WIDTH + 2 * ATTN_WIDTH:]
    k_ref[...] = k
    v_ref[...] = v
    qb_ref[...] = (p[:, SSM_WIDTH:SSM_WIDTH + ATTN_WIDTH] * (HEAD_DIM ** -0.5)).astype(BF16)
    kb_ref[...] = k.astype(BF16)
    vb_ref[...] = v.astype(BF16)


def _in_proj(x2d, ln_mix, w_in_bf16):
    n = x2d.shape[0]
    tm = min(n, 512)
    row = lambda w: pl.BlockSpec((tm, w), lambda i: (i, 0))
    return pl.pallas_call(
        _in_kernel,
        grid=(n // tm,),
        in_specs=[row(D_MODEL), _full((1, D_MODEL)), _full((D_MODEL, IN_WIDTH))],
        out_specs=[row(SSM_WIDTH)] * 6,
        out_shape=[jax.ShapeDtypeStruct((n, SSM_WIDTH), F32)] * 3
        + [jax.ShapeDtypeStruct((n, ATTN_WIDTH), BF16)] * 3,
        compiler_params=pltpu.CompilerParams(dimension_semantics=("arbitrary",),
                                             vmem_limit_bytes=VMEM_LIMIT),
        name="in_proj",
    )(x2d, ln_mix.reshape(1, D_MODEL), w_in_bf16)


def _s5_kernel(u_ref, bmat_ref, cmat_ref, are_ref, aim_ref, h0_ref, d_ref,
               y_ref, ht_ref, buf, hc, tmp, *, nb, tt, batched):
    rows = 2 * nb
    ntile = 2 * STATE_HALF // LANES
    half_tiles = ntile // 2

    def lane_tile(c):
        return slice(c * LANES, (c + 1) * LANES)

    @pl.when(pl.program_id(0) == 0)
    def _():
        hc[...] = h0_ref[...]

    if batched:
        u_all = u_ref[...].reshape(nb * tt, SSM_WIDTH)
        ub_all = u_all.astype(BF16)
        for hf in range(2):
            bu = jnp.dot(ub_all[:, hf * SSM_HALF:(hf + 1) * SSM_HALF], bmat_ref[hf],
                         preferred_element_type=F32)
            for c in range(ntile):
                tmp[c] = bu[:, lane_tile(c)]
            for c in range(ntile):
                for t in range(tt):
                    buf[c, t * rows + hf * nb:t * rows + (hf + 1) * nb, :] = tmp[c, pl.ds(t, nb, stride=tt), :]
    else:
        for b in range(nb):
            ub = u_ref[b].astype(BF16)
            for hf in range(2):
                bu = jnp.dot(ub[:, hf * SSM_HALF:(hf + 1) * SSM_HALF], bmat_ref[hf],
                             preferred_element_type=F32)
                for c in range(ntile):
                    buf[c, pl.ds(hf * nb + b, tt, stride=rows), :] = bu[:, lane_tile(c)]

    group = 4
    for s in range(rows // SUBLANES):
        r0 = s * SUBLANES
        for c0 in range(0, half_tiles, group):
            ar = [are_ref[r0:r0 + SUBLANES, lane_tile(c0 + k)] for k in range(group)]
            ai = [aim_ref[r0:r0 + SUBLANES, lane_tile(c0 + k)] for k in range(group)]
            init = tuple(hc[r0:r0 + SUBLANES, lane_tile(c0 + k)] for k in range(group)) + tuple(
                hc[r0:r0 + SUBLANES, lane_tile(half_tiles + c0 + k)] for k in range(group))

            def step(t, carry, r0=r0, c0=c0, ar=ar, ai=ai):
                row = pl.multiple_of(t * rows + r0, SUBLANES)
                out_r, out_i = [], []
                for k in range(group):
                    hr, hi = carry[k], carry[group + k]
                    xr = buf[c0 + k, pl.ds(row, SUBLANES), :]
                    xi = buf[half_tiles + c0 + k, pl.ds(row, SUBLANES), :]
                    nr = ar[k] * hr - ai[k] * hi + xr
                    ni = ar[k] * hi + ai[k] * hr + xi
                    buf[c0 + k, pl.ds(row, SUBLANES), :] = nr
                    buf[half_tiles + c0 + k, pl.ds(row, SUBLANES), :] = ni
                    out_r.append(nr)
                    out_i.append(ni)
                return tuple(out_r) + tuple(out_i)

            fin = lax.fori_loop(0, tt, step, init, unroll=min(tt, 8))
            for k in range(group):
                hc[r0:r0 + SUBLANES, lane_tile(c0 + k)] = fin[k]
                hc[r0:r0 + SUBLANES, lane_tile(half_tiles + c0 + k)] = fin[group + k]

    if batched:
        parts = []
        for hf in range(2):
            for c in range(ntile):
                for t in range(tt):
                    tmp[c, pl.ds(t, nb, stride=tt), :] = buf[c, t * rows + hf * nb:t * rows + (hf + 1) * nb, :]
            hs = jnp.concatenate([tmp[c] for c in range(ntile)], axis=1).astype(BF16)
            parts.append(jnp.dot(hs, cmat_ref[hf], preferred_element_type=F32))
        y_all = jnp.concatenate(parts, axis=1) + d_ref[...] * u_all
        y_ref[...] = y_all.reshape(nb, tt, SSM_WIDTH)
    else:
        for b in range(nb):
            parts = []
            for hf in range(2):
                hs = jnp.concatenate(
                    [buf[c, pl.ds(hf * nb + b, tt, stride=rows), :] for c in range(ntile)],
                    axis=1).astype(BF16)
                parts.append(jnp.dot(hs, cmat_ref[hf], preferred_element_type=F32))
            y_ref[b] = jnp.concatenate(parts, axis=1) + d_ref[...] * u_ref[b]

    ht_ref[...] = hc[...]


def _s5(u3, h0, bmat, cmat, a_re, a_im, d_skip):
    nb, t_len, _ = u3.shape
    tt = min(t_len, 256)
    rows = 2 * nb
    batched = tt < 16
    kern = functools.partial(_s5_kernel, nb=nb, tt=tt, batched=batched)
    ntile = 2 * STATE_HALF // LANES
    tmp_shape = (ntile, nb * tt, LANES) if batched else (1, SUBLANES, LANES)
    return pl.pallas_call(
        kern,
        grid=(t_len // tt,),
        in_specs=[pl.BlockSpec((nb, tt, SSM_WIDTH), lambda i: (0, i, 0)),
                  _full((2, SSM_HALF, 2 * STATE_HALF)),
                  _full((2, 2 * STATE_HALF, SSM_HALF)),
                  _full((rows, STATE_HALF)), _full((rows, STATE_HALF)),
                  _full((rows, 2 * STATE_HALF)), _full((1, SSM_WIDTH))],
        out_specs=[pl.BlockSpec((nb, tt, SSM_WIDTH), lambda i: (0, i, 0)),
                   _full((rows, 2 * STATE_HALF))],
        out_shape=[jax.ShapeDtypeStruct((nb, t_len, SSM_WIDTH), F32),
                   jax.ShapeDtypeStruct((rows, 2 * STATE_HALF), F32)],
        scratch_shapes=[pltpu.VMEM((ntile, tt * rows, LANES), F32),
                        pltpu.VMEM((rows, 2 * STATE_HALF), F32),
                        pltpu.VMEM(tmp_shape, F32)],
        compiler_params=pltpu.CompilerParams(dimension_semantics=("arbitrary",),
                                             vmem_limit_bytes=VMEM_LIMIT),
        name="s5_scan",
    )(u3, bmat, cmat, a_re, a_im, h0, d_skip.reshape(1, SSM_WIDTH))


def _s5_params(a_re, a_im, b_re, b_im, c_re, c_im, log_dt):
    dt = jnp.exp(log_dt)[:, None]
    mag = jnp.exp(dt * a_re)
    ang = dt * a_im
    ab_re, ab_im = mag * jnp.cos(ang), mag * jnp.sin(ang)
    den = a_re * a_re + a_im * a_im
    nr, ni = ab_re - 1.0, ab_im
    f_re = (nr * a_re + ni * a_im) / den
    f_im = (ni * a_re - nr * a_im) / den
    bb_re = f_re[..., None] * b_re - f_im[..., None] * b_im
    bb_im = f_re[..., None] * b_im + f_im[..., None] * b_re
    gh = SSM_GROUPS // 2
    eye = jnp.eye(gh, dtype=F32)

    def b_half(w):
        return jnp.einsum('gnc,gh->gchn', w, eye).reshape(gh * SSM_GROUP, gh * SSM_STATE)

    def c_half(w):
        return jnp.einsum('gcn,gh->gnhc', w, eye).reshape(gh * SSM_STATE, gh * SSM_GROUP)

    bmat = jnp.stack([jnp.concatenate([b_half(bb_re[h * gh:(h + 1) * gh]),
                                       b_half(bb_im[h * gh:(h + 1) * gh])], axis=1)
                      for h in range(2)]).astype(BF16)
    cmat = jnp.stack([jnp.concatenate([c_half(c_re[h * gh:(h + 1) * gh]),
                                       -c_half(c_im[h * gh:(h + 1) * gh])], axis=0)
                      for h in range(2)]).astype(BF16)
    return bmat, cmat, ab_re.reshape(2, 1, STATE_HALF), ab_im.reshape(2, 1, STATE_HALF)


def _state_to_rows(h_re, h_im):
    nb = h_re.shape[0]
    f = lambda h: h.reshape(nb, 2, STATE_HALF).transpose(1, 0, 2).reshape(2 * nb, STATE_HALF)
    return jnp.concatenate([f(h_re), f(h_im)], axis=1)


def _rows_to_state(ht, nb):
    f = lambda h: h.reshape(2, nb, STATE_HALF).transpose(1, 0, 2).reshape(nb, SSM_GROUPS, SSM_STATE)
    return f(ht[:, :STATE_HALF]), f(ht[:, STATE_HALF:])


def _attn_prompt_kernel(q_ref, kp_ref, kc_ref, vp_ref, vc_ref, o_ref, l_ref, *, dil):
    blk = ATTN_BLOCK
    n = pl.program_id(2)
    i_idx = lax.broadcasted_iota(I32, (blk, 2 * blk), 0)
    j_idx = lax.broadcasted_iota(I32, (blk, 2 * blk), 1)
    delta = i_idx - j_idx + blk
    valid = (delta >= 0) & (delta <= blk) & ((j_idx >= blk) | (n > 0))
    dist = (delta * dil).astype(F32)
    lane = lax.broadcasted_iota(I32, (blk, LANES), 1)
    lse_all = jnp.zeros((blk, LANES), F32)
    for hp in range(N_HEADS // 2):
        cols = slice(hp * LANES, (hp + 1) * LANES)
        q2 = q_ref[:, cols]
        kk = jnp.concatenate([kp_ref[:, cols], kc_ref[:, cols]], axis=0)
        vv = jnp.concatenate([vp_ref[:, cols], vc_ref[:, cols]], axis=0)
        outs = []
        for half in range(2):
            h = 2 * hp + half
            in_head = (lane >= half * HEAD_DIM) & (lane < (half + 1) * HEAD_DIM)
            qm = jnp.where(in_head, q2, jnp.zeros_like(q2))
            s = lax.dot_general(qm, kk, (((1,), (1,)), ((), ())), preferred_element_type=F32)
            s = jnp.where(valid, s - ALIBI_SLOPES[h] * dist, MASK_VALUE)
            m = jnp.max(s, axis=1, keepdims=True)
            p = jnp.exp(s - m)
            l = jnp.sum(p, axis=1, keepdims=True)
            outs.append(jnp.dot(p.astype(BF16), vv, preferred_element_type=F32) / l)
            lse_all = jnp.where(lane == h, m + jnp.log(l), lse_all)
        o_ref[:, cols] = jnp.where(lane < HEAD_DIM, outs[0], outs[1])
    l_ref[...] = lse_all


def _attn_prompt_branch(qb, kb, vb, bsz, s_len, dil):
    sub = s_len // dil
    nblk = sub // ATTN_BLOCK
    view = lambda t: t.reshape(bsz, sub, dil * ATTN_WIDTH)
    cur = pl.BlockSpec((None, ATTN_BLOCK, ATTN_WIDTH), lambda b, r, n: (b, n, r))
    prev = pl.BlockSpec((None, ATTN_BLOCK, ATTN_WIDTH), lambda b, r, n: (b, jnp.maximum(n - 1, 0), r))
    o, lse = pl.pallas_call(
        functools.partial(_attn_prompt_kernel, dil=dil),
        grid=(bsz, dil, nblk),
        in_specs=[cur, prev, cur, prev, cur],
        out_specs=[cur, pl.BlockSpec((None, ATTN_BLOCK, LANES), lambda b, r, n: (b, n, r))],
        out_shape=[jax.ShapeDtypeStruct((bsz, sub, dil * ATTN_WIDTH), F32),
                   jax.ShapeDtypeStruct((bsz, sub, dil * LANES), F32)],
        compiler_params=pltpu.CompilerParams(
            dimension_semantics=("arbitrary", "arbitrary", "arbitrary"),
            vmem_limit_bytes=VMEM_LIMIT),
        name=f"attn_prompt_d{dil}",
    )(view(qb), view(kb), view(kb), view(vb), view(vb))
    return o.reshape(bsz * s_len, ATTN_WIDTH), lse.reshape(bsz * s_len, LANES)


def _attn_sample_kernel(q_ref, kn_ref, vn_ref, kc_ref, vc_ref, o_ref, *, t_len, wb):
    nrow = N_HEADS * t_len
    t_shift = t_len.bit_length() - 1
    d_shift = HEAD_DIM.bit_length() - 1
    q = q_ref[...].astype(F32)
    qt = jnp.concatenate([q] * N_HEADS, axis=0)
    row_w = lax.broadcasted_iota(I32, (nrow, ATTN_WIDTH), 0)
    lane_w = lax.broadcasted_iota(I32, (nrow, ATTN_WIDTH), 1)
    qm = jnp.where((lane_w >> d_shift) == (row_w >> t_shift), qt, 0.0).astype(BF16)
    nt = (((1,), (1,)), ((), ()))
    head_col = lax.broadcasted_iota(I32, (nrow, 1), 0) >> t_shift
    slope = jnp.zeros((nrow, 1), F32)
    for h in range(N_HEADS):
        slope = jnp.where(head_col == h, ALIBI_SLOPES[h], slope)

    def scores(keys, dist):
        s = lax.dot_general(qm, keys, nt, preferred_element_type=F32)
        mult = jnp.zeros(dist.shape, F32)
        for win, dil in zip(WINDOWS, DILATIONS):
            hit = (dist >= 0) & (dist <= win) & ((dist & (dil - 1)) == 0)
            mult = mult + jnp.where(hit, 1.0, 0.0)
        s = jnp.where(mult > 0.0, s - slope * dist.astype(F32), MASK_VALUE)
        return s, mult

    tq_c = lax.broadcasted_iota(I32, (nrow, wb), 0) & (t_len - 1)
    dist_c = wb + tq_c - lax.broadcasted_iota(I32, (nrow, wb), 1)
    s_c, mult_c = scores(kc_ref[...].astype(BF16), dist_c)

    pad = LANES - t_len
    kn = jnp.concatenate([kn_ref[...].astype(F32), jnp.zeros((pad, ATTN_WIDTH), F32)], axis=0).astype(BF16)
    vn = jnp.concatenate([vn_ref[...].astype(F32), jnp.zeros((pad, ATTN_WIDTH), F32)], axis=0).astype(BF16)
    tq_n = lax.broadcasted_iota(I32, (nrow, LANES), 0) & (t_len - 1)
    tk_n = lax.broadcasted_iota(I32, (nrow, LANES), 1)
    dist_n = jnp.where(tk_n < t_len, tq_n - tk_n, -1)
    s_n, mult_n = scores(kn, dist_n)

    m = jnp.maximum(jnp.max(s_c, axis=1, keepdims=True), jnp.max(s_n, axis=1, keepdims=True))
    p_c = jnp.exp(s_c - m) * mult_c
    p_n = jnp.exp(s_n - m) * mult_n
    l = jnp.sum(p_c, axis=1, keepdims=True) + jnp.sum(p_n, axis=1, keepdims=True)
    o = (jnp.dot(p_c.astype(BF16), vc_ref[...].astype(BF16), preferred_element_type=F32)
         + jnp.dot(p_n.astype(BF16), vn, preferred_element_type=F32)) / l
    lane_o = lax.broadcasted_iota(I32, (t_len, ATTN_WIDTH), 1) >> d_shift
    out = jnp.zeros((t_len, ATTN_WIDTH), F32)
    for h in range(N_HEADS):
        out = jnp.where(lane_o == h, o[h * t_len:(h + 1) * t_len], out)
    o_ref[...] = out


def _attn_sample(qb, kb, vb, cache_k, cache_v):
    bsz, t_len, _ = qb.shape
    wb = cache_k.shape[1]
    new = pl.BlockSpec((None, t_len, ATTN_WIDTH), lambda b: (b, 0, 0))
    old = pl.BlockSpec((None, wb, ATTN_WIDTH), lambda b: (b, 0, 0))
    return pl.pallas_call(
        functools.partial(_attn_sample_kernel, t_len=t_len, wb=wb),
        grid=(bsz,),
        in_specs=[new, new, new, old, old],
        out_specs=new,
        out_shape=jax.ShapeDtypeStruct((bsz, t_len, ATTN_WIDTH), F32),
        compiler_params=pltpu.CompilerParams(dimension_semantics=("arbitrary",),
                                             vmem_limit_bytes=VMEM_LIMIT),
        name="attn_sample",
    )(qb, kb, vb, cache_k, cache_v)


def _mid_kernel(*refs, n_branch):
    x_ref, y_ref = refs[0], refs[1]
    o_refs = refs[2:2 + n_branch]
    pos = 2 + n_branch
    l_refs = refs[pos:pos + (n_branch if n_branch > 1 else 0)]
    pos += len(l_refs)
    (wglu_ref, bglu_ref, lns_ref, lna_ref, wout_ref, lnm_ref, wrh_ref, wrl_ref, br_ref,
     expand_ref, tri_ref,
     x1_ref, hrow_ref, eidx_ref, epos_ref, egate_ref, cnt_ref, carry) = refs[pos:]
    tm = x_ref.shape[0]

    @pl.when(pl.program_id(0) == 0)
    def _():
        carry[...] = jnp.zeros_like(carry)

    y = y_ref[...]
    z = y * (0.5 * (1.0 + jnp.tanh(math.sqrt(2.0 / math.pi) * (y + 0.044715 * (y * y * y)))))
    glu = z * _sigmoid(jnp.dot(z.astype(BF16), wglu_ref[...], preferred_element_type=F32) + bglu_ref[...])
    n_ssm = _rms(glu, lns_ref[...])

    if n_branch == 1:
        attn = o_refs[0][...]
    else:
        lses = [r[...] for r in l_refs]
        mx = functools.reduce(jnp.maximum, lses)
        es = [jnp.exp(l - mx) for l in lses]
        inv = 1.0 / functools.reduce(lambda a, b: a + b, es)
        attn = jnp.zeros((tm, ATTN_WIDTH), F32)
        for e, o_ref in zip(es, o_refs):
            w = e * inv
            w_hi = w.astype(BF16)
            w_lo = (w - w_hi.astype(F32)).astype(BF16)
            wide = (jnp.dot(w_hi, expand_ref[...], preferred_element_type=F32)
                    + jnp.dot(w_lo, expand_ref[...], preferred_element_type=F32))
            attn = attn + wide * o_ref[...]
    n_attn = _rms(attn, lna_ref[...])

    x1 = (x_ref[...]
          + jnp.dot(n_ssm.astype(BF16), wout_ref[:SSM_WIDTH, :], preferred_element_type=F32)
          + jnp.dot(n_attn.astype(BF16), wout_ref[SSM_WIDTH:, :], preferred_element_type=F32))
    x1_ref[...] = x1
    hm = _rms(x1, lnm_ref[...])
    for s in range(ROW_TILES):
        hrow_ref[pl.ds(s, tm, stride=ROW_TILES), :] = hm[:, s * LANES:(s + 1) * LANES]

    h_hi = hm.astype(BF16)
    h_lo = (hm - h_hi.astype(F32)).astype(BF16)
    logits = (jnp.dot(h_hi, wrh_ref[...], preferred_element_type=F32)
              + jnp.dot(h_lo, wrh_ref[...], preferred_element_type=F32)
              + jnp.dot(h_hi, wrl_ref[...], preferred_element_type=F32)
              + br_ref[...])
    lane = lax.broadcasted_iota(I32, (tm, LANES), 1)
    lane_f = lane.astype(F32)
    work = jnp.where(lane < N_EXPERTS, logits, NEG_BIG)
    vals, idxs, hots = [], [], []
    for _ in range(TOP_K):
        m = jnp.max(work, axis=1, keepdims=True)
        idx = jnp.min(jnp.where(work == m, lane_f, float(LANES)), axis=1, keepdims=True)
        hot = lane_f == idx
        vals.append(m)
        idxs.append(idx)
        hots.append(hot)
        work = jnp.where(hot, NEG_BIG, work)
    exps = [jnp.exp(v - vals[0]) for v in vals]
    inv = 1.0 / functools.reduce(lambda a, b: a + b, exps)

    sel = functools.reduce(lambda a, b: a + b, [h.astype(F32) for h in hots])
    before = jnp.dot(tri_ref[...], sel.astype(BF16), preferred_element_type=F32) + carry[0:1, :]
    eidx = jnp.zeros((tm, LANES), I32)
    epos = jnp.zeros((tm, LANES), I32)
    egate = jnp.zeros((tm, LANES), F32)
    for k in range(TOP_K):
        pk = jnp.sum(jnp.where(hots[k], before, 0.0), axis=1, keepdims=True)
        eidx = jnp.where(lane == k, idxs[k].astype(I32), eidx)
        epos = jnp.where(lane == k, pk.astype(I32), epos)
        egate = jnp.where(lane == k, exps[k] * inv, egate)
    eidx_ref[...] = eidx
    epos_ref[...] = epos
    egate_ref[...] = egate
    total = carry[0:1, :] + jnp.sum(sel, axis=0, keepdims=True)
    carry[...] = jnp.broadcast_to(total, carry.shape)
    cnt_ref[...] = jnp.broadcast_to(total, cnt_ref.shape).astype(I32)


def _mid(x2d, y2d, attn_o, attn_lse, w):
    n = x2d.shape[0]
    tm = TOKEN_TILE
    n_branch = len(attn_o)
    row = lambda width: pl.BlockSpec((tm, width), lambda i: (i, 0))
    in_specs = ([row(D_MODEL), row(SSM_WIDTH)] + [row(ATTN_WIDTH)] * n_branch
                + [row(LANES)] * len(attn_lse)
                + [_full((SSM_WIDTH, SSM_WIDTH)), _full((1, SSM_WIDTH)), _full((1, SSM_WIDTH)),
                   _full((1, ATTN_WIDTH)), _full((D_MODEL, D_MODEL)), _full((1, D_MODEL)),
                   _full((D_MODEL, LANES)), _full((D_MODEL, LANES)), _full((1, LANES)),
                   _full((LANES, ATTN_WIDTH)), _full((tm, tm))])
    out_specs = [row(D_MODEL), pl.BlockSpec((tm * ROW_TILES, LANES), lambda i: (i, 0)),
                 row(LANES), row(LANES), row(LANES), _full((SUBLANES, LANES))]
    out_shape = [jax.ShapeDtypeStruct((n, D_MODEL), F32),
                 jax.ShapeDtypeStruct((n * ROW_TILES, LANES), F32),
                 jax.ShapeDtypeStruct((n, LANES), I32),
                 jax.ShapeDtypeStruct((n, LANES), I32),
                 jax.ShapeDtypeStruct((n, LANES), F32),
                 jax.ShapeDtypeStruct((SUBLANES, LANES), I32)]
    return pl.pallas_call(
        functools.partial(_mid_kernel, n_branch=n_branch),
        grid=(n // tm,),
        in_specs=in_specs,
        out_specs=out_specs,
        out_shape=out_shape,
        scratch_shapes=[pltpu.VMEM((SUBLANES, LANES), F32)],
        compiler_params=pltpu.CompilerParams(dimension_semantics=("arbitrary",),
                                             vmem_limit_bytes=VMEM_LIMIT),
        name="mid",
    )(x2d, y2d, *attn_o, *attn_lse, w["w_glu"], w["b_glu"], w["ln_ssm_out"], w["ln_attn_out"],
      w["w_out"], w["ln_moe"], w["wr_hi"], w["wr_lo"], w["b_router"], w["expand"], w["tri"])


def _dispatch_kernel(dest_ref, hp_ref, hs_ref, xs_hbm, sem, *, n_prompt_tiles):
    i = pl.program_id(0)
    npair = TOKEN_TILE * TOP_K

    def run(src_ref):
        def issue(j, c):
            tok = j // TOP_K
            d = dest_ref[0, j]
            pltpu.make_async_copy(
                src_ref.at[pl.ds(pl.multiple_of(tok * ROW_TILES, ROW_TILES), ROW_TILES), :],
                xs_hbm.at[pl.ds(pl.multiple_of(d * ROW_TILES, ROW_TILES), ROW_TILES), :], sem).start()
            return c

        lax.fori_loop(0, npair, issue, 0, unroll=8)
        span = pl.ds(0, npair * ROW_TILES)
        pltpu.make_async_copy(xs_hbm.at[span, :], xs_hbm.at[span, :], sem).wait()

    @pl.when(i < n_prompt_tiles)
    def _():
        run(hp_ref)

    @pl.when(i >= n_prompt_tiles)
    def _():
        run(hs_ref)


def _dispatch(dest, hrow_p, hrow_s):
    n_p = hrow_p.shape[0] // ROW_TILES
    n_s = hrow_s.shape[0] // ROW_TILES
    npt = n_p // TOKEN_TILE
    ntile = (n_p + n_s) // TOKEN_TILE
    npair = TOKEN_TILE * TOP_K
    blk = (TOKEN_TILE * ROW_TILES, LANES)
    return pl.pallas_call(
        functools.partial(_dispatch_kernel, n_prompt_tiles=npt),
        grid=(ntile,),
        in_specs=[pl.BlockSpec((None, 1, npair), lambda i: (i, 0, 0), memory_space=pltpu.SMEM),
                  pl.BlockSpec(blk, lambda i: (jnp.minimum(i, npt - 1), 0)),
                  pl.BlockSpec(blk, lambda i: (jnp.maximum(i - npt, 0), 0))],
        out_specs=pl.BlockSpec(memory_space=pl.ANY),
        out_shape=jax.ShapeDtypeStruct(((n_p + n_s) * TOP_K * ROW_TILES, LANES), F32),
        scratch_shapes=[pltpu.SemaphoreType.DMA(())],
        compiler_params=pltpu.CompilerParams(dimension_semantics=("arbitrary",),
                                             vmem_limit_bytes=VMEM_LIMIT),
        name="moe_dispatch",
    )(dest.reshape(ntile, 1, npair), hrow_p, hrow_s)


def _expert_kernel(vt_ref, ve_ref, vok_ref, gs_ref, xs_ref, wu_ref, bu_ref, wd_ref, bd_ref,
                   out_ref, wu_s, wd_s, x_s):
    v = pl.program_id(0)
    e = ve_ref[v]
    j = vt_ref[v]
    vprev = jnp.maximum(v - 1, 0)
    new_e = (v == 0) | (e != ve_ref[vprev])
    new_j = (v == 0) | (j != vt_ref[vprev])
    tm = MOE_TILE

    @pl.when(new_e)
    def _():
        wu_s[...] = wu_ref[...].astype(BF16)
        wd_s[...] = wd_ref[...].astype(BF16)

    @pl.when(new_j)
    def _():
        out_ref[...] = jnp.zeros_like(out_ref)

    @pl.when(vok_ref[v] == 1)
    def _():
        for s in range(ROW_TILES):
            x_s[:, s * LANES:(s + 1) * LANES] = xs_ref[pl.ds(s, tm, stride=ROW_TILES), :].astype(BF16)
        a = jnp.dot(x_s[...], wu_s[...], preferred_element_type=F32) + bu_ref[...]
        g = jnp.minimum(a[:, :EXPERT_FF], SWIGLU_LIMIT)
        lin = jnp.clip(a[:, EXPERT_FF:], -SWIGLU_LIMIT, SWIGLU_LIMIT)
        act = (lin + 1.0) * (g * _sigmoid(SWIGLU_ALPHA * g))
        y = jnp.dot(act.astype(BF16), wd_s[...], preferred_element_type=F32) + bd_ref[...]
        rows = j * tm + lax.broadcasted_iota(I32, (tm, 1), 0)
        mine = (rows >= gs_ref[e]) & (rows < gs_ref[e + 1])
        for s in range(ROW_TILES):
            cur = out_ref[pl.ds(s, tm, stride=ROW_TILES), :]
            out_ref[pl.ds(s, tm, stride=ROW_TILES), :] = jnp.where(
                mine, y[:, s * LANES:(s + 1) * LANES], cur)


def _experts(xs, vt, ve, vok, gstart, w_up, b_up, w_down, b_down):
    tm = MOE_TILE
    nvisit = vt.shape[0]
    rows = pl.BlockSpec((tm * ROW_TILES, LANES), lambda v, vt, ve, vok, gs: (vt[v], 0))
    per_e = lambda a, b: pl.BlockSpec((None, a, b), lambda v, vt, ve, vok, gs: (ve[v], 0, 0))
    grid_spec = pltpu.PrefetchScalarGridSpec(
        num_scalar_prefetch=4,
        grid=(nvisit,),
        in_specs=[rows, per_e(D_MODEL, 2 * EXPERT_FF), per_e(1, 2 * EXPERT_FF),
                  per_e(EXPERT_FF, D_MODEL), per_e(1, D_MODEL)],
        out_specs=rows,
        scratch_shapes=[pltpu.VMEM((D_MODEL, 2 * EXPERT_FF), BF16),
                        pltpu.VMEM((EXPERT_FF, D_MODEL), BF16),
                        pltpu.VMEM((tm, D_MODEL), BF16)],
    )
    return pl.pallas_call(
        _expert_kernel,
        grid_spec=grid_spec,
        out_shape=jax.ShapeDtypeStruct(xs.shape, F32),
        compiler_params=pltpu.CompilerParams(dimension_semantics=("arbitrary",),
                                             vmem_limit_bytes=VMEM_LIMIT),
        name="moe_experts",
    )(vt, ve, vok, gstart, xs, w_up, b_up.reshape(N_EXPERTS, 1, 2 * EXPERT_FF),
      w_down, b_down.reshape(N_EXPERTS, 1, D_MODEL))


def _routing(eidx_p, epos_p, cnt_p, eidx_s, epos_s, cnt_s, n_rows):
    cnt_p = cnt_p[0, :N_EXPERTS]
    cnt_s = cnt_s[0, :N_EXPERTS]
    cnt = cnt_p + cnt_s
    gend = jnp.cumsum(cnt)
    gstart = gend - cnt
    ep = eidx_p[:, :TOP_K]
    es = eidx_s[:, :TOP_K]
    dest_p = gstart[ep] + epos_p[:, :TOP_K]
    dest_s = gstart[es] + cnt_p[es] + epos_s[:, :TOP_K]
    ntile = n_rows // MOE_TILE
    nvisit = ntile + N_EXPERTS
    first = gstart // MOE_TILE
    last = jnp.maximum(gend - 1, 0) // MOE_TILE
    nv = jnp.where(cnt > 0, last - first + 1, 0)
    vend = jnp.cumsum(nv)
    vstart = vend - nv
    total = vend[-1]
    v = jnp.arange(nvisit, dtype=I32)
    vc = jnp.minimum(v, total - 1)
    ve = jnp.sum((vend[None, :] <= vc[:, None]).astype(I32), axis=1)
    vt = (first[ve] + vc - vstart[ve]).astype(I32)
    vok = (v < total).astype(I32)
    gs = jnp.concatenate([gstart, gend[-1:]]).astype(I32)
    return dest_p.astype(I32), dest_s.astype(I32), vt, ve, vok, gs


def _out_kernel(dest_ref, y_hbm, x1_ref, gate_ref, pe_ref, lnp_ref, wg_ref, bg_ref, wp_ref, lnf_ref,
                o_ref, buf, sem):
    tm = TOKEN_TILE
    npair = tm * TOP_K

    def issue(j, c):
        d = dest_ref[0, j]
        pltpu.make_async_copy(
            y_hbm.at[pl.ds(pl.multiple_of(d * ROW_TILES, ROW_TILES), ROW_TILES), :],
            buf.at[pl.ds(pl.multiple_of(j * ROW_TILES, ROW_TILES), ROW_TILES), :], sem).start()
        return c

    lax.fori_loop(0, npair, issue, 0, unroll=8)
    pltpu.make_async_copy(y_hbm.at[pl.ds(0, npair * ROW_TILES), :], buf, sem).wait()

    gates = gate_ref[...]
    parts = []
    for s in range(ROW_TILES):
        acc = jnp.zeros((tm, LANES), F32)
        for k in range(TOP_K):
            rows = buf[pl.ds(k * ROW_TILES + s, tm, stride=TOP_K * ROW_TILES), :]
            acc = acc + gates[:, k:k + 1] * rows
        parts.append(acc)
    x2 = x1_ref[...] + jnp.concatenate(parts, axis=1)
    gate = _sigmoid(jnp.dot(_rms(x2, lnp_ref[...]).astype(BF16), wg_ref[...],
                            preferred_element_type=F32) + bg_ref[...])
    x3 = x2 + gate * jnp.dot(pe_ref[...].astype(BF16), wp_ref[...], preferred_element_type=F32)
    o_ref[...] = _rms(x3, lnf_ref[...])


def _combine(dest, y_rows, x1, egate, pe, w):
    n = x1.shape[0]
    tm = TOKEN_TILE
    npair = tm * TOP_K
    row = lambda width: pl.BlockSpec((tm, width), lambda i: (i, 0))
    return pl.pallas_call(
        _out_kernel,
        grid=(n // tm,),
        in_specs=[pl.BlockSpec((None, 1, npair), lambda i: (i, 0, 0), memory_space=pltpu.SMEM),
                  pl.BlockSpec(memory_space=pl.ANY),
                  row(D_MODEL), row(LANES), row(PLE_DIM),
                  _full((1, D_MODEL)), _full((D_MODEL, D_MODEL)), _full((1, D_MODEL)),
                  _full((PLE_DIM, D_MODEL)), _full((1, D_MODEL))],
        out_specs=row(D_MODEL),
        out_shape=jax.ShapeDtypeStruct((n, D_MODEL), F32),
        scratch_shapes=[pltpu.VMEM((npair * ROW_TILES, LANES), F32), pltpu.SemaphoreType.DMA(())],
        compiler_params=pltpu.CompilerParams(dimension_semantics=("arbitrary",),
                                             vmem_limit_bytes=VMEM_LIMIT),
        name="combine_out",
    )(dest.reshape(n // tm, 1, npair), y_rows, x1, egate, pe, w["ln_ple"], w["w_ple_gate"],
      w["b_ple_gate"], w["w_ple_proj"], w["ln_final"])


def kernel(x_prompt, x_sample, cache_attn_k, cache_attn_v, state_ssm_re, state_ssm_im, p_prompt, p_sample, ln_mix, w_in, ssm_a_re, ssm_a_im, ssm_b_re, ssm_b_im, ssm_c_re, ssm_c_im, ssm_d, ssm_log_dt, w_glu, b_glu, ln_ssm_out, ln_attn_out, w_out, ln_moe, w_router, b_router, w_up, b_up, w_down, b_down, ln_ple, w_ple_gate, b_ple_gate, w_ple_proj, ln_final):
    bsz, s_len, _ = x_prompt.shape
    dbsz, dt_len, _ = x_sample.shape
    n_p, n_s = bsz * s_len, dbsz * dt_len
    wb = cache_attn_k.shape[2]
    wb_prompt = min(WINDOWS[-1], s_len)

    wr = jnp.pad(w_router[0], ((0, 0), (0, LANES - N_EXPERTS)))
    wr_hi = wr.astype(BF16)
    ti = jnp.arange(TOKEN_TILE)
    w = {
        "w_glu": w_glu[0].astype(BF16), "b_glu": b_glu[0].reshape(1, -1),
        "ln_ssm_out": ln_ssm_out[0].reshape(1, -1), "ln_attn_out": ln_attn_out[0].reshape(1, -1),
        "w_out": w_out[0].astype(BF16), "ln_moe": ln_moe[0].reshape(1, -1),
        "wr_hi": wr_hi, "wr_lo": (wr - wr_hi.astype(F32)).astype(BF16),
        "b_router": jnp.pad(b_router[0], (0, LANES - N_EXPERTS)).reshape(1, -1),
        "expand": (jnp.arange(LANES)[:, None] == jnp.arange(ATTN_WIDTH)[None, :] // HEAD_DIM).astype(BF16),
        "tri": (ti[:, None] > ti[None, :]).astype(BF16),
        "ln_ple": ln_ple[0].reshape(1, -1), "w_ple_gate": w_ple_gate[0].astype(BF16),
        "b_ple_gate": b_ple_gate[0].reshape(1, -1), "w_ple_proj": w_ple_proj[0].astype(BF16),
        "ln_final": ln_final.reshape(1, -1),
    }
    w_in_b = w_in[0].astype(BF16)
    bmat, cmat, ab_re, ab_im = _s5_params(ssm_a_re[0], ssm_a_im[0], ssm_b_re[0], ssm_b_im[0],
                                          ssm_c_re[0], ssm_c_im[0], ssm_log_dt[0])

    def coeff(a, nb):
        return jnp.broadcast_to(a, (2, nb, STATE_HALF)).reshape(2 * nb, STATE_HALF)

    u_p, k_p, v_p, qb_p, kb_p, vb_p = _in_proj(x_prompt.reshape(n_p, D_MODEL), ln_mix[0], w_in_b)
    zeros_state = jnp.zeros((bsz, SSM_GROUPS, SSM_STATE), F32)
    y_p, ht_p = _s5(u_p.reshape(bsz, s_len, SSM_WIDTH), _state_to_rows(zeros_state, zeros_state),
                    bmat, cmat, coeff(ab_re, bsz), coeff(ab_im, bsz), ssm_d[0])
    branches = [_attn_prompt_branch(qb_p, kb_p, vb_p, bsz, s_len, d) for d in DILATIONS]
    x1_p, hrow_p, eidx_p, epos_p, egate_p, cnt_p = _mid(
        x_prompt.reshape(n_p, D_MODEL), y_p.reshape(n_p, SSM_WIDTH),
        [b[0] for b in branches], [b[1] for b in branches], w)

    u_s, k_s, v_s, qb_s, kb_s, vb_s = _in_proj(x_sample.reshape(n_s, D_MODEL), ln_mix[0], w_in_b)
    y_s, ht_s = _s5(u_s.reshape(dbsz, dt_len, SSM_WIDTH), _state_to_rows(state_ssm_re[0], state_ssm_im[0]),
                    bmat, cmat, coeff(ab_re, dbsz), coeff(ab_im, dbsz), ssm_d[0])
    as3 = lambda t: t.reshape(dbsz, dt_len, ATTN_WIDTH)
    attn_s = _attn_sample(as3(qb_s), as3(kb_s), as3(vb_s),
                          cache_attn_k[0].reshape(dbsz, wb, ATTN_WIDTH),
                          cache_attn_v[0].reshape(dbsz, wb, ATTN_WIDTH))
    x1_s, hrow_s, eidx_s, epos_s, egate_s, cnt_s = _mid(
        x_sample.reshape(n_s, D_MODEL), y_s.reshape(n_s, SSM_WIDTH),
        [attn_s.reshape(n_s, ATTN_WIDTH)], [], w)

    n_rows = (n_p + n_s) * TOP_K
    dest_p, dest_s, vt, ve, vok, gs = _routing(eidx_p, epos_p, cnt_p, eidx_s, epos_s, cnt_s, n_rows)
    xs = _dispatch(jnp.concatenate([dest_p.reshape(-1), dest_s.reshape(-1)]), hrow_p, hrow_s)
    y_rows = _experts(xs, vt, ve, vok, gs, w_up[0], b_up[0], w_down[0], b_down[0])

    out_p = _combine(dest_p, y_rows, x1_p, egate_p, p_prompt[0].reshape(n_p, PLE_DIM), w)
    out_s = _combine(dest_s, y_rows, x1_s, egate_s, p_sample[0].reshape(n_s, PLE_DIM), w)

    hr_p, hi_p = _rows_to_state(ht_p, bsz)
    hr_s, hi_s = _rows_to_state(ht_s, dbsz)
    kv_p = lambda t: t.reshape(bsz, s_len, N_HEADS, HEAD_DIM)[:, -wb_prompt:][None]
    kv_s = lambda t: t.reshape(dbsz, dt_len, N_HEADS, HEAD_DIM)[None]
    return (out_p.reshape(bsz, s_len, D_MODEL), out_s.reshape(dbsz, dt_len, D_MODEL),
            kv_p(k_p), kv_p(v_p), hr_p[None], hi_p[None],
            kv_s(k_s), kv_s(v_s), hr_s[None], hi_s[None])
```

```python
import functools
import math

import jax
import jax.numpy as jnp
from jax import lax
from jax.experimental import pallas as pl
from jax.experimental.pallas import tpu as pltpu

F32 = jnp.float32
BF16 = jnp.bfloat16
I32 = jnp.int32

D_MODEL = 1024
SSM_WIDTH = 512
SSM_GROUP = 16
SSM_GROUPS = 32
SSM_STATE = 64
ATTN_WIDTH = 512
HEAD_DIM = 64
N_HEADS = 8
IN_WIDTH = SSM_WIDTH + 3 * ATTN_WIDTH
DILATIONS = (1, 4, 16)
WINDOWS = (128, 512, 2048)
ATTN_BLOCK = 128
N_EXPERTS = 32
TOP_K = 4
EXPERT_FF = D_MODEL
SWIGLU_LIMIT = 7.0
SWIGLU_ALPHA = 1.702
PLE_DIM = 256
EPS = 1e-6
MASK_VALUE = -1e30
NEG_BIG = -3.0e38

LANES = 128
SUBLANES = 8
ROW_TILES = D_MODEL // LANES
TOKEN_TILE = 256
MOE_TILE = 256
SSM_HALF = SSM_WIDTH // 2
STATE_HALF = SSM_GROUPS * SSM_STATE // 2
ALIBI_SLOPES = tuple(2.0 ** (-8.0 * (h + 1) / N_HEADS) for h in range(N_HEADS))
VMEM_LIMIT = 56 * 1024 * 1024


def _rms(x, g):
    return x * lax.rsqrt(jnp.mean(x * x, axis=-1, keepdims=True) + EPS) * g


def _sigmoid(x):
    return 1.0 / (1.0 + jnp.exp(-x))


def _full(shape):
    n = len(shape)
    return pl.BlockSpec(shape, lambda *_: (0,) * n)


def _in_kernel(x_ref, g_ref, w_ref, u_ref, k_ref, v_ref, qb_ref, kb_ref, vb_ref, *rest, dils):
    tm = x_ref.shape[0]
    h = _rms(x_ref[...], g_ref[...]).astype(BF16)
    p = jnp.dot(h, w_ref[...], preferred_element_type=F32)
    u_ref[...] = p[:, :SSM_WIDTH]
    q = p[:, SSM_WIDTH:SSM_WIDTH + ATTN_WIDTH] * (HEAD_DIM ** -0.5)
    k = p[:, SSM_WIDTH + ATTN_WIDTH:SSM_WIDTH + 2 * ATTN_WIDTH]
    v = p[:, SSM_WIDTH + 2 * ATTN_WIDTH:]
    k_ref[...] = k
    v_ref[...] = v
    qb_ref[...] = q.astype(BF16)
    kb_ref[...] = k.astype(BF16)
    vb_ref[...] = v.astype(BF16)
    if not dils:
        return
    scr = rest[-1]
    tiles = ATTN_WIDTH // LANES
    for a, val in enumerate((q, k, v)):
        for ct in range(tiles):
            scr[a * tiles + ct] = val[:, ct * LANES:(ct + 1) * LANES]
    for di, d in enumerate(dils):
        for a in range(3):
            out = rest[di * 3 + a]
            for r in range(d):
                for ct in range(tiles):
                    piece = scr[a * tiles + ct, pl.ds(r, tm // d, stride=d), :]
                    c0 = r * ATTN_WIDTH + ct * LANES
                    out[:, c0:c0 + LANES] = piece.astype(BF16)


def _in_proj(x2d, ln_mix, w_in_bf16, dils=()):
    n = x2d.shape[0]
    tm = min(n, 512)
    row = lambda w: pl.BlockSpec((tm, w), lambda i: (i, 0))
    out_specs = [row(SSM_WIDTH)] * 6
    out_shape = ([jax.ShapeDtypeStruct((n, SSM_WIDTH), F32)] * 3
                 + [jax.ShapeDtypeStruct((n, ATTN_WIDTH), BF16)] * 3)
    for d in dils:
        out_specs += [pl.BlockSpec((tm // d, d * ATTN_WIDTH), lambda i: (i, 0))] * 3
        out_shape += [jax.ShapeDtypeStruct((n // d, d * ATTN_WIDTH), BF16)] * 3
    scratch = [pltpu.VMEM((3 * ATTN_WIDTH // LANES, tm, LANES), F32)] if dils else []
    return pl.pallas_call(
        functools.partial(_in_kernel, dils=tuple(dils)),
        grid=(n // tm,),
        in_specs=[row(D_MODEL), _full((1, D_MODEL)), _full((D_MODEL, IN_WIDTH))],
        out_specs=out_specs,
        out_shape=out_shape,
        scratch_shapes=scratch,
        compiler_params=pltpu.CompilerParams(dimension_semantics=("arbitrary",),
                                             vmem_limit_bytes=VMEM_LIMIT),
        name="in_proj",
    )(x2d, ln_mix.reshape(1, D_MODEL), w_in_bf16)


def _s5_kernel(u_ref, bmat_ref, cmat_ref, are_ref, aim_ref, h0_ref, d_ref,
               y_ref, ht_ref, buf, hc, tmp, *, nb, tt, batched):
    rows = 2 * nb
    ntile = 2 * STATE_HALF // LANES
    half_tiles = ntile // 2

    def lane_tile(c):
        return slice(c * LANES, (c + 1) * LANES)

    @pl.when(pl.program_id(0) == 0)
    def _():
        hc[...] = h0_ref[...]

    if batched:
        u_all = u_ref[...].reshape(nb * tt, SSM_WIDTH)
        ub_all = u_all.astype(BF16)
        for hf in range(2):
            bu = jnp.dot(ub_all[:, hf * SSM_HALF:(hf + 1) * SSM_HALF], bmat_ref[hf],
                         preferred_element_type=F32)
            for c in range(ntile):
                tmp[c] = bu[:, lane_tile(c)]
            for c in range(ntile):
                for t in range(tt):
                    buf[c, t * rows + hf * nb:t * rows + (hf + 1) * nb, :] = tmp[c, pl.ds(t, nb, stride=tt), :]
    else:
        for b in range(nb):
            ub = u_ref[b].astype(BF16)
            for hf in range(2):
                bu = jnp.dot(ub[:, hf * SSM_HALF:(hf + 1) * SSM_HALF], bmat_ref[hf],
                             preferred_element_type=F32)
                for c in range(ntile):
                    buf[c, pl.ds(hf * nb + b, tt, stride=rows), :] = bu[:, lane_tile(c)]

    group = 4
    for s in range(rows // SUBLANES):
        r0 = s * SUBLANES
        for c0 in range(0, half_tiles, group):
            ar = [are_ref[r0:r0 + SUBLANES, lane_tile(c0 + k)] for k in range(group)]
            ai = [aim_ref[r0:r0 + SUBLANES, lane_tile(c0 + k)] for k in range(group)]
            init = tuple(hc[r0:r0 + SUBLANES, lane_tile(c0 + k)] for k in range(group)) + tuple(
                hc[r0:r0 + SUBLANES, lane_tile(half_tiles + c0 + k)] for k in range(group))

            def step(t, carry, r0=r0, c0=c0, ar=ar, ai=ai):
                row = pl.multiple_of(t * rows + r0, SUBLANES)
                out_r, out_i = [], []
                for k in range(group):
                    hr, hi = carry[k], carry[group + k]
                    xr = buf[c0 + k, pl.ds(row, SUBLANES), :]
                    xi = buf[half_tiles + c0 + k, pl.ds(row, SUBLANES), :]
                    nr = ar[k] * hr - ai[k] * hi + xr
                    ni = ar[k] * hi + ai[k] * hr + xi
                    buf[c0 + k, pl.ds(row, SUBLANES), :] = nr
                    buf[half_tiles + c0 + k, pl.ds(row, SUBLANES), :] = ni
                    out_r.append(nr)
                    out_i.append(ni)
                return tuple(out_r) + tuple(out_i)

            fin = lax.fori_loop(0, tt, step, init, unroll=min(tt, 8))
            for k in range(group):
                hc[r0:r0 + SUBLANES, lane_tile(c0 + k)] = fin[k]
                hc[r0:r0 + SUBLANES, lane_tile(half_tiles + c0 + k)] = fin[group + k]

    if batched:
        parts = []
        for hf in range(2):
            for c in range(ntile):
                for t in range(tt):
                    tmp[c, pl.ds(t, nb, stride=tt), :] = buf[c, t * rows + hf * nb:t * rows + (hf + 1) * nb, :]
            hs = jnp.concatenate([tmp[c] for c in range(ntile)], axis=1).astype(BF16)
            parts.append(jnp.dot(hs, cmat_ref[hf], preferred_element_type=F32))
        y_all = jnp.concatenate(parts, axis=1) + d_ref[...] * u_all
        y_ref[...] = y_all.reshape(nb, tt, SSM_WIDTH)
    else:
        for b in range(nb):
            parts = []
            for hf in range(2):
                hs = jnp.concatenate(
                    [buf[c, pl.ds(hf * nb + b, tt, stride=rows), :] for c in range(ntile)],
                    axis=1).astype(BF16)
                parts.append(jnp.dot(hs, cmat_ref[hf], preferred_element_type=F32))
            y_ref[b] = jnp.concatenate(parts, axis=1) + d_ref[...] * u_ref[b]

    ht_ref[...] = hc[...]


def _s5(u3, h0, bmat, cmat, a_re, a_im, d_skip):
    nb, t_len, _ = u3.shape
    tt = min(t_len, 256)
    rows = 2 * nb
    batched = tt < 16
    kern = functools.partial(_s5_kernel, nb=nb, tt=tt, batched=batched)
    ntile = 2 * STATE_HALF // LANES
    tmp_shape = (ntile, nb * tt, LANES) if batched else (1, SUBLANES, LANES)
    return pl.pallas_call(
        kern,
        grid=(t_len // tt,),
        in_specs=[pl.BlockSpec((nb, tt, SSM_WIDTH), lambda i: (0, i, 0)),
                  _full((2, SSM_HALF, 2 * STATE_HALF)),
                  _full((2, 2 * STATE_HALF, SSM_HALF)),
                  _full((rows, STATE_HALF)), _full((rows, STATE_HALF)),
                  _full((rows, 2 * STATE_HALF)), _full((1, SSM_WIDTH))],
        out_specs=[pl.BlockSpec((nb, tt, SSM_WIDTH), lambda i: (0, i, 0)),
                   _full((rows, 2 * STATE_HALF))],
        out_shape=[jax.ShapeDtypeStruct((nb, t_len, SSM_WIDTH), F32),
                   jax.ShapeDtypeStruct((rows, 2 * STATE_HALF), F32)],
        scratch_shapes=[pltpu.VMEM((ntile, tt * rows, LANES), F32),
                        pltpu.VMEM((rows, 2 * STATE_HALF), F32),
                        pltpu.VMEM(tmp_shape, F32)],
        compiler_params=pltpu.CompilerParams(dimension_semantics=("arbitrary",),
                                             vmem_limit_bytes=VMEM_LIMIT),
        name="s5_scan",
    )(u3, bmat, cmat, a_re, a_im, h0, d_skip.reshape(1, SSM_WIDTH))


def _s5_params(a_re, a_im, b_re, b_im, c_re, c_im, log_dt):
    dt = jnp.exp(log_dt)[:, None]
    mag = jnp.exp(dt * a_re)
    ang = dt * a_im
    ab_re, ab_im = mag * jnp.cos(ang), mag * jnp.sin(ang)
    den = a_re * a_re + a_im * a_im
    nr, ni = ab_re - 1.0, ab_im
    f_re = (nr * a_re + ni * a_im) / den
    f_im = (ni * a_re - nr * a_im) / den
    bb_re = f_re[..., None] * b_re - f_im[..., None] * b_im
    bb_im = f_re[..., None] * b_im + f_im[..., None] * b_re
    gh = SSM_GROUPS // 2
    eye = jnp.eye(gh, dtype=F32)

    def b_half(w):
        return jnp.einsum('gnc,gh->gchn', w, eye).reshape(gh * SSM_GROUP, gh * SSM_STATE)

    def c_half(w):
        return jnp.einsum('gcn,gh->gnhc', w, eye).reshape(gh * SSM_STATE, gh * SSM_GROUP)

    bmat = jnp.stack([jnp.concatenate([b_half(bb_re[h * gh:(h + 1) * gh]),
                                       b_half(bb_im[h * gh:(h + 1) * gh])], axis=1)
                      for h in range(2)]).astype(BF16)
    cmat = jnp.stack([jnp.concatenate([c_half(c_re[h * gh:(h + 1) * gh]),
                                       -c_half(c_im[h * gh:(h + 1) * gh])], axis=0)
                      for h in range(2)]).astype(BF16)
    return bmat, cmat, ab_re.reshape(2, 1, STATE_HALF), ab_im.reshape(2, 1, STATE_HALF)


def _state_to_rows(h_re, h_im):
    nb = h_re.shape[0]
    f = lambda h: h.reshape(nb, 2, STATE_HALF).transpose(1, 0, 2).reshape(2 * nb, STATE_HALF)
    return jnp.concatenate([f(h_re), f(h_im)], axis=1)


def _rows_to_state(ht, nb):
    f = lambda h: h.reshape(2, nb, STATE_HALF).transpose(1, 0, 2).reshape(nb, SSM_GROUPS, SSM_STATE)
    return f(ht[:, :STATE_HALF]), f(ht[:, STATE_HALF:])


def _attn_prompt_kernel(q_ref, kp_ref, kc_ref, vp_ref, vc_ref, o_ref, l_ref, *, dil):
    blk = ATTN_BLOCK
    n = pl.program_id(2)
    i_idx = lax.broadcasted_iota(I32, (blk, 2 * blk), 0)
    j_idx = lax.broadcasted_iota(I32, (blk, 2 * blk), 1)
    delta = i_idx - j_idx + blk
    valid = (delta >= 0) & (delta <= blk) & ((j_idx >= blk) | (n > 0))
    dist = (delta * dil).astype(F32)
    lane = lax.broadcasted_iota(I32, (blk, LANES), 1)
    lse_all = jnp.zeros((blk, LANES), F32)
    for hp in range(N_HEADS // 2):
        cols = slice(hp * LANES, (hp + 1) * LANES)
        q2 = q_ref[:, cols]
        kk = jnp.concatenate([kp_ref[:, cols], kc_ref[:, cols]], axis=0)
        vv = jnp.concatenate([vp_ref[:, cols], vc_ref[:, cols]], axis=0)
        outs = []
        for half in range(2):
            h = 2 * hp + half
            in_head = (lane >= half * HEAD_DIM) & (lane < (half + 1) * HEAD_DIM)
            qm = jnp.where(in_head, q2, jnp.zeros_like(q2))
            s = lax.dot_general(qm, kk, (((1,), (1,)), ((), ())), preferred_element_type=F32)
            s = jnp.where(valid, s - ALIBI_SLOPES[h] * dist, MASK_VALUE)
            m = jnp.max(s, axis=1, keepdims=True)
            p = jnp.exp(s - m)
            l = jnp.sum(p, axis=1, keepdims=True)
            outs.append(jnp.dot(p.astype(BF16), vv, preferred_element_type=F32) / l)
            lse_all = jnp.where(lane == h, m + jnp.log(l), lse_all)
        o_ref[:, cols] = jnp.where(lane < HEAD_DIM, outs[0], outs[1])
    l_ref[...] = lse_all


def _attn_prompt_branch(qb, kb, vb, bsz, s_len, dil):
    sub = s_len // dil
    nblk = sub // ATTN_BLOCK
    view = lambda t: t.reshape(bsz, sub, dil * ATTN_WIDTH)
    cur = pl.BlockSpec((None, ATTN_BLOCK, ATTN_WIDTH), lambda b, r, n: (b, n, r))
    prev = pl.BlockSpec((None, ATTN_BLOCK, ATTN_WIDTH), lambda b, r, n: (b, jnp.maximum(n - 1, 0), r))
    o, lse = pl.pallas_call(
        functools.partial(_attn_prompt_kernel, dil=dil),
        grid=(bsz, dil, nblk),
        in_specs=[cur, prev, cur, prev, cur],
        out_specs=[cur, pl.BlockSpec((None, ATTN_BLOCK, LANES), lambda b, r, n: (b, n, r))],
        out_shape=[jax.ShapeDtypeStruct((bsz, sub, dil * ATTN_WIDTH), F32),
                   jax.ShapeDtypeStruct((bsz, sub, dil * LANES), F32)],
        compiler_params=pltpu.CompilerParams(
            dimension_semantics=("arbitrary", "arbitrary", "arbitrary"),
            vmem_limit_bytes=VMEM_LIMIT),
        name=f"attn_prompt_d{dil}",
    )(view(qb), view(kb), view(kb), view(vb), view(vb))
    return o.reshape(bsz * sub, dil * ATTN_WIDTH), lse.reshape(bsz * sub, dil * LANES)


def _attn_sample_kernel(q_ref, kn_ref, vn_ref, kc_ref, vc_ref, o_ref, bias_s, mult_s, *, t_len, wb, chunk):
    nrow = N_HEADS * t_len
    nflat = wb * N_HEADS
    t_shift = t_len.bit_length() - 1
    h_shift = N_HEADS.bit_length() - 1
    nt = (((1,), (1,)), ((), ()))

    def branch_count(dist, ok):
        mult = jnp.zeros(dist.shape, F32)
        for win, dil in zip(WINDOWS, DILATIONS):
            hit = ok & (dist >= 0) & (dist <= win) & ((dist & (dil - 1)) == 0)
            mult = mult + jnp.where(hit, 1.0, 0.0)
        return mult

    def biased(dist, mult):
        head = lax.broadcasted_iota(I32, dist.shape, 0) >> t_shift
        slope = jnp.zeros(dist.shape, F32)
        for h in range(N_HEADS):
            slope = jnp.where(head == h, ALIBI_SLOPES[h], slope)
        return jnp.where(mult > 0.0, -slope * dist.astype(F32), MASK_VALUE)

    @pl.when(pl.program_id(0) == 0)
    def _():
        for c in range(nflat // chunk):
            row = lax.broadcasted_iota(I32, (nrow, chunk), 0)
            col = lax.broadcasted_iota(I32, (nrow, chunk), 1) + c * chunk
            dist = wb + (row & (t_len - 1)) - (col >> h_shift)
            mult = branch_count(dist, (col & (N_HEADS - 1)) == (row >> t_shift))
            mult_s[:, c * chunk:(c + 1) * chunk] = mult
            bias_s[:, c * chunk:(c + 1) * chunk] = biased(dist, mult)

    def by_head(ref):
        x = ref[...].astype(F32)
        return jnp.concatenate([x[:, h * HEAD_DIM:(h + 1) * HEAD_DIM] for h in range(N_HEADS)],
                               axis=0).astype(BF16)

    qx, knx, vnx = by_head(q_ref), by_head(kn_ref), by_head(vn_ref)

    row_n = lax.broadcasted_iota(I32, (nrow, nrow), 0)
    col_n = lax.broadcasted_iota(I32, (nrow, nrow), 1)
    dist_n = (row_n & (t_len - 1)) - (col_n & (t_len - 1))
    mult_n = branch_count(dist_n, (row_n >> t_shift) == (col_n >> t_shift))
    s_n = lax.dot_general(qx, knx, nt, preferred_element_type=F32) + biased(dist_n, mult_n)
    m = jnp.max(s_n, axis=1, keepdims=True)
    p_n = jnp.exp(s_n - m) * mult_n
    l = jnp.sum(p_n, axis=1, keepdims=True)
    acc = jnp.dot(p_n.astype(BF16), vnx, preferred_element_type=F32)

    for c in range(nflat // chunk):
        cols = slice(c * chunk, (c + 1) * chunk)
        kf = kc_ref[cols, :].astype(BF16)
        s = lax.dot_general(qx, kf, nt, preferred_element_type=F32) + bias_s[:, cols]
        m_new = jnp.maximum(m, jnp.max(s, axis=1, keepdims=True))
        alpha = jnp.exp(m - m_new)
        p = jnp.exp(s - m_new) * mult_s[:, cols]
        l = alpha * l + jnp.sum(p, axis=1, keepdims=True)
        acc = alpha * acc + jnp.dot(p.astype(BF16), vc_ref[cols, :].astype(BF16),
                                    preferred_element_type=F32)
        m = m_new

    o = acc / l
    o_ref[...] = jnp.concatenate([o[h * t_len:(h + 1) * t_len] for h in range(N_HEADS)], axis=1)


def _attn_sample(qb, kb, vb, cache_k, cache_v):
    bsz, t_len, _ = qb.shape
    wb = cache_k.shape[1]
    nflat = wb * N_HEADS
    chunk = min(nflat, 4096)
    flat = lambda c: c.reshape(bsz, nflat, HEAD_DIM)
    new = pl.BlockSpec((None, t_len, ATTN_WIDTH), lambda b: (b, 0, 0))
    old = pl.BlockSpec((None, nflat, HEAD_DIM), lambda b: (b, 0, 0))
    return pl.pallas_call(
        functools.partial(_attn_sample_kernel, t_len=t_len, wb=wb, chunk=chunk),
        grid=(bsz,),
        in_specs=[new, new, new, old, old],
        out_specs=new,
        out_shape=jax.ShapeDtypeStruct((bsz, t_len, ATTN_WIDTH), F32),
        scratch_shapes=[pltpu.VMEM((N_HEADS * t_len, nflat), F32),
                        pltpu.VMEM((N_HEADS * t_len, nflat), F32)],
        compiler_params=pltpu.CompilerParams(dimension_semantics=("arbitrary",),
                                             vmem_limit_bytes=VMEM_LIMIT),
        name="attn_sample",
    )(qb, kb, vb, flat(cache_k), flat(cache_v))


def _mid_kernel(*refs, dils):
    n_branch = max(len(dils), 1)
    x_ref, y_ref = refs[0], refs[1]
    o_refs = refs[2:2 + n_branch]
    pos = 2 + n_branch
    l_refs = refs[pos:pos + len(dils)]
    pos += len(l_refs)
    (wglu_ref, bglu_ref, lns_ref, lna_ref, wout_ref, lnm_ref, wrh_ref, wrl_ref, br_ref,
     expand_ref, tri_ref,
     x1_ref, hrow_ref, eidx_ref, epos_ref, egate_ref, cnt_ref, carry, nat) = refs[pos:]
    tm = x_ref.shape[0]
    o_tiles = ATTN_WIDTH // LANES

    def natural(bi, d):
        if d == 1:
            return o_refs[bi][...], l_refs[bi][...]
        for r in range(d):
            rows = pl.ds(r, tm // d, stride=d)
            for ct in range(o_tiles):
                c0 = r * ATTN_WIDTH + ct * LANES
                nat[bi, ct, rows, :] = o_refs[bi][:, c0:c0 + LANES]
            nat[bi, o_tiles, rows, :] = l_refs[bi][:, r * LANES:(r + 1) * LANES]
        return (jnp.concatenate([nat[bi, ct] for ct in range(o_tiles)], axis=1), nat[bi, o_tiles])

    @pl.when(pl.program_id(0) == 0)
    def _():
        carry[...] = jnp.zeros_like(carry)

    y = y_ref[...]
    z = y * (0.5 * (1.0 + jnp.tanh(math.sqrt(2.0 / math.pi) * (y + 0.044715 * (y * y * y)))))
    glu = z * _sigmoid(jnp.dot(z.astype(BF16), wglu_ref[...], preferred_element_type=F32) + bglu_ref[...])
    n_ssm = _rms(glu, lns_ref[...])

    if not dils:
        attn = o_refs[0][...]
    else:
        pairs = [natural(bi, d) for bi, d in enumerate(dils)]
        lses = [p[1] for p in pairs]
        mx = functools.reduce(jnp.maximum, lses)
        es = [jnp.exp(l - mx) for l in lses]
        inv = 1.0 / functools.reduce(lambda a, b: a + b, es)
        attn = jnp.zeros((tm, ATTN_WIDTH), F32)
        for e, (o_nat, _) in zip(es, pairs):
            w = e * inv
            w_hi = w.astype(BF16)
            w_lo = (w - w_hi.astype(F32)).astype(BF16)
            wide = (jnp.dot(w_hi, expand_ref[...], preferred_element_type=F32)
                    + jnp.dot(w_lo, expand_ref[...], preferred_element_type=F32))
            attn = attn + wide * o_nat
    n_attn = _rms(attn, lna_ref[...])

    x1 = (x_ref[...]
          + jnp.dot(n_ssm.astype(BF16), wout_ref[:SSM_WIDTH, :], preferred_element_type=F32)
          + jnp.dot(n_attn.astype(BF16), wout_ref[SSM_WIDTH:, :], preferred_element_type=F32))
    x1_ref[...] = x1
    hm = _rms(x1, lnm_ref[...])
    for s in range(ROW_TILES):
        hrow_ref[pl.ds(s, tm, stride=ROW_TILES), :] = hm[:, s * LANES:(s + 1) * LANES]

    h_hi = hm.astype(BF16)
    h_lo = (hm - h_hi.astype(F32)).astype(BF16)
    logits = (jnp.dot(h_hi, wrh_ref[...], preferred_element_type=F32)
              + jnp.dot(h_lo, wrh_ref[...], preferred_element_type=F32)
              + jnp.dot(h_hi, wrl_ref[...], preferred_element_type=F32)
              + br_ref[...])
    lane = lax.broadcasted_iota(I32, (tm, LANES), 1)
    lane_f = lane.astype(F32)
    work = jnp.where(lane < N_EXPERTS, logits, NEG_BIG)
    vals, idxs, hots = [], [], []
    for _ in range(TOP_K):
        m = jnp.max(work, axis=1, keepdims=True)
        idx = jnp.min(jnp.where(work == m, lane_f, float(LANES)), axis=1, keepdims=True)
        hot = lane_f == idx
        vals.append(m)
        idxs.append(idx)
        hots.append(hot)
        work = jnp.where(hot, NEG_BIG, work)
    exps = [jnp.exp(v - vals[0]) for v in vals]
    inv = 1.0 / functools.reduce(lambda a, b: a + b, exps)

    sel = functools.reduce(lambda a, b: a + b, [h.astype(F32) for h in hots])
    before = jnp.dot(tri_ref[...], sel.astype(BF16), preferred_element_type=F32) + carry[0:1, :]
    eidx = jnp.zeros((tm, LANES), I32)
    epos = jnp.zeros((tm, LANES), I32)
    egate = jnp.zeros((tm, LANES), F32)
    for k in range(TOP_K):
        pk = jnp.sum(jnp.where(hots[k], before, 0.0), axis=1, keepdims=True)
        eidx = jnp.where(lane == k, idxs[k].astype(I32), eidx)
        epos = jnp.where(lane == k, pk.astype(I32), epos)
        egate = jnp.where(lane == k, exps[k] * inv, egate)
    eidx_ref[...] = eidx
    epos_ref[...] = epos
    egate_ref[...] = egate
    total = carry[0:1, :] + jnp.sum(sel, axis=0, keepdims=True)
    carry[...] = jnp.broadcast_to(total, carry.shape)
    cnt_ref[...] = jnp.broadcast_to(total, cnt_ref.shape).astype(I32)


def _mid(x2d, y2d, attn_o, attn_lse, dils, w):
    n = x2d.shape[0]
    tm = TOKEN_TILE
    n_branch = len(attn_o)
    row = lambda width: pl.BlockSpec((tm, width), lambda i: (i, 0))
    packed = lambda d, width: pl.BlockSpec((tm // d, d * width), lambda i: (i, 0))
    attn_specs = ([packed(d, ATTN_WIDTH) for d in dils] + [packed(d, LANES) for d in dils]
                  if dils else [row(ATTN_WIDTH)])
    in_specs = ([row(D_MODEL), row(SSM_WIDTH)] + attn_specs
                + [_full((SSM_WIDTH, SSM_WIDTH)), _full((1, SSM_WIDTH)), _full((1, SSM_WIDTH)),
                   _full((1, ATTN_WIDTH)), _full((D_MODEL, D_MODEL)), _full((1, D_MODEL)),
                   _full((D_MODEL, LANES)), _full((D_MODEL, LANES)), _full((1, LANES)),
                   _full((LANES, ATTN_WIDTH)), _full((tm, tm))])
    out_specs = [row(D_MODEL), pl.BlockSpec((tm * ROW_TILES, LANES), lambda i: (i, 0)),
                 row(LANES), row(LANES), row(LANES), _full((SUBLANES, LANES))]
    out_shape = [jax.ShapeDtypeStruct((n, D_MODEL), F32),
                 jax.ShapeDtypeStruct((n * ROW_TILES, LANES), F32),
                 jax.ShapeDtypeStruct((n, LANES), I32),
                 jax.ShapeDtypeStruct((n, LANES), I32),
                 jax.ShapeDtypeStruct((n, LANES), F32),
                 jax.ShapeDtypeStruct((SUBLANES, LANES), I32)]
    return pl.pallas_call(
        functools.partial(_mid_kernel, dils=tuple(dils)),
        grid=(n // tm,),
        in_specs=in_specs,
        out_specs=out_specs,
        out_shape=out_shape,
        scratch_shapes=[pltpu.VMEM((SUBLANES, LANES), F32),
                        pltpu.VMEM((n_branch, ATTN_WIDTH // LANES + 1, tm, LANES), F32)],
        compiler_params=pltpu.CompilerParams(dimension_semantics=("arbitrary",),
                                             vmem_limit_bytes=VMEM_LIMIT),
        name="mid",
    )(x2d, y2d, *attn_o, *attn_lse, w["w_glu"], w["b_glu"], w["ln_ssm_out"], w["ln_attn_out"],
      w["w_out"], w["ln_moe"], w["wr_hi"], w["wr_lo"], w["b_router"], w["expand"], w["tri"])


def _dispatch_kernel(dest_ref, hp_ref, hs_ref, xs_hbm, sem, *, n_prompt_tiles):
    i = pl.program_id(0)
    npair = TOKEN_TILE * TOP_K

    def run(src_ref):
        def issue(j, c):
            tok = j // TOP_K
            d = dest_ref[0, j]
            pltpu.make_async_copy(
                src_ref.at[pl.ds(pl.multiple_of(tok * ROW_TILES, ROW_TILES), ROW_TILES), :],
                xs_hbm.at[pl.ds(pl.multiple_of(d * ROW_TILES, ROW_TILES), ROW_TILES), :], sem).start()
            return c

        lax.fori_loop(0, npair, issue, 0, unroll=8)
        span = pl.ds(0, npair * ROW_TILES)
        pltpu.make_async_copy(xs_hbm.at[span, :], xs_hbm.at[span, :], sem).wait()

    @pl.when(i < n_prompt_tiles)
    def _():
        run(hp_ref)

    @pl.when(i >= n_prompt_tiles)
    def _():
        run(hs_ref)


def _dispatch(dest, hrow_p, hrow_s):
    n_p = hrow_p.shape[0] // ROW_TILES
    n_s = hrow_s.shape[0] // ROW_TILES
    npt = n_p // TOKEN_TILE
    ntile = (n_p + n_s) // TOKEN_TILE
    npair = TOKEN_TILE * TOP_K
    blk = (TOKEN_TILE * ROW_TILES, LANES)
    return pl.pallas_call(
        functools.partial(_dispatch_kernel, n_prompt_tiles=npt),
        grid=(ntile,),
        in_specs=[pl.BlockSpec((None, 1, npair), lambda i: (i, 0, 0), memory_space=pltpu.SMEM),
                  pl.BlockSpec(blk, lambda i: (jnp.minimum(i, npt - 1), 0)),
                  pl.BlockSpec(blk, lambda i: (jnp.maximum(i - npt, 0), 0))],
        out_specs=pl.BlockSpec(memory_space=pl.ANY),
        out_shape=jax.ShapeDtypeStruct(((n_p + n_s) * TOP_K * ROW_TILES, LANES), F32),
        scratch_shapes=[pltpu.SemaphoreType.DMA(())],
        compiler_params=pltpu.CompilerParams(dimension_semantics=("arbitrary",),
                                             vmem_limit_bytes=VMEM_LIMIT),
        name="moe_dispatch",
    )(dest.reshape(ntile, 1, npair), hrow_p, hrow_s)


def _expert_kernel(vt_ref, ve_ref, vok_ref, gs_ref, xs_ref, wu_ref, bu_ref, wd_ref, bd_ref,
                   out_ref, wu_s, wd_s, x_s):
    v = pl.program_id(0)
    e = ve_ref[v]
    j = vt_ref[v]
    vprev = jnp.maximum(v - 1, 0)
    new_e = (v == 0) | (e != ve_ref[vprev])
    new_j = (v == 0) | (j != vt_ref[vprev])
    tm = MOE_TILE

    @pl.when(new_e)
    def _():
        wu_s[...] = wu_ref[...].astype(BF16)
        wd_s[...] = wd_ref[...].astype(BF16)

    @pl.when(new_j)
    def _():
        out_ref[...] = jnp.zeros_like(out_ref)

    @pl.when(vok_ref[v] == 1)
    def _():
        for s in range(ROW_TILES):
            x_s[:, s * LANES:(s + 1) * LANES] = xs_ref[pl.ds(s, tm, stride=ROW_TILES), :].astype(BF16)
        a = jnp.dot(x_s[...], wu_s[...], preferred_element_type=F32) + bu_ref[...]
        g = jnp.minimum(a[:, :EXPERT_FF], SWIGLU_LIMIT)
        lin = jnp.clip(a[:, EXPERT_FF:], -SWIGLU_LIMIT, SWIGLU_LIMIT)
        act = (lin + 1.0) * (g * _sigmoid(SWIGLU_ALPHA * g))
        y = jnp.dot(act.astype(BF16), wd_s[...], preferred_element_type=F32) + bd_ref[...]
        rows = j * tm + lax.broadcasted_iota(I32, (tm, 1), 0)
        mine = (rows >= gs_ref[e]) & (rows < gs_ref[e + 1])
        for s in range(ROW_TILES):
            cur = out_ref[pl.ds(s, tm, stride=ROW_TILES), :]
            out_ref[pl.ds(s, tm, stride=ROW_TILES), :] = jnp.where(
                mine, y[:, s * LANES:(s + 1) * LANES], cur)


def _experts(xs, vt, ve, vok, gstart, w_up, b_up, w_down, b_down):
    tm = MOE_TILE
    nvisit = vt.shape[0]
    rows = pl.BlockSpec((tm * ROW_TILES, LANES), lambda v, vt, ve, vok, gs: (vt[v], 0))
    per_e = lambda a, b: pl.BlockSpec((None, a, b), lambda v, vt, ve, vok, gs: (ve[v], 0, 0))
    grid_spec = pltpu.PrefetchScalarGridSpec(
        num_scalar_prefetch=4,
        grid=(nvisit,),
        in_specs=[rows, per_e(D_MODEL, 2 * EXPERT_FF), per_e(1, 2 * EXPERT_FF),
                  per_e(EXPERT_FF, D_MODEL), per_e(1, D_MODEL)],
        out_specs=rows,
        scratch_shapes=[pltpu.VMEM((D_MODEL, 2 * EXPERT_FF), BF16),
                        pltpu.VMEM((EXPERT_FF, D_MODEL), BF16),
                        pltpu.VMEM((tm, D_MODEL), BF16)],
    )
    return pl.pallas_call(
        _expert_kernel,
        grid_spec=grid_spec,
        out_shape=jax.ShapeDtypeStruct(xs.shape, F32),
        compiler_params=pltpu.CompilerParams(dimension_semantics=("arbitrary",),
                                             vmem_limit_bytes=VMEM_LIMIT),
        name="moe_experts",
    )(vt, ve, vok, gstart, xs, w_up, b_up.reshape(N_EXPERTS, 1, 2 * EXPERT_FF),
      w_down, b_down.reshape(N_EXPERTS, 1, D_MODEL))


def _routing(eidx_p, epos_p, cnt_p, eidx_s, epos_s, cnt_s, n_rows):
    cnt_p = cnt_p[0, :N_EXPERTS]
    cnt_s = cnt_s[0, :N_EXPERTS]
    cnt = cnt_p + cnt_s
    gend = jnp.cumsum(cnt)
    gstart = gend - cnt
    ep = eidx_p[:, :TOP_K]
    es = eidx_s[:, :TOP_K]
    dest_p = gstart[ep] + epos_p[:, :TOP_K]
    dest_s = gstart[es] + cnt_p[es] + epos_s[:, :TOP_K]
    ntile = n_rows // MOE_TILE
    nvisit = ntile + N_EXPERTS
    first = gstart // MOE_TILE
    last = jnp.maximum(gend - 1, 0) // MOE_TILE
    nv = jnp.where(cnt > 0, last - first + 1, 0)
    vend = jnp.cumsum(nv)
    vstart = vend - nv
    total = vend[-1]
    v = jnp.arange(nvisit, dtype=I32)
    vc = jnp.minimum(v, total - 1)
    ve = jnp.sum((vend[None, :] <= vc[:, None]).astype(I32), axis=1)
    vt = (first[ve] + vc - vstart[ve]).astype(I32)
    vok = (v < total).astype(I32)
    gs = jnp.concatenate([gstart, gend[-1:]]).astype(I32)
    return dest_p.astype(I32), dest_s.astype(I32), vt, ve, vok, gs


def _out_kernel(dest_ref, y_hbm, x1_ref, gate_ref, pe_ref, lnp_ref, wg_ref, bg_ref, wp_ref, lnf_ref,
                o_ref, buf, sem):
    tm = TOKEN_TILE
    npair = tm * TOP_K

    def issue(j, c):
        d = dest_ref[0, j]
        pltpu.make_async_copy(
            y_hbm.at[pl.ds(pl.multiple_of(d * ROW_TILES, ROW_TILES), ROW_TILES), :],
            buf.at[pl.ds(pl.multiple_of(j * ROW_TILES, ROW_TILES), ROW_TILES), :], sem).start()
        return c

    lax.fori_loop(0, npair, issue, 0, unroll=8)
    pltpu.make_async_copy(y_hbm.at[pl.ds(0, npair * ROW_TILES), :], buf, sem).wait()

    gates = gate_ref[...]
    parts = []
    for s in range(ROW_TILES):
        acc = jnp.zeros((tm, LANES), F32)
        for k in range(TOP_K):
            rows = buf[pl.ds(k * ROW_TILES + s, tm, stride=TOP_K * ROW_TILES), :]
            acc = acc + gates[:, k:k + 1] * rows
        parts.append(acc)
    x2 = x1_ref[...] + jnp.concatenate(parts, axis=1)
    gate = _sigmoid(jnp.dot(_rms(x2, lnp_ref[...]).astype(BF16), wg_ref[...],
                            preferred_element_type=F32) + bg_ref[...])
    x3 = x2 + gate * jnp.dot(pe_ref[...].astype(BF16), wp_ref[...], preferred_element_type=F32)
    o_ref[...] = _rms(x3, lnf_ref[...])


def _combine(dest, y_rows, x1, egate, pe, w):
    n = x1.shape[0]
    tm = TOKEN_TILE
    npair = tm * TOP_K
    row = lambda width: pl.BlockSpec((tm, width), lambda i: (i, 0))
    return pl.pallas_call(
        _out_kernel,
        grid=(n // tm,),
        in_specs=[pl.BlockSpec((None, 1, npair), lambda i: (i, 0, 0), memory_space=pltpu.SMEM),
                  pl.BlockSpec(memory_space=pl.ANY),
                  row(D_MODEL), row(LANES), row(PLE_DIM),
                  _full((1, D_MODEL)), _full((D_MODEL, D_MODEL)), _full((1, D_MODEL)),
                  _full((PLE_DIM, D_MODEL)), _full((1, D_MODEL))],
        out_specs=row(D_MODEL),
        out_shape=jax.ShapeDtypeStruct((n, D_MODEL), F32),
        scratch_shapes=[pltpu.VMEM((npair * ROW_TILES, LANES), F32), pltpu.SemaphoreType.DMA(())],
        compiler_params=pltpu.CompilerParams(dimension_semantics=("arbitrary",),
                                             vmem_limit_bytes=VMEM_LIMIT),
        name="combine_out",
    )(dest.reshape(n // tm, 1, npair), y_rows, x1, egate, pe, w["ln_ple"], w["w_ple_gate"],
      w["b_ple_gate"], w["w_ple_proj"], w["ln_final"])


def kernel(x_prompt, x_sample, cache_attn_k, cache_attn_v, state_ssm_re, state_ssm_im, p_prompt, p_sample, ln_mix, w_in, ssm_a_re, ssm_a_im, ssm_b_re, ssm_b_im, ssm_c_re, ssm_c_im, ssm_d, ssm_log_dt, w_glu, b_glu, ln_ssm_out, ln_attn_out, w_out, ln_moe, w_router, b_router, w_up, b_up, w_down, b_down, ln_ple, w_ple_gate, b_ple_gate, w_ple_proj, ln_final):
    bsz, s_len, _ = x_prompt.shape
    dbsz, dt_len, _ = x_sample.shape
    n_p, n_s = bsz * s_len, dbsz * dt_len
    wb = cache_attn_k.shape[2]
    wb_prompt = min(WINDOWS[-1], s_len)

    wr = jnp.pad(w_router[0], ((0, 0), (0, LANES - N_EXPERTS)))
    wr_hi = wr.astype(BF16)
    ti = jnp.arange(TOKEN_TILE)
    w = {
        "w_glu": w_glu[0].astype(BF16), "b_glu": b_glu[0].reshape(1, -1),
        "ln_ssm_out": ln_ssm_out[0].reshape(1, -1), "ln_attn_out": ln_attn_out[0].reshape(1, -1),
        "w_out": w_out[0].astype(BF16), "ln_moe": ln_moe[0].reshape(1, -1),
        "wr_hi": wr_hi, "wr_lo": (wr - wr_hi.astype(F32)).astype(BF16),
        "b_router": jnp.pad(b_router[0], (0, LANES - N_EXPERTS)).reshape(1, -1),
        "expand": (jnp.arange(LANES)[:, None] == jnp.arange(ATTN_WIDTH)[None, :] // HEAD_DIM).astype(BF16),
        "tri": (ti[:, None] > ti[None, :]).astype(BF16),
        "ln_ple": ln_ple[0].reshape(1, -1), "w_ple_gate": w_ple_gate[0].astype(BF16),
        "b_ple_gate": b_ple_gate[0].reshape(1, -1), "w_ple_proj": w_ple_proj[0].astype(BF16),
        "ln_final": ln_final.reshape(1, -1),
    }
    w_in_b = w_in[0].astype(BF16)
    bmat, cmat, ab_re, ab_im = _s5_params(ssm_a_re[0], ssm_a_im[0], ssm_b_re[0], ssm_b_im[0],
                                          ssm_c_re[0], ssm_c_im[0], ssm_log_dt[0])

    def coeff(a, nb):
        return jnp.broadcast_to(a, (2, nb, STATE_HALF)).reshape(2 * nb, STATE_HALF)

    proj_p = _in_proj(x_prompt.reshape(n_p, D_MODEL), ln_mix[0], w_in_b, dils=DILATIONS[1:])
    u_p, k_p, v_p = proj_p[:3]
    qkv = [proj_p[3:6]] + [proj_p[6 + 3 * i:9 + 3 * i] for i in range(len(DILATIONS) - 1)]
    zeros_state = jnp.zeros((bsz, SSM_GROUPS, SSM_STATE), F32)
    y_p, ht_p = _s5(u_p.reshape(bsz, s_len, SSM_WIDTH), _state_to_rows(zeros_state, zeros_state),
                    bmat, cmat, coeff(ab_re, bsz), coeff(ab_im, bsz), ssm_d[0])
    branches = [_attn_prompt_branch(*qkv[i], bsz, s_len, d) for i, d in enumerate(DILATIONS)]
    x1_p, hrow_p, eidx_p, epos_p, egate_p, cnt_p = _mid(
        x_prompt.reshape(n_p, D_MODEL), y_p.reshape(n_p, SSM_WIDTH),
        [b[0] for b in branches], [b[1] for b in branches], DILATIONS, w)

    u_s, k_s, v_s, qb_s, kb_s, vb_s = _in_proj(x_sample.reshape(n_s, D_MODEL), ln_mix[0], w_in_b)
    y_s, ht_s = _s5(u_s.reshape(dbsz, dt_len, SSM_WIDTH), _state_to_rows(state_ssm_re[0], state_ssm_im[0]),
                    bmat, cmat, coeff(ab_re, dbsz), coeff(ab_im, dbsz), ssm_d[0])
    as3 = lambda t: t.reshape(dbsz, dt_len, ATTN_WIDTH)
    attn_s = _attn_sample(as3(qb_s), as3(kb_s), as3(vb_s), cache_attn_k[0], cache_attn_v[0])
    x1_s, hrow_s, eidx_s, epos_s, egate_s, cnt_s = _mid(
        x_sample.reshape(n_s, D_MODEL), y_s.reshape(n_s, SSM_WIDTH),
        [attn_s.reshape(n_s, ATTN_WIDTH)], [], (), w)

    n_rows = (n_p + n_s) * TOP_K
    dest_p, dest_s, vt, ve, vok, gs = _routing(eidx_p, epos_p, cnt_p, eidx_s, epos_s, cnt_s, n_rows)
    xs = _dispatch(jnp.concatenate([dest_p.reshape(-1), dest_s.reshape(-1)]), hrow_p, hrow_s)
    y_rows = _experts(xs, vt, ve, vok, gs, w_up[0], b_up[0], w_down[0], b_down[0])

    out_p = _combine(dest_p, y_rows, x1_p, egate_p, p_prompt[0].reshape(n_p, PLE_DIM), w)
    out_s = _combine(dest_s, y_rows, x1_s, egate_s, p_sample[0].reshape(n_s, PLE_DIM), w)

    hr_p, hi_p = _rows_to_state(ht_p, bsz)
    hr_s, hi_s = _rows_to_state(ht_s, dbsz)
    kv_p = lambda t: t.reshape(bsz, s_len, N_HEADS, HEAD_DIM)[:, -wb_prompt:][None]
    kv_s = lambda t: t.reshape(dbsz, dt_len, N_HEADS, HEAD_DIM)[None]
    return (out_p.reshape(bsz, s_len, D_MODEL), out_s.reshape(dbsz, dt_len, D_MODEL),
            kv_p(k_p), kv_p(v_p), hr_p[None], hi_p[None],
            kv_s(k_s), kv_s(v_s), hr_s[None], hi_s[None])
```

```python
import functools
import math

import jax
import jax.numpy as jnp
from jax import lax
from jax.experimental import pallas as pl
from jax.experimental.pallas import tpu as pltpu

F32 = jnp.float32
BF16 = jnp.bfloat16
I32 = jnp.int32

D_MODEL = 1024
SSM_WIDTH = 512
SSM_GROUP = 16
SSM_GROUPS = 32
SSM_STATE = 64
ATTN_WIDTH = 512
HEAD_DIM = 64
N_HEADS = 8
IN_WIDTH = SSM_WIDTH + 3 * ATTN_WIDTH
DILATIONS = (1, 4, 16)
WINDOWS = (128, 512, 2048)
ATTN_BLOCK = 128
N_EXPERTS = 32
TOP_K = 4
EXPERT_FF = D_MODEL
SWIGLU_LIMIT = 7.0
SWIGLU_ALPHA = 1.702
PLE_DIM = 256
EPS = 1e-6
MASK_VALUE = -1e30
NEG_BIG = -3.0e38

LANES = 128
SUBLANES = 8
ROW_TILES = D_MODEL // LANES
TOKEN_TILE = 256
MOE_TILE = 512
MOE_SUBTILE = 256
SSM_HALF = SSM_WIDTH // 2
STATE_HALF = SSM_GROUPS * SSM_STATE // 2
ALIBI_SLOPES = tuple(2.0 ** (-8.0 * (h + 1) / N_HEADS) for h in range(N_HEADS))
VMEM_LIMIT = 56 * 1024 * 1024


def _rms(x, g):
    return x * lax.rsqrt(jnp.mean(x * x, axis=-1, keepdims=True) + EPS) * g


def _sigmoid(x):
    return 1.0 / (1.0 + jnp.exp(-x))


def _full(shape):
    n = len(shape)
    return pl.BlockSpec(shape, lambda *_: (0,) * n)


def _in_kernel(x_ref, g_ref, w_ref, u_ref, k_ref, v_ref, qb_ref, kb_ref, vb_ref, *rest, dils):
    tm = x_ref.shape[0]
    h = _rms(x_ref[...], g_ref[...]).astype(BF16)
    p = jnp.dot(h, w_ref[...], preferred_element_type=F32)
    u_ref[...] = p[:, :SSM_WIDTH]
    q = p[:, SSM_WIDTH:SSM_WIDTH + ATTN_WIDTH] * (HEAD_DIM ** -0.5)
    k = p[:, SSM_WIDTH + ATTN_WIDTH:SSM_WIDTH + 2 * ATTN_WIDTH]
    v = p[:, SSM_WIDTH + 2 * ATTN_WIDTH:]
    k_ref[...] = k
    v_ref[...] = v
    qb_ref[...] = q.astype(BF16)
    kb_ref[...] = k.astype(BF16)
    vb_ref[...] = v.astype(BF16)
    if not dils:
        return
    scr = rest[-1]
    tiles = ATTN_WIDTH // LANES
    for a, val in enumerate((q, k, v)):
        for ct in range(tiles):
            scr[a * tiles + ct] = val[:, ct * LANES:(ct + 1) * LANES]
    for di, d in enumerate(dils):
        for a in range(3):
            out = rest[di * 3 + a]
            for r in range(d):
                for ct in range(tiles):
                    piece = scr[a * tiles + ct, pl.ds(r, tm // d, stride=d), :]
                    c0 = r * ATTN_WIDTH + ct * LANES
                    out[:, c0:c0 + LANES] = piece.astype(BF16)


def _in_proj(x2d, ln_mix, w_in_bf16, dils=()):
    n = x2d.shape[0]
    tm = min(n, 512)
    row = lambda w: pl.BlockSpec((tm, w), lambda i: (i, 0))
    out_specs = [row(SSM_WIDTH)] * 6
    out_shape = ([jax.ShapeDtypeStruct((n, SSM_WIDTH), F32)] * 3
                 + [jax.ShapeDtypeStruct((n, ATTN_WIDTH), BF16)] * 3)
    for d in dils:
        out_specs += [pl.BlockSpec((tm // d, d * ATTN_WIDTH), lambda i: (i, 0))] * 3
        out_shape += [jax.ShapeDtypeStruct((n // d, d * ATTN_WIDTH), BF16)] * 3
    scratch = [pltpu.VMEM((3 * ATTN_WIDTH // LANES, tm, LANES), F32)] if dils else []
    return pl.pallas_call(
        functools.partial(_in_kernel, dils=tuple(dils)),
        grid=(n // tm,),
        in_specs=[row(D_MODEL), _full((1, D_MODEL)), _full((D_MODEL, IN_WIDTH))],
        out_specs=out_specs,
        out_shape=out_shape,
        scratch_shapes=scratch,
        compiler_params=pltpu.CompilerParams(dimension_semantics=("arbitrary",),
                                             vmem_limit_bytes=VMEM_LIMIT),
        name="in_proj",
    )(x2d, ln_mix.reshape(1, D_MODEL), w_in_bf16)


def _s5_kernel(u_ref, bmat_ref, cmat_ref, are_ref, aim_ref, h0_ref, d_ref,
               y_ref, ht_ref, buf, hc, tmp, *, nb, tt, batched):
    rows = 2 * nb
    ntile = 2 * STATE_HALF // LANES
    half_tiles = ntile // 2

    def lane_tile(c):
        return slice(c * LANES, (c + 1) * LANES)

    @pl.when(pl.program_id(0) == 0)
    def _():
        hc[...] = h0_ref[...]

    if batched:
        u_all = u_ref[...].reshape(nb * tt, SSM_WIDTH)
        ub_all = u_all.astype(BF16)
        for hf in range(2):
            bu = jnp.dot(ub_all[:, hf * SSM_HALF:(hf + 1) * SSM_HALF], bmat_ref[hf],
                         preferred_element_type=F32)
            for c in range(ntile):
                tmp[c] = bu[:, lane_tile(c)]
            for c in range(ntile):
                for t in range(tt):
                    buf[c, t * rows + hf * nb:t * rows + (hf + 1) * nb, :] = tmp[c, pl.ds(t, nb, stride=tt), :]
    else:
        for b in range(nb):
            ub = u_ref[b].astype(BF16)
            for hf in range(2):
                bu = jnp.dot(ub[:, hf * SSM_HALF:(hf + 1) * SSM_HALF], bmat_ref[hf],
                             preferred_element_type=F32)
                for c in range(ntile):
                    buf[c, pl.ds(hf * nb + b, tt, stride=rows), :] = bu[:, lane_tile(c)]

    group = 4
    for s in range(rows // SUBLANES):
        r0 = s * SUBLANES
        for c0 in range(0, half_tiles, group):
            ar = [are_ref[r0:r0 + SUBLANES, lane_tile(c0 + k)] for k in range(group)]
            ai = [aim_ref[r0:r0 + SUBLANES, lane_tile(c0 + k)] for k in range(group)]
            init = tuple(hc[r0:r0 + SUBLANES, lane_tile(c0 + k)] for k in range(group)) + tuple(
                hc[r0:r0 + SUBLANES, lane_tile(half_tiles + c0 + k)] for k in range(group))

            def step(t, carry, r0=r0, c0=c0, ar=ar, ai=ai):
                row = pl.multiple_of(t * rows + r0, SUBLANES)
                out_r, out_i = [], []
                for k in range(group):
                    hr, hi = carry[k], carry[group + k]
                    xr = buf[c0 + k, pl.ds(row, SUBLANES), :]
                    xi = buf[half_tiles + c0 + k, pl.ds(row, SUBLANES), :]
                    nr = ar[k] * hr - ai[k] * hi + xr
                    ni = ar[k] * hi + ai[k] * hr + xi
                    buf[c0 + k, pl.ds(row, SUBLANES), :] = nr
                    buf[half_tiles + c0 + k, pl.ds(row, SUBLANES), :] = ni
                    out_r.append(nr)
                    out_i.append(ni)
                return tuple(out_r) + tuple(out_i)

            fin = lax.fori_loop(0, tt, step, init, unroll=min(tt, 8))
            for k in range(group):
                hc[r0:r0 + SUBLANES, lane_tile(c0 + k)] = fin[k]
                hc[r0:r0 + SUBLANES, lane_tile(half_tiles + c0 + k)] = fin[group + k]

    if batched:
        parts = []
        for hf in range(2):
            for c in range(ntile):
                for t in range(tt):
                    tmp[c, pl.ds(t, nb, stride=tt), :] = buf[c, t * rows + hf * nb:t * rows + (hf + 1) * nb, :]
            hs = jnp.concatenate([tmp[c] for c in range(ntile)], axis=1).astype(BF16)
            parts.append(jnp.dot(hs, cmat_ref[hf], preferred_element_type=F32))
        y_all = jnp.concatenate(parts, axis=1) + d_ref[...] * u_all
        y_ref[...] = y_all.reshape(nb, tt, SSM_WIDTH)
    else:
        for b in range(nb):
            parts = []
            for hf in range(2):
                hs = jnp.concatenate(
                    [buf[c, pl.ds(hf * nb + b, tt, stride=rows), :] for c in range(ntile)],
                    axis=1).astype(BF16)
                parts.append(jnp.dot(hs, cmat_ref[hf], preferred_element_type=F32))
            y_ref[b] = jnp.concatenate(parts, axis=1) + d_ref[...] * u_ref[b]

    ht_ref[...] = hc[...]


def _s5(u3, h0, bmat, cmat, a_re, a_im, d_skip):
    nb, t_len, _ = u3.shape
    tt = min(t_len, 256)
    rows = 2 * nb
    batched = tt < 16
    kern = functools.partial(_s5_kernel, nb=nb, tt=tt, batched=batched)
    ntile = 2 * STATE_HALF // LANES
    tmp_shape = (ntile, nb * tt, LANES) if batched else (1, SUBLANES, LANES)
    return pl.pallas_call(
        kern,
        grid=(t_len // tt,),
        in_specs=[pl.BlockSpec((nb, tt, SSM_WIDTH), lambda i: (0, i, 0)),
                  _full((2, SSM_HALF, 2 * STATE_HALF)),
                  _full((2, 2 * STATE_HALF, SSM_HALF)),
                  _full((rows, STATE_HALF)), _full((rows, STATE_HALF)),
                  _full((rows, 2 * STATE_HALF)), _full((1, SSM_WIDTH))],
        out_specs=[pl.BlockSpec((nb, tt, SSM_WIDTH), lambda i: (0, i, 0)),
                   _full((rows, 2 * STATE_HALF))],
        out_shape=[jax.ShapeDtypeStruct((nb, t_len, SSM_WIDTH), F32),
                   jax.ShapeDtypeStruct((rows, 2 * STATE_HALF), F32)],
        scratch_shapes=[pltpu.VMEM((ntile, tt * rows, LANES), F32),
                        pltpu.VMEM((rows, 2 * STATE_HALF), F32),
                        pltpu.VMEM(tmp_shape, F32)],
        compiler_params=pltpu.CompilerParams(dimension_semantics=("arbitrary",),
                                             vmem_limit_bytes=VMEM_LIMIT),
        name="s5_scan",
    )(u3, bmat, cmat, a_re, a_im, h0, d_skip.reshape(1, SSM_WIDTH))


def _s5_params(a_re, a_im, b_re, b_im, c_re, c_im, log_dt):
    dt = jnp.exp(log_dt)[:, None]
    mag = jnp.exp(dt * a_re)
    ang = dt * a_im
    ab_re, ab_im = mag * jnp.cos(ang), mag * jnp.sin(ang)
    den = a_re * a_re + a_im * a_im
    nr, ni = ab_re - 1.0, ab_im
    f_re = (nr * a_re + ni * a_im) / den
    f_im = (ni * a_re - nr * a_im) / den
    bb_re = f_re[..., None] * b_re - f_im[..., None] * b_im
    bb_im = f_re[..., None] * b_im + f_im[..., None] * b_re
    gh = SSM_GROUPS // 2
    eye = jnp.eye(gh, dtype=F32)

    def b_half(w):
        return jnp.einsum('gnc,gh->gchn', w, eye).reshape(gh * SSM_GROUP, gh * SSM_STATE)

    def c_half(w):
        return jnp.einsum('gcn,gh->gnhc', w, eye).reshape(gh * SSM_STATE, gh * SSM_GROUP)

    bmat = jnp.stack([jnp.concatenate([b_half(bb_re[h * gh:(h + 1) * gh]),
                                       b_half(bb_im[h * gh:(h + 1) * gh])], axis=1)
                      for h in range(2)]).astype(BF16)
    cmat = jnp.stack([jnp.concatenate([c_half(c_re[h * gh:(h + 1) * gh]),
                                       -c_half(c_im[h * gh:(h + 1) * gh])], axis=0)
                      for h in range(2)]).astype(BF16)
    return bmat, cmat, ab_re.reshape(2, 1, STATE_HALF), ab_im.reshape(2, 1, STATE_HALF)


def _state_to_rows(h_re, h_im):
    nb = h_re.shape[0]
    f = lambda h: h.reshape(nb, 2, STATE_HALF).transpose(1, 0, 2).reshape(2 * nb, STATE_HALF)
    return jnp.concatenate([f(h_re), f(h_im)], axis=1)


def _rows_to_state(ht, nb):
    f = lambda h: h.reshape(2, nb, STATE_HALF).transpose(1, 0, 2).reshape(nb, SSM_GROUPS, SSM_STATE)
    return f(ht[:, :STATE_HALF]), f(ht[:, STATE_HALF:])


def _attn_prompt_kernel(q_ref, kp_ref, kc_ref, vp_ref, vc_ref, o_ref, l_ref, *, dil):
    blk = ATTN_BLOCK
    n = pl.program_id(2)
    i_idx = lax.broadcasted_iota(I32, (blk, 2 * blk), 0)
    j_idx = lax.broadcasted_iota(I32, (blk, 2 * blk), 1)
    delta = i_idx - j_idx + blk
    valid = (delta >= 0) & (delta <= blk) & ((j_idx >= blk) | (n > 0))
    dist = (delta * dil).astype(F32)
    lane = lax.broadcasted_iota(I32, (blk, LANES), 1)
    lse_all = jnp.zeros((blk, LANES), F32)
    for hp in range(N_HEADS // 2):
        cols = slice(hp * LANES, (hp + 1) * LANES)
        q2 = q_ref[:, cols]
        kk = jnp.concatenate([kp_ref[:, cols], kc_ref[:, cols]], axis=0)
        vv = jnp.concatenate([vp_ref[:, cols], vc_ref[:, cols]], axis=0)
        outs = []
        for half in range(2):
            h = 2 * hp + half
            in_head = (lane >= half * HEAD_DIM) & (lane < (half + 1) * HEAD_DIM)
            qm = jnp.where(in_head, q2, jnp.zeros_like(q2))
            s = lax.dot_general(qm, kk, (((1,), (1,)), ((), ())), preferred_element_type=F32)
            s = jnp.where(valid, s - ALIBI_SLOPES[h] * dist, MASK_VALUE)
            m = jnp.max(s, axis=1, keepdims=True)
            p = jnp.exp(s - m)
            l = jnp.sum(p, axis=1, keepdims=True)
            outs.append(jnp.dot(p.astype(BF16), vv, preferred_element_type=F32) / l)
            lse_all = jnp.where(lane == h, m + jnp.log(l), lse_all)
        o_ref[:, cols] = jnp.where(lane < HEAD_DIM, outs[0], outs[1])
    l_ref[...] = lse_all


def _attn_prompt_branch(qb, kb, vb, bsz, s_len, dil):
    sub = s_len // dil
    nblk = sub // ATTN_BLOCK
    view = lambda t: t.reshape(bsz, sub, dil * ATTN_WIDTH)
    cur = pl.BlockSpec((None, ATTN_BLOCK, ATTN_WIDTH), lambda b, r, n: (b, n, r))
    prev = pl.BlockSpec((None, ATTN_BLOCK, ATTN_WIDTH), lambda b, r, n: (b, jnp.maximum(n - 1, 0), r))
    o, lse = pl.pallas_call(
        functools.partial(_attn_prompt_kernel, dil=dil),
        grid=(bsz, dil, nblk),
        in_specs=[cur, prev, cur, prev, cur],
        out_specs=[cur, pl.BlockSpec((None, ATTN_BLOCK, LANES), lambda b, r, n: (b, n, r))],
        out_shape=[jax.ShapeDtypeStruct((bsz, sub, dil * ATTN_WIDTH), F32),
                   jax.ShapeDtypeStruct((bsz, sub, dil * LANES), F32)],
        compiler_params=pltpu.CompilerParams(
            dimension_semantics=("arbitrary", "arbitrary", "arbitrary"),
            vmem_limit_bytes=VMEM_LIMIT),
        name=f"attn_prompt_d{dil}",
    )(view(qb), view(kb), view(kb), view(vb), view(vb))
    return o.reshape(bsz * sub, dil * ATTN_WIDTH), lse.reshape(bsz * sub, dil * LANES)


def _attn_sample_kernel(q_ref, kn_ref, vn_ref, kc_ref, vc_ref, o_ref, bias_s, mult_s, *, t_len, wb):
    nrow = N_HEADS * t_len
    t_shift = t_len.bit_length() - 1
    d_shift = HEAD_DIM.bit_length() - 1
    nt = (((1,), (1,)), ((), ()))

    def branch_count(dist):
        mult = jnp.zeros(dist.shape, F32)
        for win, dil in zip(WINDOWS, DILATIONS):
            hit = (dist >= 0) & (dist <= win) & ((dist & (dil - 1)) == 0)
            mult = mult + jnp.where(hit, 1.0, 0.0)
        return mult

    def biased(dist, mult):
        head = lax.broadcasted_iota(I32, dist.shape, 0) >> t_shift
        slope = jnp.zeros(dist.shape, F32)
        for h in range(N_HEADS):
            slope = jnp.where(head == h, ALIBI_SLOPES[h], slope)
        return jnp.where(mult > 0.0, -slope * dist.astype(F32), MASK_VALUE)

    @pl.when(pl.program_id(0) == 0)
    def _():
        row = lax.broadcasted_iota(I32, (nrow, wb), 0)
        col = lax.broadcasted_iota(I32, (nrow, wb), 1)
        dist = wb + (row & (t_len - 1)) - col
        mult = branch_count(dist)
        mult_s[...] = mult
        bias_s[...] = biased(dist, mult)

    q = q_ref[...].astype(F32)
    qt = jnp.concatenate([q] * N_HEADS, axis=0)
    row_w = lax.broadcasted_iota(I32, (nrow, ATTN_WIDTH), 0)
    lane_w = lax.broadcasted_iota(I32, (nrow, ATTN_WIDTH), 1)
    qm = jnp.where((lane_w >> d_shift) == (row_w >> t_shift), qt, 0.0).astype(BF16)

    pad = LANES - t_len
    kn = jnp.concatenate([kn_ref[...].astype(F32), jnp.zeros((pad, ATTN_WIDTH), F32)], axis=0).astype(BF16)
    vn = jnp.concatenate([vn_ref[...].astype(F32), jnp.zeros((pad, ATTN_WIDTH), F32)], axis=0).astype(BF16)
    row_n = lax.broadcasted_iota(I32, (nrow, LANES), 0)
    col_n = lax.broadcasted_iota(I32, (nrow, LANES), 1)
    dist_n = jnp.where(col_n < t_len, (row_n & (t_len - 1)) - col_n, -1)
    mult_n = branch_count(dist_n)
    s_n = lax.dot_general(qm, kn, nt, preferred_element_type=F32) + biased(dist_n, mult_n)
    s_c = jnp.dot(qm, kc_ref[...].astype(BF16), preferred_element_type=F32) + bias_s[...]

    m = jnp.maximum(jnp.max(s_c, axis=1, keepdims=True), jnp.max(s_n, axis=1, keepdims=True))
    p_c = jnp.exp(s_c - m) * mult_s[...]
    p_n = jnp.exp(s_n - m) * mult_n
    l = jnp.sum(p_c, axis=1, keepdims=True) + jnp.sum(p_n, axis=1, keepdims=True)
    o = (lax.dot_general(p_c.astype(BF16), vc_ref[...].astype(BF16), nt, preferred_element_type=F32)
         + jnp.dot(p_n.astype(BF16), vn, preferred_element_type=F32)) / l
    lane_o = lax.broadcasted_iota(I32, (t_len, ATTN_WIDTH), 1) >> d_shift
    out = jnp.zeros((t_len, ATTN_WIDTH), F32)
    for h in range(N_HEADS):
        out = jnp.where(lane_o == h, o[h * t_len:(h + 1) * t_len], out)
    o_ref[...] = out


def _attn_sample(qb, kb, vb, cache_k, cache_v):
    bsz, t_len, _ = qb.shape
    wb = cache_k.shape[1]
    feature_major = lambda c: jnp.transpose(c, (0, 2, 3, 1)).reshape(bsz, ATTN_WIDTH, wb)
    new = pl.BlockSpec((None, t_len, ATTN_WIDTH), lambda b: (b, 0, 0))
    old = pl.BlockSpec((None, ATTN_WIDTH, wb), lambda b: (b, 0, 0))
    return pl.pallas_call(
        functools.partial(_attn_sample_kernel, t_len=t_len, wb=wb),
        grid=(bsz,),
        in_specs=[new, new, new, old, old],
        out_specs=new,
        out_shape=jax.ShapeDtypeStruct((bsz, t_len, ATTN_WIDTH), F32),
        scratch_shapes=[pltpu.VMEM((N_HEADS * t_len, wb), F32),
                        pltpu.VMEM((N_HEADS * t_len, wb), F32)],
        compiler_params=pltpu.CompilerParams(dimension_semantics=("arbitrary",),
                                             vmem_limit_bytes=VMEM_LIMIT),
        name="attn_sample",
    )(qb, kb, vb, feature_major(cache_k), feature_major(cache_v))


def _mid_kernel(*refs, dils):
    n_branch = max(len(dils), 1)
    x_ref, y_ref = refs[0], refs[1]
    o_refs = refs[2:2 + n_branch]
    pos = 2 + n_branch
    l_refs = refs[pos:pos + len(dils)]
    pos += len(l_refs)
    (wglu_ref, bglu_ref, lns_ref, lna_ref, wout_ref, lnm_ref, wrh_ref, wrl_ref, br_ref,
     expand_ref, tri_ref,
     x1_ref, hrow_ref, eidx_ref, epos_ref, egate_ref, cnt_ref, carry, nat) = refs[pos:]
    tm = x_ref.shape[0]
    o_tiles = ATTN_WIDTH // LANES

    def natural(bi, d):
        if d == 1:
            return o_refs[bi][...], l_refs[bi][...]
        for r in range(d):
            rows = pl.ds(r, tm // d, stride=d)
            for ct in range(o_tiles):
                c0 = r * ATTN_WIDTH + ct * LANES
                nat[bi, ct, rows, :] = o_refs[bi][:, c0:c0 + LANES]
            nat[bi, o_tiles, rows, :] = l_refs[bi][:, r * LANES:(r + 1) * LANES]
        return (jnp.concatenate([nat[bi, ct] for ct in range(o_tiles)], axis=1), nat[bi, o_tiles])

    @pl.when(pl.program_id(0) == 0)
    def _():
        carry[...] = jnp.zeros_like(carry)

    y = y_ref[...]
    z = y * (0.5 * (1.0 + jnp.tanh(math.sqrt(2.0 / math.pi) * (y + 0.044715 * (y * y * y)))))
    glu = z * _sigmoid(jnp.dot(z.astype(BF16), wglu_ref[...], preferred_element_type=F32) + bglu_ref[...])
    n_ssm = _rms(glu, lns_ref[...])

    if not dils:
        attn = o_refs[0][...]
    else:
        pairs = [natural(bi, d) for bi, d in enumerate(dils)]
        lses = [p[1] for p in pairs]
        mx = functools.reduce(jnp.maximum, lses)
        es = [jnp.exp(l - mx) for l in lses]
        inv = 1.0 / functools.reduce(lambda a, b: a + b, es)
        attn = jnp.zeros((tm, ATTN_WIDTH), F32)
        for e, (o_nat, _) in zip(es, pairs):
            w = e * inv
            w_hi = w.astype(BF16)
            w_lo = (w - w_hi.astype(F32)).astype(BF16)
            wide = (jnp.dot(w_hi, expand_ref[...], preferred_element_type=F32)
                    + jnp.dot(w_lo, expand_ref[...], preferred_element_type=F32))
            attn = attn + wide * o_nat
    n_attn = _rms(attn, lna_ref[...])

    x1 = (x_ref[...]
          + jnp.dot(n_ssm.astype(BF16), wout_ref[:SSM_WIDTH, :], preferred_element_type=F32)
          + jnp.dot(n_attn.astype(BF16), wout_ref[SSM_WIDTH:, :], preferred_element_type=F32))
    x1_ref[...] = x1
    hm = _rms(x1, lnm_ref[...])
    for s in range(ROW_TILES):
        hrow_ref[pl.ds(s, tm, stride=ROW_TILES), :] = hm[:, s * LANES:(s + 1) * LANES]

    h_hi = hm.astype(BF16)
    h_lo = (hm - h_hi.astype(F32)).astype(BF16)
    logits = (jnp.dot(h_hi, wrh_ref[...], preferred_element_type=F32)
              + jnp.dot(h_lo, wrh_ref[...], preferred_element_type=F32)
              + jnp.dot(h_hi, wrl_ref[...], preferred_element_type=F32)
              + br_ref[...])
    lane = lax.broadcasted_iota(I32, (tm, LANES), 1)
    lane_f = lane.astype(F32)
    work = jnp.where(lane < N_EXPERTS, logits, NEG_BIG)
    vals, idxs, hots = [], [], []
    for _ in range(TOP_K):
        m = jnp.max(work, axis=1, keepdims=True)
        idx = jnp.min(jnp.where(work == m, lane_f, float(LANES)), axis=1, keepdims=True)
        hot = lane_f == idx
        vals.append(m)
        idxs.append(idx)
        hots.append(hot)
        work = jnp.where(hot, NEG_BIG, work)
    exps = [jnp.exp(v - vals[0]) for v in vals]
    inv = 1.0 / functools.reduce(lambda a, b: a + b, exps)

    sel = functools.reduce(lambda a, b: a + b, [h.astype(F32) for h in hots])
    before = jnp.dot(tri_ref[...], sel.astype(BF16), preferred_element_type=F32) + carry[0:1, :]
    eidx = jnp.zeros((tm, LANES), I32)
    epos = jnp.zeros((tm, LANES), I32)
    egate = jnp.zeros((tm, LANES), F32)
    for k in range(TOP_K):
        pk = jnp.sum(jnp.where(hots[k], before, 0.0), axis=1, keepdims=True)
        eidx = jnp.where(lane == k, idxs[k].astype(I32), eidx)
        epos = jnp.where(lane == k, pk.astype(I32), epos)
        egate = jnp.where(lane == k, exps[k] * inv, egate)
    eidx_ref[...] = eidx
    epos_ref[...] = epos
    egate_ref[...] = egate
    total = carry[0:1, :] + jnp.sum(sel, axis=0, keepdims=True)
    carry[...] = jnp.broadcast_to(total, carry.shape)
    cnt_ref[...] = jnp.broadcast_to(total, cnt_ref.shape).astype(I32)


def _mid(x2d, y2d, attn_o, attn_lse, dils, w):
    n = x2d.shape[0]
    tm = TOKEN_TILE
    n_branch = len(attn_o)
    row = lambda width: pl.BlockSpec((tm, width), lambda i: (i, 0))
    packed = lambda d, width: pl.BlockSpec((tm // d, d * width), lambda i: (i, 0))
    attn_specs = ([packed(d, ATTN_WIDTH) for d in dils] + [packed(d, LANES) for d in dils]
                  if dils else [row(ATTN_WIDTH)])
    in_specs = ([row(D_MODEL), row(SSM_WIDTH)] + attn_specs
                + [_full((SSM_WIDTH, SSM_WIDTH)), _full((1, SSM_WIDTH)), _full((1, SSM_WIDTH)),
                   _full((1, ATTN_WIDTH)), _full((D_MODEL, D_MODEL)), _full((1, D_MODEL)),
                   _full((D_MODEL, LANES)), _full((D_MODEL, LANES)), _full((1, LANES)),
                   _full((LANES, ATTN_WIDTH)), _full((tm, tm))])
    out_specs = [row(D_MODEL), pl.BlockSpec((tm * ROW_TILES, LANES), lambda i: (i, 0)),
                 row(LANES), row(LANES), row(LANES), _full((SUBLANES, LANES))]
    out_shape = [jax.ShapeDtypeStruct((n, D_MODEL), F32),
                 jax.ShapeDtypeStruct((n * ROW_TILES, LANES), F32),
                 jax.ShapeDtypeStruct((n, LANES), I32),
                 jax.ShapeDtypeStruct((n, LANES), I32),
                 jax.ShapeDtypeStruct((n, LANES), F32),
                 jax.ShapeDtypeStruct((SUBLANES, LANES), I32)]
    return pl.pallas_call(
        functools.partial(_mid_kernel, dils=tuple(dils)),
        grid=(n // tm,),
        in_specs=in_specs,
        out_specs=out_specs,
        out_shape=out_shape,
        scratch_shapes=[pltpu.VMEM((SUBLANES, LANES), F32),
                        pltpu.VMEM((n_branch, ATTN_WIDTH // LANES + 1, tm, LANES), F32)],
        compiler_params=pltpu.CompilerParams(dimension_semantics=("arbitrary",),
                                             vmem_limit_bytes=VMEM_LIMIT),
        name="mid",
    )(x2d, y2d, *attn_o, *attn_lse, w["w_glu"], w["b_glu"], w["ln_ssm_out"], w["ln_attn_out"],
      w["w_out"], w["ln_moe"], w["wr_hi"], w["wr_lo"], w["b_router"], w["expand"], w["tri"])


def _dispatch_kernel(dest_ref, hp_ref, hs_ref, xs_hbm, sem, *, n_prompt_tiles):
    i = pl.program_id(0)
    npair = TOKEN_TILE * TOP_K

    def run(src_ref):
        def issue(t, c):
            src = src_ref.at[pl.ds(pl.multiple_of(t * ROW_TILES, ROW_TILES), ROW_TILES), :]
            for k in range(TOP_K):
                d = dest_ref[0, t * TOP_K + k]
                pltpu.make_async_copy(
                    src, xs_hbm.at[pl.ds(pl.multiple_of(d * ROW_TILES, ROW_TILES), ROW_TILES), :],
                    sem).start()
            return c

        lax.fori_loop(0, TOKEN_TILE, issue, 0, unroll=2)
        span = pl.ds(0, npair * ROW_TILES)
        pltpu.make_async_copy(xs_hbm.at[span, :], xs_hbm.at[span, :], sem).wait()

    @pl.when(i < n_prompt_tiles)
    def _():
        run(hp_ref)

    @pl.when(i >= n_prompt_tiles)
    def _():
        run(hs_ref)


def _dispatch(dest, hrow_p, hrow_s):
    n_p = hrow_p.shape[0] // ROW_TILES
    n_s = hrow_s.shape[0] // ROW_TILES
    npt = n_p // TOKEN_TILE
    ntile = (n_p + n_s) // TOKEN_TILE
    npair = TOKEN_TILE * TOP_K
    blk = (TOKEN_TILE * ROW_TILES, LANES)
    return pl.pallas_call(
        functools.partial(_dispatch_kernel, n_prompt_tiles=npt),
        grid=(ntile,),
        in_specs=[pl.BlockSpec((None, 1, npair), lambda i: (i, 0, 0), memory_space=pltpu.SMEM),
                  pl.BlockSpec(blk, lambda i: (jnp.minimum(i, npt - 1), 0)),
                  pl.BlockSpec(blk, lambda i: (jnp.maximum(i - npt, 0), 0))],
        out_specs=pl.BlockSpec(memory_space=pl.ANY),
        out_shape=jax.ShapeDtypeStruct(((n_p + n_s) * TOP_K * ROW_TILES, LANES), F32),
        scratch_shapes=[pltpu.SemaphoreType.DMA(())],
        compiler_params=pltpu.CompilerParams(dimension_semantics=("arbitrary",),
                                             vmem_limit_bytes=VMEM_LIMIT),
        name="moe_dispatch",
    )(dest.reshape(ntile, 1, npair), hrow_p, hrow_s)


def _expert_kernel(vt_ref, ve_ref, vok_ref, gs_ref, xs_ref, wu_ref, bu_ref, wd_ref, bd_ref,
                   out_ref, wu_s, wd_s, x_s):
    v = pl.program_id(0)
    e = ve_ref[v]
    j = vt_ref[v]
    vprev = jnp.maximum(v - 1, 0)
    new_e = (v == 0) | (e != ve_ref[vprev])
    new_j = (v == 0) | (j != vt_ref[vprev])
    tm = MOE_TILE

    @pl.when(new_e)
    def _():
        wu_s[...] = wu_ref[...].astype(BF16)
        wd_s[...] = wd_ref[...].astype(BF16)

    @pl.when(new_j)
    def _():
        out_ref[...] = jnp.zeros_like(out_ref)

    @pl.when(vok_ref[v] == 1)
    def _():
        sub = x_s.shape[0]
        for part in range(tm // sub):
            base = part * sub * ROW_TILES
            for s in range(ROW_TILES):
                x_s[:, s * LANES:(s + 1) * LANES] = xs_ref[
                    pl.ds(base + s, sub, stride=ROW_TILES), :].astype(BF16)
            a = jnp.dot(x_s[...], wu_s[...], preferred_element_type=F32) + bu_ref[...]
            g = jnp.minimum(a[:, :EXPERT_FF], SWIGLU_LIMIT)
            lin = jnp.clip(a[:, EXPERT_FF:], -SWIGLU_LIMIT, SWIGLU_LIMIT)
            act = (lin + 1.0) * (g * _sigmoid(SWIGLU_ALPHA * g))
            y = jnp.dot(act.astype(BF16), wd_s[...], preferred_element_type=F32) + bd_ref[...]
            rows = j * tm + part * sub + lax.broadcasted_iota(I32, (sub, 1), 0)
            mine = (rows >= gs_ref[e]) & (rows < gs_ref[e + 1])
            for s in range(ROW_TILES):
                cur = out_ref[pl.ds(base + s, sub, stride=ROW_TILES), :]
                out_ref[pl.ds(base + s, sub, stride=ROW_TILES), :] = jnp.where(
                    mine, y[:, s * LANES:(s + 1) * LANES], cur)


def _experts(xs, vt, ve, vok, gstart, w_up, b_up, w_down, b_down):
    tm = MOE_TILE
    nvisit = vt.shape[0]
    rows = pl.BlockSpec((tm * ROW_TILES, LANES), lambda v, vt, ve, vok, gs: (vt[v], 0))
    per_e = lambda a, b: pl.BlockSpec((None, a, b), lambda v, vt, ve, vok, gs: (ve[v], 0, 0))
    grid_spec = pltpu.PrefetchScalarGridSpec(
        num_scalar_prefetch=4,
        grid=(nvisit,),
        in_specs=[rows, per_e(D_MODEL, 2 * EXPERT_FF), per_e(1, 2 * EXPERT_FF),
                  per_e(EXPERT_FF, D_MODEL), per_e(1, D_MODEL)],
        out_specs=rows,
        scratch_shapes=[pltpu.VMEM((D_MODEL, 2 * EXPERT_FF), BF16),
                        pltpu.VMEM((EXPERT_FF, D_MODEL), BF16),
                        pltpu.VMEM((MOE_SUBTILE, D_MODEL), BF16)],
    )
    return pl.pallas_call(
        _expert_kernel,
        grid_spec=grid_spec,
        out_shape=jax.ShapeDtypeStruct(xs.shape, F32),
        compiler_params=pltpu.CompilerParams(dimension_semantics=("arbitrary",),
                                             vmem_limit_bytes=VMEM_LIMIT),
        name="moe_experts",
    )(vt, ve, vok, gstart, xs, w_up, b_up.reshape(N_EXPERTS, 1, 2 * EXPERT_FF),
      w_down, b_down.reshape(N_EXPERTS, 1, D_MODEL))


def _routing(eidx_p, epos_p, cnt_p, eidx_s, epos_s, cnt_s, n_rows):
    cnt_p = cnt_p[0, :N_EXPERTS]
    cnt_s = cnt_s[0, :N_EXPERTS]
    cnt = cnt_p + cnt_s
    gend = jnp.cumsum(cnt)
    gstart = gend - cnt
    ep = eidx_p[:, :TOP_K]
    es = eidx_s[:, :TOP_K]
    dest_p = gstart[ep] + epos_p[:, :TOP_K]
    dest_s = gstart[es] + cnt_p[es] + epos_s[:, :TOP_K]
    ntile = n_rows // MOE_TILE
    nvisit = ntile + N_EXPERTS
    first = gstart // MOE_TILE
    last = jnp.maximum(gend - 1, 0) // MOE_TILE
    nv = jnp.where(cnt > 0, last - first + 1, 0)
    vend = jnp.cumsum(nv)
    vstart = vend - nv
    total = vend[-1]
    v = jnp.arange(nvisit, dtype=I32)
    vc = jnp.minimum(v, total - 1)
    ve = jnp.sum((vend[None, :] <= vc[:, None]).astype(I32), axis=1)
    vt = (first[ve] + vc - vstart[ve]).astype(I32)
    vok = (v < total).astype(I32)
    gs = jnp.concatenate([gstart, gend[-1:]]).astype(I32)
    return dest_p.astype(I32), dest_s.astype(I32), vt, ve, vok, gs


def _out_kernel(dest_ref, next_ref, y_hbm, x1_ref, gate_ref, pe_ref, lnp_ref, wg_ref, bg_ref, wp_ref,
                lnf_ref, o_ref, buf, sem, *, ntile):
    tm = TOKEN_TILE
    npair = tm * TOP_K
    i = pl.program_id(0)
    slot = i % 2

    def gather(idx_ref, to):
        def issue(t, c):
            for k in range(TOP_K):
                d = idx_ref[0, t * TOP_K + k]
                pltpu.make_async_copy(
                    y_hbm.at[pl.ds(pl.multiple_of(d * ROW_TILES, ROW_TILES), ROW_TILES), :],
                    buf.at[to, pl.ds(pl.multiple_of((k * tm + t) * ROW_TILES, ROW_TILES), ROW_TILES), :],
                    sem.at[to]).start()
            return c

        lax.fori_loop(0, tm, issue, 0, unroll=2)

    @pl.when(i == 0)
    def _():
        gather(dest_ref, 0)

    if ntile > 1:
        @pl.when(i + 1 < ntile)
        def _():
            gather(next_ref, 1 - slot)

    pltpu.make_async_copy(y_hbm.at[pl.ds(0, npair * ROW_TILES), :], buf.at[slot], sem.at[slot]).wait()

    gates = gate_ref[...]
    parts = []
    for s in range(ROW_TILES):
        acc = jnp.zeros((tm, LANES), F32)
        for k in range(TOP_K):
            rows = buf[slot, pl.ds(k * tm * ROW_TILES + s, tm, stride=ROW_TILES), :]
            acc = acc + gates[:, k:k + 1] * rows
        parts.append(acc)
    x2 = x1_ref[...] + jnp.concatenate(parts, axis=1)
    gate = _sigmoid(jnp.dot(_rms(x2, lnp_ref[...]).astype(BF16), wg_ref[...],
                            preferred_element_type=F32) + bg_ref[...])
    x3 = x2 + gate * jnp.dot(pe_ref[...].astype(BF16), wp_ref[...], preferred_element_type=F32)
    o_ref[...] = _rms(x3, lnf_ref[...])


def _combine(dest, y_rows, x1, egate, pe, w):
    n = x1.shape[0]
    tm = TOKEN_TILE
    npair = tm * TOP_K
    row = lambda width: pl.BlockSpec((tm, width), lambda i: (i, 0))
    ntile = n // tm
    dest3 = dest.reshape(ntile, 1, npair)
    return pl.pallas_call(
        functools.partial(_out_kernel, ntile=ntile),
        grid=(ntile,),
        in_specs=[pl.BlockSpec((None, 1, npair), lambda i: (i, 0, 0), memory_space=pltpu.SMEM),
                  pl.BlockSpec((None, 1, npair), lambda i: (jnp.minimum(i + 1, ntile - 1), 0, 0),
                               memory_space=pltpu.SMEM),
                  pl.BlockSpec(memory_space=pl.ANY),
                  row(D_MODEL), row(LANES), row(PLE_DIM),
                  _full((1, D_MODEL)), _full((D_MODEL, D_MODEL)), _full((1, D_MODEL)),
                  _full((PLE_DIM, D_MODEL)), _full((1, D_MODEL))],
        out_specs=row(D_MODEL),
        out_shape=jax.ShapeDtypeStruct((n, D_MODEL), F32),
        scratch_shapes=[pltpu.VMEM((2, npair * ROW_TILES, LANES), F32), pltpu.SemaphoreType.DMA((2,))],
        compiler_params=pltpu.CompilerParams(dimension_semantics=("arbitrary",),
                                             vmem_limit_bytes=VMEM_LIMIT),
        name="combine_out",
    )(dest3, dest3, y_rows, x1, egate, pe, w["ln_ple"], w["w_ple_gate"],
      w["b_ple_gate"], w["w_ple_proj"], w["ln_final"])


def kernel(x_prompt, x_sample, cache_attn_k, cache_attn_v, state_ssm_re, state_ssm_im, p_prompt, p_sample, ln_mix, w_in, ssm_a_re, ssm_a_im, ssm_b_re, ssm_b_im, ssm_c_re, ssm_c_im, ssm_d, ssm_log_dt, w_glu, b_glu, ln_ssm_out, ln_attn_out, w_out, ln_moe, w_router, b_router, w_up, b_up, w_down, b_down, ln_ple, w_ple_gate, b_ple_gate, w_ple_proj, ln_final):
    bsz, s_len, _ = x_prompt.shape
    dbsz, dt_len, _ = x_sample.shape
    n_p, n_s = bsz * s_len, dbsz * dt_len
    wb = cache_attn_k.shape[2]
    wb_prompt = min(WINDOWS[-1], s_len)

    wr = jnp.pad(w_router[0], ((0, 0), (0, LANES - N_EXPERTS)))
    wr_hi = wr.astype(BF16)
    ti = jnp.arange(TOKEN_TILE)
    w = {
        "w_glu": w_glu[0].astype(BF16), "b_glu": b_glu[0].reshape(1, -1),
        "ln_ssm_out": ln_ssm_out[0].reshape(1, -1), "ln_attn_out": ln_attn_out[0].reshape(1, -1),
        "w_out": w_out[0].astype(BF16), "ln_moe": ln_moe[0].reshape(1, -1),
        "wr_hi": wr_hi, "wr_lo": (wr - wr_hi.astype(F32)).astype(BF16),
        "b_router": jnp.pad(b_router[0], (0, LANES - N_EXPERTS)).reshape(1, -1),
        "expand": (jnp.arange(LANES)[:, None] == jnp.arange(ATTN_WIDTH)[None, :] // HEAD_DIM).astype(BF16),
        "tri": (ti[:, None] > ti[None, :]).astype(BF16),
        "ln_ple": ln_ple[0].reshape(1, -1), "w_ple_gate": w_ple_gate[0].astype(BF16),
        "b_ple_gate": b_ple_gate[0].reshape(1, -1), "w_ple_proj": w_ple_proj[0].astype(BF16),
        "ln_final": ln_final.reshape(1, -1),
    }
    w_in_b = w_in[0].astype(BF16)
    bmat, cmat, ab_re, ab_im = _s5_params(ssm_a_re[0], ssm_a_im[0], ssm_b_re[0], ssm_b_im[0],
                                          ssm_c_re[0], ssm_c_im[0], ssm_log_dt[0])

    def coeff(a, nb):
        return jnp.broadcast_to(a, (2, nb, STATE_HALF)).reshape(2 * nb, STATE_HALF)

    proj_p = _in_proj(x_prompt.reshape(n_p, D_MODEL), ln_mix[0], w_in_b, dils=DILATIONS[1:])
    u_p, k_p, v_p = proj_p[:3]
    qkv = [proj_p[3:6]] + [proj_p[6 + 3 * i:9 + 3 * i] for i in range(len(DILATIONS) - 1)]
    zeros_state = jnp.zeros((bsz, SSM_GROUPS, SSM_STATE), F32)
    y_p, ht_p = _s5(u_p.reshape(bsz, s_len, SSM_WIDTH), _state_to_rows(zeros_state, zeros_state),
                    bmat, cmat, coeff(ab_re, bsz), coeff(ab_im, bsz), ssm_d[0])
    branches = [_attn_prompt_branch(*qkv[i], bsz, s_len, d) for i, d in enumerate(DILATIONS)]
    x1_p, hrow_p, eidx_p, epos_p, egate_p, cnt_p = _mid(
        x_prompt.reshape(n_p, D_MODEL), y_p.reshape(n_p, SSM_WIDTH),
        [b[0] for b in branches], [b[1] for b in branches], DILATIONS, w)

    u_s, k_s, v_s, qb_s, kb_s, vb_s = _in_proj(x_sample.reshape(n_s, D_MODEL), ln_mix[0], w_in_b)
    y_s, ht_s = _s5(u_s.reshape(dbsz, dt_len, SSM_WIDTH), _state_to_rows(state_ssm_re[0], state_ssm_im[0]),
                    bmat, cmat, coeff(ab_re, dbsz), coeff(ab_im, dbsz), ssm_d[0])
    as3 = lambda t: t.reshape(dbsz, dt_len, ATTN_WIDTH)
    attn_s = _attn_sample(as3(qb_s), as3(kb_s), as3(vb_s), cache_attn_k[0], cache_attn_v[0])
    x1_s, hrow_s, eidx_s, epos_s, egate_s, cnt_s = _mid(
        x_sample.reshape(n_s, D_MODEL), y_s.reshape(n_s, SSM_WIDTH),
        [attn_s.reshape(n_s, ATTN_WIDTH)], [], (), w)

    n_rows = (n_p + n_s) * TOP_K
    dest_p, dest_s, vt, ve, vok, gs = _routing(eidx_p, epos_p, cnt_p, eidx_s, epos_s, cnt_s, n_rows)
    xs = _dispatch(jnp.concatenate([dest_p.reshape(-1), dest_s.reshape(-1)]), hrow_p, hrow_s)
    y_rows = _experts(xs, vt, ve, vok, gs, w_up[0], b_up[0], w_down[0], b_down[0])

    out_p = _combine(dest_p, y_rows, x1_p, egate_p, p_prompt[0].reshape(n_p, PLE_DIM), w)
    out_s = _combine(dest_s, y_rows, x1_s, egate_s, p_sample[0].reshape(n_s, PLE_DIM), w)

    hr_p, hi_p = _rows_to_state(ht_p, bsz)
    hr_s, hi_s = _rows_to_state(ht_s, dbsz)
    kv_p = lambda t: t.reshape(bsz, s_len, N_HEADS, HEAD_DIM)[:, -wb_prompt:][None]
    kv_s = lambda t: t.reshape(dbsz, dt_len, N_HEADS, HEAD_DIM)[None]
    return (out_p.reshape(bsz, s_len, D_MODEL), out_s.reshape(dbsz, dt_len, D_MODEL),
            kv_p(k_p), kv_p(v_p), hr_p[None], hi_p[None],
            kv_s(k_s), kv_s(v_s), hr_s[None], hi_s[None])
```

```python
import functools
import math

import jax
import jax.numpy as jnp
from jax import lax
from jax.experimental import pallas as pl
from jax.experimental.pallas import tpu as pltpu

F32 = jnp.float32
BF16 = jnp.bfloat16
I32 = jnp.int32

D_MODEL = 1024
SSM_WIDTH = 512
SSM_GROUP = 16
SSM_GROUPS = 32
SSM_STATE = 64
ATTN_WIDTH = 512
HEAD_DIM = 64
N_HEADS = 8
IN_WIDTH = SSM_WIDTH + 3 * ATTN_WIDTH
DILATIONS = (1, 4, 16)
WINDOWS = (128, 512, 2048)
ATTN_BLOCK = 128
N_EXPERTS = 32
TOP_K = 4
EXPERT_FF = D_MODEL
SWIGLU_LIMIT = 7.0
SWIGLU_ALPHA = 1.702
PLE_DIM = 256
EPS = 1e-6
MASK_VALUE = -1e30
NEG_BIG = -3.0e38

LANES = 128
SUBLANES = 8
ROW_TILES = D_MODEL // LANES
TOKEN_TILE = 256
MOE_TILE = 512
MOE_SUBTILE = 256
SSM_HALF = SSM_WIDTH // 2
STATE_HALF = SSM_GROUPS * SSM_STATE // 2
ALIBI_SLOPES = tuple(2.0 ** (-8.0 * (h + 1) / N_HEADS) for h in range(N_HEADS))
VMEM_LIMIT = 56 * 1024 * 1024


def _rms(x, g):
    return x * lax.rsqrt(jnp.mean(x * x, axis=-1, keepdims=True) + EPS) * g


def _sigmoid(x):
    return 1.0 / (1.0 + jnp.exp(-x))


def _full(shape):
    n = len(shape)
    return pl.BlockSpec(shape, lambda *_: (0,) * n)


def _in_kernel(x_ref, g_ref, w_ref, u_ref, k_ref, v_ref, qb_ref, kb_ref, vb_ref, *rest, dils, window_tiles):
    tm = x_ref.shape[0]
    h = _rms(x_ref[...], g_ref[...]).astype(BF16)
    p = jnp.dot(h, w_ref[...], preferred_element_type=F32)
    u_ref[...] = p[:, :SSM_WIDTH]
    q = p[:, SSM_WIDTH:SSM_WIDTH + ATTN_WIDTH] * (HEAD_DIM ** -0.5)
    k = p[:, SSM_WIDTH + ATTN_WIDTH:SSM_WIDTH + 2 * ATTN_WIDTH]
    v = p[:, SSM_WIDTH + 2 * ATTN_WIDTH:]
    if window_tiles is None:
        k_ref[...] = k
        v_ref[...] = v
    else:
        seq_tiles, first = window_tiles

        @pl.when(pl.program_id(0) % seq_tiles >= first)
        def _():
            k_ref[...] = k.T
            v_ref[...] = v.T
    qb_ref[...] = q.astype(BF16)
    kb_ref[...] = k.astype(BF16)
    vb_ref[...] = v.astype(BF16)
    if not dils:
        return
    scr = rest[-1]
    tiles = ATTN_WIDTH // LANES
    for a, val in enumerate((q, k, v)):
        for ct in range(tiles):
            scr[a * tiles + ct] = val[:, ct * LANES:(ct + 1) * LANES]
    for di, d in enumerate(dils):
        for a in range(3):
            out = rest[di * 3 + a]
            for r in range(d):
                for ct in range(tiles):
                    piece = scr[a * tiles + ct, pl.ds(r, tm // d, stride=d), :]
                    c0 = r * ATTN_WIDTH + ct * LANES
                    out[:, c0:c0 + LANES] = piece.astype(BF16)


def _in_proj(x2d, ln_mix, w_in_bf16, dils=(), seq_window=None):
    n = x2d.shape[0]
    tm = min(n, 512)
    row = lambda w: pl.BlockSpec((tm, w), lambda i: (i, 0))
    window_tiles = None
    kv_spec, kv_shape = row(ATTN_WIDTH), jax.ShapeDtypeStruct((n, ATTN_WIDTH), F32)
    if seq_window is not None:
        s_len, window = seq_window
        seq_tiles, first = s_len // tm, (s_len - window) // tm
        window_tiles = (seq_tiles, first)
        kv_spec = pl.BlockSpec(
            (None, ATTN_WIDTH, tm),
            lambda i: (i // seq_tiles, 0, jnp.maximum(i % seq_tiles - first, 0)))
        kv_shape = jax.ShapeDtypeStruct((n // s_len, ATTN_WIDTH, window), F32)
    out_specs = [row(SSM_WIDTH), kv_spec, kv_spec] + [row(ATTN_WIDTH)] * 3
    out_shape = ([jax.ShapeDtypeStruct((n, SSM_WIDTH), F32), kv_shape, kv_shape]
                 + [jax.ShapeDtypeStruct((n, ATTN_WIDTH), BF16)] * 3)
    for d in dils:
        out_specs += [pl.BlockSpec((tm // d, d * ATTN_WIDTH), lambda i: (i, 0))] * 3
        out_shape += [jax.ShapeDtypeStruct((n // d, d * ATTN_WIDTH), BF16)] * 3
    scratch = [pltpu.VMEM((3 * ATTN_WIDTH // LANES, tm, LANES), F32)] if dils else []
    return pl.pallas_call(
        functools.partial(_in_kernel, dils=tuple(dils), window_tiles=window_tiles),
        grid=(n // tm,),
        in_specs=[row(D_MODEL), _full((1, D_MODEL)), _full((D_MODEL, IN_WIDTH))],
        out_specs=out_specs,
        out_shape=out_shape,
        scratch_shapes=scratch,
        compiler_params=pltpu.CompilerParams(dimension_semantics=("arbitrary",),
                                             vmem_limit_bytes=VMEM_LIMIT),
        name="in_proj",
    )(x2d, ln_mix.reshape(1, D_MODEL), w_in_bf16)


def _s5_kernel(u_ref, bmat_ref, cmat_ref, are_ref, aim_ref, h0_ref, d_ref,
               y_ref, ht_ref, buf, hc, tmp, *, nb, tt, batched):
    rows = 2 * nb
    ntile = 2 * STATE_HALF // LANES
    half_tiles = ntile // 2

    def lane_tile(c):
        return slice(c * LANES, (c + 1) * LANES)

    @pl.when(pl.program_id(0) == 0)
    def _():
        hc[...] = h0_ref[...]

    if batched:
        u_all = u_ref[...].reshape(nb * tt, SSM_WIDTH)
        ub_all = u_all.astype(BF16)
        for hf in range(2):
            bu = jnp.dot(ub_all[:, hf * SSM_HALF:(hf + 1) * SSM_HALF], bmat_ref[hf],
                         preferred_element_type=F32)
            for c in range(ntile):
                tmp[c] = bu[:, lane_tile(c)]
            for c in range(ntile):
                for t in range(tt):
                    buf[c, t * rows + hf * nb:t * rows + (hf + 1) * nb, :] = tmp[c, pl.ds(t, nb, stride=tt), :]
    else:
        for b in range(nb):
            ub = u_ref[b].astype(BF16)
            for hf in range(2):
                bu = jnp.dot(ub[:, hf * SSM_HALF:(hf + 1) * SSM_HALF], bmat_ref[hf],
                             preferred_element_type=F32)
                for c in range(ntile):
                    buf[c, pl.ds(hf * nb + b, tt, stride=rows), :] = bu[:, lane_tile(c)]

    group = 4
    for s in range(rows // SUBLANES):
        r0 = s * SUBLANES
        for c0 in range(0, half_tiles, group):
            ar = [are_ref[r0:r0 + SUBLANES, lane_tile(c0 + k)] for k in range(group)]
            ai = [aim_ref[r0:r0 + SUBLANES, lane_tile(c0 + k)] for k in range(group)]
            init = tuple(hc[r0:r0 + SUBLANES, lane_tile(c0 + k)] for k in range(group)) + tuple(
                hc[r0:r0 + SUBLANES, lane_tile(half_tiles + c0 + k)] for k in range(group))

            def step(t, carry, r0=r0, c0=c0, ar=ar, ai=ai):
                row = pl.multiple_of(t * rows + r0, SUBLANES)
                out_r, out_i = [], []
                for k in range(group):
                    hr, hi = carry[k], carry[group + k]
                    xr = buf[c0 + k, pl.ds(row, SUBLANES), :]
                    xi = buf[half_tiles + c0 + k, pl.ds(row, SUBLANES), :]
                    nr = ar[k] * hr - ai[k] * hi + xr
                    ni = ar[k] * hi + ai[k] * hr + xi
                    buf[c0 + k, pl.ds(row, SUBLANES), :] = nr
                    buf[half_tiles + c0 + k, pl.ds(row, SUBLANES), :] = ni
                    out_r.append(nr)
                    out_i.append(ni)
                return tuple(out_r) + tuple(out_i)

            fin = lax.fori_loop(0, tt, step, init, unroll=min(tt, 8))
            for k in range(group):
                hc[r0:r0 + SUBLANES, lane_tile(c0 + k)] = fin[k]
                hc[r0:r0 + SUBLANES, lane_tile(half_tiles + c0 + k)] = fin[group + k]

    if batched:
        parts = []
        for hf in range(2):
            for c in range(ntile):
                for t in range(tt):
                    tmp[c, pl.ds(t, nb, stride=tt), :] = buf[c, t * rows + hf * nb:t * rows + (hf + 1) * nb, :]
            hs = jnp.concatenate([tmp[c] for c in range(ntile)], axis=1).astype(BF16)
            parts.append(jnp.dot(hs, cmat_ref[hf], preferred_element_type=F32))
        y_all = jnp.concatenate(parts, axis=1) + d_ref[...] * u_all
        y_ref[...] = y_all.reshape(nb, tt, SSM_WIDTH)
    else:
        for b in range(nb):
            parts = []
            for hf in range(2):
                hs = jnp.concatenate(
                    [buf[c, pl.ds(hf * nb + b, tt, stride=rows), :] for c in range(ntile)],
                    axis=1).astype(BF16)
                parts.append(jnp.dot(hs, cmat_ref[hf], preferred_element_type=F32))
            y_ref[b] = jnp.concatenate(parts, axis=1) + d_ref[...] * u_ref[b]

    ht_ref[...] = hc[...]


def _s5(u3, h0, bmat, cmat, a_re, a_im, d_skip):
    nb, t_len, _ = u3.shape
    tt = min(t_len, 256)
    rows = 2 * nb
    batched = tt < 16
    kern = functools.partial(_s5_kernel, nb=nb, tt=tt, batched=batched)
    ntile = 2 * STATE_HALF // LANES
    tmp_shape = (ntile, nb * tt, LANES) if batched else (1, SUBLANES, LANES)
    return pl.pallas_call(
        kern,
        grid=(t_len // tt,),
        in_specs=[pl.BlockSpec((nb, tt, SSM_WIDTH), lambda i: (0, i, 0)),
                  _full((2, SSM_HALF, 2 * STATE_HALF)),
                  _full((2, 2 * STATE_HALF, SSM_HALF)),
                  _full((rows, STATE_HALF)), _full((rows, STATE_HALF)),
                  _full((rows, 2 * STATE_HALF)), _full((1, SSM_WIDTH))],
        out_specs=[pl.BlockSpec((nb, tt, SSM_WIDTH), lambda i: (0, i, 0)),
                   _full((rows, 2 * STATE_HALF))],
        out_shape=[jax.ShapeDtypeStruct((nb, t_len, SSM_WIDTH), F32),
                   jax.ShapeDtypeStruct((rows, 2 * STATE_HALF), F32)],
        scratch_shapes=[pltpu.VMEM((ntile, tt * rows, LANES), F32),
                        pltpu.VMEM((rows, 2 * STATE_HALF), F32),
                        pltpu.VMEM(tmp_shape, F32)],
        compiler_params=pltpu.CompilerParams(dimension_semantics=("arbitrary",),
                                             vmem_limit_bytes=VMEM_LIMIT),
        name="s5_scan",
    )(u3, bmat, cmat, a_re, a_im, h0, d_skip.reshape(1, SSM_WIDTH))


def _s5_params(a_re, a_im, b_re, b_im, c_re, c_im, log_dt):
    dt = jnp.exp(log_dt)[:, None]
    mag = jnp.exp(dt * a_re)
    ang = dt * a_im
    ab_re, ab_im = mag * jnp.cos(ang), mag * jnp.sin(ang)
    den = a_re * a_re + a_im * a_im
    nr, ni = ab_re - 1.0, ab_im
    f_re = (nr * a_re + ni * a_im) / den
    f_im = (ni * a_re - nr * a_im) / den
    bb_re = f_re[..., None] * b_re - f_im[..., None] * b_im
    bb_im = f_re[..., None] * b_im + f_im[..., None] * b_re
    gh = SSM_GROUPS // 2
    eye = jnp.eye(gh, dtype=F32)

    def b_half(w):
        return jnp.einsum('gnc,gh->gchn', w, eye).reshape(gh * SSM_GROUP, gh * SSM_STATE)

    def c_half(w):
        return jnp.einsum('gcn,gh->gnhc', w, eye).reshape(gh * SSM_STATE, gh * SSM_GROUP)

    bmat = jnp.stack([jnp.concatenate([b_half(bb_re[h * gh:(h + 1) * gh]),
                                       b_half(bb_im[h * gh:(h + 1) * gh])], axis=1)
                      for h in range(2)]).astype(BF16)
    cmat = jnp.stack([jnp.concatenate([c_half(c_re[h * gh:(h + 1) * gh]),
                                       -c_half(c_im[h * gh:(h + 1) * gh])], axis=0)
                      for h in range(2)]).astype(BF16)
    return bmat, cmat, ab_re.reshape(2, 1, STATE_HALF), ab_im.reshape(2, 1, STATE_HALF)


def _state_to_rows(h_re, h_im):
    nb = h_re.shape[0]
    f = lambda h: h.reshape(nb, 2, STATE_HALF).transpose(1, 0, 2).reshape(2 * nb, STATE_HALF)
    return jnp.concatenate([f(h_re), f(h_im)], axis=1)


def _rows_to_state(ht, nb):
    f = lambda h: h.reshape(2, nb, STATE_HALF).transpose(1, 0, 2).reshape(nb, SSM_GROUPS, SSM_STATE)
    return f(ht[:, :STATE_HALF]), f(ht[:, STATE_HALF:])


def _attn_prompt_kernel(q_ref, kp_ref, kc_ref, vp_ref, vc_ref, o_ref, l_ref, bias_s, *, dil):
    blk = ATTN_BLOCK
    n = pl.program_id(2)

    @pl.when((pl.program_id(0) == 0) & (pl.program_id(1) == 0) & (n == 0))
    def _():
        i_idx = lax.broadcasted_iota(I32, (blk, 2 * blk), 0)
        j_idx = lax.broadcasted_iota(I32, (blk, 2 * blk), 1)
        delta = i_idx - j_idx + blk
        in_band = (delta >= 0) & (delta <= blk)
        dist = (delta * dil).astype(F32)
        for h in range(N_HEADS):
            biased = -ALIBI_SLOPES[h] * dist
            bias_s[0, h] = jnp.where(in_band & (j_idx >= blk), biased, MASK_VALUE)
            bias_s[1, h] = jnp.where(in_band, biased, MASK_VALUE)

    which = jnp.minimum(n, 1)
    lane = lax.broadcasted_iota(I32, (blk, LANES), 1)
    lse_all = jnp.zeros((blk, LANES), F32)
    for hp in range(N_HEADS // 2):
        cols = slice(hp * LANES, (hp + 1) * LANES)
        q2 = q_ref[:, cols]
        kk = jnp.concatenate([kp_ref[:, cols], kc_ref[:, cols]], axis=0)
        vv = jnp.concatenate([vp_ref[:, cols], vc_ref[:, cols]], axis=0)
        outs = []
        for half in range(2):
            h = 2 * hp + half
            in_head = (lane >= half * HEAD_DIM) & (lane < (half + 1) * HEAD_DIM)
            qm = jnp.where(in_head, q2, jnp.zeros_like(q2))
            s = lax.dot_general(qm, kk, (((1,), (1,)), ((), ())), preferred_element_type=F32)
            s = s + bias_s[which, h]
            m = jnp.max(s, axis=1, keepdims=True)
            p = jnp.exp(s - m)
            l = jnp.sum(p, axis=1, keepdims=True)
            outs.append(jnp.dot(p.astype(BF16), vv, preferred_element_type=F32) / l)
            lse_all = jnp.where(lane == h, m + jnp.log(l), lse_all)
        o_ref[:, cols] = jnp.where(lane < HEAD_DIM, outs[0], outs[1])
    l_ref[...] = lse_all


def _attn_prompt_branch(qb, kb, vb, bsz, s_len, dil):
    sub = s_len // dil
    nblk = sub // ATTN_BLOCK
    view = lambda t: t.reshape(bsz, sub, dil * ATTN_WIDTH)
    cur = pl.BlockSpec((None, ATTN_BLOCK, ATTN_WIDTH), lambda b, r, n: (b, n, r))
    prev = pl.BlockSpec((None, ATTN_BLOCK, ATTN_WIDTH), lambda b, r, n: (b, jnp.maximum(n - 1, 0), r))
    o, lse = pl.pallas_call(
        functools.partial(_attn_prompt_kernel, dil=dil),
        grid=(bsz, dil, nblk),
        in_specs=[cur, prev, cur, prev, cur],
        out_specs=[cur, pl.BlockSpec((None, ATTN_BLOCK, LANES), lambda b, r, n: (b, n, r))],
        out_shape=[jax.ShapeDtypeStruct((bsz, sub, dil * ATTN_WIDTH), F32),
                   jax.ShapeDtypeStruct((bsz, sub, dil * LANES), F32)],
        scratch_shapes=[pltpu.VMEM((2, N_HEADS, ATTN_BLOCK, 2 * ATTN_BLOCK), F32)],
        compiler_params=pltpu.CompilerParams(
            dimension_semantics=("arbitrary", "arbitrary", "arbitrary"),
            vmem_limit_bytes=VMEM_LIMIT),
        name=f"attn_prompt_d{dil}",
    )(view(qb), view(kb), view(kb), view(vb), view(vb))
    return o.reshape(bsz * sub, dil * ATTN_WIDTH), lse.reshape(bsz * sub, dil * LANES)


def _attn_sample_kernel(q_ref, kn_ref, vn_ref, kc_ref, vc_ref, o_ref, bias_s, mult_s, *, t_len, wb):
    nrow = N_HEADS * t_len
    t_shift = t_len.bit_length() - 1
    d_shift = HEAD_DIM.bit_length() - 1
    nt = (((1,), (1,)), ((), ()))

    def branch_count(dist):
        mult = jnp.zeros(dist.shape, F32)
        for win, dil in zip(WINDOWS, DILATIONS):
            hit = (dist >= 0) & (dist <= win) & ((dist & (dil - 1)) == 0)
            mult = mult + jnp.where(hit, 1.0, 0.0)
        return mult

    def biased(dist, mult):
        head = lax.broadcasted_iota(I32, dist.shape, 0) >> t_shift
        slope = jnp.zeros(dist.shape, F32)
        for h in range(N_HEADS):
            slope = jnp.where(head == h, ALIBI_SLOPES[h], slope)
        return jnp.where(mult > 0.0, -slope * dist.astype(F32), MASK_VALUE)

    @pl.when(pl.program_id(0) == 0)
    def _():
        row = lax.broadcasted_iota(I32, (nrow, wb), 0)
        col = lax.broadcasted_iota(I32, (nrow, wb), 1)
        dist = wb + (row & (t_len - 1)) - col
        mult = branch_count(dist)
        mult_s[...] = mult
        bias_s[...] = biased(dist, mult)

    q = q_ref[...].astype(F32)
    qt = jnp.concatenate([q] * N_HEADS, axis=0)
    row_w = lax.broadcasted_iota(I32, (nrow, ATTN_WIDTH), 0)
    lane_w = lax.broadcasted_iota(I32, (nrow, ATTN_WIDTH), 1)
    qm = jnp.where((lane_w >> d_shift) == (row_w >> t_shift), qt, 0.0).astype(BF16)

    pad = LANES - t_len
    kn = jnp.concatenate([kn_ref[...].astype(F32), jnp.zeros((pad, ATTN_WIDTH), F32)], axis=0).astype(BF16)
    vn = jnp.concatenate([vn_ref[...].astype(F32), jnp.zeros((pad, ATTN_WIDTH), F32)], axis=0).astype(BF16)
    row_n = lax.broadcasted_iota(I32, (nrow, LANES), 0)
    col_n = lax.broadcasted_iota(I32, (nrow, LANES), 1)
    dist_n = jnp.where(col_n < t_len, (row_n & (t_len - 1)) - col_n, -1)
    mult_n = branch_count(dist_n)
    s_n = lax.dot_general(qm, kn, nt, preferred_element_type=F32) + biased(dist_n, mult_n)
    s_c = jnp.dot(qm, kc_ref[...].astype(BF16), preferred_element_type=F32) + bias_s[...]

    m = jnp.maximum(jnp.max(s_c, axis=1, keepdims=True), jnp.max(s_n, axis=1, keepdims=True))
    p_c = jnp.exp(s_c - m) * mult_s[...]
    p_n = jnp.exp(s_n - m) * mult_n
    l = jnp.sum(p_c, axis=1, keepdims=True) + jnp.sum(p_n, axis=1, keepdims=True)
    o = (lax.dot_general(p_c.astype(BF16), vc_ref[...].astype(BF16), nt, preferred_element_type=F32)
         + jnp.dot(p_n.astype(BF16), vn, preferred_element_type=F32)) / l
    lane_o = lax.broadcasted_iota(I32, (t_len, ATTN_WIDTH), 1) >> d_shift
    out = jnp.zeros((t_len, ATTN_WIDTH), F32)
    for h in range(N_HEADS):
        out = jnp.where(lane_o == h, o[h * t_len:(h + 1) * t_len], out)
    o_ref[...] = out


def _attn_sample(qb, kb, vb, cache_k, cache_v):
    bsz, t_len, _ = qb.shape
    wb = cache_k.shape[1]
    feature_major = lambda c: jnp.transpose(c, (0, 2, 3, 1)).reshape(bsz, ATTN_WIDTH, wb)
    new = pl.BlockSpec((None, t_len, ATTN_WIDTH), lambda b: (b, 0, 0))
    old = pl.BlockSpec((None, ATTN_WIDTH, wb), lambda b: (b, 0, 0))
    return pl.pallas_call(
        functools.partial(_attn_sample_kernel, t_len=t_len, wb=wb),
        grid=(bsz,),
        in_specs=[new, new, new, old, old],
        out_specs=new,
        out_shape=jax.ShapeDtypeStruct((bsz, t_len, ATTN_WIDTH), F32),
        scratch_shapes=[pltpu.VMEM((N_HEADS * t_len, wb), F32),
                        pltpu.VMEM((N_HEADS * t_len, wb), F32)],
        compiler_params=pltpu.CompilerParams(dimension_semantics=("arbitrary",),
                                             vmem_limit_bytes=VMEM_LIMIT),
        name="attn_sample",
    )(qb, kb, vb, feature_major(cache_k), feature_major(cache_v))


def _mid_kernel(*refs, dils):
    n_branch = max(len(dils), 1)
    x_ref, y_ref = refs[0], refs[1]
    o_refs = refs[2:2 + n_branch]
    pos = 2 + n_branch
    l_refs = refs[pos:pos + len(dils)]
    pos += len(l_refs)
    (wglu_ref, bglu_ref, lns_ref, lna_ref, wout_ref, lnm_ref, wrh_ref, wrl_ref, br_ref,
     expand_ref, tri_ref,
     x1_ref, hrow_ref, eidx_ref, epos_ref, egate_ref, cnt_ref, carry, nat) = refs[pos:]
    tm = x_ref.shape[0]
    o_tiles = ATTN_WIDTH // LANES

    def natural(bi, d):
        if d == 1:
            return o_refs[bi][...], l_refs[bi][...]
        for r in range(d):
            rows = pl.ds(r, tm // d, stride=d)
            for ct in range(o_tiles):
                c0 = r * ATTN_WIDTH + ct * LANES
                nat[bi, ct, rows, :] = o_refs[bi][:, c0:c0 + LANES]
            nat[bi, o_tiles, rows, :] = l_refs[bi][:, r * LANES:(r + 1) * LANES]
        return (jnp.concatenate([nat[bi, ct] for ct in range(o_tiles)], axis=1), nat[bi, o_tiles])

    @pl.when(pl.program_id(0) == 0)
    def _():
        carry[...] = jnp.zeros_like(carry)

    y = y_ref[...]
    z = y * (0.5 * (1.0 + jnp.tanh(math.sqrt(2.0 / math.pi) * (y + 0.044715 * (y * y * y)))))
    glu = z * _sigmoid(jnp.dot(z.astype(BF16), wglu_ref[...], preferred_element_type=F32) + bglu_ref[...])
    n_ssm = _rms(glu, lns_ref[...])

    if not dils:
        attn = o_refs[0][...]
    else:
        pairs = [natural(bi, d) for bi, d in enumerate(dils)]
        lses = [p[1] for p in pairs]
        mx = functools.reduce(jnp.maximum, lses)
        es = [jnp.exp(l - mx) for l in lses]
        inv = 1.0 / functools.reduce(lambda a, b: a + b, es)
        attn = jnp.zeros((tm, ATTN_WIDTH), F32)
        for e, (o_nat, _) in zip(es, pairs):
            w = e * inv
            w_hi = w.astype(BF16)
            w_lo = (w - w_hi.astype(F32)).astype(BF16)
            wide = (jnp.dot(w_hi, expand_ref[...], preferred_element_type=F32)
                    + jnp.dot(w_lo, expand_ref[...], preferred_element_type=F32))
            attn = attn + wide * o_nat
    n_attn = _rms(attn, lna_ref[...])

    x1 = (x_ref[...]
          + jnp.dot(n_ssm.astype(BF16), wout_ref[:SSM_WIDTH, :], preferred_element_type=F32)
          + jnp.dot(n_attn.astype(BF16), wout_ref[SSM_WIDTH:, :], preferred_element_type=F32))
    x1_ref[...] = x1
    hm = _rms(x1, lnm_ref[...])
    for s in range(ROW_TILES):
        hrow_ref[pl.ds(s, tm, stride=ROW_TILES), :] = hm[:, s * LANES:(s + 1) * LANES]

    h_hi = hm.astype(BF16)
    h_lo = (hm - h_hi.astype(F32)).astype(BF16)
    logits = (jnp.dot(h_hi, wrh_ref[...], preferred_element_type=F32)
              + jnp.dot(h_lo, wrh_ref[...], preferred_element_type=F32)
              + jnp.dot(h_hi, wrl_ref[...], preferred_element_type=F32)
              + br_ref[...])
    lane = lax.broadcasted_iota(I32, (tm, LANES), 1)
    lane_f = lane.astype(F32)
    work = jnp.where(lane < N_EXPERTS, logits, NEG_BIG)
    vals, idxs, hots = [], [], []
    for _ in range(TOP_K):
        m = jnp.max(work, axis=1, keepdims=True)
        idx = jnp.min(jnp.where(work == m, lane_f, float(LANES)), axis=1, keepdims=True)
        hot = lane_f == idx
        vals.append(m)
        idxs.append(idx)
        hots.append(hot)
        work = jnp.where(hot, NEG_BIG, work)
    exps = [jnp.exp(v - vals[0]) for v in vals]
    inv = 1.0 / functools.reduce(lambda a, b: a + b, exps)

    sel = functools.reduce(lambda a, b: a + b, [h.astype(F32) for h in hots])
    before = jnp.dot(tri_ref[...], sel.astype(BF16), preferred_element_type=F32) + carry[0:1, :]
    eidx = jnp.zeros((tm, LANES), I32)
    epos = jnp.zeros((tm, LANES), I32)
    egate = jnp.zeros((tm, LANES), F32)
    for k in range(TOP_K):
        pk = jnp.sum(jnp.where(hots[k], before, 0.0), axis=1, keepdims=True)
        eidx = jnp.where(lane == k, idxs[k].astype(I32), eidx)
        epos = jnp.where(lane == k, pk.astype(I32), epos)
        egate = jnp.where(lane == k, exps[k] * inv, egate)
    eidx_ref[...] = eidx
    epos_ref[...] = epos
    egate_ref[...] = egate
    total = carry[0:1, :] + jnp.sum(sel, axis=0, keepdims=True)
    carry[...] = jnp.broadcast_to(total, carry.shape)
    cnt_ref[...] = jnp.broadcast_to(total, cnt_ref.shape).astype(I32)


def _mid(x2d, y2d, attn_o, attn_lse, dils, w):
    n = x2d.shape[0]
    tm = TOKEN_TILE
    n_branch = len(attn_o)
    row = lambda width: pl.BlockSpec((tm, width), lambda i: (i, 0))
    packed = lambda d, width: pl.BlockSpec((tm // d, d * width), lambda i: (i, 0))
    attn_specs = ([packed(d, ATTN_WIDTH) for d in dils] + [packed(d, LANES) for d in dils]
                  if dils else [row(ATTN_WIDTH)])
    in_specs = ([row(D_MODEL), row(SSM_WIDTH)] + attn_specs
                + [_full((SSM_WIDTH, SSM_WIDTH)), _full((1, SSM_WIDTH)), _full((1, SSM_WIDTH)),
                   _full((1, ATTN_WIDTH)), _full((D_MODEL, D_MODEL)), _full((1, D_MODEL)),
                   _full((D_MODEL, LANES)), _full((D_MODEL, LANES)), _full((1, LANES)),
                   _full((LANES, ATTN_WIDTH)), _full((tm, tm))])
    out_specs = [row(D_MODEL), pl.BlockSpec((tm * ROW_TILES, LANES), lambda i: (i, 0)),
                 row(LANES), row(LANES), row(LANES), _full((SUBLANES, LANES))]
    out_shape = [jax.ShapeDtypeStruct((n, D_MODEL), F32),
                 jax.ShapeDtypeStruct((n * ROW_TILES, LANES), F32),
                 jax.ShapeDtypeStruct((n, LANES), I32),
                 jax.ShapeDtypeStruct((n, LANES), I32),
                 jax.ShapeDtypeStruct((n, LANES), F32),
                 jax.ShapeDtypeStruct((SUBLANES, LANES), I32)]
    return pl.pallas_call(
        functools.partial(_mid_kernel, dils=tuple(dils)),
        grid=(n // tm,),
        in_specs=in_specs,
        out_specs=out_specs,
        out_shape=out_shape,
        scratch_shapes=[pltpu.VMEM((SUBLANES, LANES), F32),
                        pltpu.VMEM((n_branch, ATTN_WIDTH // LANES + 1, tm, LANES), F32)],
        compiler_params=pltpu.CompilerParams(dimension_semantics=("arbitrary",),
                                             vmem_limit_bytes=VMEM_LIMIT),
        name="mid",
    )(x2d, y2d, *attn_o, *attn_lse, w["w_glu"], w["b_glu"], w["ln_ssm_out"], w["ln_attn_out"],
      w["w_out"], w["ln_moe"], w["wr_hi"], w["wr_lo"], w["b_router"], w["expand"], w["tri"])


def _dispatch_kernel(dest_ref, hp_ref, hs_ref, xs_hbm, sem, *, n_prompt_tiles):
    i = pl.program_id(0)
    npair = TOKEN_TILE * TOP_K

    def run(src_ref):
        def issue(t, c):
            src = src_ref.at[pl.ds(pl.multiple_of(t * ROW_TILES, ROW_TILES), ROW_TILES), :]
            for k in range(TOP_K):
                d = dest_ref[0, t * TOP_K + k]
                pltpu.make_async_copy(
                    src, xs_hbm.at[pl.ds(pl.multiple_of(d * ROW_TILES, ROW_TILES), ROW_TILES), :],
                    sem).start()
            return c

        lax.fori_loop(0, TOKEN_TILE, issue, 0, unroll=2)
        span = pl.ds(0, npair * ROW_TILES)
        pltpu.make_async_copy(xs_hbm.at[span, :], xs_hbm.at[span, :], sem).wait()

    @pl.when(i < n_prompt_tiles)
    def _():
        run(hp_ref)

    @pl.when(i >= n_prompt_tiles)
    def _():
        run(hs_ref)


def _dispatch(dest, hrow_p, hrow_s):
    n_p = hrow_p.shape[0] // ROW_TILES
    n_s = hrow_s.shape[0] // ROW_TILES
    npt = n_p // TOKEN_TILE
    ntile = (n_p + n_s) // TOKEN_TILE
    npair = TOKEN_TILE * TOP_K
    blk = (TOKEN_TILE * ROW_TILES, LANES)
    return pl.pallas_call(
        functools.partial(_dispatch_kernel, n_prompt_tiles=npt),
        grid=(ntile,),
        in_specs=[pl.BlockSpec((None, 1, npair), lambda i: (i, 0, 0), memory_space=pltpu.SMEM),
                  pl.BlockSpec(blk, lambda i: (jnp.minimum(i, npt - 1), 0)),
                  pl.BlockSpec(blk, lambda i: (jnp.maximum(i - npt, 0), 0))],
        out_specs=pl.BlockSpec(memory_space=pl.ANY),
        out_shape=jax.ShapeDtypeStruct(((n_p + n_s) * TOP_K * ROW_TILES, LANES), F32),
        scratch_shapes=[pltpu.SemaphoreType.DMA(())],
        compiler_params=pltpu.CompilerParams(dimension_semantics=("arbitrary",),
                                             vmem_limit_bytes=VMEM_LIMIT),
        name="moe_dispatch",
    )(dest.reshape(ntile, 1, npair), hrow_p, hrow_s)


def _expert_kernel(vt_ref, ve_ref, vok_ref, gs_ref, xs_ref, wu_ref, bu_ref, wd_ref, bd_ref,
                   out_ref, wu_s, wd_s, x_s):
    v = pl.program_id(0)
    e = ve_ref[v]
    j = vt_ref[v]
    vprev = jnp.maximum(v - 1, 0)
    new_e = (v == 0) | (e != ve_ref[vprev])
    new_j = (v == 0) | (j != vt_ref[vprev])
    tm = MOE_TILE

    @pl.when(new_e)
    def _():
        wu_s[...] = wu_ref[...].astype(BF16)
        wd_s[...] = wd_ref[...].astype(BF16)

    @pl.when(new_j)
    def _():
        out_ref[...] = jnp.zeros_like(out_ref)

    sub = x_s.shape[0]
    for part in range(tm // sub):
        base = part * sub * ROW_TILES
        first_row = j * tm + part * sub

        @pl.when((vok_ref[v] == 1) & (gs_ref[e] < first_row + sub) & (gs_ref[e + 1] > first_row))
        def _(part=part, base=base):
            for s in range(ROW_TILES):
                x_s[:, s * LANES:(s + 1) * LANES] = xs_ref[
                    pl.ds(base + s, sub, stride=ROW_TILES), :].astype(BF16)
            a = jnp.dot(x_s[...], wu_s[...], preferred_element_type=F32) + bu_ref[...]
            g = jnp.minimum(a[:, :EXPERT_FF], SWIGLU_LIMIT)
            lin = jnp.clip(a[:, EXPERT_FF:], -SWIGLU_LIMIT, SWIGLU_LIMIT)
            act = (lin + 1.0) * (g * _sigmoid(SWIGLU_ALPHA * g))
            y = jnp.dot(act.astype(BF16), wd_s[...], preferred_element_type=F32) + bd_ref[...]
            rows = j * tm + part * sub + lax.broadcasted_iota(I32, (sub, 1), 0)
            mine = (rows >= gs_ref[e]) & (rows < gs_ref[e + 1])
            for s in range(ROW_TILES):
                cur = out_ref[pl.ds(base + s, sub, stride=ROW_TILES), :]
                out_ref[pl.ds(base + s, sub, stride=ROW_TILES), :] = jnp.where(
                    mine, y[:, s * LANES:(s + 1) * LANES], cur)


def _experts(xs, vt, ve, vok, gstart, w_up, b_up, w_down, b_down):
    tm = MOE_TILE
    nvisit = vt.shape[0]
    rows = pl.BlockSpec((tm * ROW_TILES, LANES), lambda v, vt, ve, vok, gs: (vt[v], 0))
    per_e = lambda a, b: pl.BlockSpec((None, a, b), lambda v, vt, ve, vok, gs: (ve[v], 0, 0))
    grid_spec = pltpu.PrefetchScalarGridSpec(
        num_scalar_prefetch=4,
        grid=(nvisit,),
        in_specs=[rows, per_e(D_MODEL, 2 * EXPERT_FF), per_e(1, 2 * EXPERT_FF),
                  per_e(EXPERT_FF, D_MODEL), per_e(1, D_MODEL)],
        out_specs=rows,
        scratch_shapes=[pltpu.VMEM((D_MODEL, 2 * EXPERT_FF), BF16),
                        pltpu.VMEM((EXPERT_FF, D_MODEL), BF16),
                        pltpu.VMEM((MOE_SUBTILE, D_MODEL), BF16)],
    )
    return pl.pallas_call(
        _expert_kernel,
        grid_spec=grid_spec,
        out_shape=jax.ShapeDtypeStruct(xs.shape, F32),
        compiler_params=pltpu.CompilerParams(dimension_semantics=("arbitrary",),
                                             vmem_limit_bytes=VMEM_LIMIT),
        name="moe_experts",
    )(vt, ve, vok, gstart, xs, w_up, b_up.reshape(N_EXPERTS, 1, 2 * EXPERT_FF),
      w_down, b_down.reshape(N_EXPERTS, 1, D_MODEL))


def _routing(eidx_p, epos_p, cnt_p, eidx_s, epos_s, cnt_s, n_rows):
    cnt_p = cnt_p[0, :N_EXPERTS]
    cnt_s = cnt_s[0, :N_EXPERTS]
    cnt = cnt_p + cnt_s
    gend = jnp.cumsum(cnt)
    gstart = gend - cnt
    ep = eidx_p[:, :TOP_K]
    es = eidx_s[:, :TOP_K]
    dest_p = gstart[ep] + epos_p[:, :TOP_K]
    dest_s = gstart[es] + cnt_p[es] + epos_s[:, :TOP_K]
    ntile = n_rows // MOE_TILE
    nvisit = ntile + N_EXPERTS
    first = gstart // MOE_TILE
    last = jnp.maximum(gend - 1, 0) // MOE_TILE
    nv = jnp.where(cnt > 0, last - first + 1, 0)
    vend = jnp.cumsum(nv)
    vstart = vend - nv
    total = vend[-1]
    v = jnp.arange(nvisit, dtype=I32)
    vc = jnp.minimum(v, total - 1)
    ve = jnp.sum((vend[None, :] <= vc[:, None]).astype(I32), axis=1)
    vt = (first[ve] + vc - vstart[ve]).astype(I32)
    vok = (v < total).astype(I32)
    gs = jnp.concatenate([gstart, gend[-1:]]).astype(I32)
    return dest_p.astype(I32), dest_s.astype(I32), vt, ve, vok, gs


def _out_kernel(dest_ref, next_ref, y_hbm, x1_ref, gate_ref, pe_ref, lnp_ref, wg_ref, bg_ref, wp_ref,
                lnf_ref, o_ref, buf, sem, *, ntile):
    tm = TOKEN_TILE
    npair = tm * TOP_K
    i = pl.program_id(0)
    slot = i % 2

    def gather(idx_ref, to):
        def issue(t, c):
            for k in range(TOP_K):
                d = idx_ref[0, t * TOP_K + k]
                pltpu.make_async_copy(
                    y_hbm.at[pl.ds(pl.multiple_of(d * ROW_TILES, ROW_TILES), ROW_TILES), :],
                    buf.at[to, pl.ds(pl.multiple_of((k * tm + t) * ROW_TILES, ROW_TILES), ROW_TILES), :],
                    sem.at[to]).start()
            return c

        lax.fori_loop(0, tm, issue, 0, unroll=2)

    @pl.when(i == 0)
    def _():
        gather(dest_ref, 0)

    if ntile > 1:
        @pl.when(i + 1 < ntile)
        def _():
            gather(next_ref, 1 - slot)

    pltpu.make_async_copy(y_hbm.at[pl.ds(0, npair * ROW_TILES), :], buf.at[slot], sem.at[slot]).wait()

    gates = gate_ref[...]
    parts = []
    for s in range(ROW_TILES):
        acc = jnp.zeros((tm, LANES), F32)
        for k in range(TOP_K):
            rows = buf[slot, pl.ds(k * tm * ROW_TILES + s, tm, stride=ROW_TILES), :]
            acc = acc + gates[:, k:k + 1] * rows
        parts.append(acc)
    x2 = x1_ref[...] + jnp.concatenate(parts, axis=1)
    gate = _sigmoid(jnp.dot(_rms(x2, lnp_ref[...]).astype(BF16), wg_ref[...],
                            preferred_element_type=F32) + bg_ref[...])
    x3 = x2 + gate * jnp.dot(pe_ref[...].astype(BF16), wp_ref[...], preferred_element_type=F32)
    o_ref[...] = _rms(x3, lnf_ref[...])


def _combine(dest, y_rows, x1, egate, pe, w):
    n = x1.shape[0]
    tm = TOKEN_TILE
    npair = tm * TOP_K
    row = lambda width: pl.BlockSpec((tm, width), lambda i: (i, 0))
    ntile = n // tm
    dest3 = dest.reshape(ntile, 1, npair)
    return pl.pallas_call(
        functools.partial(_out_kernel, ntile=ntile),
        grid=(ntile,),
        in_specs=[pl.BlockSpec((None, 1, npair), lambda i: (i, 0, 0), memory_space=pltpu.SMEM),
                  pl.BlockSpec((None, 1, npair), lambda i: (jnp.minimum(i + 1, ntile - 1), 0, 0),
                               memory_space=pltpu.SMEM),
                  pl.BlockSpec(memory_space=pl.ANY),
                  row(D_MODEL), row(LANES), row(PLE_DIM),
                  _full((1, D_MODEL)), _full((D_MODEL, D_MODEL)), _full((1, D_MODEL)),
                  _full((PLE_DIM, D_MODEL)), _full((1, D_MODEL))],
        out_specs=row(D_MODEL),
        out_shape=jax.ShapeDtypeStruct((n, D_MODEL), F32),
        scratch_shapes=[pltpu.VMEM((2, npair * ROW_TILES, LANES), F32), pltpu.SemaphoreType.DMA((2,))],
        compiler_params=pltpu.CompilerParams(dimension_semantics=("arbitrary",),
                                             vmem_limit_bytes=VMEM_LIMIT),
        name="combine_out",
    )(dest3, dest3, y_rows, x1, egate, pe, w["ln_ple"], w["w_ple_gate"],
      w["b_ple_gate"], w["w_ple_proj"], w["ln_final"])


def kernel(x_prompt, x_sample, cache_attn_k, cache_attn_v, state_ssm_re, state_ssm_im, p_prompt, p_sample, ln_mix, w_in, ssm_a_re, ssm_a_im, ssm_b_re, ssm_b_im, ssm_c_re, ssm_c_im, ssm_d, ssm_log_dt, w_glu, b_glu, ln_ssm_out, ln_attn_out, w_out, ln_moe, w_router, b_router, w_up, b_up, w_down, b_down, ln_ple, w_ple_gate, b_ple_gate, w_ple_proj, ln_final):
    bsz, s_len, _ = x_prompt.shape
    dbsz, dt_len, _ = x_sample.shape
    n_p, n_s = bsz * s_len, dbsz * dt_len
    wb = cache_attn_k.shape[2]
    wb_prompt = min(WINDOWS[-1], s_len)

    wr = jnp.pad(w_router[0], ((0, 0), (0, LANES - N_EXPERTS)))
    wr_hi = wr.astype(BF16)
    ti = jnp.arange(TOKEN_TILE)
    w = {
        "w_glu": w_glu[0].astype(BF16), "b_glu": b_glu[0].reshape(1, -1),
        "ln_ssm_out": ln_ssm_out[0].reshape(1, -1), "ln_attn_out": ln_attn_out[0].reshape(1, -1),
        "w_out": w_out[0].astype(BF16), "ln_moe": ln_moe[0].reshape(1, -1),
        "wr_hi": wr_hi, "wr_lo": (wr - wr_hi.astype(F32)).astype(BF16),
        "b_router": jnp.pad(b_router[0], (0, LANES - N_EXPERTS)).reshape(1, -1),
        "expand": (jnp.arange(LANES)[:, None] == jnp.arange(ATTN_WIDTH)[None, :] // HEAD_DIM).astype(BF16),
        "tri": (ti[:, None] > ti[None, :]).astype(BF16),
        "ln_ple": ln_ple[0].reshape(1, -1), "w_ple_gate": w_ple_gate[0].astype(BF16),
        "b_ple_gate": b_ple_gate[0].reshape(1, -1), "w_ple_proj": w_ple_proj[0].astype(BF16),
        "ln_final": ln_final.reshape(1, -1),
    }
    w_in_b = w_in[0].astype(BF16)
    bmat, cmat, ab_re, ab_im = _s5_params(ssm_a_re[0], ssm_a_im[0], ssm_b_re[0], ssm_b_im[0],
                                          ssm_c_re[0], ssm_c_im[0], ssm_log_dt[0])

    def coeff(a, nb):
        return jnp.broadcast_to(a, (2, nb, STATE_HALF)).reshape(2 * nb, STATE_HALF)

    proj_p = _in_proj(x_prompt.reshape(n_p, D_MODEL), ln_mix[0], w_in_b, dils=DILATIONS[1:],
                      seq_window=(s_len, wb_prompt))
    u_p, k_p, v_p = proj_p[:3]
    qkv = [proj_p[3:6]] + [proj_p[6 + 3 * i:9 + 3 * i] for i in range(len(DILATIONS) - 1)]
    zeros_state = jnp.zeros((bsz, SSM_GROUPS, SSM_STATE), F32)
    y_p, ht_p = _s5(u_p.reshape(bsz, s_len, SSM_WIDTH), _state_to_rows(zeros_state, zeros_state),
                    bmat, cmat, coeff(ab_re, bsz), coeff(ab_im, bsz), ssm_d[0])
    branches = [_attn_prompt_branch(*qkv[i], bsz, s_len, d) for i, d in enumerate(DILATIONS)]
    x1_p, hrow_p, eidx_p, epos_p, egate_p, cnt_p = _mid(
        x_prompt.reshape(n_p, D_MODEL), y_p.reshape(n_p, SSM_WIDTH),
        [b[0] for b in branches], [b[1] for b in branches], DILATIONS, w)

    u_s, k_s, v_s, qb_s, kb_s, vb_s = _in_proj(x_sample.reshape(n_s, D_MODEL), ln_mix[0], w_in_b)
    y_s, ht_s = _s5(u_s.reshape(dbsz, dt_len, SSM_WIDTH), _state_to_rows(state_ssm_re[0], state_ssm_im[0]),
                    bmat, cmat, coeff(ab_re, dbsz), coeff(ab_im, dbsz), ssm_d[0])
    as3 = lambda t: t.reshape(dbsz, dt_len, ATTN_WIDTH)
    attn_s = _attn_sample(as3(qb_s), as3(kb_s), as3(vb_s), cache_attn_k[0], cache_attn_v[0])
    x1_s, hrow_s, eidx_s, epos_s, egate_s, cnt_s = _mid(
        x_sample.reshape(n_s, D_MODEL), y_s.reshape(n_s, SSM_WIDTH),
        [attn_s.reshape(n_s, ATTN_WIDTH)], [], (), w)

    n_rows = (n_p + n_s) * TOP_K
    dest_p, dest_s, vt, ve, vok, gs = _routing(eidx_p, epos_p, cnt_p, eidx_s, epos_s, cnt_s, n_rows)
    xs = _dispatch(jnp.concatenate([dest_p.reshape(-1), dest_s.reshape(-1)]), hrow_p, hrow_s)
    y_rows = _experts(xs, vt, ve, vok, gs, w_up[0], b_up[0], w_down[0], b_down[0])

    out_p = _combine(dest_p, y_rows, x1_p, egate_p, p_prompt[0].reshape(n_p, PLE_DIM), w)
    out_s = _combine(dest_s, y_rows, x1_s, egate_s, p_sample[0].reshape(n_s, PLE_DIM), w)

    hr_p, hi_p = _rows_to_state(ht_p, bsz)
    hr_s, hi_s = _rows_to_state(ht_s, dbsz)
    kv_p = lambda t: jnp.transpose(t.reshape(bsz, N_HEADS, HEAD_DIM, wb_prompt), (0, 3, 1, 2))[None]
    kv_s = lambda t: t.reshape(dbsz, dt_len, N_HEADS, HEAD_DIM)[None]
    return (out_p.reshape(bsz, s_len, D_MODEL), out_s.reshape(dbsz, dt_len, D_MODEL),
            kv_p(k_p), kv_p(v_p), hr_p[None], hi_p[None],
            kv_s(k_s), kv_s(v_s), hr_s[None], hi_s[None])
```

```python
import functools
import math

import jax
import jax.numpy as jnp
from jax import lax
from jax.experimental import pallas as pl
from jax.experimental.pallas import tpu as pltpu

F32 = jnp.float32
BF16 = jnp.bfloat16
I32 = jnp.int32

D_MODEL = 1024
SSM_WIDTH = 512
SSM_GROUP = 16
SSM_GROUPS = 32
SSM_STATE = 64
ATTN_WIDTH = 512
HEAD_DIM = 64
N_HEADS = 8
IN_WIDTH = SSM_WIDTH + 3 * ATTN_WIDTH
DILATIONS = (1, 4, 16)
WINDOWS = (128, 512, 2048)
ATTN_BLOCK = 128
N_EXPERTS = 32
TOP_K = 4
EXPERT_FF = D_MODEL
SWIGLU_LIMIT = 7.0
SWIGLU_ALPHA = 1.702
PLE_DIM = 256
EPS = 1e-6
MASK_VALUE = -1e30
NEG_BIG = -3.0e38

LANES = 128
SUBLANES = 8
ROW_TILES = D_MODEL // LANES
TOKEN_TILE = 256
MOE_TILE = 512
MOE_SUBTILE = 256
SSM_HALF = SSM_WIDTH // 2
STATE_HALF = SSM_GROUPS * SSM_STATE // 2
ALIBI_SLOPES = tuple(2.0 ** (-8.0 * (h + 1) / N_HEADS) for h in range(N_HEADS))
VMEM_LIMIT = 56 * 1024 * 1024


def _rms(x, g):
    return x * lax.rsqrt(jnp.mean(x * x, axis=-1, keepdims=True) + EPS) * g


def _sigmoid(x):
    return 1.0 / (1.0 + jnp.exp(-x))


def _full(shape):
    n = len(shape)
    return pl.BlockSpec(shape, lambda *_: (0,) * n)


def _in_kernel(x_ref, g_ref, w_ref, u_ref, k_ref, v_ref, qb_ref, kb_ref, vb_ref, *rest, dils, window_tiles):
    tm = x_ref.shape[0]
    h = _rms(x_ref[...], g_ref[...]).astype(BF16)
    p = jnp.dot(h, w_ref[...], preferred_element_type=F32)
    u_ref[...] = p[:, :SSM_WIDTH]
    q = p[:, SSM_WIDTH:SSM_WIDTH + ATTN_WIDTH] * (HEAD_DIM ** -0.5)
    k = p[:, SSM_WIDTH + ATTN_WIDTH:SSM_WIDTH + 2 * ATTN_WIDTH]
    v = p[:, SSM_WIDTH + 2 * ATTN_WIDTH:]
    if window_tiles is None:
        k_ref[...] = k
        v_ref[...] = v
    else:
        seq_tiles, first = window_tiles

        @pl.when(pl.program_id(0) % seq_tiles >= first)
        def _():
            k_ref[...] = k.T
            v_ref[...] = v.T
    qb_ref[...] = q.astype(BF16)
    kb_ref[...] = k.astype(BF16)
    vb_ref[...] = v.astype(BF16)
    if not dils:
        return
    scr = rest[-1]
    tiles = ATTN_WIDTH // LANES
    for a, val in enumerate((q, k, v)):
        for ct in range(tiles):
            scr[a * tiles + ct] = val[:, ct * LANES:(ct + 1) * LANES]
    for di, d in enumerate(dils):
        for a in range(3):
            out = rest[di * 3 + a]
            for r in range(d):
                for ct in range(tiles):
                    piece = scr[a * tiles + ct, pl.ds(r, tm // d, stride=d), :]
                    c0 = r * ATTN_WIDTH + ct * LANES
                    out[:, c0:c0 + LANES] = piece.astype(BF16)


def _in_proj(x2d, ln_mix, w_in_bf16, dils=(), seq_window=None):
    n = x2d.shape[0]
    tm = min(n, 512)
    row = lambda w: pl.BlockSpec((tm, w), lambda i: (i, 0))
    window_tiles = None
    kv_spec, kv_shape = row(ATTN_WIDTH), jax.ShapeDtypeStruct((n, ATTN_WIDTH), F32)
    if seq_window is not None:
        s_len, window = seq_window
        seq_tiles, first = s_len // tm, (s_len - window) // tm
        window_tiles = (seq_tiles, first)
        kv_spec = pl.BlockSpec(
            (None, ATTN_WIDTH, tm),
            lambda i: (i // seq_tiles, 0, jnp.maximum(i % seq_tiles - first, 0)))
        kv_shape = jax.ShapeDtypeStruct((n // s_len, ATTN_WIDTH, window), F32)
    out_specs = [row(SSM_WIDTH), kv_spec, kv_spec] + [row(ATTN_WIDTH)] * 3
    out_shape = ([jax.ShapeDtypeStruct((n, SSM_WIDTH), F32), kv_shape, kv_shape]
                 + [jax.ShapeDtypeStruct((n, ATTN_WIDTH), BF16)] * 3)
    for d in dils:
        out_specs += [pl.BlockSpec((tm // d, d * ATTN_WIDTH), lambda i: (i, 0))] * 3
        out_shape += [jax.ShapeDtypeStruct((n // d, d * ATTN_WIDTH), BF16)] * 3
    scratch = [pltpu.VMEM((3 * ATTN_WIDTH // LANES, tm, LANES), F32)] if dils else []
    return pl.pallas_call(
        functools.partial(_in_kernel, dils=tuple(dils), window_tiles=window_tiles),
        grid=(n // tm,),
        in_specs=[row(D_MODEL), _full((1, D_MODEL)), _full((D_MODEL, IN_WIDTH))],
        out_specs=out_specs,
        out_shape=out_shape,
        scratch_shapes=scratch,
        compiler_params=pltpu.CompilerParams(dimension_semantics=("arbitrary",),
                                             vmem_limit_bytes=VMEM_LIMIT),
        name="in_proj",
    )(x2d, ln_mix.reshape(1, D_MODEL), w_in_bf16)


def _s5_kernel(u_ref, bmat_ref, cmat_ref, are_ref, aim_ref, h0_ref, d_ref,
               y_ref, ht_ref, buf, hc, tmp, *, nb, tt, batched):
    rows = 2 * nb
    ntile = 2 * STATE_HALF // LANES
    half_tiles = ntile // 2

    def lane_tile(c):
        return slice(c * LANES, (c + 1) * LANES)

    @pl.when(pl.program_id(0) == 0)
    def _():
        hc[...] = h0_ref[...]

    if batched:
        u_all = u_ref[...].reshape(nb * tt, SSM_WIDTH)
        ub_all = u_all.astype(BF16)
        for hf in range(2):
            bu = jnp.dot(ub_all[:, hf * SSM_HALF:(hf + 1) * SSM_HALF], bmat_ref[hf],
                         preferred_element_type=F32)
            for c in range(ntile):
                tmp[c] = bu[:, lane_tile(c)]
            for c in range(ntile):
                for t in range(tt):
                    buf[c, t * rows + hf * nb:t * rows + (hf + 1) * nb, :] = tmp[c, pl.ds(t, nb, stride=tt), :]
    else:
        for b in range(nb):
            ub = u_ref[b].astype(BF16)
            for hf in range(2):
                bu = jnp.dot(ub[:, hf * SSM_HALF:(hf + 1) * SSM_HALF], bmat_ref[hf],
                             preferred_element_type=F32)
                for c in range(ntile):
                    buf[c, pl.ds(hf * nb + b, tt, stride=rows), :] = bu[:, lane_tile(c)]

    group = 8
    for s in range(rows // SUBLANES):
        r0 = s * SUBLANES
        for c0 in range(0, half_tiles, group):
            ar = [are_ref[r0:r0 + SUBLANES, lane_tile(c0 + k)] for k in range(group)]
            ai = [aim_ref[r0:r0 + SUBLANES, lane_tile(c0 + k)] for k in range(group)]
            init = tuple(hc[r0:r0 + SUBLANES, lane_tile(c0 + k)] for k in range(group)) + tuple(
                hc[r0:r0 + SUBLANES, lane_tile(half_tiles + c0 + k)] for k in range(group))

            def step(t, carry, r0=r0, c0=c0, ar=ar, ai=ai):
                row = pl.multiple_of(t * rows + r0, SUBLANES)
                out_r, out_i = [], []
                for k in range(group):
                    hr, hi = carry[k], carry[group + k]
                    xr = buf[c0 + k, pl.ds(row, SUBLANES), :]
                    xi = buf[half_tiles + c0 + k, pl.ds(row, SUBLANES), :]
                    nr = ar[k] * hr - ai[k] * hi + xr
                    ni = ar[k] * hi + ai[k] * hr + xi
                    buf[c0 + k, pl.ds(row, SUBLANES), :] = nr
                    buf[half_tiles + c0 + k, pl.ds(row, SUBLANES), :] = ni
                    out_r.append(nr)
                    out_i.append(ni)
                return tuple(out_r) + tuple(out_i)

            fin = lax.fori_loop(0, tt, step, init, unroll=min(tt, 8))
            for k in range(group):
                hc[r0:r0 + SUBLANES, lane_tile(c0 + k)] = fin[k]
                hc[r0:r0 + SUBLANES, lane_tile(half_tiles + c0 + k)] = fin[group + k]

    if batched:
        parts = []
        for hf in range(2):
            for c in range(ntile):
                for t in range(tt):
                    tmp[c, pl.ds(t, nb, stride=tt), :] = buf[c, t * rows + hf * nb:t * rows + (hf + 1) * nb, :]
            hs = jnp.concatenate([tmp[c] for c in range(ntile)], axis=1).astype(BF16)
            parts.append(jnp.dot(hs, cmat_ref[hf], preferred_element_type=F32))
        y_all = jnp.concatenate(parts, axis=1) + d_ref[...] * u_all
        y_ref[...] = y_all.reshape(nb, tt, SSM_WIDTH)
    else:
        for b in range(nb):
            parts = []
            for hf in range(2):
                hs = jnp.concatenate(
                    [buf[c, pl.ds(hf * nb + b, tt, stride=rows), :] for c in range(ntile)],
                    axis=1).astype(BF16)
                parts.append(jnp.dot(hs, cmat_ref[hf], preferred_element_type=F32))
            y_ref[b] = jnp.concatenate(parts, axis=1) + d_ref[...] * u_ref[b]

    ht_ref[...] = hc[...]


def _s5(u3, h0, bmat, cmat, a_re, a_im, d_skip):
    nb, t_len, _ = u3.shape
    tt = min(t_len, 256)
    rows = 2 * nb
    batched = tt < 16
    kern = functools.partial(_s5_kernel, nb=nb, tt=tt, batched=batched)
    ntile = 2 * STATE_HALF // LANES
    tmp_shape = (ntile, nb * tt, LANES) if batched else (1, SUBLANES, LANES)
    return pl.pallas_call(
        kern,
        grid=(t_len // tt,),
        in_specs=[pl.BlockSpec((nb, tt, SSM_WIDTH), lambda i: (0, i, 0)),
                  _full((2, SSM_HALF, 2 * STATE_HALF)),
                  _full((2, 2 * STATE_HALF, SSM_HALF)),
                  _full((rows, STATE_HALF)), _full((rows, STATE_HALF)),
                  _full((rows, 2 * STATE_HALF)), _full((1, SSM_WIDTH))],
        out_specs=[pl.BlockSpec((nb, tt, SSM_WIDTH), lambda i: (0, i, 0)),
                   _full((rows, 2 * STATE_HALF))],
        out_shape=[jax.ShapeDtypeStruct((nb, t_len, SSM_WIDTH), F32),
                   jax.ShapeDtypeStruct((rows, 2 * STATE_HALF), F32)],
        scratch_shapes=[pltpu.VMEM((ntile, tt * rows, LANES), F32),
                        pltpu.VMEM((rows, 2 * STATE_HALF), F32),
                        pltpu.VMEM(tmp_shape, F32)],
        compiler_params=pltpu.CompilerParams(dimension_semantics=("arbitrary",),
                                             vmem_limit_bytes=VMEM_LIMIT),
        name="s5_scan",
    )(u3, bmat, cmat, a_re, a_im, h0, d_skip.reshape(1, SSM_WIDTH))


def _s5_params(a_re, a_im, b_re, b_im, c_re, c_im, log_dt):
    dt = jnp.exp(log_dt)[:, None]
    mag = jnp.exp(dt * a_re)
    ang = dt * a_im
    ab_re, ab_im = mag * jnp.cos(ang), mag * jnp.sin(ang)
    den = a_re * a_re + a_im * a_im
    nr, ni = ab_re - 1.0, ab_im
    f_re = (nr * a_re + ni * a_im) / den
    f_im = (ni * a_re - nr * a_im) / den
    bb_re = f_re[..., None] * b_re - f_im[..., None] * b_im
    bb_im = f_re[..., None] * b_im + f_im[..., None] * b_re
    gh = SSM_GROUPS // 2
    eye = jnp.eye(gh, dtype=F32)

    def b_half(w):
        return jnp.einsum('gnc,gh->gchn', w, eye).reshape(gh * SSM_GROUP, gh * SSM_STATE)

    def c_half(w):
        return jnp.einsum('gcn,gh->gnhc', w, eye).reshape(gh * SSM_STATE, gh * SSM_GROUP)

    bmat = jnp.stack([jnp.concatenate([b_half(bb_re[h * gh:(h + 1) * gh]),
                                       b_half(bb_im[h * gh:(h + 1) * gh])], axis=1)
                      for h in range(2)]).astype(BF16)
    cmat = jnp.stack([jnp.concatenate([c_half(c_re[h * gh:(h + 1) * gh]),
                                       -c_half(c_im[h * gh:(h + 1) * gh])], axis=0)
                      for h in range(2)]).astype(BF16)
    return bmat, cmat, ab_re.reshape(2, 1, STATE_HALF), ab_im.reshape(2, 1, STATE_HALF)


def _state_to_rows(h_re, h_im):
    nb = h_re.shape[0]
    f = lambda h: h.reshape(nb, 2, STATE_HALF).transpose(1, 0, 2).reshape(2 * nb, STATE_HALF)
    return jnp.concatenate([f(h_re), f(h_im)], axis=1)


def _rows_to_state(ht, nb):
    f = lambda h: h.reshape(2, nb, STATE_HALF).transpose(1, 0, 2).reshape(nb, SSM_GROUPS, SSM_STATE)
    return f(ht[:, :STATE_HALF]), f(ht[:, STATE_HALF:])


def _attn_prompt_kernel(q_ref, kp_ref, kc_ref, vp_ref, vc_ref, o_ref, l_ref, bias_s, s_s, p_s, *, dil):
    blk = ATTN_BLOCK
    n = pl.program_id(2)

    @pl.when((pl.program_id(0) == 0) & (pl.program_id(1) == 0) & (n == 0))
    def _():
        i_idx = lax.broadcasted_iota(I32, (blk, 2 * blk), 0)
        j_idx = lax.broadcasted_iota(I32, (blk, 2 * blk), 1)
        delta = i_idx - j_idx + blk
        in_band = (delta >= 0) & (delta <= blk)
        dist = (delta * dil).astype(F32)
        for h in range(N_HEADS):
            biased = -ALIBI_SLOPES[h] * dist
            bias_s[0, h] = jnp.where(in_band & (j_idx >= blk), biased, MASK_VALUE)
            bias_s[1, h] = jnp.where(in_band, biased, MASK_VALUE)

    which = jnp.minimum(n, 1)
    lane = lax.broadcasted_iota(I32, (blk, LANES), 1)
    for hp in range(N_HEADS // 2):
        cols = slice(hp * LANES, (hp + 1) * LANES)
        q2 = q_ref[:, cols]
        kk = jnp.concatenate([kp_ref[:, cols], kc_ref[:, cols]], axis=0)
        for half in range(2):
            h = 2 * hp + half
            in_head = (lane >= half * HEAD_DIM) & (lane < (half + 1) * HEAD_DIM)
            qm = jnp.where(in_head, q2, jnp.zeros_like(q2))
            s = lax.dot_general(qm, kk, (((1,), (1,)), ((), ())), preferred_element_type=F32)
            s_s[h] = s + bias_s[which, h]
    lse_all = jnp.zeros((blk, LANES), F32)
    for h in range(N_HEADS):
        s = s_s[h]
        m = jnp.max(s, axis=1, keepdims=True)
        p = jnp.exp(s - m)
        l = jnp.sum(p, axis=1, keepdims=True)
        p_s[h] = (p * (1.0 / l)).astype(BF16)
        lse_all = jnp.where(lane == h, m + jnp.log(l), lse_all)
    l_ref[...] = lse_all
    for hp in range(N_HEADS // 2):
        cols = slice(hp * LANES, (hp + 1) * LANES)
        vv = jnp.concatenate([vp_ref[:, cols], vc_ref[:, cols]], axis=0)
        outs = [jnp.dot(p_s[2 * hp + half], vv, preferred_element_type=F32) for half in range(2)]
        o_ref[:, cols] = jnp.where(lane < HEAD_DIM, outs[0], outs[1])


def _attn_prompt_branch(qb, kb, vb, bsz, s_len, dil):
    sub = s_len // dil
    nblk = sub // ATTN_BLOCK
    view = lambda t: t.reshape(bsz, sub, dil * ATTN_WIDTH)
    cur = pl.BlockSpec((None, ATTN_BLOCK, ATTN_WIDTH), lambda b, r, n: (b, n, r))
    prev = pl.BlockSpec((None, ATTN_BLOCK, ATTN_WIDTH), lambda b, r, n: (b, jnp.maximum(n - 1, 0), r))
    o, lse = pl.pallas_call(
        functools.partial(_attn_prompt_kernel, dil=dil),
        grid=(bsz, dil, nblk),
        in_specs=[cur, prev, cur, prev, cur],
        out_specs=[cur, pl.BlockSpec((None, ATTN_BLOCK, LANES), lambda b, r, n: (b, n, r))],
        out_shape=[jax.ShapeDtypeStruct((bsz, sub, dil * ATTN_WIDTH), F32),
                   jax.ShapeDtypeStruct((bsz, sub, dil * LANES), F32)],
        scratch_shapes=[pltpu.VMEM((2, N_HEADS, ATTN_BLOCK, 2 * ATTN_BLOCK), F32),
                        pltpu.VMEM((N_HEADS, ATTN_BLOCK, 2 * ATTN_BLOCK), F32),
                        pltpu.VMEM((N_HEADS, ATTN_BLOCK, 2 * ATTN_BLOCK), BF16)],
        compiler_params=pltpu.CompilerParams(
            dimension_semantics=("arbitrary", "arbitrary", "arbitrary"),
            vmem_limit_bytes=VMEM_LIMIT),
        name=f"attn_prompt_d{dil}",
    )(view(qb), view(kb), view(kb), view(vb), view(vb))
    return o.reshape(bsz * sub, dil * ATTN_WIDTH), lse.reshape(bsz * sub, dil * LANES)


def _attn_sample_kernel(q_ref, kn_ref, vn_ref, kc_ref, vc_ref, o_ref, bias_s, mult_s, *, t_len, wb):
    nrow = N_HEADS * t_len
    t_shift = t_len.bit_length() - 1
    d_shift = HEAD_DIM.bit_length() - 1
    nt = (((1,), (1,)), ((), ()))

    def branch_count(dist):
        mult = jnp.zeros(dist.shape, F32)
        for win, dil in zip(WINDOWS, DILATIONS):
            hit = (dist >= 0) & (dist <= win) & ((dist & (dil - 1)) == 0)
            mult = mult + jnp.where(hit, 1.0, 0.0)
        return mult

    def biased(dist, mult):
        head = lax.broadcasted_iota(I32, dist.shape, 0) >> t_shift
        slope = jnp.zeros(dist.shape, F32)
        for h in range(N_HEADS):
            slope = jnp.where(head == h, ALIBI_SLOPES[h], slope)
        return jnp.where(mult > 0.0, -slope * dist.astype(F32), MASK_VALUE)

    @pl.when(pl.program_id(0) == 0)
    def _():
        row = lax.broadcasted_iota(I32, (nrow, wb), 0)
        col = lax.broadcasted_iota(I32, (nrow, wb), 1)
        dist = wb + (row & (t_len - 1)) - col
        mult = branch_count(dist)
        mult_s[...] = mult
        bias_s[...] = biased(dist, mult)

    q = q_ref[...].astype(F32)
    qt = jnp.concatenate([q] * N_HEADS, axis=0)
    row_w = lax.broadcasted_iota(I32, (nrow, ATTN_WIDTH), 0)
    lane_w = lax.broadcasted_iota(I32, (nrow, ATTN_WIDTH), 1)
    qm = jnp.where((lane_w >> d_shift) == (row_w >> t_shift), qt, 0.0).astype(BF16)

    pad = LANES - t_len
    kn = jnp.concatenate([kn_ref[...].astype(F32), jnp.zeros((pad, ATTN_WIDTH), F32)], axis=0).astype(BF16)
    vn = jnp.concatenate([vn_ref[...].astype(F32), jnp.zeros((pad, ATTN_WIDTH), F32)], axis=0).astype(BF16)
    row_n = lax.broadcasted_iota(I32, (nrow, LANES), 0)
    col_n = lax.broadcasted_iota(I32, (nrow, LANES), 1)
    dist_n = jnp.where(col_n < t_len, (row_n & (t_len - 1)) - col_n, -1)
    mult_n = branch_count(dist_n)
    s_n = lax.dot_general(qm, kn, nt, preferred_element_type=F32) + biased(dist_n, mult_n)
    s_c = jnp.dot(qm, kc_ref[...].astype(BF16), preferred_element_type=F32) + bias_s[...]

    m = jnp.maximum(jnp.max(s_c, axis=1, keepdims=True), jnp.max(s_n, axis=1, keepdims=True))
    p_c = jnp.exp(s_c - m) * mult_s[...]
    p_n = jnp.exp(s_n - m) * mult_n
    l = jnp.sum(p_c, axis=1, keepdims=True) + jnp.sum(p_n, axis=1, keepdims=True)
    o = (lax.dot_general(p_c.astype(BF16), vc_ref[...].astype(BF16), nt, preferred_element_type=F32)
         + jnp.dot(p_n.astype(BF16), vn, preferred_element_type=F32)) / l
    lane_o = lax.broadcasted_iota(I32, (t_len, ATTN_WIDTH), 1) >> d_shift
    out = jnp.zeros((t_len, ATTN_WIDTH), F32)
    for h in range(N_HEADS):
        out = jnp.where(lane_o == h, o[h * t_len:(h + 1) * t_len], out)
    o_ref[...] = out


def _attn_sample(qb, kb, vb, cache_k, cache_v):
    bsz, t_len, _ = qb.shape
    wb = cache_k.shape[1]
    feature_major = lambda c: jnp.transpose(c, (0, 2, 3, 1)).reshape(bsz, ATTN_WIDTH, wb)
    new = pl.BlockSpec((None, t_len, ATTN_WIDTH), lambda b: (b, 0, 0))
    old = pl.BlockSpec((None, ATTN_WIDTH, wb), lambda b: (b, 0, 0))
    return pl.pallas_call(
        functools.partial(_attn_sample_kernel, t_len=t_len, wb=wb),
        grid=(bsz,),
        in_specs=[new, new, new, old, old],
        out_specs=new,
        out_shape=jax.ShapeDtypeStruct((bsz, t_len, ATTN_WIDTH), F32),
        scratch_shapes=[pltpu.VMEM((N_HEADS * t_len, wb), F32),
                        pltpu.VMEM((N_HEADS * t_len, wb), F32)],
        compiler_params=pltpu.CompilerParams(dimension_semantics=("arbitrary",),
                                             vmem_limit_bytes=VMEM_LIMIT),
        name="attn_sample",
    )(qb, kb, vb, feature_major(cache_k), feature_major(cache_v))


def _mid_kernel(*refs, dils):
    n_branch = max(len(dils), 1)
    x_ref, y_ref = refs[0], refs[1]
    o_refs = refs[2:2 + n_branch]
    pos = 2 + n_branch
    l_refs = refs[pos:pos + len(dils)]
    pos += len(l_refs)
    (wglu_ref, bglu_ref, lns_ref, lna_ref, wout_ref, lnm_ref, wrh_ref, wrl_ref, br_ref,
     expand_ref, tri_ref,
     x1_ref, hrow_ref, eidx_ref, epos_ref, egate_ref, cnt_ref, carry, nat) = refs[pos:]
    tm = x_ref.shape[0]
    o_tiles = ATTN_WIDTH // LANES

    def natural(bi, d):
        if d == 1:
            return o_refs[bi][...], l_refs[bi][...]
        for r in range(d):
            rows = pl.ds(r, tm // d, stride=d)
            for ct in range(o_tiles):
                c0 = r * ATTN_WIDTH + ct * LANES
                nat[bi, ct, rows, :] = o_refs[bi][:, c0:c0 + LANES]
            nat[bi, o_tiles, rows, :] = l_refs[bi][:, r * LANES:(r + 1) * LANES]
        return (jnp.concatenate([nat[bi, ct] for ct in range(o_tiles)], axis=1), nat[bi, o_tiles])

    @pl.when(pl.program_id(0) == 0)
    def _():
        carry[...] = jnp.zeros_like(carry)

    y = y_ref[...]
    z = y * (0.5 * (1.0 + jnp.tanh(math.sqrt(2.0 / math.pi) * (y + 0.044715 * (y * y * y)))))
    glu = z * _sigmoid(jnp.dot(z.astype(BF16), wglu_ref[...], preferred_element_type=F32) + bglu_ref[...])
    n_ssm = _rms(glu, lns_ref[...])

    if not dils:
        attn = o_refs[0][...]
    else:
        pairs = [natural(bi, d) for bi, d in enumerate(dils)]
        lses = [p[1] for p in pairs]
        mx = functools.reduce(jnp.maximum, lses)
        es = [jnp.exp(l - mx) for l in lses]
        inv = 1.0 / functools.reduce(lambda a, b: a + b, es)
        attn = jnp.zeros((tm, ATTN_WIDTH), F32)
        for e, (o_nat, _) in zip(es, pairs):
            w = e * inv
            w_hi = w.astype(BF16)
            w_lo = (w - w_hi.astype(F32)).astype(BF16)
            wide = (jnp.dot(w_hi, expand_ref[...], preferred_element_type=F32)
                    + jnp.dot(w_lo, expand_ref[...], preferred_element_type=F32))
            attn = attn + wide * o_nat
    n_attn = _rms(attn, lna_ref[...])

    x1 = (x_ref[...]
          + jnp.dot(n_ssm.astype(BF16), wout_ref[:SSM_WIDTH, :], preferred_element_type=F32)
          + jnp.dot(n_attn.astype(BF16), wout_ref[SSM_WIDTH:, :], preferred_element_type=F32))
    x1_ref[...] = x1
    hm = _rms(x1, lnm_ref[...])
    for s in range(ROW_TILES):
        hrow_ref[pl.ds(s, tm, stride=ROW_TILES), :] = hm[:, s * LANES:(s + 1) * LANES]

    h_hi = hm.astype(BF16)
    h_lo = (hm - h_hi.astype(F32)).astype(BF16)
    logits = (jnp.dot(h_hi, wrh_ref[...], preferred_element_type=F32)
              + jnp.dot(h_lo, wrh_ref[...], preferred_element_type=F32)
              + jnp.dot(h_hi, wrl_ref[...], preferred_element_type=F32)
              + br_ref[...])
    lane = lax.broadcasted_iota(I32, (tm, LANES), 1)
    lane_f = lane.astype(F32)
    work = jnp.where(lane < N_EXPERTS, logits, NEG_BIG)
    vals, idxs, hots = [], [], []
    for _ in range(TOP_K):
        m = jnp.max(work, axis=1, keepdims=True)
        idx = jnp.min(jnp.where(work == m, lane_f, float(LANES)), axis=1, keepdims=True)
        hot = lane_f == idx
        vals.append(m)
        idxs.append(idx)
        hots.append(hot)
        work = jnp.where(hot, NEG_BIG, work)
    exps = [jnp.exp(v - vals[0]) for v in vals]
    inv = 1.0 / functools.reduce(lambda a, b: a + b, exps)

    sel = functools.reduce(lambda a, b: a + b, [h.astype(F32) for h in hots])
    before = jnp.dot(tri_ref[...], sel.astype(BF16), preferred_element_type=F32) + carry[0:1, :]
    eidx = jnp.zeros((tm, LANES), I32)
    epos = jnp.zeros((tm, LANES), I32)
    egate = jnp.zeros((tm, LANES), F32)
    for k in range(TOP_K):
        pk = jnp.sum(jnp.where(hots[k], before, 0.0), axis=1, keepdims=True)
        eidx = jnp.where(lane == k, idxs[k].astype(I32), eidx)
        epos = jnp.where(lane == k, pk.astype(I32), epos)
        egate = jnp.where(lane == k, exps[k] * inv, egate)
    eidx_ref[...] = eidx
    epos_ref[...] = epos
    egate_ref[...] = egate
    total = carry[0:1, :] + jnp.sum(sel, axis=0, keepdims=True)
    carry[...] = jnp.broadcast_to(total, carry.shape)
    cnt_ref[...] = jnp.broadcast_to(total, cnt_ref.shape).astype(I32)


def _mid(x2d, y2d, attn_o, attn_lse, dils, w):
    n = x2d.shape[0]
    tm = TOKEN_TILE
    n_branch = len(attn_o)
    row = lambda width: pl.BlockSpec((tm, width), lambda i: (i, 0))
    packed = lambda d, width: pl.BlockSpec((tm // d, d * width), lambda i: (i, 0))
    attn_specs = ([packed(d, ATTN_WIDTH) for d in dils] + [packed(d, LANES) for d in dils]
                  if dils else [row(ATTN_WIDTH)])
    in_specs = ([row(D_MODEL), row(SSM_WIDTH)] + attn_specs
                + [_full((SSM_WIDTH, SSM_WIDTH)), _full((1, SSM_WIDTH)), _full((1, SSM_WIDTH)),
                   _full((1, ATTN_WIDTH)), _full((D_MODEL, D_MODEL)), _full((1, D_MODEL)),
                   _full((D_MODEL, LANES)), _full((D_MODEL, LANES)), _full((1, LANES)),
                   _full((LANES, ATTN_WIDTH)), _full((tm, tm))])
    out_specs = [row(D_MODEL), pl.BlockSpec((tm * ROW_TILES, LANES), lambda i: (i, 0)),
                 row(LANES), row(LANES), row(LANES), _full((SUBLANES, LANES))]
    out_shape = [jax.ShapeDtypeStruct((n, D_MODEL), F32),
                 jax.ShapeDtypeStruct((n * ROW_TILES, LANES), F32),
                 jax.ShapeDtypeStruct((n, LANES), I32),
                 jax.ShapeDtypeStruct((n, LANES), I32),
                 jax.ShapeDtypeStruct((n, LANES), F32),
                 jax.ShapeDtypeStruct((SUBLANES, LANES), I32)]
    return pl.pallas_call(
        functools.partial(_mid_kernel, dils=tuple(dils)),
        grid=(n // tm,),
        in_specs=in_specs,
        out_specs=out_specs,
        out_shape=out_shape,
        scratch_shapes=[pltpu.VMEM((SUBLANES, LANES), F32),
                        pltpu.VMEM((n_branch, ATTN_WIDTH // LANES + 1, tm, LANES), F32)],
        compiler_params=pltpu.CompilerParams(dimension_semantics=("arbitrary",),
                                             vmem_limit_bytes=VMEM_LIMIT),
        name="mid",
    )(x2d, y2d, *attn_o, *attn_lse, w["w_glu"], w["b_glu"], w["ln_ssm_out"], w["ln_attn_out"],
      w["w_out"], w["ln_moe"], w["wr_hi"], w["wr_lo"], w["b_router"], w["expand"], w["tri"])


def _dispatch_kernel(dest_ref, hp_ref, hs_ref, xs_hbm, sem, *, n_prompt_tiles):
    i = pl.program_id(0)
    npair = TOKEN_TILE * TOP_K

    def run(src_ref):
        def issue(t, c):
            src = src_ref.at[pl.ds(pl.multiple_of(t * ROW_TILES, ROW_TILES), ROW_TILES), :]
            for k in range(TOP_K):
                d = dest_ref[0, t * TOP_K + k]
                pltpu.make_async_copy(
                    src, xs_hbm.at[pl.ds(pl.multiple_of(d * ROW_TILES, ROW_TILES), ROW_TILES), :],
                    sem).start()
            return c

        lax.fori_loop(0, TOKEN_TILE, issue, 0, unroll=2)
        span = pl.ds(0, npair * ROW_TILES)
        pltpu.make_async_copy(xs_hbm.at[span, :], xs_hbm.at[span, :], sem).wait()

    @pl.when(i < n_prompt_tiles)
    def _():
        run(hp_ref)

    @pl.when(i >= n_prompt_tiles)
    def _():
        run(hs_ref)


def _dispatch(dest, hrow_p, hrow_s):
    n_p = hrow_p.shape[0] // ROW_TILES
    n_s = hrow_s.shape[0] // ROW_TILES
    npt = n_p // TOKEN_TILE
    ntile = (n_p + n_s) // TOKEN_TILE
    npair = TOKEN_TILE * TOP_K
    blk = (TOKEN_TILE * ROW_TILES, LANES)
    return pl.pallas_call(
        functools.partial(_dispatch_kernel, n_prompt_tiles=npt),
        grid=(ntile,),
        in_specs=[pl.BlockSpec((None, 1, npair), lambda i: (i, 0, 0), memory_space=pltpu.SMEM),
                  pl.BlockSpec(blk, lambda i: (jnp.minimum(i, npt - 1), 0)),
                  pl.BlockSpec(blk, lambda i: (jnp.maximum(i - npt, 0), 0))],
        out_specs=pl.BlockSpec(memory_space=pl.ANY),
        out_shape=jax.ShapeDtypeStruct(((n_p + n_s) * TOP_K * ROW_TILES, LANES), F32),
        scratch_shapes=[pltpu.SemaphoreType.DMA(())],
        compiler_params=pltpu.CompilerParams(dimension_semantics=("arbitrary",),
                                             vmem_limit_bytes=VMEM_LIMIT),
        name="moe_dispatch",
    )(dest.reshape(ntile, 1, npair), hrow_p, hrow_s)


def _expert_kernel(vt_ref, ve_ref, vok_ref, gs_ref, xs_ref, wu_ref, bu_ref, wd_ref, bd_ref,
                   out_ref, wu_s, wd_s, x_s):
    v = pl.program_id(0)
    e = ve_ref[v]
    j = vt_ref[v]
    vprev = jnp.maximum(v - 1, 0)
    new_e = (v == 0) | (e != ve_ref[vprev])
    new_j = (v == 0) | (j != vt_ref[vprev])
    tm = MOE_TILE

    @pl.when(new_e)
    def _():
        wu_s[...] = wu_ref[...].astype(BF16)
        wd_s[...] = wd_ref[...].astype(BF16)

    @pl.when(new_j)
    def _():
        out_ref[...] = jnp.zeros_like(out_ref)

    sub = x_s.shape[0]
    for part in range(tm // sub):
        base = part * sub * ROW_TILES
        first_row = j * tm + part * sub

        @pl.when((vok_ref[v] == 1) & (gs_ref[e] < first_row + sub) & (gs_ref[e + 1] > first_row))
        def _(part=part, base=base):
            for s in range(ROW_TILES):
                x_s[:, s * LANES:(s + 1) * LANES] = xs_ref[
                    pl.ds(base + s, sub, stride=ROW_TILES), :].astype(BF16)
            a = jnp.dot(x_s[...], wu_s[...], preferred_element_type=F32) + bu_ref[...]
            g = jnp.minimum(a[:, :EXPERT_FF], SWIGLU_LIMIT)
            lin = jnp.clip(a[:, EXPERT_FF:], -SWIGLU_LIMIT, SWIGLU_LIMIT)
            act = (lin + 1.0) * (g * _sigmoid(SWIGLU_ALPHA * g))
            y = jnp.dot(act.astype(BF16), wd_s[...], preferred_element_type=F32) + bd_ref[...]
            rows = j * tm + part * sub + lax.broadcasted_iota(I32, (sub, 1), 0)
            mine = (rows >= gs_ref[e]) & (rows < gs_ref[e + 1])
            for s in range(ROW_TILES):
                cur = out_ref[pl.ds(base + s, sub, stride=ROW_TILES), :]
                out_ref[pl.ds(base + s, sub, stride=ROW_TILES), :] = jnp.where(
                    mine, y[:, s * LANES:(s + 1) * LANES], cur)


def _experts(xs, vt, ve, vok, gstart, w_up, b_up, w_down, b_down):
    tm = MOE_TILE
    nvisit = vt.shape[0]
    rows = pl.BlockSpec((tm * ROW_TILES, LANES), lambda v, vt, ve, vok, gs: (vt[v], 0))
    per_e = lambda a, b: pl.BlockSpec((None, a, b), lambda v, vt, ve, vok, gs: (ve[v], 0, 0))
    grid_spec = pltpu.PrefetchScalarGridSpec(
        num_scalar_prefetch=4,
        grid=(nvisit,),
        in_specs=[rows, per_e(D_MODEL, 2 * EXPERT_FF), per_e(1, 2 * EXPERT_FF),
                  per_e(EXPERT_FF, D_MODEL), per_e(1, D_MODEL)],
        out_specs=rows,
        scratch_shapes=[pltpu.VMEM((D_MODEL, 2 * EXPERT_FF), BF16),
                        pltpu.VMEM((EXPERT_FF, D_MODEL), BF16),
                        pltpu.VMEM((MOE_SUBTILE, D_MODEL), BF16)],
    )
    return pl.pallas_call(
        _expert_kernel,
        grid_spec=grid_spec,
        out_shape=jax.ShapeDtypeStruct(xs.shape, F32),
        compiler_params=pltpu.CompilerParams(dimension_semantics=("arbitrary",),
                                             vmem_limit_bytes=VMEM_LIMIT),
        name="moe_experts",
    )(vt, ve, vok, gstart, xs, w_up, b_up.reshape(N_EXPERTS, 1, 2 * EXPERT_FF),
      w_down, b_down.reshape(N_EXPERTS, 1, D_MODEL))


def _routing(eidx_p, epos_p, cnt_p, eidx_s, epos_s, cnt_s, n_rows):
    cnt_p = cnt_p[0, :N_EXPERTS]
    cnt_s = cnt_s[0, :N_EXPERTS]
    cnt = cnt_p + cnt_s
    gend = jnp.cumsum(cnt)
    gstart = gend - cnt
    experts = jnp.arange(N_EXPERTS, dtype=I32)

    def lookup(table, idx):
        return jnp.sum(jnp.where(idx[..., None] == experts, table, 0), axis=-1)

    ep = eidx_p[:, :TOP_K]
    es = eidx_s[:, :TOP_K]
    dest_p = lookup(gstart, ep) + epos_p[:, :TOP_K]
    dest_s = lookup(gstart + cnt_p, es) + epos_s[:, :TOP_K]
    ntile = n_rows // MOE_TILE
    nvisit = ntile + N_EXPERTS
    first = gstart // MOE_TILE
    last = jnp.maximum(gend - 1, 0) // MOE_TILE
    nv = jnp.where(cnt > 0, last - first + 1, 0)
    vend = jnp.cumsum(nv)
    vstart = vend - nv
    total = vend[-1]
    v = jnp.arange(nvisit, dtype=I32)
    vc = jnp.minimum(v, total - 1)
    ve = jnp.sum((vend[None, :] <= vc[:, None]).astype(I32), axis=1)
    vt = (lookup(first - vstart, ve) + vc).astype(I32)
    vok = (v < total).astype(I32)
    gs = jnp.concatenate([gstart, gend[-1:]]).astype(I32)
    return dest_p.astype(I32), dest_s.astype(I32), vt, ve, vok, gs


def _out_kernel(dest_ref, next_ref, y_hbm, x1_ref, gate_ref, pe_ref, lnp_ref, wg_ref, bg_ref, wp_ref,
                lnf_ref, o_ref, buf, sem, *, ntile):
    tm = TOKEN_TILE
    npair = tm * TOP_K
    i = pl.program_id(0)
    slot = i % 2

    def gather(idx_ref, to):
        def issue(t, c):
            for k in range(TOP_K):
                d = idx_ref[0, t * TOP_K + k]
                pltpu.make_async_copy(
                    y_hbm.at[pl.ds(pl.multiple_of(d * ROW_TILES, ROW_TILES), ROW_TILES), :],
                    buf.at[to, pl.ds(pl.multiple_of((k * tm + t) * ROW_TILES, ROW_TILES), ROW_TILES), :],
                    sem.at[to]).start()
            return c

        lax.fori_loop(0, tm, issue, 0, unroll=2)

    @pl.when(i == 0)
    def _():
        gather(dest_ref, 0)

    if ntile > 1:
        @pl.when(i + 1 < ntile)
        def _():
            gather(next_ref, 1 - slot)

    pltpu.make_async_copy(y_hbm.at[pl.ds(0, npair * ROW_TILES), :], buf.at[slot], sem.at[slot]).wait()

    gates = gate_ref[...]
    parts = []
    for s in range(ROW_TILES):
        acc = jnp.zeros((tm, LANES), F32)
        for k in range(TOP_K):
            rows = buf[slot, pl.ds(k * tm * ROW_TILES + s, tm, stride=ROW_TILES), :]
            acc = acc + gates[:, k:k + 1] * rows
        parts.append(acc)
    x2 = x1_ref[...] + jnp.concatenate(parts, axis=1)
    gate = _sigmoid(jnp.dot(_rms(x2, lnp_ref[...]).astype(BF16), wg_ref[...],
                            preferred_element_type=F32) + bg_ref[...])
    x3 = x2 + gate * jnp.dot(pe_ref[...].astype(BF16), wp_ref[...], preferred_element_type=F32)
    o_ref[...] = _rms(x3, lnf_ref[...])


def _combine(dest, y_rows, x1, egate, pe, w):
    n = x1.shape[0]
    tm = TOKEN_TILE
    npair = tm * TOP_K
    row = lambda width: pl.BlockSpec((tm, width), lambda i: (i, 0))
    ntile = n // tm
    dest3 = dest.reshape(ntile, 1, npair)
    return pl.pallas_call(
        functools.partial(_out_kernel, ntile=ntile),
        grid=(ntile,),
        in_specs=[pl.BlockSpec((None, 1, npair), lambda i: (i, 0, 0), memory_space=pltpu.SMEM),
                  pl.BlockSpec((None, 1, npair), lambda i: (jnp.minimum(i + 1, ntile - 1), 0, 0),
                               memory_space=pltpu.SMEM),
                  pl.BlockSpec(memory_space=pl.ANY),
                  row(D_MODEL), row(LANES), row(PLE_DIM),
                  _full((1, D_MODEL)), _full((D_MODEL, D_MODEL)), _full((1, D_MODEL)),
                  _full((PLE_DIM, D_MODEL)), _full((1, D_MODEL))],
        out_specs=row(D_MODEL),
        out_shape=jax.ShapeDtypeStruct((n, D_MODEL), F32),
        scratch_shapes=[pltpu.VMEM((2, npair * ROW_TILES, LANES), F32), pltpu.SemaphoreType.DMA((2,))],
        compiler_params=pltpu.CompilerParams(dimension_semantics=("arbitrary",),
                                             vmem_limit_bytes=VMEM_LIMIT),
        name="combine_out",
    )(dest3, dest3, y_rows, x1, egate, pe, w["ln_ple"], w["w_ple_gate"],
      w["b_ple_gate"], w["w_ple_proj"], w["ln_final"])


def kernel(x_prompt, x_sample, cache_attn_k, cache_attn_v, state_ssm_re, state_ssm_im, p_prompt, p_sample, ln_mix, w_in, ssm_a_re, ssm_a_im, ssm_b_re, ssm_b_im, ssm_c_re, ssm_c_im, ssm_d, ssm_log_dt, w_glu, b_glu, ln_ssm_out, ln_attn_out, w_out, ln_moe, w_router, b_router, w_up, b_up, w_down, b_down, ln_ple, w_ple_gate, b_ple_gate, w_ple_proj, ln_final):
    bsz, s_len, _ = x_prompt.shape
    dbsz, dt_len, _ = x_sample.shape
    n_p, n_s = bsz * s_len, dbsz * dt_len
    wb = cache_attn_k.shape[2]
    wb_prompt = min(WINDOWS[-1], s_len)

    wr = jnp.pad(w_router[0], ((0, 0), (0, LANES - N_EXPERTS)))
    wr_hi = wr.astype(BF16)
    ti = jnp.arange(TOKEN_TILE)
    w = {
        "w_glu": w_glu[0].astype(BF16), "b_glu": b_glu[0].reshape(1, -1),
        "ln_ssm_out": ln_ssm_out[0].reshape(1, -1), "ln_attn_out": ln_attn_out[0].reshape(1, -1),
        "w_out": w_out[0].astype(BF16), "ln_moe": ln_moe[0].reshape(1, -1),
        "wr_hi": wr_hi, "wr_lo": (wr - wr_hi.astype(F32)).astype(BF16),
        "b_router": jnp.pad(b_router[0], (0, LANES - N_EXPERTS)).reshape(1, -1),
        "expand": (jnp.arange(LANES)[:, None] == jnp.arange(ATTN_WIDTH)[None, :] // HEAD_DIM).astype(BF16),
        "tri": (ti[:, None] > ti[None, :]).astype(BF16),
        "ln_ple": ln_ple[0].reshape(1, -1), "w_ple_gate": w_ple_gate[0].astype(BF16),
        "b_ple_gate": b_ple_gate[0].reshape(1, -1), "w_ple_proj": w_ple_proj[0].astype(BF16),
        "ln_final": ln_final.reshape(1, -1),
    }
    w_in_b = w_in[0].astype(BF16)
    bmat, cmat, ab_re, ab_im = _s5_params(ssm_a_re[0], ssm_a_im[0], ssm_b_re[0], ssm_b_im[0],
                                          ssm_c_re[0], ssm_c_im[0], ssm_log_dt[0])

    def coeff(a, nb):
        return jnp.broadcast_to(a, (2, nb, STATE_HALF)).reshape(2 * nb, STATE_HALF)

    proj_p = _in_proj(x_prompt.reshape(n_p, D_MODEL), ln_mix[0], w_in_b, dils=DILATIONS[1:],
                      seq_window=(s_len, wb_prompt))
    u_p, k_p, v_p = proj_p[:3]
    qkv = [proj_p[3:6]] + [proj_p[6 + 3 * i:9 + 3 * i] for i in range(len(DILATIONS) - 1)]
    zeros_state = jnp.zeros((bsz, SSM_GROUPS, SSM_STATE), F32)
    y_p, ht_p = _s5(u_p.reshape(bsz, s_len, SSM_WIDTH), _state_to_rows(zeros_state, zeros_state),
                    bmat, cmat, coeff(ab_re, bsz), coeff(ab_im, bsz), ssm_d[0])
    branches = [_attn_prompt_branch(*qkv[i], bsz, s_len, d) for i, d in enumerate(DILATIONS)]
    x1_p, hrow_p, eidx_p, epos_p, egate_p, cnt_p = _mid(
        x_prompt.reshape(n_p, D_MODEL), y_p.reshape(n_p, SSM_WIDTH),
        [b[0] for b in branches], [b[1] for b in branches], DILATIONS, w)

    u_s, k_s, v_s, qb_s, kb_s, vb_s = _in_proj(x_sample.reshape(n_s, D_MODEL), ln_mix[0], w_in_b)
    y_s, ht_s = _s5(u_s.reshape(dbsz, dt_len, SSM_WIDTH), _state_to_rows(state_ssm_re[0], state_ssm_im[0]),
                    bmat, cmat, coeff(ab_re, dbsz), coeff(ab_im, dbsz), ssm_d[0])
    as3 = lambda t: t.reshape(dbsz, dt_len, ATTN_WIDTH)
    attn_s = _attn_sample(as3(qb_s), as3(kb_s), as3(vb_s), cache_attn_k[0], cache_attn_v[0])
    x1_s, hrow_s, eidx_s, epos_s, egate_s, cnt_s = _mid(
        x_sample.reshape(n_s, D_MODEL), y_s.reshape(n_s, SSM_WIDTH),
        [attn_s.reshape(n_s, ATTN_WIDTH)], [], (), w)

    n_rows = (n_p + n_s) * TOP_K
    dest_p, dest_s, vt, ve, vok, gs = _routing(eidx_p, epos_p, cnt_p, eidx_s, epos_s, cnt_s, n_rows)
    xs = _dispatch(jnp.concatenate([dest_p.reshape(-1), dest_s.reshape(-1)]), hrow_p, hrow_s)
    y_rows = _experts(xs, vt, ve, vok, gs, w_up[0], b_up[0], w_down[0], b_down[0])

    out_p = _combine(dest_p, y_rows, x1_p, egate_p, p_prompt[0].reshape(n_p, PLE_DIM), w)
    out_s = _combine(dest_s, y_rows, x1_s, egate_s, p_sample[0].reshape(n_s, PLE_DIM), w)

    hr_p, hi_p = _rows_to_state(ht_p, bsz)
    hr_s, hi_s = _rows_to_state(ht_s, dbsz)
    kv_p = lambda t: jnp.transpose(t.reshape(bsz, N_HEADS, HEAD_DIM, wb_prompt), (0, 3, 1, 2))[None]
    kv_s = lambda t: t.reshape(dbsz, dt_len, N_HEADS, HEAD_DIM)[None]
    return (out_p.reshape(bsz, s_len, D_MODEL), out_s.reshape(dbsz, dt_len, D_MODEL),
            kv_p(k_p), kv_p(v_p), hr_p[None], hi_p[None],
            kv_s(k_s), kv_s(v_s), hr_s[None], hi_s[None])
```

```python
import functools
import math

import jax
import jax.numpy as jnp
from jax import lax
from jax.experimental import pallas as pl
from jax.experimental.pallas import tpu as pltpu

F32 = jnp.float32
BF16 = jnp.bfloat16
I32 = jnp.int32

D_MODEL = 1024
SSM_WIDTH = 512
SSM_GROUP = 16
SSM_GROUPS = 32
SSM_STATE = 64
ATTN_WIDTH = 512
HEAD_DIM = 64
N_HEADS = 8
IN_WIDTH = SSM_WIDTH + 3 * ATTN_WIDTH
DILATIONS = (1, 4, 16)
WINDOWS = (128, 512, 2048)
ATTN_BLOCK = 128
ATTN_STEP_BLOCKS = 2
N_EXPERTS = 32
TOP_K = 4
EXPERT_FF = D_MODEL
SWIGLU_LIMIT = 7.0
SWIGLU_ALPHA = 1.702
PLE_DIM = 256
EPS = 1e-6
MASK_VALUE = -1e30
NEG_BIG = -3.0e38

LANES = 128
SUBLANES = 8
ROW_TILES = D_MODEL // LANES
TOKEN_TILE = 256
IN_ROW_CHUNK = 256
MID_ROW_BLOCK = 256
MOE_TILE = 512
MOE_SUBTILE = 256
SSM_HALF = SSM_WIDTH // 2
STATE_HALF = SSM_GROUPS * SSM_STATE // 2
ALIBI_SLOPES = tuple(2.0 ** (-8.0 * (h + 1) / N_HEADS) for h in range(N_HEADS))
VMEM_LIMIT = 56 * 1024 * 1024


def _rms(x, g):
    return x * lax.rsqrt(jnp.mean(x * x, axis=-1, keepdims=True) + EPS) * g


def _sigmoid(x):
    return 1.0 / (1.0 + jnp.exp(-x))


def _full(shape):
    n = len(shape)
    return pl.BlockSpec(shape, lambda *_: (0,) * n)


def _in_kernel(x_ref, g_ref, w_ref, u_ref, k_ref, v_ref, qb_ref, kb_ref, vb_ref, *rest, dils, window_tiles):
    tm = x_ref.shape[0]
    rc = min(tm, IN_ROW_CHUNK)
    tiles = ATTN_WIDTH // LANES
    for c in range(tm // rc):
        rows = slice(c * rc, (c + 1) * rc)
        h = _rms(x_ref[rows, :], g_ref[...]).astype(BF16)
        p = jnp.dot(h, w_ref[...], preferred_element_type=F32)
        u_ref[rows, :] = p[:, :SSM_WIDTH]
        q = p[:, SSM_WIDTH:SSM_WIDTH + ATTN_WIDTH] * (HEAD_DIM ** -0.5)
        k = p[:, SSM_WIDTH + ATTN_WIDTH:SSM_WIDTH + 2 * ATTN_WIDTH]
        v = p[:, SSM_WIDTH + 2 * ATTN_WIDTH:]
        if window_tiles is None:
            k_ref[rows, :] = k
            v_ref[rows, :] = v
        else:
            k_ref[:, rows] = k.T
            v_ref[:, rows] = v.T
        qb_ref[rows, :] = q.astype(BF16)
        kb_ref[rows, :] = k.astype(BF16)
        vb_ref[rows, :] = v.astype(BF16)
        if not dils:
            continue
        scr = rest[-1]
        for a, val in enumerate((q, k, v)):
            for ct in range(tiles):
                scr[c, a * tiles + ct] = val[:, ct * LANES:(ct + 1) * LANES]
        for di, d in enumerate(dils):
            out_rows = slice(c * rc // d, (c + 1) * rc // d)
            for a in range(3):
                out = rest[di * 3 + a]
                for r in range(d):
                    for ct in range(tiles):
                        piece = scr[c, a * tiles + ct, pl.ds(r, rc // d, stride=d), :]
                        c0 = r * ATTN_WIDTH + ct * LANES
                        out[out_rows, c0:c0 + LANES] = piece.astype(BF16)


def _in_proj(x2d, ln_mix, w_in_bf16, dils=(), seq_window=None):
    n = x2d.shape[0]
    tm = min(n, 512)
    row = lambda w: pl.BlockSpec((tm, w), lambda i: (i, 0))
    window_tiles = None
    kv_spec, kv_shape = row(ATTN_WIDTH), jax.ShapeDtypeStruct((n, ATTN_WIDTH), F32)
    if seq_window is not None:
        s_len, window = seq_window
        seq_tiles, first = s_len // tm, (s_len - window) // tm
        window_tiles = (seq_tiles, first)
        kv_spec = pl.BlockSpec(
            (None, ATTN_WIDTH, tm),
            lambda i: (i // seq_tiles, 0, jnp.maximum(i % seq_tiles - first, 0)))
        kv_shape = jax.ShapeDtypeStruct((n // s_len, ATTN_WIDTH, window), F32)
    out_specs = [row(SSM_WIDTH), kv_spec, kv_spec] + [row(ATTN_WIDTH)] * 3
    out_shape = ([jax.ShapeDtypeStruct((n, SSM_WIDTH), F32), kv_shape, kv_shape]
                 + [jax.ShapeDtypeStruct((n, ATTN_WIDTH), BF16)] * 3)
    for d in dils:
        out_specs += [pl.BlockSpec((tm // d, d * ATTN_WIDTH), lambda i: (i, 0))] * 3
        out_shape += [jax.ShapeDtypeStruct((n // d, d * ATTN_WIDTH), BF16)] * 3
    rc = min(tm, IN_ROW_CHUNK)
    scratch = [pltpu.VMEM((tm // rc, 3 * ATTN_WIDTH // LANES, rc, LANES), F32)] if dils else []
    return pl.pallas_call(
        functools.partial(_in_kernel, dils=tuple(dils), window_tiles=window_tiles),
        grid=(n // tm,),
        in_specs=[row(D_MODEL), _full((1, D_MODEL)), _full((D_MODEL, IN_WIDTH))],
        out_specs=out_specs,
        out_shape=out_shape,
        scratch_shapes=scratch,
        compiler_params=pltpu.CompilerParams(dimension_semantics=("arbitrary",),
                                             vmem_limit_bytes=VMEM_LIMIT),
        name="in_proj",
    )(x2d, ln_mix.reshape(1, D_MODEL), w_in_bf16)


def _s5_kernel(u_ref, bmat_ref, cmat_ref, are_ref, aim_ref, h0_ref, d_ref,
               y_ref, ht_ref, buf, hc, tmp, *, nb, tt, batched):
    rows = 2 * nb
    ntile = 2 * STATE_HALF // LANES
    half_tiles = ntile // 2

    def lane_tile(c):
        return slice(c * LANES, (c + 1) * LANES)

    @pl.when(pl.program_id(0) == 0)
    def _():
        hc[...] = h0_ref[...]

    if batched:
        u_all = u_ref[...].reshape(nb * tt, SSM_WIDTH)
        ub_all = u_all.astype(BF16)
        for hf in range(2):
            bu = jnp.dot(ub_all[:, hf * SSM_HALF:(hf + 1) * SSM_HALF], bmat_ref[hf],
                         preferred_element_type=F32)
            for c in range(ntile):
                tmp[c] = bu[:, lane_tile(c)]
            for c in range(ntile):
                for t in range(tt):
                    buf[c, t * rows + hf * nb:t * rows + (hf + 1) * nb, :] = tmp[c, pl.ds(t, nb, stride=tt), :]
    else:
        for b in range(nb):
            ub = u_ref[b].astype(BF16)
            for hf in range(2):
                bu = jnp.dot(ub[:, hf * SSM_HALF:(hf + 1) * SSM_HALF], bmat_ref[hf],
                             preferred_element_type=F32)
                for c in range(ntile):
                    buf[c, pl.ds(hf * nb + b, tt, stride=rows), :] = bu[:, lane_tile(c)]

    group = 8
    for s in range(rows // SUBLANES):
        r0 = s * SUBLANES
        for c0 in range(0, half_tiles, group):
            ar = [are_ref[r0:r0 + SUBLANES, lane_tile(c0 + k)] for k in range(group)]
            ai = [aim_ref[r0:r0 + SUBLANES, lane_tile(c0 + k)] for k in range(group)]
            init = tuple(hc[r0:r0 + SUBLANES, lane_tile(c0 + k)] for k in range(group)) + tuple(
                hc[r0:r0 + SUBLANES, lane_tile(half_tiles + c0 + k)] for k in range(group))

            def step(t, carry, r0=r0, c0=c0, ar=ar, ai=ai):
                row = pl.multiple_of(t * rows + r0, SUBLANES)
                out_r, out_i = [], []
                for k in range(group):
                    hr, hi = carry[k], carry[group + k]
                    xr = buf[c0 + k, pl.ds(row, SUBLANES), :]
                    xi = buf[half_tiles + c0 + k, pl.ds(row, SUBLANES), :]
                    nr = ar[k] * hr - ai[k] * hi + xr
                    ni = ar[k] * hi + ai[k] * hr + xi
                    buf[c0 + k, pl.ds(row, SUBLANES), :] = nr
                    buf[half_tiles + c0 + k, pl.ds(row, SUBLANES), :] = ni
                    out_r.append(nr)
                    out_i.append(ni)
                return tuple(out_r) + tuple(out_i)

            fin = lax.fori_loop(0, tt, step, init, unroll=min(tt, 8))
            for k in range(group):
                hc[r0:r0 + SUBLANES, lane_tile(c0 + k)] = fin[k]
                hc[r0:r0 + SUBLANES, lane_tile(half_tiles + c0 + k)] = fin[group + k]

    if batched:
        parts = []
        for hf in range(2):
            for c in range(ntile):
                for t in range(tt):
                    tmp[c, pl.ds(t, nb, stride=tt), :] = buf[c, t * rows + hf * nb:t * rows + (hf + 1) * nb, :]
            hs = jnp.concatenate([tmp[c] for c in range(ntile)], axis=1).astype(BF16)
            parts.append(jnp.dot(hs, cmat_ref[hf], preferred_element_type=F32))
        y_all = jnp.concatenate(parts, axis=1) + d_ref[...] * u_all
        y_ref[...] = y_all.reshape(nb, tt, SSM_WIDTH)
    else:
        for b in range(nb):
            parts = []
            for hf in range(2):
                hs = jnp.concatenate(
                    [buf[c, pl.ds(hf * nb + b, tt, stride=rows), :] for c in range(ntile)],
                    axis=1).astype(BF16)
                parts.append(jnp.dot(hs, cmat_ref[hf], preferred_element_type=F32))
            y_ref[b] = jnp.concatenate(parts, axis=1) + d_ref[...] * u_ref[b]

    ht_ref[...] = hc[...]


def _s5(u3, h0, bmat, cmat, a_re, a_im, d_skip):
    nb, t_len, _ = u3.shape
    tt = min(t_len, 256)
    rows = 2 * nb
    batched = tt < 16
    kern = functools.partial(_s5_kernel, nb=nb, tt=tt, batched=batched)
    ntile = 2 * STATE_HALF // LANES
    tmp_shape = (ntile, nb * tt, LANES) if batched else (1, SUBLANES, LANES)
    return pl.pallas_call(
        kern,
        grid=(t_len // tt,),
        in_specs=[pl.BlockSpec((nb, tt, SSM_WIDTH), lambda i: (0, i, 0)),
                  _full((2, SSM_HALF, 2 * STATE_HALF)),
                  _full((2, 2 * STATE_HALF, SSM_HALF)),
                  _full((rows, STATE_HALF)), _full((rows, STATE_HALF)),
                  _full((rows, 2 * STATE_HALF)), _full((1, SSM_WIDTH))],
        out_specs=[pl.BlockSpec((nb, tt, SSM_WIDTH), lambda i: (0, i, 0)),
                   _full((rows, 2 * STATE_HALF))],
        out_shape=[jax.ShapeDtypeStruct((nb, t_len, SSM_WIDTH), F32),
                   jax.ShapeDtypeStruct((rows, 2 * STATE_HALF), F32)],
        scratch_shapes=[pltpu.VMEM((ntile, tt * rows, LANES), F32),
                        pltpu.VMEM((rows, 2 * STATE_HALF), F32),
                        pltpu.VMEM(tmp_shape, F32)],
        compiler_params=pltpu.CompilerParams(dimension_semantics=("arbitrary",),
                                             vmem_limit_bytes=VMEM_LIMIT),
        name="s5_scan",
    )(u3, bmat, cmat, a_re, a_im, h0, d_skip.reshape(1, SSM_WIDTH))


def _s5_params(a_re, a_im, b_re, b_im, c_re, c_im, log_dt):
    dt = jnp.exp(log_dt)[:, None]
    mag = jnp.exp(dt * a_re)
    ang = dt * a_im
    ab_re, ab_im = mag * jnp.cos(ang), mag * jnp.sin(ang)
    den = a_re * a_re + a_im * a_im
    nr, ni = ab_re - 1.0, ab_im
    f_re = (nr * a_re + ni * a_im) / den
    f_im = (ni * a_re - nr * a_im) / den
    bb_re = f_re[..., None] * b_re - f_im[..., None] * b_im
    bb_im = f_re[..., None] * b_im + f_im[..., None] * b_re
    gh = SSM_GROUPS // 2
    eye = jnp.eye(gh, dtype=F32)

    def b_half(w):
        return jnp.einsum('gnc,gh->gchn', w, eye).reshape(gh * SSM_GROUP, gh * SSM_STATE)

    def c_half(w):
        return jnp.einsum('gcn,gh->gnhc', w, eye).reshape(gh * SSM_STATE, gh * SSM_GROUP)

    bmat = jnp.stack([jnp.concatenate([b_half(bb_re[h * gh:(h + 1) * gh]),
                                       b_half(bb_im[h * gh:(h + 1) * gh])], axis=1)
                      for h in range(2)]).astype(BF16)
    cmat = jnp.stack([jnp.concatenate([c_half(c_re[h * gh:(h + 1) * gh]),
                                       -c_half(c_im[h * gh:(h + 1) * gh])], axis=0)
                      for h in range(2)]).astype(BF16)
    return bmat, cmat, ab_re.reshape(2, 1, STATE_HALF), ab_im.reshape(2, 1, STATE_HALF)


def _state_to_rows(h_re, h_im):
    nb = h_re.shape[0]
    f = lambda h: h.reshape(nb, 2, STATE_HALF).transpose(1, 0, 2).reshape(2 * nb, STATE_HALF)
    return jnp.concatenate([f(h_re), f(h_im)], axis=1)


def _rows_to_state(ht, nb):
    f = lambda h: h.reshape(2, nb, STATE_HALF).transpose(1, 0, 2).reshape(nb, SSM_GROUPS, SSM_STATE)
    return f(ht[:, :STATE_HALF]), f(ht[:, STATE_HALF:])


def _attn_prompt_kernel(q_ref, kp_ref, kc_ref, vp_ref, vc_ref, o_ref, l_ref, bias_s, s_s, p_s, *, dil):
    blk = ATTN_BLOCK
    n = pl.program_id(2)

    @pl.when((pl.program_id(0) == 0) & (pl.program_id(1) == 0) & (n == 0))
    def _():
        i_idx = lax.broadcasted_iota(I32, (blk, 2 * blk), 0)
        j_idx = lax.broadcasted_iota(I32, (blk, 2 * blk), 1)
        delta = i_idx - j_idx + blk
        in_band = (delta >= 0) & (delta <= blk)
        dist = (delta * dil).astype(F32)
        for h in range(N_HEADS):
            biased = -ALIBI_SLOPES[h] * dist
            bias_s[0, h] = jnp.where(in_band & (j_idx >= blk), biased, MASK_VALUE)
            bias_s[1, h] = jnp.where(in_band, biased, MASK_VALUE)

    lane = lax.broadcasted_iota(I32, (blk, LANES), 1)
    nqb = q_ref.shape[0] // blk
    for qb in range(nqb):
        which = jnp.minimum(n, 1) if qb == 0 else 1
        rows = slice(qb * blk, (qb + 1) * blk)
        for hp in range(N_HEADS // 2):
            cols = slice(hp * LANES, (hp + 1) * LANES)
            q2 = q_ref[rows, cols]
            before = kp_ref[:, cols] if qb == 0 else kc_ref[(qb - 1) * blk:qb * blk, cols]
            kk = jnp.concatenate([before, kc_ref[rows, cols]], axis=0)
            for half in range(2):
                h = 2 * hp + half
                in_head = (lane >= half * HEAD_DIM) & (lane < (half + 1) * HEAD_DIM)
                qm = jnp.where(in_head, q2, jnp.zeros_like(q2))
                s = lax.dot_general(qm, kk, (((1,), (1,)), ((), ())), preferred_element_type=F32)
                s_s[qb * N_HEADS + h] = s + bias_s[which, h]
    for qb in range(nqb):
        lse_all = jnp.zeros((blk, LANES), F32)
        for h in range(N_HEADS):
            s = s_s[qb * N_HEADS + h]
            m = jnp.max(s, axis=1, keepdims=True)
            p = jnp.exp(s - m)
            l = jnp.sum(p, axis=1, keepdims=True)
            p_s[qb * N_HEADS + h] = (p * (1.0 / l)).astype(BF16)
            lse_all = jnp.where(lane == h, m + jnp.log(l), lse_all)
        l_ref[qb * blk:(qb + 1) * blk, :] = lse_all
    for qb in range(nqb):
        rows = slice(qb * blk, (qb + 1) * blk)
        for hp in range(N_HEADS // 2):
            cols = slice(hp * LANES, (hp + 1) * LANES)
            before = vp_ref[:, cols] if qb == 0 else vc_ref[(qb - 1) * blk:qb * blk, cols]
            vv = jnp.concatenate([before, vc_ref[rows, cols]], axis=0)
            outs = [jnp.dot(p_s[qb * N_HEADS + 2 * hp + half], vv, preferred_element_type=F32)
                    for half in range(2)]
            o_ref[rows, cols] = jnp.where(lane < HEAD_DIM, outs[0], outs[1])


def _attn_prompt_branch(qb, kb, vb, bsz, s_len, dil):
    sub = s_len // dil
    nqb = ATTN_STEP_BLOCKS
    step = nqb * ATTN_BLOCK
    nstep = sub // step
    view = lambda t: t.reshape(bsz, sub, dil * ATTN_WIDTH)
    cur = pl.BlockSpec((None, step, ATTN_WIDTH), lambda b, r, n: (b, n, r))
    prev = pl.BlockSpec((None, ATTN_BLOCK, ATTN_WIDTH),
                        lambda b, r, n: (b, jnp.maximum(n * nqb - 1, 0), r))
    o, lse = pl.pallas_call(
        functools.partial(_attn_prompt_kernel, dil=dil),
        grid=(bsz, dil, nstep),
        in_specs=[cur, prev, cur, prev, cur],
        out_specs=[cur, pl.BlockSpec((None, step, LANES), lambda b, r, n: (b, n, r))],
        out_shape=[jax.ShapeDtypeStruct((bsz, sub, dil * ATTN_WIDTH), F32),
                   jax.ShapeDtypeStruct((bsz, sub, dil * LANES), F32)],
        scratch_shapes=[pltpu.VMEM((2, N_HEADS, ATTN_BLOCK, 2 * ATTN_BLOCK), F32),
                        pltpu.VMEM((nqb * N_HEADS, ATTN_BLOCK, 2 * ATTN_BLOCK), F32),
                        pltpu.VMEM((nqb * N_HEADS, ATTN_BLOCK, 2 * ATTN_BLOCK), BF16)],
        compiler_params=pltpu.CompilerParams(
            dimension_semantics=("arbitrary", "arbitrary", "arbitrary"),
            vmem_limit_bytes=VMEM_LIMIT),
        name=f"attn_prompt_d{dil}",
    )(view(qb), view(kb), view(kb), view(vb), view(vb))
    return o.reshape(bsz * sub, dil * ATTN_WIDTH), lse.reshape(bsz * sub, dil * LANES)


def _attn_sample_kernel(q_ref, kn_ref, vn_ref, kc_ref, vc_ref, o_ref, bias_s, mult_s, *, t_len, wb):
    nrow = N_HEADS * t_len
    t_shift = t_len.bit_length() - 1
    d_shift = HEAD_DIM.bit_length() - 1
    nt = (((1,), (1,)), ((), ()))

    def branch_count(dist):
        mult = jnp.zeros(dist.shape, F32)
        for win, dil in zip(WINDOWS, DILATIONS):
            hit = (dist >= 0) & (dist <= win) & ((dist & (dil - 1)) == 0)
            mult = mult + jnp.where(hit, 1.0, 0.0)
        return mult

    def biased(dist, mult):
        head = lax.broadcasted_iota(I32, dist.shape, 0) >> t_shift
        slope = jnp.zeros(dist.shape, F32)
        for h in range(N_HEADS):
            slope = jnp.where(head == h, ALIBI_SLOPES[h], slope)
        return jnp.where(mult > 0.0, -slope * dist.astype(F32), MASK_VALUE)

    @pl.when(pl.program_id(0) == 0)
    def _():
        row = lax.broadcasted_iota(I32, (nrow, wb), 0)
        col = lax.broadcasted_iota(I32, (nrow, wb), 1)
        dist = wb + (row & (t_len - 1)) - col
        mult = branch_count(dist)
        mult_s[...] = mult
        bias_s[...] = biased(dist, mult)

    q = q_ref[...].astype(F32)
    qt = jnp.concatenate([q] * N_HEADS, axis=0)
    row_w = lax.broadcasted_iota(I32, (nrow, ATTN_WIDTH), 0)
    lane_w = lax.broadcasted_iota(I32, (nrow, ATTN_WIDTH), 1)
    qm = jnp.where((lane_w >> d_shift) == (row_w >> t_shift), qt, 0.0).astype(BF16)

    pad = LANES - t_len
    kn = jnp.concatenate([kn_ref[...].astype(F32), jnp.zeros((pad, ATTN_WIDTH), F32)], axis=0).astype(BF16)
    vn = jnp.concatenate([vn_ref[...].astype(F32), jnp.zeros((pad, ATTN_WIDTH), F32)], axis=0).astype(BF16)
    row_n = lax.broadcasted_iota(I32, (nrow, LANES), 0)
    col_n = lax.broadcasted_iota(I32, (nrow, LANES), 1)
    dist_n = jnp.where(col_n < t_len, (row_n & (t_len - 1)) - col_n, -1)
    mult_n = branch_count(dist_n)
    s_n = lax.dot_general(qm, kn, nt, preferred_element_type=F32) + biased(dist_n, mult_n)
    s_c = jnp.dot(qm, kc_ref[...].astype(BF16), preferred_element_type=F32) + bias_s[...]

    m = jnp.maximum(jnp.max(s_c, axis=1, keepdims=True), jnp.max(s_n, axis=1, keepdims=True))
    p_c = jnp.exp(s_c - m) * mult_s[...]
    p_n = jnp.exp(s_n - m) * mult_n
    l = jnp.sum(p_c, axis=1, keepdims=True) + jnp.sum(p_n, axis=1, keepdims=True)
    o = (lax.dot_general(p_c.astype(BF16), vc_ref[...].astype(BF16), nt, preferred_element_type=F32)
         + jnp.dot(p_n.astype(BF16), vn, preferred_element_type=F32)) / l
    lane_o = lax.broadcasted_iota(I32, (t_len, ATTN_WIDTH), 1) >> d_shift
    out = jnp.zeros((t_len, ATTN_WIDTH), F32)
    for h in range(N_HEADS):
        out = jnp.where(lane_o == h, o[h * t_len:(h + 1) * t_len], out)
    o_ref[...] = out


def _attn_sample(qb, kb, vb, cache_k, cache_v):
    bsz, t_len, _ = qb.shape
    wb = cache_k.shape[1]
    feature_major = lambda c: jnp.transpose(c, (0, 2, 3, 1)).reshape(bsz, ATTN_WIDTH, wb)
    new = pl.BlockSpec((None, t_len, ATTN_WIDTH), lambda b: (b, 0, 0))
    old = pl.BlockSpec((None, ATTN_WIDTH, wb), lambda b: (b, 0, 0))
    return pl.pallas_call(
        functools.partial(_attn_sample_kernel, t_len=t_len, wb=wb),
        grid=(bsz,),
        in_specs=[new, new, new, old, old],
        out_specs=new,
        out_shape=jax.ShapeDtypeStruct((bsz, t_len, ATTN_WIDTH), F32),
        scratch_shapes=[pltpu.VMEM((N_HEADS * t_len, wb), F32),
                        pltpu.VMEM((N_HEADS * t_len, wb), F32)],
        compiler_params=pltpu.CompilerParams(dimension_semantics=("arbitrary",),
                                             vmem_limit_bytes=VMEM_LIMIT),
        name="attn_sample",
    )(qb, kb, vb, feature_major(cache_k), feature_major(cache_v))


def _mid_kernel(*refs, dils):
    n_branch = max(len(dils), 1)
    x_ref, y_ref = refs[0], refs[1]
    o_refs = refs[2:2 + n_branch]
    pos = 2 + n_branch
    l_refs = refs[pos:pos + len(dils)]
    pos += len(l_refs)
    (wglu_ref, bglu_ref, lns_ref, lna_ref, wout_ref, lnm_ref, wrh_ref, wrl_ref, br_ref,
     expand_ref, tri_ref,
     x1_ref, hrow_ref, eidx_ref, epos_ref, egate_ref, cnt_ref, carry, nat) = refs[pos:]
    tm = x_ref.shape[0]
    o_tiles = ATTN_WIDTH // LANES

    rb = min(tm, MID_ROW_BLOCK)

    for bi, d in enumerate(dils):
        if d == 1:
            continue
        for r in range(d):
            spread = pl.ds(r, tm // d, stride=d)
            for ct in range(o_tiles):
                c0 = r * ATTN_WIDTH + ct * LANES
                nat[bi, ct, spread, :] = o_refs[bi][:, c0:c0 + LANES]
            nat[bi, o_tiles, spread, :] = l_refs[bi][:, r * LANES:(r + 1) * LANES]

    def natural(bi, d, rows):
        if d == 1:
            return o_refs[bi][rows, :], l_refs[bi][rows, :]
        return (jnp.concatenate([nat[bi, ct, rows, :] for ct in range(o_tiles)], axis=1),
                nat[bi, o_tiles, rows, :])

    @pl.when(pl.program_id(0) == 0)
    def _():
        carry[...] = jnp.zeros_like(carry)

    lane = lax.broadcasted_iota(I32, (rb, LANES), 1)
    lane_f = lane.astype(F32)
    run = carry[0:1, :]
    for blk in range(tm // rb):
        rows = slice(blk * rb, (blk + 1) * rb)
        y = y_ref[rows, :]
        z = y * (0.5 * (1.0 + jnp.tanh(math.sqrt(2.0 / math.pi) * (y + 0.044715 * (y * y * y)))))
        glu = z * _sigmoid(jnp.dot(z.astype(BF16), wglu_ref[...], preferred_element_type=F32)
                           + bglu_ref[...])
        n_ssm = _rms(glu, lns_ref[...])

        if not dils:
            attn = o_refs[0][rows, :]
        else:
            pairs = [natural(bi, d, rows) for bi, d in enumerate(dils)]
            lses = [p[1] for p in pairs]
            mx = functools.reduce(jnp.maximum, lses)
            es = [jnp.exp(l - mx) for l in lses]
            inv = 1.0 / functools.reduce(lambda a, b: a + b, es)
            attn = jnp.zeros((rb, ATTN_WIDTH), F32)
            for e, (o_nat, _) in zip(es, pairs):
                w = e * inv
                w_hi = w.astype(BF16)
                w_lo = (w - w_hi.astype(F32)).astype(BF16)
                wide = (jnp.dot(w_hi, expand_ref[...], preferred_element_type=F32)
                        + jnp.dot(w_lo, expand_ref[...], preferred_element_type=F32))
                attn = attn + wide * o_nat
        n_attn = _rms(attn, lna_ref[...])

        x1 = (x_ref[rows, :]
              + jnp.dot(n_ssm.astype(BF16), wout_ref[:SSM_WIDTH, :], preferred_element_type=F32)
              + jnp.dot(n_attn.astype(BF16), wout_ref[SSM_WIDTH:, :], preferred_element_type=F32))
        x1_ref[rows, :] = x1
        hm = _rms(x1, lnm_ref[...])
        for s in range(ROW_TILES):
            hrow_ref[pl.ds(blk * rb * ROW_TILES + s, rb, stride=ROW_TILES), :] = hm[:, s * LANES:(s + 1) * LANES]

        h_hi = hm.astype(BF16)
        h_lo = (hm - h_hi.astype(F32)).astype(BF16)
        logits = (jnp.dot(h_hi, wrh_ref[...], preferred_element_type=F32)
                  + jnp.dot(h_lo, wrh_ref[...], preferred_element_type=F32)
                  + jnp.dot(h_hi, wrl_ref[...], preferred_element_type=F32)
                  + br_ref[...])
        work = jnp.where(lane < N_EXPERTS, logits, NEG_BIG)
        vals, idxs, hots = [], [], []
        for _ in range(TOP_K):
            m = jnp.max(work, axis=1, keepdims=True)
            idx = jnp.min(jnp.where(work == m, lane_f, float(LANES)), axis=1, keepdims=True)
            hot = lane_f == idx
            vals.append(m)
            idxs.append(idx)
            hots.append(hot)
            work = jnp.where(hot, NEG_BIG, work)
        exps = [jnp.exp(v - vals[0]) for v in vals]
        inv = 1.0 / functools.reduce(lambda a, b: a + b, exps)

        sel = functools.reduce(lambda a, b: a + b, [h.astype(F32) for h in hots])
        before = jnp.dot(tri_ref[...], sel.astype(BF16), preferred_element_type=F32) + run
        eidx = jnp.zeros((rb, LANES), I32)
        epos = jnp.zeros((rb, LANES), I32)
        egate = jnp.zeros((rb, LANES), F32)
        for k in range(TOP_K):
            pk = jnp.sum(jnp.where(hots[k], before, 0.0), axis=1, keepdims=True)
            eidx = jnp.where(lane == k, idxs[k].astype(I32), eidx)
            epos = jnp.where(lane == k, pk.astype(I32), epos)
            egate = jnp.where(lane == k, exps[k] * inv, egate)
        eidx_ref[rows, :] = eidx
        epos_ref[rows, :] = epos
        egate_ref[rows, :] = egate
        run = run + jnp.sum(sel, axis=0, keepdims=True)
    carry[...] = jnp.broadcast_to(run, carry.shape)
    cnt_ref[...] = jnp.broadcast_to(run, cnt_ref.shape).astype(I32)


def _mid(x2d, y2d, attn_o, attn_lse, dils, w):
    n = x2d.shape[0]
    tm = TOKEN_TILE
    n_branch = len(attn_o)
    row = lambda width: pl.BlockSpec((tm, width), lambda i: (i, 0))
    packed = lambda d, width: pl.BlockSpec((tm // d, d * width), lambda i: (i, 0))
    attn_specs = ([packed(d, ATTN_WIDTH) for d in dils] + [packed(d, LANES) for d in dils]
                  if dils else [row(ATTN_WIDTH)])
    in_specs = ([row(D_MODEL), row(SSM_WIDTH)] + attn_specs
                + [_full((SSM_WIDTH, SSM_WIDTH)), _full((1, SSM_WIDTH)), _full((1, SSM_WIDTH)),
                   _full((1, ATTN_WIDTH)), _full((D_MODEL, D_MODEL)), _full((1, D_MODEL)),
                   _full((D_MODEL, LANES)), _full((D_MODEL, LANES)), _full((1, LANES)),
                   _full((LANES, ATTN_WIDTH)), _full((MID_ROW_BLOCK, MID_ROW_BLOCK))])
    out_specs = [row(D_MODEL), pl.BlockSpec((tm * ROW_TILES, LANES), lambda i: (i, 0)),
                 row(LANES), row(LANES), row(LANES), _full((SUBLANES, LANES))]
    out_shape = [jax.ShapeDtypeStruct((n, D_MODEL), F32),
                 jax.ShapeDtypeStruct((n * ROW_TILES, LANES), F32),
                 jax.ShapeDtypeStruct((n, LANES), I32),
                 jax.ShapeDtypeStruct((n, LANES), I32),
                 jax.ShapeDtypeStruct((n, LANES), F32),
                 jax.ShapeDtypeStruct((SUBLANES, LANES), I32)]
    return pl.pallas_call(
        functools.partial(_mid_kernel, dils=tuple(dils)),
        grid=(n // tm,),
        in_specs=in_specs,
        out_specs=out_specs,
        out_shape=out_shape,
        scratch_shapes=[pltpu.VMEM((SUBLANES, LANES), F32),
                        pltpu.VMEM((n_branch, ATTN_WIDTH // LANES + 1, tm, LANES), F32)],
        compiler_params=pltpu.CompilerParams(dimension_semantics=("arbitrary",),
                                             vmem_limit_bytes=VMEM_LIMIT),
        name="mid",
    )(x2d, y2d, *attn_o, *attn_lse, w["w_glu"], w["b_glu"], w["ln_ssm_out"], w["ln_attn_out"],
      w["w_out"], w["ln_moe"], w["wr_hi"], w["wr_lo"], w["b_router"], w["expand"], w["tri"])


def _dispatch_kernel(dest_ref, hp_ref, hs_ref, xs_hbm, sem, *, n_prompt_tiles):
    i = pl.program_id(0)
    npair = TOKEN_TILE * TOP_K

    def run(src_ref):
        def issue(t, c):
            src = src_ref.at[pl.ds(pl.multiple_of(t * ROW_TILES, ROW_TILES), ROW_TILES), :]
            for k in range(TOP_K):
                d = dest_ref[0, t * TOP_K + k]
                pltpu.make_async_copy(
                    src, xs_hbm.at[pl.ds(pl.multiple_of(d * ROW_TILES, ROW_TILES), ROW_TILES), :],
                    sem).start()
            return c

        lax.fori_loop(0, TOKEN_TILE, issue, 0, unroll=2)
        span = pl.ds(0, npair * ROW_TILES)
        pltpu.make_async_copy(xs_hbm.at[span, :], xs_hbm.at[span, :], sem).wait()

    @pl.when(i < n_prompt_tiles)
    def _():
        run(hp_ref)

    @pl.when(i >= n_prompt_tiles)
    def _():
        run(hs_ref)


def _dispatch(dest, hrow_p, hrow_s):
    n_p = hrow_p.shape[0] // ROW_TILES
    n_s = hrow_s.shape[0] // ROW_TILES
    npt = n_p // TOKEN_TILE
    ntile = (n_p + n_s) // TOKEN_TILE
    npair = TOKEN_TILE * TOP_K
    blk = (TOKEN_TILE * ROW_TILES, LANES)
    return pl.pallas_call(
        functools.partial(_dispatch_kernel, n_prompt_tiles=npt),
        grid=(ntile,),
        in_specs=[pl.BlockSpec((None, 1, npair), lambda i: (i, 0, 0), memory_space=pltpu.SMEM),
                  pl.BlockSpec(blk, lambda i: (jnp.minimum(i, npt - 1), 0)),
                  pl.BlockSpec(blk, lambda i: (jnp.maximum(i - npt, 0), 0))],
        out_specs=pl.BlockSpec(memory_space=pl.ANY),
        out_shape=jax.ShapeDtypeStruct(((n_p + n_s) * TOP_K * ROW_TILES, LANES), F32),
        scratch_shapes=[pltpu.SemaphoreType.DMA(())],
        compiler_params=pltpu.CompilerParams(dimension_semantics=("arbitrary",),
                                             vmem_limit_bytes=VMEM_LIMIT),
        name="moe_dispatch",
    )(dest.reshape(ntile, 1, npair), hrow_p, hrow_s)


def _expert_kernel(vt_ref, ve_ref, vok_ref, gs_ref, xs_ref, wu_ref, bu_ref, wd_ref, bd_ref,
                   out_ref, wu_s, wd_s, x_s):
    v = pl.program_id(0)
    e = ve_ref[v]
    j = vt_ref[v]
    vprev = jnp.maximum(v - 1, 0)
    new_e = (v == 0) | (e != ve_ref[vprev])
    new_j = (v == 0) | (j != vt_ref[vprev])
    tm = MOE_TILE

    @pl.when(new_e)
    def _():
        wu_s[...] = wu_ref[...].astype(BF16)
        wd_s[...] = wd_ref[...].astype(BF16)

    @pl.when(new_j)
    def _():
        out_ref[...] = jnp.zeros_like(out_ref)

    sub = x_s.shape[0]
    for part in range(tm // sub):
        base = part * sub * ROW_TILES
        first_row = j * tm + part * sub

        @pl.when((vok_ref[v] == 1) & (gs_ref[e] < first_row + sub) & (gs_ref[e + 1] > first_row))
        def _(part=part, base=base):
            for s in range(ROW_TILES):
                x_s[:, s * LANES:(s + 1) * LANES] = xs_ref[
                    pl.ds(base + s, sub, stride=ROW_TILES), :].astype(BF16)
            a = jnp.dot(x_s[...], wu_s[...], preferred_element_type=F32) + bu_ref[...]
            g = jnp.minimum(a[:, :EXPERT_FF], SWIGLU_LIMIT)
            lin = jnp.clip(a[:, EXPERT_FF:], -SWIGLU_LIMIT, SWIGLU_LIMIT)
            act = (lin + 1.0) * (g * _sigmoid(SWIGLU_ALPHA * g))
            y = jnp.dot(act.astype(BF16), wd_s[...], preferred_element_type=F32) + bd_ref[...]
            rows = j * tm + part * sub + lax.broadcasted_iota(I32, (sub, 1), 0)
            mine = (rows >= gs_ref[e]) & (rows < gs_ref[e + 1])
            for s in range(ROW_TILES):
                cur = out_ref[pl.ds(base + s, sub, stride=ROW_TILES), :]
                out_ref[pl.ds(base + s, sub, stride=ROW_TILES), :] = jnp.where(
                    mine, y[:, s * LANES:(s + 1) * LANES], cur)


def _experts(xs, vt, ve, vok, gstart, w_up, b_up, w_down, b_down):
    tm = MOE_TILE
    nvisit = vt.shape[0]
    rows = pl.BlockSpec((tm * ROW_TILES, LANES), lambda v, vt, ve, vok, gs: (vt[v], 0))
    per_e = lambda a, b: pl.BlockSpec((None, a, b), lambda v, vt, ve, vok, gs: (ve[v], 0, 0))
    grid_spec = pltpu.PrefetchScalarGridSpec(
        num_scalar_prefetch=4,
        grid=(nvisit,),
        in_specs=[rows, per_e(D_MODEL, 2 * EXPERT_FF), per_e(1, 2 * EXPERT_FF),
                  per_e(EXPERT_FF, D_MODEL), per_e(1, D_MODEL)],
        out_specs=rows,
        scratch_shapes=[pltpu.VMEM((D_MODEL, 2 * EXPERT_FF), BF16),
                        pltpu.VMEM((EXPERT_FF, D_MODEL), BF16),
                        pltpu.VMEM((MOE_SUBTILE, D_MODEL), BF16)],
    )
    return pl.pallas_call(
        _expert_kernel,
        grid_spec=grid_spec,
        out_shape=jax.ShapeDtypeStruct(xs.shape, F32),
        compiler_params=pltpu.CompilerParams(dimension_semantics=("arbitrary",),
                                             vmem_limit_bytes=VMEM_LIMIT),
        name="moe_experts",
    )(vt, ve, vok, gstart, xs, w_up, b_up.reshape(N_EXPERTS, 1, 2 * EXPERT_FF),
      w_down, b_down.reshape(N_EXPERTS, 1, D_MODEL))


def _routing(eidx_p, epos_p, cnt_p, eidx_s, epos_s, cnt_s, n_rows):
    cnt_p = cnt_p[0, :N_EXPERTS]
    cnt_s = cnt_s[0, :N_EXPERTS]
    cnt = cnt_p + cnt_s
    gend = jnp.cumsum(cnt)
    gstart = gend - cnt
    experts = jnp.arange(N_EXPERTS, dtype=I32)

    def lookup(table, idx):
        return jnp.sum(jnp.where(idx[..., None] == experts, table, 0), axis=-1)

    ep = eidx_p[:, :TOP_K]
    es = eidx_s[:, :TOP_K]
    dest_p = lookup(gstart, ep) + epos_p[:, :TOP_K]
    dest_s = lookup(gstart + cnt_p, es) + epos_s[:, :TOP_K]
    ntile = n_rows // MOE_TILE
    nvisit = ntile + N_EXPERTS
    first = gstart // MOE_TILE
    last = jnp.maximum(gend - 1, 0) // MOE_TILE
    nv = jnp.where(cnt > 0, last - first + 1, 0)
    vend = jnp.cumsum(nv)
    vstart = vend - nv
    total = vend[-1]
    v = jnp.arange(nvisit, dtype=I32)
    vc = jnp.minimum(v, total - 1)
    ve = jnp.sum((vend[None, :] <= vc[:, None]).astype(I32), axis=1)
    vt = (lookup(first - vstart, ve) + vc).astype(I32)
    vok = (v < total).astype(I32)
    gs = jnp.concatenate([gstart, gend[-1:]]).astype(I32)
    return dest_p.astype(I32), dest_s.astype(I32), vt, ve, vok, gs


def _out_kernel(dest_ref, next_ref, y_hbm, x1_ref, gate_ref, pe_ref, lnp_ref, wg_ref, bg_ref, wp_ref,
                lnf_ref, o_ref, buf, sem, *, ntile):
    tm = TOKEN_TILE
    npair = tm * TOP_K
    i = pl.program_id(0)
    slot = i % 2

    def gather(idx_ref, to):
        def issue(t, c):
            for k in range(TOP_K):
                d = idx_ref[0, t * TOP_K + k]
                pltpu.make_async_copy(
                    y_hbm.at[pl.ds(pl.multiple_of(d * ROW_TILES, ROW_TILES), ROW_TILES), :],
                    buf.at[to, pl.ds(pl.multiple_of((k * tm + t) * ROW_TILES, ROW_TILES), ROW_TILES), :],
                    sem.at[to]).start()
            return c

        lax.fori_loop(0, tm, issue, 0, unroll=2)

    @pl.when(i == 0)
    def _():
        gather(dest_ref, 0)

    if ntile > 1:
        @pl.when(i + 1 < ntile)
        def _():
            gather(next_ref, 1 - slot)

    pltpu.make_async_copy(y_hbm.at[pl.ds(0, npair * ROW_TILES), :], buf.at[slot], sem.at[slot]).wait()

    gates = gate_ref[...]
    parts = []
    for s in range(ROW_TILES):
        acc = jnp.zeros((tm, LANES), F32)
        for k in range(TOP_K):
            rows = buf[slot, pl.ds(k * tm * ROW_TILES + s, tm, stride=ROW_TILES), :]
            acc = acc + gates[:, k:k + 1] * rows
        parts.append(acc)
    x2 = x1_ref[...] + jnp.concatenate(parts, axis=1)
    gate = _sigmoid(jnp.dot(_rms(x2, lnp_ref[...]).astype(BF16), wg_ref[...],
                            preferred_element_type=F32) + bg_ref[...])
    x3 = x2 + gate * jnp.dot(pe_ref[...].astype(BF16), wp_ref[...], preferred_element_type=F32)
    o_ref[...] = _rms(x3, lnf_ref[...])


def _combine(dest, y_rows, x1, egate, pe, w):
    n = x1.shape[0]
    tm = TOKEN_TILE
    npair = tm * TOP_K
    row = lambda width: pl.BlockSpec((tm, width), lambda i: (i, 0))
    ntile = n // tm
    dest3 = dest.reshape(ntile, 1, npair)
    return pl.pallas_call(
        functools.partial(_out_kernel, ntile=ntile),
        grid=(ntile,),
        in_specs=[pl.BlockSpec((None, 1, npair), lambda i: (i, 0, 0), memory_space=pltpu.SMEM),
                  pl.BlockSpec((None, 1, npair), lambda i: (jnp.minimum(i + 1, ntile - 1), 0, 0),
                               memory_space=pltpu.SMEM),
                  pl.BlockSpec(memory_space=pl.ANY),
                  row(D_MODEL), row(LANES), row(PLE_DIM),
                  _full((1, D_MODEL)), _full((D_MODEL, D_MODEL)), _full((1, D_MODEL)),
                  _full((PLE_DIM, D_MODEL)), _full((1, D_MODEL))],
        out_specs=row(D_MODEL),
        out_shape=jax.ShapeDtypeStruct((n, D_MODEL), F32),
        scratch_shapes=[pltpu.VMEM((2, npair * ROW_TILES, LANES), F32), pltpu.SemaphoreType.DMA((2,))],
        compiler_params=pltpu.CompilerParams(dimension_semantics=("arbitrary",),
                                             vmem_limit_bytes=VMEM_LIMIT),
        name="combine_out",
    )(dest3, dest3, y_rows, x1, egate, pe, w["ln_ple"], w["w_ple_gate"],
      w["b_ple_gate"], w["w_ple_proj"], w["ln_final"])


def kernel(x_prompt, x_sample, cache_attn_k, cache_attn_v, state_ssm_re, state_ssm_im, p_prompt, p_sample, ln_mix, w_in, ssm_a_re, ssm_a_im, ssm_b_re, ssm_b_im, ssm_c_re, ssm_c_im, ssm_d, ssm_log_dt, w_glu, b_glu, ln_ssm_out, ln_attn_out, w_out, ln_moe, w_router, b_router, w_up, b_up, w_down, b_down, ln_ple, w_ple_gate, b_ple_gate, w_ple_proj, ln_final):
    bsz, s_len, _ = x_prompt.shape
    dbsz, dt_len, _ = x_sample.shape
    n_p, n_s = bsz * s_len, dbsz * dt_len
    wb = cache_attn_k.shape[2]
    wb_prompt = min(WINDOWS[-1], s_len)

    wr = jnp.pad(w_router[0], ((0, 0), (0, LANES - N_EXPERTS)))
    wr_hi = wr.astype(BF16)
    ti = jnp.arange(MID_ROW_BLOCK)
    w = {
        "w_glu": w_glu[0].astype(BF16), "b_glu": b_glu[0].reshape(1, -1),
        "ln_ssm_out": ln_ssm_out[0].reshape(1, -1), "ln_attn_out": ln_attn_out[0].reshape(1, -1),
        "w_out": w_out[0].astype(BF16), "ln_moe": ln_moe[0].reshape(1, -1),
        "wr_hi": wr_hi, "wr_lo": (wr - wr_hi.astype(F32)).astype(BF16),
        "b_router": jnp.pad(b_router[0], (0, LANES - N_EXPERTS)).reshape(1, -1),
        "expand": (jnp.arange(LANES)[:, None] == jnp.arange(ATTN_WIDTH)[None, :] // HEAD_DIM).astype(BF16),
        "tri": (ti[:, None] > ti[None, :]).astype(BF16),
        "ln_ple": ln_ple[0].reshape(1, -1), "w_ple_gate": w_ple_gate[0].astype(BF16),
        "b_ple_gate": b_ple_gate[0].reshape(1, -1), "w_ple_proj": w_ple_proj[0].astype(BF16),
        "ln_final": ln_final.reshape(1, -1),
    }
    w_in_b = w_in[0].astype(BF16)
    bmat, cmat, ab_re, ab_im = _s5_params(ssm_a_re[0], ssm_a_im[0], ssm_b_re[0], ssm_b_im[0],
                                          ssm_c_re[0], ssm_c_im[0], ssm_log_dt[0])

    def coeff(a, nb):
        return jnp.broadcast_to(a, (2, nb, STATE_HALF)).reshape(2 * nb, STATE_HALF)

    proj_p = _in_proj(x_prompt.reshape(n_p, D_MODEL), ln_mix[0], w_in_b, dils=DILATIONS[1:],
                      seq_window=(s_len, wb_prompt))
    u_p, k_p, v_p = proj_p[:3]
    qkv = [proj_p[3:6]] + [proj_p[6 + 3 * i:9 + 3 * i] for i in range(len(DILATIONS) - 1)]
    zeros_state = jnp.zeros((bsz, SSM_GROUPS, SSM_STATE), F32)
    y_p, ht_p = _s5(u_p.reshape(bsz, s_len, SSM_WIDTH), _state_to_rows(zeros_state, zeros_state),
                    bmat, cmat, coeff(ab_re, bsz), coeff(ab_im, bsz), ssm_d[0])
    branches = [_attn_prompt_branch(*qkv[i], bsz, s_len, d) for i, d in enumerate(DILATIONS)]
    x1_p, hrow_p, eidx_p, epos_p, egate_p, cnt_p = _mid(
        x_prompt.reshape(n_p, D_MODEL), y_p.reshape(n_p, SSM_WIDTH),
        [b[0] for b in branches], [b[1] for b in branches], DILATIONS, w)

    u_s, k_s, v_s, qb_s, kb_s, vb_s = _in_proj(x_sample.reshape(n_s, D_MODEL), ln_mix[0], w_in_b)
    y_s, ht_s = _s5(u_s.reshape(dbsz, dt_len, SSM_WIDTH), _state_to_rows(state_ssm_re[0], state_ssm_im[0]),
                    bmat, cmat, coeff(ab_re, dbsz), coeff(ab_im, dbsz), ssm_d[0])
    as3 = lambda t: t.reshape(dbsz, dt_len, ATTN_WIDTH)
    attn_s = _attn_sample(as3(qb_s), as3(kb_s), as3(vb_s), cache_attn_k[0], cache_attn_v[0])
    x1_s, hrow_s, eidx_s, epos_s, egate_s, cnt_s = _mid(
        x_sample.reshape(n_s, D_MODEL), y_s.reshape(n_s, SSM_WIDTH),
        [attn_s.reshape(n_s, ATTN_WIDTH)], [], (), w)

    n_rows = (n_p + n_s) * TOP_K
    dest_p, dest_s, vt, ve, vok, gs = _routing(eidx_p, epos_p, cnt_p, eidx_s, epos_s, cnt_s, n_rows)
    xs = _dispatch(jnp.concatenate([dest_p.reshape(-1), dest_s.reshape(-1)]), hrow_p, hrow_s)
    y_rows = _experts(xs, vt, ve, vok, gs, w_up[0], b_up[0], w_down[0], b_down[0])

    out_p = _combine(dest_p, y_rows, x1_p, egate_p, p_prompt[0].reshape(n_p, PLE_DIM), w)
    out_s = _combine(dest_s, y_rows, x1_s, egate_s, p_sample[0].reshape(n_s, PLE_DIM), w)

    hr_p, hi_p = _rows_to_state(ht_p, bsz)
    hr_s, hi_s = _rows_to_state(ht_s, dbsz)
    kv_p = lambda t: jnp.transpose(t.reshape(bsz, N_HEADS, HEAD_DIM, wb_prompt), (0, 3, 1, 2))[None]
    kv_s = lambda t: t.reshape(dbsz, dt_len, N_HEADS, HEAD_DIM)[None]
    return (out_p.reshape(bsz, s_len, D_MODEL), out_s.reshape(dbsz, dt_len, D_MODEL),
            kv_p(k_p), kv_p(v_p), hr_p[None], hi_p[None],
            kv_s(k_s), kv_s(v_s), hr_s[None], hi_s[None])
```

```python
import functools
import math

import jax
import jax.numpy as jnp
from jax import lax
from jax.experimental import pallas as pl
from jax.experimental.pallas import tpu as pltpu

F32 = jnp.float32
BF16 = jnp.bfloat16
I32 = jnp.int32

D_MODEL = 1024
SSM_WIDTH = 512
SSM_GROUP = 16
SSM_GROUPS = 32
SSM_STATE = 64
ATTN_WIDTH = 512
HEAD_DIM = 64
N_HEADS = 8
IN_WIDTH = SSM_WIDTH + 3 * ATTN_WIDTH
DILATIONS = (1, 4, 16)
WINDOWS = (128, 512, 2048)
ATTN_BLOCK = 128
ATTN_STEP_BLOCKS = 2
N_EXPERTS = 32
TOP_K = 4
EXPERT_FF = D_MODEL
SWIGLU_LIMIT = 7.0
SWIGLU_ALPHA = 1.702
PLE_DIM = 256
EPS = 1e-6
MASK_VALUE = -1e30
NEG_BIG = -3.0e38

LANES = 128
SUBLANES = 8
ROW_TILES = D_MODEL // LANES
TOKEN_TILE = 256
IN_TILE = 1024
IN_ROW_CHUNK = 512
MID_ROW_BLOCK = 256
MOE_TILE = 512
MOE_SUBTILE = 512
SSM_HALF = SSM_WIDTH // 2
STATE_HALF = SSM_GROUPS * SSM_STATE // 2
ALIBI_SLOPES = tuple(2.0 ** (-8.0 * (h + 1) / N_HEADS) for h in range(N_HEADS))
VMEM_LIMIT = 56 * 1024 * 1024


def _rms(x, g):
    return x * lax.rsqrt(jnp.mean(x * x, axis=-1, keepdims=True) + EPS) * g


def _sigmoid(x):
    return 1.0 / (1.0 + jnp.exp(-x))


def _full(shape):
    n = len(shape)
    return pl.BlockSpec(shape, lambda *_: (0,) * n)


def _in_kernel(x_ref, g_ref, w_ref, u_ref, k_ref, v_ref, qb_ref, kb_ref, vb_ref, *rest, dils, window_tiles):
    tm = x_ref.shape[0]
    rc = min(tm, IN_ROW_CHUNK)
    tiles = ATTN_WIDTH // LANES
    for c in range(tm // rc):
        rows = slice(c * rc, (c + 1) * rc)
        h = _rms(x_ref[rows, :], g_ref[...]).astype(BF16)
        p = jnp.dot(h, w_ref[...], preferred_element_type=F32)
        u_ref[rows, :] = p[:, :SSM_WIDTH]
        q = p[:, SSM_WIDTH:SSM_WIDTH + ATTN_WIDTH] * (HEAD_DIM ** -0.5)
        k = p[:, SSM_WIDTH + ATTN_WIDTH:SSM_WIDTH + 2 * ATTN_WIDTH]
        v = p[:, SSM_WIDTH + 2 * ATTN_WIDTH:]
        if window_tiles is None:
            k_ref[rows, :] = k
            v_ref[rows, :] = v
        else:
            k_ref[:, rows] = k.T
            v_ref[:, rows] = v.T
        qb_ref[rows, :] = q.astype(BF16)
        kb_ref[rows, :] = k.astype(BF16)
        vb_ref[rows, :] = v.astype(BF16)
        if not dils:
            continue
        scr = rest[-1]
        for a, val in enumerate((q, k, v)):
            for ct in range(tiles):
                scr[c, a * tiles + ct] = val[:, ct * LANES:(ct + 1) * LANES]
        for di, d in enumerate(dils):
            out_rows = slice(c * rc // d, (c + 1) * rc // d)
            for a in range(3):
                out = rest[di * 3 + a]
                for r in range(d):
                    for ct in range(tiles):
                        piece = scr[c, a * tiles + ct, pl.ds(r, rc // d, stride=d), :]
                        c0 = r * ATTN_WIDTH + ct * LANES
                        out[out_rows, c0:c0 + LANES] = piece.astype(BF16)


def _in_proj(x2d, ln_mix, w_in_bf16, dils=(), seq_window=None):
    n = x2d.shape[0]
    tm = min(n, IN_TILE)
    row = lambda w: pl.BlockSpec((tm, w), lambda i: (i, 0))
    window_tiles = None
    kv_spec, kv_shape = row(ATTN_WIDTH), jax.ShapeDtypeStruct((n, ATTN_WIDTH), F32)
    if seq_window is not None:
        s_len, window = seq_window
        seq_tiles, first = s_len // tm, (s_len - window) // tm
        window_tiles = (seq_tiles, first)
        kv_spec = pl.BlockSpec(
            (None, ATTN_WIDTH, tm),
            lambda i: (i // seq_tiles, 0, jnp.maximum(i % seq_tiles - first, 0)))
        kv_shape = jax.ShapeDtypeStruct((n // s_len, ATTN_WIDTH, window), F32)
    out_specs = [row(SSM_WIDTH), kv_spec, kv_spec] + [row(ATTN_WIDTH)] * 3
    out_shape = ([jax.ShapeDtypeStruct((n, SSM_WIDTH), F32), kv_shape, kv_shape]
                 + [jax.ShapeDtypeStruct((n, ATTN_WIDTH), BF16)] * 3)
    for d in dils:
        out_specs += [pl.BlockSpec((tm // d, d * ATTN_WIDTH), lambda i: (i, 0))] * 3
        out_shape += [jax.ShapeDtypeStruct((n // d, d * ATTN_WIDTH), BF16)] * 3
    rc = min(tm, IN_ROW_CHUNK)
    scratch = [pltpu.VMEM((tm // rc, 3 * ATTN_WIDTH // LANES, rc, LANES), F32)] if dils else []
    return pl.pallas_call(
        functools.partial(_in_kernel, dils=tuple(dils), window_tiles=window_tiles),
        grid=(n // tm,),
        in_specs=[row(D_MODEL), _full((1, D_MODEL)), _full((D_MODEL, IN_WIDTH))],
        out_specs=out_specs,
        out_shape=out_shape,
        scratch_shapes=scratch,
        compiler_params=pltpu.CompilerParams(dimension_semantics=("arbitrary",),
                                             vmem_limit_bytes=VMEM_LIMIT),
        name="in_proj",
    )(x2d, ln_mix.reshape(1, D_MODEL), w_in_bf16)


def _s5_kernel(u_ref, bmat_ref, cmat_ref, are_ref, aim_ref, h0_ref, d_ref,
               y_ref, ht_ref, buf, hc, tmp, *, nb, tt, batched):
    rows = 2 * nb
    ntile = 2 * STATE_HALF // LANES
    half_tiles = ntile // 2

    def lane_tile(c):
        return slice(c * LANES, (c + 1) * LANES)

    @pl.when(pl.program_id(0) == 0)
    def _():
        hc[...] = h0_ref[...]

    if batched:
        u_all = u_ref[...].reshape(nb * tt, SSM_WIDTH)
        ub_all = u_all.astype(BF16)
        for hf in range(2):
            bu = jnp.dot(ub_all[:, hf * SSM_HALF:(hf + 1) * SSM_HALF], bmat_ref[hf],
                         preferred_element_type=F32)
            for c in range(ntile):
                tmp[c] = bu[:, lane_tile(c)]
            for c in range(ntile):
                for t in range(tt):
                    buf[c, t * rows + hf * nb:t * rows + (hf + 1) * nb, :] = tmp[c, pl.ds(t, nb, stride=tt), :]
    else:
        ub_all = u_ref[...].reshape(nb * tt, SSM_WIDTH).astype(BF16)
        for hf in range(2):
            bu = jnp.dot(ub_all[:, hf * SSM_HALF:(hf + 1) * SSM_HALF], bmat_ref[hf],
                         preferred_element_type=F32)
            for b in range(nb):
                for c in range(ntile):
                    buf[c, pl.ds(hf * nb + b, tt, stride=rows), :] = bu[b * tt:(b + 1) * tt, lane_tile(c)]

    group = 8
    for s in range(rows // SUBLANES):
        r0 = s * SUBLANES
        for c0 in range(0, half_tiles, group):
            ar = [are_ref[r0:r0 + SUBLANES, lane_tile(c0 + k)] for k in range(group)]
            ai = [aim_ref[r0:r0 + SUBLANES, lane_tile(c0 + k)] for k in range(group)]
            init = tuple(hc[r0:r0 + SUBLANES, lane_tile(c0 + k)] for k in range(group)) + tuple(
                hc[r0:r0 + SUBLANES, lane_tile(half_tiles + c0 + k)] for k in range(group))

            def step(t, carry, r0=r0, c0=c0, ar=ar, ai=ai):
                row = pl.multiple_of(t * rows + r0, SUBLANES)
                out_r, out_i = [], []
                for k in range(group):
                    hr, hi = carry[k], carry[group + k]
                    xr = buf[c0 + k, pl.ds(row, SUBLANES), :]
                    xi = buf[half_tiles + c0 + k, pl.ds(row, SUBLANES), :]
                    nr = ar[k] * hr - ai[k] * hi + xr
                    ni = ar[k] * hi + ai[k] * hr + xi
                    buf[c0 + k, pl.ds(row, SUBLANES), :] = nr
                    buf[half_tiles + c0 + k, pl.ds(row, SUBLANES), :] = ni
                    out_r.append(nr)
                    out_i.append(ni)
                return tuple(out_r) + tuple(out_i)

            fin = lax.fori_loop(0, tt, step, init, unroll=min(tt, 8))
            for k in range(group):
                hc[r0:r0 + SUBLANES, lane_tile(c0 + k)] = fin[k]
                hc[r0:r0 + SUBLANES, lane_tile(half_tiles + c0 + k)] = fin[group + k]

    if batched:
        parts = []
        for hf in range(2):
            for c in range(ntile):
                for t in range(tt):
                    tmp[c, pl.ds(t, nb, stride=tt), :] = buf[c, t * rows + hf * nb:t * rows + (hf + 1) * nb, :]
            hs = jnp.concatenate([tmp[c] for c in range(ntile)], axis=1).astype(BF16)
            parts.append(jnp.dot(hs, cmat_ref[hf], preferred_element_type=F32))
        y_all = jnp.concatenate(parts, axis=1) + d_ref[...] * u_all
        y_ref[...] = y_all.reshape(nb, tt, SSM_WIDTH)
    else:
        parts = []
        for hf in range(2):
            hs = jnp.concatenate(
                [jnp.concatenate([buf[c, pl.ds(hf * nb + b, tt, stride=rows), :] for c in range(ntile)],
                                 axis=1).astype(BF16) for b in range(nb)], axis=0)
            parts.append(jnp.dot(hs, cmat_ref[hf], preferred_element_type=F32))
        y_all = jnp.concatenate(parts, axis=1) + d_ref[...] * u_ref[...].reshape(nb * tt, SSM_WIDTH)
        y_ref[...] = y_all.reshape(nb, tt, SSM_WIDTH)

    ht_ref[...] = hc[...]


def _s5(u3, h0, bmat, cmat, a_re, a_im, d_skip):
    nb, t_len, _ = u3.shape
    tt = min(t_len, 256)
    rows = 2 * nb
    batched = tt < 16
    kern = functools.partial(_s5_kernel, nb=nb, tt=tt, batched=batched)
    ntile = 2 * STATE_HALF // LANES
    tmp_shape = (ntile, nb * tt, LANES) if batched else (1, SUBLANES, LANES)
    return pl.pallas_call(
        kern,
        grid=(t_len // tt,),
        in_specs=[pl.BlockSpec((nb, tt, SSM_WIDTH), lambda i: (0, i, 0)),
                  _full((2, SSM_HALF, 2 * STATE_HALF)),
                  _full((2, 2 * STATE_HALF, SSM_HALF)),
                  _full((rows, STATE_HALF)), _full((rows, STATE_HALF)),
                  _full((rows, 2 * STATE_HALF)), _full((1, SSM_WIDTH))],
        out_specs=[pl.BlockSpec((nb, tt, SSM_WIDTH), lambda i: (0, i, 0)),
                   _full((rows, 2 * STATE_HALF))],
        out_shape=[jax.ShapeDtypeStruct((nb, t_len, SSM_WIDTH), F32),
                   jax.ShapeDtypeStruct((rows, 2 * STATE_HALF), F32)],
        scratch_shapes=[pltpu.VMEM((ntile, tt * rows, LANES), F32),
                        pltpu.VMEM((rows, 2 * STATE_HALF), F32),
                        pltpu.VMEM(tmp_shape, F32)],
        compiler_params=pltpu.CompilerParams(dimension_semantics=("arbitrary",),
                                             vmem_limit_bytes=VMEM_LIMIT),
        name="s5_scan",
    )(u3, bmat, cmat, a_re, a_im, h0, d_skip.reshape(1, SSM_WIDTH))


def _s5_params(a_re, a_im, b_re, b_im, c_re, c_im, log_dt):
    dt = jnp.exp(log_dt)[:, None]
    mag = jnp.exp(dt * a_re)
    ang = dt * a_im
    ab_re, ab_im = mag * jnp.cos(ang), mag * jnp.sin(ang)
    den = a_re * a_re + a_im * a_im
    nr, ni = ab_re - 1.0, ab_im
    f_re = (nr * a_re + ni * a_im) / den
    f_im = (ni * a_re - nr * a_im) / den
    bb_re = f_re[..., None] * b_re - f_im[..., None] * b_im
    bb_im = f_re[..., None] * b_im + f_im[..., None] * b_re
    gh = SSM_GROUPS // 2
    eye = jnp.eye(gh, dtype=F32)

    def b_half(w):
        return jnp.einsum('gnc,gh->gchn', w, eye).reshape(gh * SSM_GROUP, gh * SSM_STATE)

    def c_half(w):
        return jnp.einsum('gcn,gh->gnhc', w, eye).reshape(gh * SSM_STATE, gh * SSM_GROUP)

    bmat = jnp.stack([jnp.concatenate([b_half(bb_re[h * gh:(h + 1) * gh]),
                                       b_half(bb_im[h * gh:(h + 1) * gh])], axis=1)
                      for h in range(2)]).astype(BF16)
    cmat = jnp.stack([jnp.concatenate([c_half(c_re[h * gh:(h + 1) * gh]),
                                       -c_half(c_im[h * gh:(h + 1) * gh])], axis=0)
                      for h in range(2)]).astype(BF16)
    return bmat, cmat, ab_re.reshape(2, 1, STATE_HALF), ab_im.reshape(2, 1, STATE_HALF)


def _state_to_rows(h_re, h_im):
    nb = h_re.shape[0]
    f = lambda h: h.reshape(nb, 2, STATE_HALF).transpose(1, 0, 2).reshape(2 * nb, STATE_HALF)
    return jnp.concatenate([f(h_re), f(h_im)], axis=1)


def _rows_to_state(ht, nb):
    f = lambda h: h.reshape(2, nb, STATE_HALF).transpose(1, 0, 2).reshape(nb, SSM_GROUPS, SSM_STATE)
    return f(ht[:, :STATE_HALF]), f(ht[:, STATE_HALF:])


def _attn_prompt_kernel(q_ref, kp_ref, kc_ref, vp_ref, vc_ref, o_ref, l_ref, bias_s, s_s, p_s, *, dil):
    blk = ATTN_BLOCK
    n = pl.program_id(2)

    @pl.when((pl.program_id(0) == 0) & (pl.program_id(1) == 0) & (n == 0))
    def _():
        i_idx = lax.broadcasted_iota(I32, (blk, 2 * blk), 0)
        j_idx = lax.broadcasted_iota(I32, (blk, 2 * blk), 1)
        delta = i_idx - j_idx + blk
        in_band = (delta >= 0) & (delta <= blk)
        dist = (delta * dil).astype(F32)
        for h in range(N_HEADS):
            biased = -ALIBI_SLOPES[h] * dist
            bias_s[0, h] = jnp.where(in_band & (j_idx >= blk), biased, MASK_VALUE)
            bias_s[1, h] = jnp.where(in_band, biased, MASK_VALUE)

    lane = lax.broadcasted_iota(I32, (blk, LANES), 1)
    nqb = q_ref.shape[0] // blk
    for qb in range(nqb):
        which = jnp.minimum(n, 1) if qb == 0 else 1
        rows = slice(qb * blk, (qb + 1) * blk)
        for hp in range(N_HEADS // 2):
            cols = slice(hp * LANES, (hp + 1) * LANES)
            q2 = q_ref[rows, cols]
            before = kp_ref[:, cols] if qb == 0 else kc_ref[(qb - 1) * blk:qb * blk, cols]
            kk = jnp.concatenate([before, kc_ref[rows, cols]], axis=0)
            for half in range(2):
                h = 2 * hp + half
                in_head = (lane >= half * HEAD_DIM) & (lane < (half + 1) * HEAD_DIM)
                qm = jnp.where(in_head, q2, jnp.zeros_like(q2))
                s = lax.dot_general(qm, kk, (((1,), (1,)), ((), ())), preferred_element_type=F32)
                s_s[qb * N_HEADS + h] = s + bias_s[which, h]
    for qb in range(nqb):
        lse_all = jnp.zeros((blk, LANES), F32)
        for h in range(N_HEADS):
            s = s_s[qb * N_HEADS + h]
            m = jnp.max(s, axis=1, keepdims=True)
            p = jnp.exp(s - m)
            l = jnp.sum(p, axis=1, keepdims=True)
            p_s[qb * N_HEADS + h] = (p * (1.0 / l)).astype(BF16)
            lse_all = jnp.where(lane == h, m + jnp.log(l), lse_all)
        l_ref[qb * blk:(qb + 1) * blk, :] = lse_all
    for qb in range(nqb):
        rows = slice(qb * blk, (qb + 1) * blk)
        for hp in range(N_HEADS // 2):
            cols = slice(hp * LANES, (hp + 1) * LANES)
            before = vp_ref[:, cols] if qb == 0 else vc_ref[(qb - 1) * blk:qb * blk, cols]
            vv = jnp.concatenate([before, vc_ref[rows, cols]], axis=0)
            outs = [jnp.dot(p_s[qb * N_HEADS + 2 * hp + half], vv, preferred_element_type=F32)
                    for half in range(2)]
            o_ref[rows, cols] = jnp.where(lane < HEAD_DIM, outs[0], outs[1])


def _attn_prompt_branch(qb, kb, vb, bsz, s_len, dil):
    sub = s_len // dil
    nqb = ATTN_STEP_BLOCKS
    step = nqb * ATTN_BLOCK
    nstep = sub // step
    view = lambda t: t.reshape(bsz, sub, dil * ATTN_WIDTH)
    cur = pl.BlockSpec((None, step, ATTN_WIDTH), lambda b, r, n: (b, n, r))
    prev = pl.BlockSpec((None, ATTN_BLOCK, ATTN_WIDTH),
                        lambda b, r, n: (b, jnp.maximum(n * nqb - 1, 0), r))
    o, lse = pl.pallas_call(
        functools.partial(_attn_prompt_kernel, dil=dil),
        grid=(bsz, dil, nstep),
        in_specs=[cur, prev, cur, prev, cur],
        out_specs=[cur, pl.BlockSpec((None, step, LANES), lambda b, r, n: (b, n, r))],
        out_shape=[jax.ShapeDtypeStruct((bsz, sub, dil * ATTN_WIDTH), F32),
                   jax.ShapeDtypeStruct((bsz, sub, dil * LANES), F32)],
        scratch_shapes=[pltpu.VMEM((2, N_HEADS, ATTN_BLOCK, 2 * ATTN_BLOCK), F32),
                        pltpu.VMEM((nqb * N_HEADS, ATTN_BLOCK, 2 * ATTN_BLOCK), F32),
                        pltpu.VMEM((nqb * N_HEADS, ATTN_BLOCK, 2 * ATTN_BLOCK), BF16)],
        compiler_params=pltpu.CompilerParams(
            dimension_semantics=("arbitrary", "arbitrary", "arbitrary"),
            vmem_limit_bytes=VMEM_LIMIT),
        name=f"attn_prompt_d{dil}",
    )(view(qb), view(kb), view(kb), view(vb), view(vb))
    return o.reshape(bsz * sub, dil * ATTN_WIDTH), lse.reshape(bsz * sub, dil * LANES)


def _attn_sample_kernel(q_ref, kn_ref, vn_ref, kc_ref, vc_ref, o_ref, bias_s, mult_s, *, t_len, wb):
    nrow = N_HEADS * t_len
    t_shift = t_len.bit_length() - 1
    d_shift = HEAD_DIM.bit_length() - 1
    nt = (((1,), (1,)), ((), ()))

    def branch_count(dist):
        mult = jnp.zeros(dist.shape, F32)
        for win, dil in zip(WINDOWS, DILATIONS):
            hit = (dist >= 0) & (dist <= win) & ((dist & (dil - 1)) == 0)
            mult = mult + jnp.where(hit, 1.0, 0.0)
        return mult

    def biased(dist, mult):
        head = lax.broadcasted_iota(I32, dist.shape, 0) >> t_shift
        slope = jnp.zeros(dist.shape, F32)
        for h in range(N_HEADS):
            slope = jnp.where(head == h, ALIBI_SLOPES[h], slope)
        return jnp.where(mult > 0.0, -slope * dist.astype(F32), MASK_VALUE)

    @pl.when(pl.program_id(0) == 0)
    def _():
        row = lax.broadcasted_iota(I32, (nrow, wb), 0)
        col = lax.broadcasted_iota(I32, (nrow, wb), 1)
        dist = wb + (row & (t_len - 1)) - col
        mult = branch_count(dist)
        mult_s[...] = mult
        bias_s[...] = biased(dist, mult)

    q = q_ref[...].astype(F32)
    qt = jnp.concatenate([q] * N_HEADS, axis=0)
    row_w = lax.broadcasted_iota(I32, (nrow, ATTN_WIDTH), 0)
    lane_w = lax.broadcasted_iota(I32, (nrow, ATTN_WIDTH), 1)
    qm = jnp.where((lane_w >> d_shift) == (row_w >> t_shift), qt, 0.0).astype(BF16)

    pad = LANES - t_len
    kn = jnp.concatenate([kn_ref[...].astype(F32), jnp.zeros((pad, ATTN_WIDTH), F32)], axis=0).astype(BF16)
    vn = jnp.concatenate([vn_ref[...].astype(F32), jnp.zeros((pad, ATTN_WIDTH), F32)], axis=0).astype(BF16)
    row_n = lax.broadcasted_iota(I32, (nrow, LANES), 0)
    col_n = lax.broadcasted_iota(I32, (nrow, LANES), 1)
    dist_n = jnp.where(col_n < t_len, (row_n & (t_len - 1)) - col_n, -1)
    mult_n = branch_count(dist_n)
    s_n = lax.dot_general(qm, kn, nt, preferred_element_type=F32) + biased(dist_n, mult_n)
    s_c = jnp.dot(qm, kc_ref[...].astype(BF16), preferred_element_type=F32) + bias_s[...]

    m = jnp.maximum(jnp.max(s_c, axis=1, keepdims=True), jnp.max(s_n, axis=1, keepdims=True))
    p_c = jnp.exp(s_c - m) * mult_s[...]
    p_n = jnp.exp(s_n - m) * mult_n
    l = jnp.sum(p_c, axis=1, keepdims=True) + jnp.sum(p_n, axis=1, keepdims=True)
    o = (lax.dot_general(p_c.astype(BF16), vc_ref[...].astype(BF16), nt, preferred_element_type=F32)
         + jnp.dot(p_n.astype(BF16), vn, preferred_element_type=F32)) / l
    lane_o = lax.broadcasted_iota(I32, (t_len, ATTN_WIDTH), 1) >> d_shift
    out = jnp.zeros((t_len, ATTN_WIDTH), F32)
    for h in range(N_HEADS):
        out = jnp.where(lane_o == h, o[h * t_len:(h + 1) * t_len], out)
    o_ref[...] = out


def _attn_sample(qb, kb, vb, cache_k, cache_v):
    bsz, t_len, _ = qb.shape
    wb = cache_k.shape[1]
    feature_major = lambda c: jnp.transpose(c, (0, 2, 3, 1)).reshape(bsz, ATTN_WIDTH, wb)
    new = pl.BlockSpec((None, t_len, ATTN_WIDTH), lambda b: (b, 0, 0))
    old = pl.BlockSpec((None, ATTN_WIDTH, wb), lambda b: (b, 0, 0))
    return pl.pallas_call(
        functools.partial(_attn_sample_kernel, t_len=t_len, wb=wb),
        grid=(bsz,),
        in_specs=[new, new, new, old, old],
        out_specs=new,
        out_shape=jax.ShapeDtypeStruct((bsz, t_len, ATTN_WIDTH), F32),
        scratch_shapes=[pltpu.VMEM((N_HEADS * t_len, wb), F32),
                        pltpu.VMEM((N_HEADS * t_len, wb), F32)],
        compiler_params=pltpu.CompilerParams(dimension_semantics=("arbitrary",),
                                             vmem_limit_bytes=VMEM_LIMIT),
        name="attn_sample",
    )(qb, kb, vb, feature_major(cache_k), feature_major(cache_v))


def _mid_kernel(*refs, dils):
    n_branch = max(len(dils), 1)
    x_ref, y_ref = refs[0], refs[1]
    o_refs = refs[2:2 + n_branch]
    pos = 2 + n_branch
    l_refs = refs[pos:pos + len(dils)]
    pos += len(l_refs)
    (wglu_ref, bglu_ref, lns_ref, lna_ref, wout_ref, lnm_ref, wrh_ref, wrl_ref, br_ref,
     expand_ref, tri_ref,
     x1_ref, hrow_ref, eidx_ref, epos_ref, egate_ref, cnt_ref, carry, nat) = refs[pos:]
    tm = x_ref.shape[0]
    o_tiles = ATTN_WIDTH // LANES

    rb = min(tm, MID_ROW_BLOCK)

    for bi, d in enumerate(dils):
        if d == 1:
            continue
        for r in range(d):
            spread = pl.ds(r, tm // d, stride=d)
            for ct in range(o_tiles):
                c0 = r * ATTN_WIDTH + ct * LANES
                nat[bi, ct, spread, :] = o_refs[bi][:, c0:c0 + LANES]
            nat[bi, o_tiles, spread, :] = l_refs[bi][:, r * LANES:(r + 1) * LANES]

    def natural(bi, d, rows):
        if d == 1:
            return o_refs[bi][rows, :], l_refs[bi][rows, :]
        return (jnp.concatenate([nat[bi, ct, rows, :] for ct in range(o_tiles)], axis=1),
                nat[bi, o_tiles, rows, :])

    @pl.when(pl.program_id(0) == 0)
    def _():
        carry[...] = jnp.zeros_like(carry)

    lane = lax.broadcasted_iota(I32, (rb, LANES), 1)
    lane_f = lane.astype(F32)
    run = carry[0:1, :]
    for blk in range(tm // rb):
        rows = slice(blk * rb, (blk + 1) * rb)
        y = y_ref[rows, :]
        z = y * (0.5 * (1.0 + jnp.tanh(math.sqrt(2.0 / math.pi) * (y + 0.044715 * (y * y * y)))))
        glu = z * _sigmoid(jnp.dot(z.astype(BF16), wglu_ref[...], preferred_element_type=F32)
                           + bglu_ref[...])
        n_ssm = _rms(glu, lns_ref[...])

        if not dils:
            attn = o_refs[0][rows, :]
        else:
            pairs = [natural(bi, d, rows) for bi, d in enumerate(dils)]
            lses = [p[1] for p in pairs]
            mx = functools.reduce(jnp.maximum, lses)
            es = [jnp.exp(l - mx) for l in lses]
            inv = 1.0 / functools.reduce(lambda a, b: a + b, es)
            attn = jnp.zeros((rb, ATTN_WIDTH), F32)
            for e, (o_nat, _) in zip(es, pairs):
                w = e * inv
                w_hi = w.astype(BF16)
                w_lo = (w - w_hi.astype(F32)).astype(BF16)
                wide = (jnp.dot(w_hi, expand_ref[...], preferred_element_type=F32)
                        + jnp.dot(w_lo, expand_ref[...], preferred_element_type=F32))
                attn = attn + wide * o_nat
        n_attn = _rms(attn, lna_ref[...])

        x1 = (x_ref[rows, :]
              + jnp.dot(n_ssm.astype(BF16), wout_ref[:SSM_WIDTH, :], preferred_element_type=F32)
              + jnp.dot(n_attn.astype(BF16), wout_ref[SSM_WIDTH:, :], preferred_element_type=F32))
        x1_ref[rows, :] = x1
        hm = _rms(x1, lnm_ref[...])
        for s in range(ROW_TILES):
            hrow_ref[pl.ds(blk * rb * ROW_TILES + s, rb, stride=ROW_TILES), :] = hm[:, s * LANES:(s + 1) * LANES]

        h_hi = hm.astype(BF16)
        h_lo = (hm - h_hi.astype(F32)).astype(BF16)
        logits = (jnp.dot(h_hi, wrh_ref[...], preferred_element_type=F32)
                  + jnp.dot(h_lo, wrh_ref[...], preferred_element_type=F32)
                  + jnp.dot(h_hi, wrl_ref[...], preferred_element_type=F32)
                  + br_ref[...])
        work = jnp.where(lane < N_EXPERTS, logits, NEG_BIG)
        vals, idxs, hots = [], [], []
        for _ in range(TOP_K):
            m = jnp.max(work, axis=1, keepdims=True)
            idx = jnp.min(jnp.where(work == m, lane_f, float(LANES)), axis=1, keepdims=True)
            hot = lane_f == idx
            vals.append(m)
            idxs.append(idx)
            hots.append(hot)
            work = jnp.where(hot, NEG_BIG, work)
        exps = [jnp.exp(v - vals[0]) for v in vals]
        inv = 1.0 / functools.reduce(lambda a, b: a + b, exps)

        sel = functools.reduce(lambda a, b: a + b, [h.astype(F32) for h in hots])
        before = jnp.dot(tri_ref[...], sel.astype(BF16), preferred_element_type=F32) + run
        eidx = jnp.zeros((rb, LANES), I32)
        epos = jnp.zeros((rb, LANES), I32)
        egate = jnp.zeros((rb, LANES), F32)
        for k in range(TOP_K):
            pk = jnp.sum(jnp.where(hots[k], before, 0.0), axis=1, keepdims=True)
            eidx = jnp.where(lane == k, idxs[k].astype(I32), eidx)
            epos = jnp.where(lane == k, pk.astype(I32), epos)
            egate = jnp.where(lane == k, exps[k] * inv, egate)
        eidx_ref[rows, :] = eidx
        epos_ref[rows, :] = epos
        egate_ref[rows, :] = egate
        run = run + jnp.sum(sel, axis=0, keepdims=True)
    carry[...] = jnp.broadcast_to(run, carry.shape)
    cnt_ref[...] = jnp.broadcast_to(run, cnt_ref.shape).astype(I32)


def _mid(x2d, y2d, attn_o, attn_lse, dils, w):
    n = x2d.shape[0]
    tm = TOKEN_TILE
    n_branch = len(attn_o)
    row = lambda width: pl.BlockSpec((tm, width), lambda i: (i, 0))
    packed = lambda d, width: pl.BlockSpec((tm // d, d * width), lambda i: (i, 0))
    attn_specs = ([packed(d, ATTN_WIDTH) for d in dils] + [packed(d, LANES) for d in dils]
                  if dils else [row(ATTN_WIDTH)])
    in_specs = ([row(D_MODEL), row(SSM_WIDTH)] + attn_specs
                + [_full((SSM_WIDTH, SSM_WIDTH)), _full((1, SSM_WIDTH)), _full((1, SSM_WIDTH)),
                   _full((1, ATTN_WIDTH)), _full((D_MODEL, D_MODEL)), _full((1, D_MODEL)),
                   _full((D_MODEL, LANES)), _full((D_MODEL, LANES)), _full((1, LANES)),
                   _full((LANES, ATTN_WIDTH)), _full((MID_ROW_BLOCK, MID_ROW_BLOCK))])
    out_specs = [row(D_MODEL), pl.BlockSpec((tm * ROW_TILES, LANES), lambda i: (i, 0)),
                 row(LANES), row(LANES), row(LANES), _full((SUBLANES, LANES))]
    out_shape = [jax.ShapeDtypeStruct((n, D_MODEL), F32),
                 jax.ShapeDtypeStruct((n * ROW_TILES, LANES), F32),
                 jax.ShapeDtypeStruct((n, LANES), I32),
                 jax.ShapeDtypeStruct((n, LANES), I32),
                 jax.ShapeDtypeStruct((n, LANES), F32),
                 jax.ShapeDtypeStruct((SUBLANES, LANES), I32)]
    return pl.pallas_call(
        functools.partial(_mid_kernel, dils=tuple(dils)),
        grid=(n // tm,),
        in_specs=in_specs,
        out_specs=out_specs,
        out_shape=out_shape,
        scratch_shapes=[pltpu.VMEM((SUBLANES, LANES), F32),
                        pltpu.VMEM((n_branch, ATTN_WIDTH // LANES + 1, tm, LANES), F32)],
        compiler_params=pltpu.CompilerParams(dimension_semantics=("arbitrary",),
                                             vmem_limit_bytes=VMEM_LIMIT),
        name="mid",
    )(x2d, y2d, *attn_o, *attn_lse, w["w_glu"], w["b_glu"], w["ln_ssm_out"], w["ln_attn_out"],
      w["w_out"], w["ln_moe"], w["wr_hi"], w["wr_lo"], w["b_router"], w["expand"], w["tri"])


def _dispatch_kernel(dest_ref, hp_ref, hs_ref, xs_hbm, sem, *, n_prompt_tiles):
    i = pl.program_id(0)
    npair = TOKEN_TILE * TOP_K

    def run(src_ref):
        def issue(t, c):
            src = src_ref.at[pl.ds(pl.multiple_of(t * ROW_TILES, ROW_TILES), ROW_TILES), :]
            for k in range(TOP_K):
                d = dest_ref[0, t * TOP_K + k]
                pltpu.make_async_copy(
                    src, xs_hbm.at[pl.ds(pl.multiple_of(d * ROW_TILES, ROW_TILES), ROW_TILES), :],
                    sem).start()
            return c

        lax.fori_loop(0, TOKEN_TILE, issue, 0, unroll=2)
        span = pl.ds(0, npair * ROW_TILES)
        pltpu.make_async_copy(xs_hbm.at[span, :], xs_hbm.at[span, :], sem).wait()

    @pl.when(i < n_prompt_tiles)
    def _():
        run(hp_ref)

    @pl.when(i >= n_prompt_tiles)
    def _():
        run(hs_ref)


def _dispatch(dest, hrow_p, hrow_s):
    n_p = hrow_p.shape[0] // ROW_TILES
    n_s = hrow_s.shape[0] // ROW_TILES
    npt = n_p // TOKEN_TILE
    ntile = (n_p + n_s) // TOKEN_TILE
    npair = TOKEN_TILE * TOP_K
    blk = (TOKEN_TILE * ROW_TILES, LANES)
    return pl.pallas_call(
        functools.partial(_dispatch_kernel, n_prompt_tiles=npt),
        grid=(ntile,),
        in_specs=[pl.BlockSpec((None, 1, npair), lambda i: (i, 0, 0), memory_space=pltpu.SMEM),
                  pl.BlockSpec(blk, lambda i: (jnp.minimum(i, npt - 1), 0)),
                  pl.BlockSpec(blk, lambda i: (jnp.maximum(i - npt, 0), 0))],
        out_specs=pl.BlockSpec(memory_space=pl.ANY),
        out_shape=jax.ShapeDtypeStruct(((n_p + n_s) * TOP_K * ROW_TILES, LANES), F32),
        scratch_shapes=[pltpu.SemaphoreType.DMA(())],
        compiler_params=pltpu.CompilerParams(dimension_semantics=("arbitrary",),
                                             vmem_limit_bytes=VMEM_LIMIT),
        name="moe_dispatch",
    )(dest.reshape(ntile, 1, npair), hrow_p, hrow_s)


def _expert_kernel(vt_ref, ve_ref, vok_ref, gs_ref, xs_ref, wu_ref, bu_ref, wd_ref, bd_ref,
                   out_ref, wu_s, wd_s, x_s):
    v = pl.program_id(0)
    e = ve_ref[v]
    j = vt_ref[v]
    vprev = jnp.maximum(v - 1, 0)
    new_e = (v == 0) | (e != ve_ref[vprev])
    new_j = (v == 0) | (j != vt_ref[vprev])
    tm = MOE_TILE

    @pl.when(new_e)
    def _():
        wu_s[...] = wu_ref[...].astype(BF16)
        wd_s[...] = wd_ref[...].astype(BF16)

    @pl.when(new_j)
    def _():
        out_ref[...] = jnp.zeros_like(out_ref)

    sub = x_s.shape[0]
    for part in range(tm // sub):
        base = part * sub * ROW_TILES
        first_row = j * tm + part * sub

        @pl.when((vok_ref[v] == 1) & (gs_ref[e] < first_row + sub) & (gs_ref[e + 1] > first_row))
        def _(part=part, base=base):
            for s in range(ROW_TILES):
                x_s[:, s * LANES:(s + 1) * LANES] = xs_ref[
                    pl.ds(base + s, sub, stride=ROW_TILES), :].astype(BF16)
            a = jnp.dot(x_s[...], wu_s[...], preferred_element_type=F32) + bu_ref[...]
            g = jnp.minimum(a[:, :EXPERT_FF], SWIGLU_LIMIT)
            lin = jnp.clip(a[:, EXPERT_FF:], -SWIGLU_LIMIT, SWIGLU_LIMIT)
            act = (lin + 1.0) * (g * _sigmoid(SWIGLU_ALPHA * g))
            y = jnp.dot(act.astype(BF16), wd_s[...], preferred_element_type=F32) + bd_ref[...]
            rows = j * tm + part * sub + lax.broadcasted_iota(I32, (sub, 1), 0)
            mine = (rows >= gs_ref[e]) & (rows < gs_ref[e + 1])
            for s in range(ROW_TILES):
                cur = out_ref[pl.ds(base + s, sub, stride=ROW_TILES), :]
                out_ref[pl.ds(base + s, sub, stride=ROW_TILES), :] = jnp.where(
                    mine, y[:, s * LANES:(s + 1) * LANES], cur)


def _experts(xs, vt, ve, vok, gstart, w_up, b_up, w_down, b_down):
    tm = MOE_TILE
    nvisit = vt.shape[0]
    rows = pl.BlockSpec((tm * ROW_TILES, LANES), lambda v, vt, ve, vok, gs: (vt[v], 0))
    per_e = lambda a, b: pl.BlockSpec((None, a, b), lambda v, vt, ve, vok, gs: (ve[v], 0, 0))
    grid_spec = pltpu.PrefetchScalarGridSpec(
        num_scalar_prefetch=4,
        grid=(nvisit,),
        in_specs=[rows, per_e(D_MODEL, 2 * EXPERT_FF), per_e(1, 2 * EXPERT_FF),
                  per_e(EXPERT_FF, D_MODEL), per_e(1, D_MODEL)],
        out_specs=rows,
        scratch_shapes=[pltpu.VMEM((D_MODEL, 2 * EXPERT_FF), BF16),
                        pltpu.VMEM((EXPERT_FF, D_MODEL), BF16),
                        pltpu.VMEM((MOE_SUBTILE, D_MODEL), BF16)],
    )
    return pl.pallas_call(
        _expert_kernel,
        grid_spec=grid_spec,
        out_shape=jax.ShapeDtypeStruct(xs.shape, F32),
        compiler_params=pltpu.CompilerParams(dimension_semantics=("arbitrary",),
                                             vmem_limit_bytes=VMEM_LIMIT),
        name="moe_experts",
    )(vt, ve, vok, gstart, xs, w_up, b_up.reshape(N_EXPERTS, 1, 2 * EXPERT_FF),
      w_down, b_down.reshape(N_EXPERTS, 1, D_MODEL))


def _routing(eidx_p, epos_p, cnt_p, eidx_s, epos_s, cnt_s, n_rows):
    cnt_p = cnt_p[0, :N_EXPERTS]
    cnt_s = cnt_s[0, :N_EXPERTS]
    cnt = cnt_p + cnt_s
    gend = jnp.cumsum(cnt)
    gstart = gend - cnt
    experts = jnp.arange(N_EXPERTS, dtype=I32)

    def lookup(table, idx):
        return jnp.sum(jnp.where(idx[..., None] == experts, table, 0), axis=-1)

    ep = eidx_p[:, :TOP_K]
    es = eidx_s[:, :TOP_K]
    dest_p = lookup(gstart, ep) + epos_p[:, :TOP_K]
    dest_s = lookup(gstart + cnt_p, es) + epos_s[:, :TOP_K]
    ntile = n_rows // MOE_TILE
    nvisit = ntile + N_EXPERTS
    first = gstart // MOE_TILE
    last = jnp.maximum(gend - 1, 0) // MOE_TILE
    nv = jnp.where(cnt > 0, last - first + 1, 0)
    vend = jnp.cumsum(nv)
    vstart = vend - nv
    total = vend[-1]
    v = jnp.arange(nvisit, dtype=I32)
    vc = jnp.minimum(v, total - 1)
    ve = jnp.sum((vend[None, :] <= vc[:, None]).astype(I32), axis=1)
    vt = (lookup(first - vstart, ve) + vc).astype(I32)
    vok = (v < total).astype(I32)
    gs = jnp.concatenate([gstart, gend[-1:]]).astype(I32)
    return dest_p.astype(I32), dest_s.astype(I32), vt, ve, vok, gs


def _out_kernel(dest_ref, next_ref, y_hbm, x1_ref, gate_ref, pe_ref, lnp_ref, wg_ref, bg_ref, wp_ref,
                lnf_ref, o_ref, buf, sem, *, ntile):
    tm = TOKEN_TILE
    npair = tm * TOP_K
    i = pl.program_id(0)
    slot = i % 2

    def gather(idx_ref, to):
        def issue(t, c):
            for k in range(TOP_K):
                d = idx_ref[0, t * TOP_K + k]
                pltpu.make_async_copy(
                    y_hbm.at[pl.ds(pl.multiple_of(d * ROW_TILES, ROW_TILES), ROW_TILES), :],
                    buf.at[to, pl.ds(pl.multiple_of((k * tm + t) * ROW_TILES, ROW_TILES), ROW_TILES), :],
                    sem.at[to]).start()
            return c

        lax.fori_loop(0, tm, issue, 0, unroll=2)

    @pl.when(i == 0)
    def _():
        gather(dest_ref, 0)

    if ntile > 1:
        @pl.when(i + 1 < ntile)
        def _():
            gather(next_ref, 1 - slot)

    pltpu.make_async_copy(y_hbm.at[pl.ds(0, npair * ROW_TILES), :], buf.at[slot], sem.at[slot]).wait()

    gates = gate_ref[...]
    parts = []
    for s in range(ROW_TILES):
        acc = jnp.zeros((tm, LANES), F32)
        for k in range(TOP_K):
            rows = buf[slot, pl.ds(k * tm * ROW_TILES + s, tm, stride=ROW_TILES), :]
            acc = acc + gates[:, k:k + 1] * rows
        parts.append(acc)
    x2 = x1_ref[...] + jnp.concatenate(parts, axis=1)
    gate = _sigmoid(jnp.dot(_rms(x2, lnp_ref[...]).astype(BF16), wg_ref[...],
                            preferred_element_type=F32) + bg_ref[...])
    x3 = x2 + gate * jnp.dot(pe_ref[...].astype(BF16), wp_ref[...], preferred_element_type=F32)
    o_ref[...] = _rms(x3, lnf_ref[...])


def _combine(dest, y_rows, x1, egate, pe, w):
    n = x1.shape[0]
    tm = TOKEN_TILE
    npair = tm * TOP_K
    row = lambda width: pl.BlockSpec((tm, width), lambda i: (i, 0))
    ntile = n // tm
    dest3 = dest.reshape(ntile, 1, npair)
    return pl.pallas_call(
        functools.partial(_out_kernel, ntile=ntile),
        grid=(ntile,),
        in_specs=[pl.BlockSpec((None, 1, npair), lambda i: (i, 0, 0), memory_space=pltpu.SMEM),
                  pl.BlockSpec((None, 1, npair), lambda i: (jnp.minimum(i + 1, ntile - 1), 0, 0),
                               memory_space=pltpu.SMEM),
                  pl.BlockSpec(memory_space=pl.ANY),
                  row(D_MODEL), row(LANES), row(PLE_DIM),
                  _full((1, D_MODEL)), _full((D_MODEL, D_MODEL)), _full((1, D_MODEL)),
                  _full((PLE_DIM, D_MODEL)), _full((1, D_MODEL))],
        out_specs=row(D_MODEL),
        out_shape=jax.ShapeDtypeStruct((n, D_MODEL), F32),
        scratch_shapes=[pltpu.VMEM((2, npair * ROW_TILES, LANES), F32), pltpu.SemaphoreType.DMA((2,))],
        compiler_params=pltpu.CompilerParams(dimension_semantics=("arbitrary",),
                                             vmem_limit_bytes=VMEM_LIMIT),
        name="combine_out",
    )(dest3, dest3, y_rows, x1, egate, pe, w["ln_ple"], w["w_ple_gate"],
      w["b_ple_gate"], w["w_ple_proj"], w["ln_final"])


def kernel(x_prompt, x_sample, cache_attn_k, cache_attn_v, state_ssm_re, state_ssm_im, p_prompt, p_sample, ln_mix, w_in, ssm_a_re, ssm_a_im, ssm_b_re, ssm_b_im, ssm_c_re, ssm_c_im, ssm_d, ssm_log_dt, w_glu, b_glu, ln_ssm_out, ln_attn_out, w_out, ln_moe, w_router, b_router, w_up, b_up, w_down, b_down, ln_ple, w_ple_gate, b_ple_gate, w_ple_proj, ln_final):
    bsz, s_len, _ = x_prompt.shape
    dbsz, dt_len, _ = x_sample.shape
    n_p, n_s = bsz * s_len, dbsz * dt_len
    wb = cache_attn_k.shape[2]
    wb_prompt = min(WINDOWS[-1], s_len)

    wr = jnp.pad(w_router[0], ((0, 0), (0, LANES - N_EXPERTS)))
    wr_hi = wr.astype(BF16)
    ti = jnp.arange(MID_ROW_BLOCK)
    w = {
        "w_glu": w_glu[0].astype(BF16), "b_glu": b_glu[0].reshape(1, -1),
        "ln_ssm_out": ln_ssm_out[0].reshape(1, -1), "ln_attn_out": ln_attn_out[0].reshape(1, -1),
        "w_out": w_out[0].astype(BF16), "ln_moe": ln_moe[0].reshape(1, -1),
        "wr_hi": wr_hi, "wr_lo": (wr - wr_hi.astype(F32)).astype(BF16),
        "b_router": jnp.pad(b_router[0], (0, LANES - N_EXPERTS)).reshape(1, -1),
        "expand": (jnp.arange(LANES)[:, None] == jnp.arange(ATTN_WIDTH)[None, :] // HEAD_DIM).astype(BF16),
        "tri": (ti[:, None] > ti[None, :]).astype(BF16),
        "ln_ple": ln_ple[0].reshape(1, -1), "w_ple_gate": w_ple_gate[0].astype(BF16),
        "b_ple_gate": b_ple_gate[0].reshape(1, -1), "w_ple_proj": w_ple_proj[0].astype(BF16),
        "ln_final": ln_final.reshape(1, -1),
    }
    w_in_b = w_in[0].astype(BF16)
    bmat, cmat, ab_re, ab_im = _s5_params(ssm_a_re[0], ssm_a_im[0], ssm_b_re[0], ssm_b_im[0],
                                          ssm_c_re[0], ssm_c_im[0], ssm_log_dt[0])

    def coeff(a, nb):
        return jnp.broadcast_to(a, (2, nb, STATE_HALF)).reshape(2 * nb, STATE_HALF)

    proj_p = _in_proj(x_prompt.reshape(n_p, D_MODEL), ln_mix[0], w_in_b, dils=DILATIONS[1:],
                      seq_window=(s_len, wb_prompt))
    u_p, k_p, v_p = proj_p[:3]
    qkv = [proj_p[3:6]] + [proj_p[6 + 3 * i:9 + 3 * i] for i in range(len(DILATIONS) - 1)]
    zeros_state = jnp.zeros((bsz, SSM_GROUPS, SSM_STATE), F32)
    y_p, ht_p = _s5(u_p.reshape(bsz, s_len, SSM_WIDTH), _state_to_rows(zeros_state, zeros_state),
                    bmat, cmat, coeff(ab_re, bsz), coeff(ab_im, bsz), ssm_d[0])
    branches = [_attn_prompt_branch(*qkv[i], bsz, s_len, d) for i, d in enumerate(DILATIONS)]
    x1_p, hrow_p, eidx_p, epos_p, egate_p, cnt_p = _mid(
        x_prompt.reshape(n_p, D_MODEL), y_p.reshape(n_p, SSM_WIDTH),
        [b[0] for b in branches], [b[1] for b in branches], DILATIONS, w)

    u_s, k_s, v_s, qb_s, kb_s, vb_s = _in_proj(x_sample.reshape(n_s, D_MODEL), ln_mix[0], w_in_b)
    y_s, ht_s = _s5(u_s.reshape(dbsz, dt_len, SSM_WIDTH), _state_to_rows(state_ssm_re[0], state_ssm_im[0]),
                    bmat, cmat, coeff(ab_re, dbsz), coeff(ab_im, dbsz), ssm_d[0])
    as3 = lambda t: t.reshape(dbsz, dt_len, ATTN_WIDTH)
    attn_s = _attn_sample(as3(qb_s), as3(kb_s), as3(vb_s), cache_attn_k[0], cache_attn_v[0])
    x1_s, hrow_s, eidx_s, epos_s, egate_s, cnt_s = _mid(
        x_sample.reshape(n_s, D_MODEL), y_s.reshape(n_s, SSM_WIDTH),
        [attn_s.reshape(n_s, ATTN_WIDTH)], [], (), w)

    n_rows = (n_p + n_s) * TOP_K
    dest_p, dest_s, vt, ve, vok, gs = _routing(eidx_p, epos_p, cnt_p, eidx_s, epos_s, cnt_s, n_rows)
    xs = _dispatch(jnp.concatenate([dest_p.reshape(-1), dest_s.reshape(-1)]), hrow_p, hrow_s)
    y_rows = _experts(xs, vt, ve, vok, gs, w_up[0], b_up[0], w_down[0], b_down[0])

    out_p = _combine(dest_p, y_rows, x1_p, egate_p, p_prompt[0].reshape(n_p, PLE_DIM), w)
    out_s = _combine(dest_s, y_rows, x1_s, egate_s, p_sample[0].reshape(n_s, PLE_DIM), w)

    hr_p, hi_p = _rows_to_state(ht_p, bsz)
    hr_s, hi_s = _rows_to_state(ht_s, dbsz)
    kv_p = lambda t: jnp.transpose(t.reshape(bsz, N_HEADS, HEAD_DIM, wb_prompt), (0, 3, 1, 2))[None]
    kv_s = lambda t: t.reshape(dbsz, dt_len, N_HEADS, HEAD_DIM)[None]
    return (out_p.reshape(bsz, s_len, D_MODEL), out_s.reshape(dbsz, dt_len, D_MODEL),
            kv_p(k_p), kv_p(v_p), hr_p[None], hi_p[None],
            kv_s(k_s), kv_s(v_s), hr_s[None], hi_s[None])
```

```python
import functools
import math

import jax
import jax.numpy as jnp
from jax import lax
from jax.experimental import pallas as pl
from jax.experimental.pallas import tpu as pltpu

F32 = jnp.float32
BF16 = jnp.bfloat16
I32 = jnp.int32

D_MODEL = 1024
SSM_WIDTH = 512
SSM_GROUP = 16
SSM_GROUPS = 32
SSM_STATE = 64
ATTN_WIDTH = 512
HEAD_DIM = 64
N_HEADS = 8
IN_WIDTH = SSM_WIDTH + 3 * ATTN_WIDTH
DILATIONS = (1, 4, 16)
WINDOWS = (128, 512, 2048)
ATTN_BLOCK = 128
ATTN_STEP_BLOCKS = 2
N_EXPERTS = 32
TOP_K = 4
EXPERT_FF = D_MODEL
SWIGLU_LIMIT = 7.0
SWIGLU_ALPHA = 1.702
PLE_DIM = 256
EPS = 1e-6
MASK_VALUE = -1e30
NEG_BIG = -3.0e38

LANES = 128
SUBLANES = 8
ROW_TILES = D_MODEL // LANES
TOKEN_TILE = 256
IN_TILE = 1024
IN_ROW_CHUNK = 512
MID_ROW_BLOCK = 256
MOE_TILE = 512
MOE_SUBTILE = 512
MOE_FF_CHUNK = 256
SSM_HALF = SSM_WIDTH // 2
STATE_HALF = SSM_GROUPS * SSM_STATE // 2
ALIBI_SLOPES = tuple(2.0 ** (-8.0 * (h + 1) / N_HEADS) for h in range(N_HEADS))
VMEM_LIMIT = 56 * 1024 * 1024


def _rms(x, g):
    return x * lax.rsqrt(jnp.mean(x * x, axis=-1, keepdims=True) + EPS) * g


def _sigmoid(x):
    return 1.0 / (1.0 + jnp.exp(-x))


def _full(shape):
    n = len(shape)
    return pl.BlockSpec(shape, lambda *_: (0,) * n)


def _in_kernel(x_ref, g_ref, w_ref, u_ref, k_ref, v_ref, qb_ref, kb_ref, vb_ref, *rest, dils, window_tiles):
    tm = x_ref.shape[0]
    rc = min(tm, IN_ROW_CHUNK)
    tiles = ATTN_WIDTH // LANES
    for c in range(tm // rc):
        rows = slice(c * rc, (c + 1) * rc)
        h = _rms(x_ref[rows, :], g_ref[...]).astype(BF16)
        p = jnp.dot(h, w_ref[...], preferred_element_type=F32)
        u_ref[rows, :] = p[:, :SSM_WIDTH]
        q = p[:, SSM_WIDTH:SSM_WIDTH + ATTN_WIDTH] * (HEAD_DIM ** -0.5)
        k = p[:, SSM_WIDTH + ATTN_WIDTH:SSM_WIDTH + 2 * ATTN_WIDTH]
        v = p[:, SSM_WIDTH + 2 * ATTN_WIDTH:]
        if window_tiles is None:
            k_ref[rows, :] = k
            v_ref[rows, :] = v
        else:
            k_ref[:, rows] = k.T
            v_ref[:, rows] = v.T
        qb_ref[rows, :] = q.astype(BF16)
        kb_ref[rows, :] = k.astype(BF16)
        vb_ref[rows, :] = v.astype(BF16)
        if not dils:
            continue
        scr = rest[-1]
        for a, val in enumerate((q, k, v)):
            for ct in range(tiles):
                scr[c, a * tiles + ct] = val[:, ct * LANES:(ct + 1) * LANES]
        for di, d in enumerate(dils):
            out_rows = slice(c * rc // d, (c + 1) * rc // d)
            for a in range(3):
                out = rest[di * 3 + a]
                for r in range(d):
                    for ct in range(tiles):
                        piece = scr[c, a * tiles + ct, pl.ds(r, rc // d, stride=d), :]
                        c0 = r * ATTN_WIDTH + ct * LANES
                        out[out_rows, c0:c0 + LANES] = piece.astype(BF16)


def _in_proj(x2d, ln_mix, w_in_bf16, dils=(), seq_window=None):
    n = x2d.shape[0]
    tm = min(n, IN_TILE)
    row = lambda w: pl.BlockSpec((tm, w), lambda i: (i, 0))
    window_tiles = None
    kv_spec, kv_shape = row(ATTN_WIDTH), jax.ShapeDtypeStruct((n, ATTN_WIDTH), F32)
    if seq_window is not None:
        s_len, window = seq_window
        seq_tiles, first = s_len // tm, (s_len - window) // tm
        window_tiles = (seq_tiles, first)
        kv_spec = pl.BlockSpec(
            (None, ATTN_WIDTH, tm),
            lambda i: (i // seq_tiles, 0, jnp.maximum(i % seq_tiles - first, 0)))
        kv_shape = jax.ShapeDtypeStruct((n // s_len, ATTN_WIDTH, window), F32)
    out_specs = [row(SSM_WIDTH), kv_spec, kv_spec] + [row(ATTN_WIDTH)] * 3
    out_shape = ([jax.ShapeDtypeStruct((n, SSM_WIDTH), F32), kv_shape, kv_shape]
                 + [jax.ShapeDtypeStruct((n, ATTN_WIDTH), BF16)] * 3)
    for d in dils:
        out_specs += [pl.BlockSpec((tm // d, d * ATTN_WIDTH), lambda i: (i, 0))] * 3
        out_shape += [jax.ShapeDtypeStruct((n // d, d * ATTN_WIDTH), BF16)] * 3
    rc = min(tm, IN_ROW_CHUNK)
    scratch = [pltpu.VMEM((tm // rc, 3 * ATTN_WIDTH // LANES, rc, LANES), F32)] if dils else []
    return pl.pallas_call(
        functools.partial(_in_kernel, dils=tuple(dils), window_tiles=window_tiles),
        grid=(n // tm,),
        in_specs=[row(D_MODEL), _full((1, D_MODEL)), _full((D_MODEL, IN_WIDTH))],
        out_specs=out_specs,
        out_shape=out_shape,
        scratch_shapes=scratch,
        compiler_params=pltpu.CompilerParams(dimension_semantics=("arbitrary",),
                                             vmem_limit_bytes=VMEM_LIMIT),
        name="in_proj",
    )(x2d, ln_mix.reshape(1, D_MODEL), w_in_bf16)


def _s5_kernel(u_ref, bmat_ref, cmat_ref, are_ref, aim_ref, h0_ref, d_ref,
               y_ref, ht_ref, buf, hc, tmp, *, nb, tt, batched):
    rows = 2 * nb
    ntile = 2 * STATE_HALF // LANES
    half_tiles = ntile // 2

    def lane_tile(c):
        return slice(c * LANES, (c + 1) * LANES)

    @pl.when(pl.program_id(0) == 0)
    def _():
        hc[...] = h0_ref[...]

    if batched:
        u_all = u_ref[...].reshape(nb * tt, SSM_WIDTH)
        ub_all = u_all.astype(BF16)
        for hf in range(2):
            bu = jnp.dot(ub_all[:, hf * SSM_HALF:(hf + 1) * SSM_HALF], bmat_ref[hf],
                         preferred_element_type=F32)
            for c in range(ntile):
                tmp[c] = bu[:, lane_tile(c)]
            for c in range(ntile):
                for t in range(tt):
                    buf[c, t * rows + hf * nb:t * rows + (hf + 1) * nb, :] = tmp[c, pl.ds(t, nb, stride=tt), :]
    else:
        ub_all = u_ref[...].reshape(nb * tt, SSM_WIDTH).astype(BF16)
        for hf in range(2):
            bu = jnp.dot(ub_all[:, hf * SSM_HALF:(hf + 1) * SSM_HALF], bmat_ref[hf],
                         preferred_element_type=F32)
            for b in range(nb):
                for c in range(ntile):
                    buf[c, pl.ds(hf * nb + b, tt, stride=rows), :] = bu[b * tt:(b + 1) * tt, lane_tile(c)]

    group = 8
    for s in range(rows // SUBLANES):
        r0 = s * SUBLANES
        for c0 in range(0, half_tiles, group):
            ar = [are_ref[r0:r0 + SUBLANES, lane_tile(c0 + k)] for k in range(group)]
            ai = [aim_ref[r0:r0 + SUBLANES, lane_tile(c0 + k)] for k in range(group)]
            init = tuple(hc[r0:r0 + SUBLANES, lane_tile(c0 + k)] for k in range(group)) + tuple(
                hc[r0:r0 + SUBLANES, lane_tile(half_tiles + c0 + k)] for k in range(group))

            def step(t, carry, r0=r0, c0=c0, ar=ar, ai=ai):
                row = pl.multiple_of(t * rows + r0, SUBLANES)
                out_r, out_i = [], []
                for k in range(group):
                    hr, hi = carry[k], carry[group + k]
                    xr = buf[c0 + k, pl.ds(row, SUBLANES), :]
                    xi = buf[half_tiles + c0 + k, pl.ds(row, SUBLANES), :]
                    nr = ar[k] * hr - ai[k] * hi + xr
                    ni = ar[k] * hi + ai[k] * hr + xi
                    buf[c0 + k, pl.ds(row, SUBLANES), :] = nr
                    buf[half_tiles + c0 + k, pl.ds(row, SUBLANES), :] = ni
                    out_r.append(nr)
                    out_i.append(ni)
                return tuple(out_r) + tuple(out_i)

            fin = lax.fori_loop(0, tt, step, init, unroll=min(tt, 8))
            for k in range(group):
                hc[r0:r0 + SUBLANES, lane_tile(c0 + k)] = fin[k]
                hc[r0:r0 + SUBLANES, lane_tile(half_tiles + c0 + k)] = fin[group + k]

    if batched:
        parts = []
        for hf in range(2):
            for c in range(ntile):
                for t in range(tt):
                    tmp[c, pl.ds(t, nb, stride=tt), :] = buf[c, t * rows + hf * nb:t * rows + (hf + 1) * nb, :]
            hs = jnp.concatenate([tmp[c] for c in range(ntile)], axis=1).astype(BF16)
            parts.append(jnp.dot(hs, cmat_ref[hf], preferred_element_type=F32))
        y_all = jnp.concatenate(parts, axis=1) + d_ref[...] * u_all
        y_ref[...] = y_all.reshape(nb, tt, SSM_WIDTH)
    else:
        parts = []
        for hf in range(2):
            hs = jnp.concatenate(
                [jnp.concatenate([buf[c, pl.ds(hf * nb + b, tt, stride=rows), :] for c in range(ntile)],
                                 axis=1).astype(BF16) for b in range(nb)], axis=0)
            parts.append(jnp.dot(hs, cmat_ref[hf], preferred_element_type=F32))
        y_all = jnp.concatenate(parts, axis=1) + d_ref[...] * u_ref[...].reshape(nb * tt, SSM_WIDTH)
        y_ref[...] = y_all.reshape(nb, tt, SSM_WIDTH)

    ht_ref[...] = hc[...]


def _s5(u3, h0, bmat, cmat, a_re, a_im, d_skip):
    nb, t_len, _ = u3.shape
    tt = min(t_len, 256)
    rows = 2 * nb
    batched = tt < 16
    kern = functools.partial(_s5_kernel, nb=nb, tt=tt, batched=batched)
    ntile = 2 * STATE_HALF // LANES
    tmp_shape = (ntile, nb * tt, LANES) if batched else (1, SUBLANES, LANES)
    return pl.pallas_call(
        kern,
        grid=(t_len // tt,),
        in_specs=[pl.BlockSpec((nb, tt, SSM_WIDTH), lambda i: (0, i, 0)),
                  _full((2, SSM_HALF, 2 * STATE_HALF)),
                  _full((2, 2 * STATE_HALF, SSM_HALF)),
                  _full((rows, STATE_HALF)), _full((rows, STATE_HALF)),
                  _full((rows, 2 * STATE_HALF)), _full((1, SSM_WIDTH))],
        out_specs=[pl.BlockSpec((nb, tt, SSM_WIDTH), lambda i: (0, i, 0)),
                   _full((rows, 2 * STATE_HALF))],
        out_shape=[jax.ShapeDtypeStruct((nb, t_len, SSM_WIDTH), F32),
                   jax.ShapeDtypeStruct((rows, 2 * STATE_HALF), F32)],
        scratch_shapes=[pltpu.VMEM((ntile, tt * rows, LANES), F32),
                        pltpu.VMEM((rows, 2 * STATE_HALF), F32),
                        pltpu.VMEM(tmp_shape, F32)],
        compiler_params=pltpu.CompilerParams(dimension_semantics=("arbitrary",),
                                             vmem_limit_bytes=VMEM_LIMIT),
        name="s5_scan",
    )(u3, bmat, cmat, a_re, a_im, h0, d_skip.reshape(1, SSM_WIDTH))


def _s5_params(a_re, a_im, b_re, b_im, c_re, c_im, log_dt):
    dt = jnp.exp(log_dt)[:, None]
    mag = jnp.exp(dt * a_re)
    ang = dt * a_im
    ab_re, ab_im = mag * jnp.cos(ang), mag * jnp.sin(ang)
    den = a_re * a_re + a_im * a_im
    nr, ni = ab_re - 1.0, ab_im
    f_re = (nr * a_re + ni * a_im) / den
    f_im = (ni * a_re - nr * a_im) / den
    bb_re = f_re[..., None] * b_re - f_im[..., None] * b_im
    bb_im = f_re[..., None] * b_im + f_im[..., None] * b_re
    gh = SSM_GROUPS // 2
    eye = jnp.eye(gh, dtype=F32)

    def b_half(w):
        return jnp.einsum('gnc,gh->gchn', w, eye).reshape(gh * SSM_GROUP, gh * SSM_STATE)

    def c_half(w):
        return jnp.einsum('gcn,gh->gnhc', w, eye).reshape(gh * SSM_STATE, gh * SSM_GROUP)

    bmat = jnp.stack([jnp.concatenate([b_half(bb_re[h * gh:(h + 1) * gh]),
                                       b_half(bb_im[h * gh:(h + 1) * gh])], axis=1)
                      for h in range(2)]).astype(BF16)
    cmat = jnp.stack([jnp.concatenate([c_half(c_re[h * gh:(h + 1) * gh]),
                                       -c_half(c_im[h * gh:(h + 1) * gh])], axis=0)
                      for h in range(2)]).astype(BF16)
    return bmat, cmat, ab_re.reshape(2, 1, STATE_HALF), ab_im.reshape(2, 1, STATE_HALF)


def _state_to_rows(h_re, h_im):
    nb = h_re.shape[0]
    f = lambda h: h.reshape(nb, 2, STATE_HALF).transpose(1, 0, 2).reshape(2 * nb, STATE_HALF)
    return jnp.concatenate([f(h_re), f(h_im)], axis=1)


def _rows_to_state(ht, nb):
    f = lambda h: h.reshape(2, nb, STATE_HALF).transpose(1, 0, 2).reshape(nb, SSM_GROUPS, SSM_STATE)
    return f(ht[:, :STATE_HALF]), f(ht[:, STATE_HALF:])


def _attn_prompt_kernel(q_ref, kp_ref, kc_ref, vp_ref, vc_ref, o_ref, l_ref, bias_s, s_s, p_s, *, dil):
    blk = ATTN_BLOCK
    n = pl.program_id(2)

    @pl.when((pl.program_id(0) == 0) & (pl.program_id(1) == 0) & (n == 0))
    def _():
        i_idx = lax.broadcasted_iota(I32, (blk, 2 * blk), 0)
        j_idx = lax.broadcasted_iota(I32, (blk, 2 * blk), 1)
        delta = i_idx - j_idx + blk
        in_band = (delta >= 0) & (delta <= blk)
        dist = (delta * dil).astype(F32)
        for h in range(N_HEADS):
            biased = -ALIBI_SLOPES[h] * dist
            bias_s[0, h] = jnp.where(in_band & (j_idx >= blk), biased, MASK_VALUE)
            bias_s[1, h] = jnp.where(in_band, biased, MASK_VALUE)

    lane = lax.broadcasted_iota(I32, (blk, LANES), 1)
    nqb = q_ref.shape[0] // blk
    for qb in range(nqb):
        which = jnp.minimum(n, 1) if qb == 0 else 1
        rows = slice(qb * blk, (qb + 1) * blk)
        for hp in range(N_HEADS // 2):
            cols = slice(hp * LANES, (hp + 1) * LANES)
            q2 = q_ref[rows, cols]
            before = kp_ref[:, cols] if qb == 0 else kc_ref[(qb - 1) * blk:qb * blk, cols]
            kk = jnp.concatenate([before, kc_ref[rows, cols]], axis=0)
            for half in range(2):
                h = 2 * hp + half
                in_head = (lane >= half * HEAD_DIM) & (lane < (half + 1) * HEAD_DIM)
                qm = jnp.where(in_head, q2, jnp.zeros_like(q2))
                s = lax.dot_general(qm, kk, (((1,), (1,)), ((), ())), preferred_element_type=F32)
                s_s[qb * N_HEADS + h] = s + bias_s[which, h]
    for qb in range(nqb):
        lse_all = jnp.zeros((blk, LANES), F32)
        for h in range(N_HEADS):
            s = s_s[qb * N_HEADS + h]
            m = jnp.max(s, axis=1, keepdims=True)
            p = jnp.exp(s - m)
            l = jnp.sum(p, axis=1, keepdims=True)
            p_s[qb * N_HEADS + h] = (p * (1.0 / l)).astype(BF16)
            lse_all = jnp.where(lane == h, m + jnp.log(l), lse_all)
        l_ref[qb * blk:(qb + 1) * blk, :] = lse_all
    for qb in range(nqb):
        rows = slice(qb * blk, (qb + 1) * blk)
        for hp in range(N_HEADS // 2):
            cols = slice(hp * LANES, (hp + 1) * LANES)
            before = vp_ref[:, cols] if qb == 0 else vc_ref[(qb - 1) * blk:qb * blk, cols]
            vv = jnp.concatenate([before, vc_ref[rows, cols]], axis=0)
            outs = [jnp.dot(p_s[qb * N_HEADS + 2 * hp + half], vv, preferred_element_type=F32)
                    for half in range(2)]
            o_ref[rows, cols] = jnp.where(lane < HEAD_DIM, outs[0], outs[1])


def _attn_prompt_branch(qb, kb, vb, bsz, s_len, dil):
    sub = s_len // dil
    nqb = ATTN_STEP_BLOCKS
    step = nqb * ATTN_BLOCK
    nstep = sub // step
    view = lambda t: t.reshape(bsz, sub, dil * ATTN_WIDTH)
    cur = pl.BlockSpec((None, step, ATTN_WIDTH), lambda b, r, n: (b, n, r))
    prev = pl.BlockSpec((None, ATTN_BLOCK, ATTN_WIDTH),
                        lambda b, r, n: (b, jnp.maximum(n * nqb - 1, 0), r))
    o, lse = pl.pallas_call(
        functools.partial(_attn_prompt_kernel, dil=dil),
        grid=(bsz, dil, nstep),
        in_specs=[cur, prev, cur, prev, cur],
        out_specs=[cur, pl.BlockSpec((None, step, LANES), lambda b, r, n: (b, n, r))],
        out_shape=[jax.ShapeDtypeStruct((bsz, sub, dil * ATTN_WIDTH), F32),
                   jax.ShapeDtypeStruct((bsz, sub, dil * LANES), F32)],
        scratch_shapes=[pltpu.VMEM((2, N_HEADS, ATTN_BLOCK, 2 * ATTN_BLOCK), F32),
                        pltpu.VMEM((nqb * N_HEADS, ATTN_BLOCK, 2 * ATTN_BLOCK), F32),
                        pltpu.VMEM((nqb * N_HEADS, ATTN_BLOCK, 2 * ATTN_BLOCK), BF16)],
        compiler_params=pltpu.CompilerParams(
            dimension_semantics=("arbitrary", "arbitrary", "arbitrary"),
            vmem_limit_bytes=VMEM_LIMIT),
        name=f"attn_prompt_d{dil}",
    )(view(qb), view(kb), view(kb), view(vb), view(vb))
    return o.reshape(bsz * sub, dil * ATTN_WIDTH), lse.reshape(bsz * sub, dil * LANES)


def _attn_sample_kernel(q_ref, kn_ref, vn_ref, kc_ref, vc_ref, o_ref, bias_s, mult_s, *, t_len, wb):
    nrow = N_HEADS * t_len
    t_shift = t_len.bit_length() - 1
    d_shift = HEAD_DIM.bit_length() - 1
    nt = (((1,), (1,)), ((), ()))

    def branch_count(dist):
        mult = jnp.zeros(dist.shape, F32)
        for win, dil in zip(WINDOWS, DILATIONS):
            hit = (dist >= 0) & (dist <= win) & ((dist & (dil - 1)) == 0)
            mult = mult + jnp.where(hit, 1.0, 0.0)
        return mult

    def biased(dist, mult):
        head = lax.broadcasted_iota(I32, dist.shape, 0) >> t_shift
        slope = jnp.zeros(dist.shape, F32)
        for h in range(N_HEADS):
            slope = jnp.where(head == h, ALIBI_SLOPES[h], slope)
        return jnp.where(mult > 0.0, -slope * dist.astype(F32), MASK_VALUE)

    @pl.when(pl.program_id(0) == 0)
    def _():
        row = lax.broadcasted_iota(I32, (nrow, wb), 0)
        col = lax.broadcasted_iota(I32, (nrow, wb), 1)
        dist = wb + (row & (t_len - 1)) - col
        mult = branch_count(dist)
        mult_s[...] = mult
        bias_s[...] = biased(dist, mult)

    q = q_ref[...].astype(F32)
    qt = jnp.concatenate([q] * N_HEADS, axis=0)
    row_w = lax.broadcasted_iota(I32, (nrow, ATTN_WIDTH), 0)
    lane_w = lax.broadcasted_iota(I32, (nrow, ATTN_WIDTH), 1)
    qm = jnp.where((lane_w >> d_shift) == (row_w >> t_shift), qt, 0.0).astype(BF16)

    pad = LANES - t_len
    kn = jnp.concatenate([kn_ref[...].astype(F32), jnp.zeros((pad, ATTN_WIDTH), F32)], axis=0).astype(BF16)
    vn = jnp.concatenate([vn_ref[...].astype(F32), jnp.zeros((pad, ATTN_WIDTH), F32)], axis=0).astype(BF16)
    row_n = lax.broadcasted_iota(I32, (nrow, LANES), 0)
    col_n = lax.broadcasted_iota(I32, (nrow, LANES), 1)
    dist_n = jnp.where(col_n < t_len, (row_n & (t_len - 1)) - col_n, -1)
    mult_n = branch_count(dist_n)
    s_n = lax.dot_general(qm, kn, nt, preferred_element_type=F32) + biased(dist_n, mult_n)
    s_c = jnp.dot(qm, kc_ref[...].astype(BF16), preferred_element_type=F32) + bias_s[...]

    m = jnp.maximum(jnp.max(s_c, axis=1, keepdims=True), jnp.max(s_n, axis=1, keepdims=True))
    p_c = jnp.exp(s_c - m) * mult_s[...]
    p_n = jnp.exp(s_n - m) * mult_n
    l = jnp.sum(p_c, axis=1, keepdims=True) + jnp.sum(p_n, axis=1, keepdims=True)
    o = (lax.dot_general(p_c.astype(BF16), vc_ref[...].astype(BF16), nt, preferred_element_type=F32)
         + jnp.dot(p_n.astype(BF16), vn, preferred_element_type=F32)) / l
    lane_o = lax.broadcasted_iota(I32, (t_len, ATTN_WIDTH), 1) >> d_shift
    out = jnp.zeros((t_len, ATTN_WIDTH), F32)
    for h in range(N_HEADS):
        out = jnp.where(lane_o == h, o[h * t_len:(h + 1) * t_len], out)
    o_ref[...] = out


def _attn_sample(qb, kb, vb, cache_k, cache_v):
    bsz, t_len, _ = qb.shape
    wb = cache_k.shape[1]
    feature_major = lambda c: jnp.transpose(c, (0, 2, 3, 1)).reshape(bsz, ATTN_WIDTH, wb)
    new = pl.BlockSpec((None, t_len, ATTN_WIDTH), lambda b: (b, 0, 0))
    old = pl.BlockSpec((None, ATTN_WIDTH, wb), lambda b: (b, 0, 0))
    return pl.pallas_call(
        functools.partial(_attn_sample_kernel, t_len=t_len, wb=wb),
        grid=(bsz,),
        in_specs=[new, new, new, old, old],
        out_specs=new,
        out_shape=jax.ShapeDtypeStruct((bsz, t_len, ATTN_WIDTH), F32),
        scratch_shapes=[pltpu.VMEM((N_HEADS * t_len, wb), F32),
                        pltpu.VMEM((N_HEADS * t_len, wb), F32)],
        compiler_params=pltpu.CompilerParams(dimension_semantics=("arbitrary",),
                                             vmem_limit_bytes=VMEM_LIMIT),
        name="attn_sample",
    )(qb, kb, vb, feature_major(cache_k), feature_major(cache_v))


def _mid_kernel(*refs, dils):
    n_branch = max(len(dils), 1)
    x_ref, y_ref = refs[0], refs[1]
    o_refs = refs[2:2 + n_branch]
    pos = 2 + n_branch
    l_refs = refs[pos:pos + len(dils)]
    pos += len(l_refs)
    (wglu_ref, bglu_ref, lns_ref, lna_ref, wout_ref, lnm_ref, wrh_ref, wrl_ref, br_ref,
     expand_ref, tri_ref,
     x1_ref, hrow_ref, eidx_ref, epos_ref, egate_ref, cnt_ref, carry, nat) = refs[pos:]
    tm = x_ref.shape[0]
    o_tiles = ATTN_WIDTH // LANES

    rb = min(tm, MID_ROW_BLOCK)

    for bi, d in enumerate(dils):
        if d == 1:
            continue
        for r in range(d):
            spread = pl.ds(r, tm // d, stride=d)
            for ct in range(o_tiles):
                c0 = r * ATTN_WIDTH + ct * LANES
                nat[bi, ct, spread, :] = o_refs[bi][:, c0:c0 + LANES]
            nat[bi, o_tiles, spread, :] = l_refs[bi][:, r * LANES:(r + 1) * LANES]

    def natural(bi, d, rows):
        if d == 1:
            return o_refs[bi][rows, :], l_refs[bi][rows, :]
        return (jnp.concatenate([nat[bi, ct, rows, :] for ct in range(o_tiles)], axis=1),
                nat[bi, o_tiles, rows, :])

    @pl.when(pl.program_id(0) == 0)
    def _():
        carry[...] = jnp.zeros_like(carry)

    lane = lax.broadcasted_iota(I32, (rb, LANES), 1)
    lane_f = lane.astype(F32)
    run = carry[0:1, :]
    for blk in range(tm // rb):
        rows = slice(blk * rb, (blk + 1) * rb)
        y = y_ref[rows, :]
        z = y * (0.5 * (1.0 + jnp.tanh(math.sqrt(2.0 / math.pi) * (y + 0.044715 * (y * y * y)))))
        glu = z * _sigmoid(jnp.dot(z.astype(BF16), wglu_ref[...], preferred_element_type=F32)
                           + bglu_ref[...])
        n_ssm = _rms(glu, lns_ref[...])

        if not dils:
            attn = o_refs[0][rows, :]
        else:
            pairs = [natural(bi, d, rows) for bi, d in enumerate(dils)]
            lses = [p[1] for p in pairs]
            mx = functools.reduce(jnp.maximum, lses)
            es = [jnp.exp(l - mx) for l in lses]
            inv = 1.0 / functools.reduce(lambda a, b: a + b, es)
            attn = jnp.zeros((rb, ATTN_WIDTH), F32)
            for e, (o_nat, _) in zip(es, pairs):
                w = e * inv
                w_hi = w.astype(BF16)
                w_lo = (w - w_hi.astype(F32)).astype(BF16)
                wide = jnp.dot(jnp.concatenate([w_hi, w_lo], axis=1), expand_ref[...],
                               preferred_element_type=F32)
                attn = attn + wide * o_nat
        n_attn = _rms(attn, lna_ref[...])

        x1 = (x_ref[rows, :]
              + jnp.dot(n_ssm.astype(BF16), wout_ref[:SSM_WIDTH, :], preferred_element_type=F32)
              + jnp.dot(n_attn.astype(BF16), wout_ref[SSM_WIDTH:, :], preferred_element_type=F32))
        x1_ref[rows, :] = x1
        hm = _rms(x1, lnm_ref[...])
        for s in range(ROW_TILES):
            hrow_ref[pl.ds(blk * rb * ROW_TILES + s, rb, stride=ROW_TILES), :] = hm[:, s * LANES:(s + 1) * LANES]

        h_hi = hm.astype(BF16)
        h_lo = (hm - h_hi.astype(F32)).astype(BF16)
        both = jnp.dot(h_hi, wrl_ref[...], preferred_element_type=F32)
        logits = (both[:, :LANES] + both[:, LANES:]
                  + jnp.dot(h_lo, wrh_ref[...], preferred_element_type=F32)
                  + br_ref[...])
        work = jnp.where(lane < N_EXPERTS, logits, NEG_BIG)
        vals, idxs, hots = [], [], []
        for _ in range(TOP_K):
            m = jnp.max(work, axis=1, keepdims=True)
            idx = jnp.min(jnp.where(work == m, lane_f, float(LANES)), axis=1, keepdims=True)
            hot = lane_f == idx
            vals.append(m)
            idxs.append(idx)
            hots.append(hot)
            work = jnp.where(hot, NEG_BIG, work)
        exps = [jnp.exp(v - vals[0]) for v in vals]
        inv = 1.0 / functools.reduce(lambda a, b: a + b, exps)

        sel = functools.reduce(lambda a, b: a + b, [h.astype(F32) for h in hots])
        before = jnp.dot(tri_ref[...], sel.astype(BF16), preferred_element_type=F32) + run
        eidx = jnp.zeros((rb, LANES), I32)
        epos = jnp.zeros((rb, LANES), I32)
        egate = jnp.zeros((rb, LANES), F32)
        for k in range(TOP_K):
            pk = jnp.sum(jnp.where(hots[k], before, 0.0), axis=1, keepdims=True)
            eidx = jnp.where(lane == k, idxs[k].astype(I32), eidx)
            epos = jnp.where(lane == k, pk.astype(I32), epos)
            egate = jnp.where(lane == k, exps[k] * inv, egate)
        eidx_ref[rows, :] = eidx
        epos_ref[rows, :] = epos
        egate_ref[rows, :] = egate
        run = run + jnp.sum(sel, axis=0, keepdims=True)
    carry[...] = jnp.broadcast_to(run, carry.shape)
    cnt_ref[...] = jnp.broadcast_to(run, cnt_ref.shape).astype(I32)


def _mid(x2d, y2d, attn_o, attn_lse, dils, w):
    n = x2d.shape[0]
    tm = TOKEN_TILE
    n_branch = len(attn_o)
    row = lambda width: pl.BlockSpec((tm, width), lambda i: (i, 0))
    packed = lambda d, width: pl.BlockSpec((tm // d, d * width), lambda i: (i, 0))
    attn_specs = ([packed(d, ATTN_WIDTH) for d in dils] + [packed(d, LANES) for d in dils]
                  if dils else [row(ATTN_WIDTH)])
    in_specs = ([row(D_MODEL), row(SSM_WIDTH)] + attn_specs
                + [_full((SSM_WIDTH, SSM_WIDTH)), _full((1, SSM_WIDTH)), _full((1, SSM_WIDTH)),
                   _full((1, ATTN_WIDTH)), _full((D_MODEL, D_MODEL)), _full((1, D_MODEL)),
                   _full((D_MODEL, LANES)), _full((D_MODEL, 2 * LANES)), _full((1, LANES)),
                   _full((2 * LANES, ATTN_WIDTH)), _full((MID_ROW_BLOCK, MID_ROW_BLOCK))])
    out_specs = [row(D_MODEL), pl.BlockSpec((tm * ROW_TILES, LANES), lambda i: (i, 0)),
                 row(LANES), row(LANES), row(LANES), _full((SUBLANES, LANES))]
    out_shape = [jax.ShapeDtypeStruct((n, D_MODEL), F32),
                 jax.ShapeDtypeStruct((n * ROW_TILES, LANES), F32),
                 jax.ShapeDtypeStruct((n, LANES), I32),
                 jax.ShapeDtypeStruct((n, LANES), I32),
                 jax.ShapeDtypeStruct((n, LANES), F32),
                 jax.ShapeDtypeStruct((SUBLANES, LANES), I32)]
    return pl.pallas_call(
        functools.partial(_mid_kernel, dils=tuple(dils)),
        grid=(n // tm,),
        in_specs=in_specs,
        out_specs=out_specs,
        out_shape=out_shape,
        scratch_shapes=[pltpu.VMEM((SUBLANES, LANES), F32),
                        pltpu.VMEM((n_branch, ATTN_WIDTH // LANES + 1, tm, LANES), F32)],
        compiler_params=pltpu.CompilerParams(dimension_semantics=("arbitrary",),
                                             vmem_limit_bytes=VMEM_LIMIT),
        name="mid",
    )(x2d, y2d, *attn_o, *attn_lse, w["w_glu"], w["b_glu"], w["ln_ssm_out"], w["ln_attn_out"],
      w["w_out"], w["ln_moe"], w["wr_hi"], w["wr_lo"], w["b_router"], w["expand"], w["tri"])


def _dispatch_kernel(dest_ref, hp_ref, hs_ref, xs_hbm, sem, *, n_prompt_tiles):
    i = pl.program_id(0)
    npair = TOKEN_TILE * TOP_K

    def run(src_ref):
        def issue(t, c):
            src = src_ref.at[pl.ds(pl.multiple_of(t * ROW_TILES, ROW_TILES), ROW_TILES), :]
            for k in range(TOP_K):
                d = dest_ref[0, t * TOP_K + k]
                pltpu.make_async_copy(
                    src, xs_hbm.at[pl.ds(pl.multiple_of(d * ROW_TILES, ROW_TILES), ROW_TILES), :],
                    sem).start()
            return c

        lax.fori_loop(0, TOKEN_TILE, issue, 0, unroll=2)
        span = pl.ds(0, npair * ROW_TILES)
        pltpu.make_async_copy(xs_hbm.at[span, :], xs_hbm.at[span, :], sem).wait()

    @pl.when(i < n_prompt_tiles)
    def _():
        run(hp_ref)

    @pl.when(i >= n_prompt_tiles)
    def _():
        run(hs_ref)


def _dispatch(dest, hrow_p, hrow_s):
    n_p = hrow_p.shape[0] // ROW_TILES
    n_s = hrow_s.shape[0] // ROW_TILES
    npt = n_p // TOKEN_TILE
    ntile = (n_p + n_s) // TOKEN_TILE
    npair = TOKEN_TILE * TOP_K
    blk = (TOKEN_TILE * ROW_TILES, LANES)
    return pl.pallas_call(
        functools.partial(_dispatch_kernel, n_prompt_tiles=npt),
        grid=(ntile,),
        in_specs=[pl.BlockSpec((None, 1, npair), lambda i: (i, 0, 0), memory_space=pltpu.SMEM),
                  pl.BlockSpec(blk, lambda i: (jnp.minimum(i, npt - 1), 0)),
                  pl.BlockSpec(blk, lambda i: (jnp.maximum(i - npt, 0), 0))],
        out_specs=pl.BlockSpec(memory_space=pl.ANY),
        out_shape=jax.ShapeDtypeStruct(((n_p + n_s) * TOP_K * ROW_TILES, LANES), F32),
        scratch_shapes=[pltpu.SemaphoreType.DMA(())],
        compiler_params=pltpu.CompilerParams(dimension_semantics=("arbitrary",),
                                             vmem_limit_bytes=VMEM_LIMIT),
        name="moe_dispatch",
    )(dest.reshape(ntile, 1, npair), hrow_p, hrow_s)


def _expert_kernel(vt_ref, ve_ref, vok_ref, gs_ref, xs_ref, wu_ref, bu_ref, wd_ref, bd_ref,
                   out_ref, wu_s, wd_s, x_s):
    v = pl.program_id(0)
    e = ve_ref[v]
    j = vt_ref[v]
    vprev = jnp.maximum(v - 1, 0)
    new_e = (v == 0) | (e != ve_ref[vprev])
    new_j = (v == 0) | (j != vt_ref[vprev])
    tm = MOE_TILE

    @pl.when(new_e)
    def _():
        wu_s[...] = wu_ref[...].astype(BF16)
        wd_s[...] = wd_ref[...].astype(BF16)

    @pl.when(new_j)
    def _():
        out_ref[...] = jnp.zeros_like(out_ref)

    sub = x_s.shape[0]
    for part in range(tm // sub):
        base = part * sub * ROW_TILES
        first_row = j * tm + part * sub

        @pl.when((vok_ref[v] == 1) & (gs_ref[e] < first_row + sub) & (gs_ref[e + 1] > first_row))
        def _(part=part, base=base):
            for s in range(ROW_TILES):
                x_s[:, s * LANES:(s + 1) * LANES] = xs_ref[
                    pl.ds(base + s, sub, stride=ROW_TILES), :].astype(BF16)
            x = x_s[...]
            y = None
            for c in range(EXPERT_FF // MOE_FF_CHUNK):
                gate_cols = slice(c * MOE_FF_CHUNK, (c + 1) * MOE_FF_CHUNK)
                lin_cols = slice(EXPERT_FF + c * MOE_FF_CHUNK, EXPERT_FF + (c + 1) * MOE_FF_CHUNK)
                g = jnp.dot(x, wu_s[:, gate_cols], preferred_element_type=F32) + bu_ref[:, gate_cols]
                lin = jnp.dot(x, wu_s[:, lin_cols], preferred_element_type=F32) + bu_ref[:, lin_cols]
                g = jnp.minimum(g, SWIGLU_LIMIT)
                lin = jnp.clip(lin, -SWIGLU_LIMIT, SWIGLU_LIMIT)
                act = (lin + 1.0) * (g * _sigmoid(SWIGLU_ALPHA * g))
                down = jnp.dot(act.astype(BF16), wd_s[gate_cols, :], preferred_element_type=F32)
                y = down if y is None else y + down
            y = y + bd_ref[...]
            rows = j * tm + part * sub + lax.broadcasted_iota(I32, (sub, 1), 0)
            mine = (rows >= gs_ref[e]) & (rows < gs_ref[e + 1])
            for s in range(ROW_TILES):
                cur = out_ref[pl.ds(base + s, sub, stride=ROW_TILES), :]
                out_ref[pl.ds(base + s, sub, stride=ROW_TILES), :] = jnp.where(
                    mine, y[:, s * LANES:(s + 1) * LANES], cur)


def _experts(xs, vt, ve, vok, gstart, w_up, b_up, w_down, b_down):
    tm = MOE_TILE
    nvisit = vt.shape[0]
    rows = pl.BlockSpec((tm * ROW_TILES, LANES), lambda v, vt, ve, vok, gs: (vt[v], 0))
    per_e = lambda a, b: pl.BlockSpec((None, a, b), lambda v, vt, ve, vok, gs: (ve[v], 0, 0))
    grid_spec = pltpu.PrefetchScalarGridSpec(
        num_scalar_prefetch=4,
        grid=(nvisit,),
        in_specs=[rows, per_e(D_MODEL, 2 * EXPERT_FF), per_e(1, 2 * EXPERT_FF),
                  per_e(EXPERT_FF, D_MODEL), per_e(1, D_MODEL)],
        out_specs=rows,
        scratch_shapes=[pltpu.VMEM((D_MODEL, 2 * EXPERT_FF), BF16),
                        pltpu.VMEM((EXPERT_FF, D_MODEL), BF16),
                        pltpu.VMEM((MOE_SUBTILE, D_MODEL), BF16)],
    )
    return pl.pallas_call(
        _expert_kernel,
        grid_spec=grid_spec,
        out_shape=jax.ShapeDtypeStruct(xs.shape, F32),
        compiler_params=pltpu.CompilerParams(dimension_semantics=("arbitrary",),
                                             vmem_limit_bytes=VMEM_LIMIT),
        name="moe_experts",
    )(vt, ve, vok, gstart, xs, w_up, b_up.reshape(N_EXPERTS, 1, 2 * EXPERT_FF),
      w_down, b_down.reshape(N_EXPERTS, 1, D_MODEL))


def _routing(eidx_p, epos_p, cnt_p, eidx_s, epos_s, cnt_s, n_rows):
    cnt_p = cnt_p[0, :N_EXPERTS]
    cnt_s = cnt_s[0, :N_EXPERTS]
    cnt = cnt_p + cnt_s
    gend = jnp.cumsum(cnt)
    gstart = gend - cnt
    experts = jnp.arange(N_EXPERTS, dtype=I32)

    def lookup(table, idx):
        return jnp.sum(jnp.where(idx[..., None] == experts, table, 0), axis=-1)

    ep = eidx_p[:, :TOP_K]
    es = eidx_s[:, :TOP_K]
    dest_p = lookup(gstart, ep) + epos_p[:, :TOP_K]
    dest_s = lookup(gstart + cnt_p, es) + epos_s[:, :TOP_K]
    ntile = n_rows // MOE_TILE
    nvisit = ntile + N_EXPERTS
    first = gstart // MOE_TILE
    last = jnp.maximum(gend - 1, 0) // MOE_TILE
    nv = jnp.where(cnt > 0, last - first + 1, 0)
    vend = jnp.cumsum(nv)
    vstart = vend - nv
    total = vend[-1]
    v = jnp.arange(nvisit, dtype=I32)
    vc = jnp.minimum(v, total - 1)
    ve = jnp.sum((vend[None, :] <= vc[:, None]).astype(I32), axis=1)
    vt = (lookup(first - vstart, ve) + vc).astype(I32)
    vok = (v < total).astype(I32)
    gs = jnp.concatenate([gstart, gend[-1:]]).astype(I32)
    return dest_p.astype(I32), dest_s.astype(I32), vt, ve, vok, gs


def _out_kernel(dest_ref, next_ref, y_hbm, x1_ref, gate_ref, pe_ref, lnp_ref, wg_ref, bg_ref, wp_ref,
                lnf_ref, o_ref, buf, sem, *, ntile):
    tm = TOKEN_TILE
    npair = tm * TOP_K
    i = pl.program_id(0)
    slot = i % 2

    def gather(idx_ref, to):
        def issue(t, c):
            for k in range(TOP_K):
                d = idx_ref[0, t * TOP_K + k]
                pltpu.make_async_copy(
                    y_hbm.at[pl.ds(pl.multiple_of(d * ROW_TILES, ROW_TILES), ROW_TILES), :],
                    buf.at[to, pl.ds(pl.multiple_of((k * tm + t) * ROW_TILES, ROW_TILES), ROW_TILES), :],
                    sem.at[to]).start()
            return c

        lax.fori_loop(0, tm, issue, 0, unroll=2)

    @pl.when(i == 0)
    def _():
        gather(dest_ref, 0)

    if ntile > 1:
        @pl.when(i + 1 < ntile)
        def _():
            gather(next_ref, 1 - slot)

    pltpu.make_async_copy(y_hbm.at[pl.ds(0, npair * ROW_TILES), :], buf.at[slot], sem.at[slot]).wait()

    gates = gate_ref[...]
    parts = []
    for s in range(ROW_TILES):
        acc = jnp.zeros((tm, LANES), F32)
        for k in range(TOP_K):
            rows = buf[slot, pl.ds(k * tm * ROW_TILES + s, tm, stride=ROW_TILES), :]
            acc = acc + gates[:, k:k + 1] * rows
        parts.append(acc)
    x2 = x1_ref[...] + jnp.concatenate(parts, axis=1)
    gate = _sigmoid(jnp.dot(_rms(x2, lnp_ref[...]).astype(BF16), wg_ref[...],
                            preferred_element_type=F32) + bg_ref[...])
    x3 = x2 + gate * jnp.dot(pe_ref[...].astype(BF16), wp_ref[...], preferred_element_type=F32)
    o_ref[...] = _rms(x3, lnf_ref[...])


def _combine(dest, y_rows, x1, egate, pe, w):
    n = x1.shape[0]
    tm = TOKEN_TILE
    npair = tm * TOP_K
    row = lambda width: pl.BlockSpec((tm, width), lambda i: (i, 0))
    ntile = n // tm
    dest3 = dest.reshape(ntile, 1, npair)
    return pl.pallas_call(
        functools.partial(_out_kernel, ntile=ntile),
        grid=(ntile,),
        in_specs=[pl.BlockSpec((None, 1, npair), lambda i: (i, 0, 0), memory_space=pltpu.SMEM),
                  pl.BlockSpec((None, 1, npair), lambda i: (jnp.minimum(i + 1, ntile - 1), 0, 0),
                               memory_space=pltpu.SMEM),
                  pl.BlockSpec(memory_space=pl.ANY),
                  row(D_MODEL), row(LANES), row(PLE_DIM),
                  _full((1, D_MODEL)), _full((D_MODEL, D_MODEL)), _full((1, D_MODEL)),
                  _full((PLE_DIM, D_MODEL)), _full((1, D_MODEL))],
        out_specs=row(D_MODEL),
        out_shape=jax.ShapeDtypeStruct((n, D_MODEL), F32),
        scratch_shapes=[pltpu.VMEM((2, npair * ROW_TILES, LANES), F32), pltpu.SemaphoreType.DMA((2,))],
        compiler_params=pltpu.CompilerParams(dimension_semantics=("arbitrary",),
                                             vmem_limit_bytes=VMEM_LIMIT),
        name="combine_out",
    )(dest3, dest3, y_rows, x1, egate, pe, w["ln_ple"], w["w_ple_gate"],
      w["b_ple_gate"], w["w_ple_proj"], w["ln_final"])


def kernel(x_prompt, x_sample, cache_attn_k, cache_attn_v, state_ssm_re, state_ssm_im, p_prompt, p_sample, ln_mix, w_in, ssm_a_re, ssm_a_im, ssm_b_re, ssm_b_im, ssm_c_re, ssm_c_im, ssm_d, ssm_log_dt, w_glu, b_glu, ln_ssm_out, ln_attn_out, w_out, ln_moe, w_router, b_router, w_up, b_up, w_down, b_down, ln_ple, w_ple_gate, b_ple_gate, w_ple_proj, ln_final):
    bsz, s_len, _ = x_prompt.shape
    dbsz, dt_len, _ = x_sample.shape
    n_p, n_s = bsz * s_len, dbsz * dt_len
    wb = cache_attn_k.shape[2]
    wb_prompt = min(WINDOWS[-1], s_len)

    wr = jnp.pad(w_router[0], ((0, 0), (0, LANES - N_EXPERTS)))
    wr_hi = wr.astype(BF16)
    ti = jnp.arange(MID_ROW_BLOCK)
    w = {
        "w_glu": w_glu[0].astype(BF16), "b_glu": b_glu[0].reshape(1, -1),
        "ln_ssm_out": ln_ssm_out[0].reshape(1, -1), "ln_attn_out": ln_attn_out[0].reshape(1, -1),
        "w_out": w_out[0].astype(BF16), "ln_moe": ln_moe[0].reshape(1, -1),
        "wr_hi": wr_hi,
        "wr_lo": jnp.concatenate([wr_hi, (wr - wr_hi.astype(F32)).astype(BF16)], axis=1),
        "b_router": jnp.pad(b_router[0], (0, LANES - N_EXPERTS)).reshape(1, -1),
        "expand": (jnp.arange(2 * LANES)[:, None] % LANES
                   == jnp.arange(ATTN_WIDTH)[None, :] // HEAD_DIM).astype(BF16),
        "tri": (ti[:, None] > ti[None, :]).astype(BF16),
        "ln_ple": ln_ple[0].reshape(1, -1), "w_ple_gate": w_ple_gate[0].astype(BF16),
        "b_ple_gate": b_ple_gate[0].reshape(1, -1), "w_ple_proj": w_ple_proj[0].astype(BF16),
        "ln_final": ln_final.reshape(1, -1),
    }
    w_in_b = w_in[0].astype(BF16)
    bmat, cmat, ab_re, ab_im = _s5_params(ssm_a_re[0], ssm_a_im[0], ssm_b_re[0], ssm_b_im[0],
                                          ssm_c_re[0], ssm_c_im[0], ssm_log_dt[0])

    def coeff(a, nb):
        return jnp.broadcast_to(a, (2, nb, STATE_HALF)).reshape(2 * nb, STATE_HALF)

    proj_p = _in_proj(x_prompt.reshape(n_p, D_MODEL), ln_mix[0], w_in_b, dils=DILATIONS[1:],
                      seq_window=(s_len, wb_prompt))
    u_p, k_p, v_p = proj_p[:3]
    qkv = [proj_p[3:6]] + [proj_p[6 + 3 * i:9 + 3 * i] for i in range(len(DILATIONS) - 1)]
    zeros_state = jnp.zeros((bsz, SSM_GROUPS, SSM_STATE), F32)
    y_p, ht_p = _s5(u_p.reshape(bsz, s_len, SSM_WIDTH), _state_to_rows(zeros_state, zeros_state),
                    bmat, cmat, coeff(ab_re, bsz), coeff(ab_im, bsz), ssm_d[0])
    branches = [_attn_prompt_branch(*qkv[i], bsz, s_len, d) for i, d in enumerate(DILATIONS)]
    x1_p, hrow_p, eidx_p, epos_p, egate_p, cnt_p = _mid(
        x_prompt.reshape(n_p, D_MODEL), y_p.reshape(n_p, SSM_WIDTH),
        [b[0] for b in branches], [b[1] for b in branches], DILATIONS, w)

    u_s, k_s, v_s, qb_s, kb_s, vb_s = _in_proj(x_sample.reshape(n_s, D_MODEL), ln_mix[0], w_in_b)
    y_s, ht_s = _s5(u_s.reshape(dbsz, dt_len, SSM_WIDTH), _state_to_rows(state_ssm_re[0], state_ssm_im[0]),
                    bmat, cmat, coeff(ab_re, dbsz), coeff(ab_im, dbsz), ssm_d[0])
    as3 = lambda t: t.reshape(dbsz, dt_len, ATTN_WIDTH)
    attn_s = _attn_sample(as3(qb_s), as3(kb_s), as3(vb_s), cache_attn_k[0], cache_attn_v[0])
    x1_s, hrow_s, eidx_s, epos_s, egate_s, cnt_s = _mid(
        x_sample.reshape(n_s, D_MODEL), y_s.reshape(n_s, SSM_WIDTH),
        [attn_s.reshape(n_s, ATTN_WIDTH)], [], (), w)

    n_rows = (n_p + n_s) * TOP_K
    dest_p, dest_s, vt, ve, vok, gs = _routing(eidx_p, epos_p, cnt_p, eidx_s, epos_s, cnt_s, n_rows)
    xs = _dispatch(jnp.concatenate([dest_p.reshape(-1), dest_s.reshape(-1)]), hrow_p, hrow_s)
    y_rows = _experts(xs, vt, ve, vok, gs, w_up[0], b_up[0], w_down[0], b_down[0])

    out_p = _combine(dest_p, y_rows, x1_p, egate_p, p_prompt[0].reshape(n_p, PLE_DIM), w)
    out_s = _combine(dest_s, y_rows, x1_s, egate_s, p_sample[0].reshape(n_s, PLE_DIM), w)

    hr_p, hi_p = _rows_to_state(ht_p, bsz)
    hr_s, hi_s = _rows_to_state(ht_s, dbsz)
    kv_p = lambda t: jnp.transpose(t.reshape(bsz, N_HEADS, HEAD_DIM, wb_prompt), (0, 3, 1, 2))[None]
    kv_s = lambda t: t.reshape(dbsz, dt_len, N_HEADS, HEAD_DIM)[None]
    return (out_p.reshape(bsz, s_len, D_MODEL), out_s.reshape(dbsz, dt_len, D_MODEL),
            kv_p(k_p), kv_p(v_p), hr_p[None], hi_p[None],
            kv_s(k_s), kv_s(v_s), hr_s[None], hi_s[None])
```

```python
import functools
import math

import jax
import jax.numpy as jnp
from jax import lax
from jax.experimental import pallas as pl
from jax.experimental.pallas import tpu as pltpu

F32 = jnp.float32
BF16 = jnp.bfloat16
I32 = jnp.int32

D_MODEL = 1024
SSM_WIDTH = 512
SSM_GROUP = 16
SSM_GROUPS = 32
SSM_STATE = 64
ATTN_WIDTH = 512
HEAD_DIM = 64
N_HEADS = 8
IN_WIDTH = SSM_WIDTH + 3 * ATTN_WIDTH
DILATIONS = (1, 4, 16)
WINDOWS = (128, 512, 2048)
ATTN_BLOCK = 128
ATTN_STEP_BLOCKS = 2
N_EXPERTS = 32
TOP_K = 4
EXPERT_FF = D_MODEL
SWIGLU_LIMIT = 7.0
SWIGLU_ALPHA = 1.702
PLE_DIM = 256
EPS = 1e-6
MASK_VALUE = -1e30
NEG_BIG = -3.0e38

LANES = 128
SUBLANES = 8
ROW_TILES = D_MODEL // LANES
TOKEN_TILE = 256
IN_TILE = 1024
IN_ROW_CHUNK = 512
MID_TILE = 512
MID_ROW_BLOCK = 512
MOE_TILE = 512
MOE_SUBTILE = 512
SSM_HALF = SSM_WIDTH // 2
STATE_HALF = SSM_GROUPS * SSM_STATE // 2
ALIBI_SLOPES = tuple(2.0 ** (-8.0 * (h + 1) / N_HEADS) for h in range(N_HEADS))
VMEM_LIMIT = 56 * 1024 * 1024


def _rms(x, g):
    return x * lax.rsqrt(jnp.mean(x * x, axis=-1, keepdims=True) + EPS) * g


def _sigmoid(x):
    return 1.0 / (1.0 + jnp.exp(-x))


def _full(shape):
    n = len(shape)
    return pl.BlockSpec(shape, lambda *_: (0,) * n)


def _in_kernel(x_ref, g_ref, w_ref, u_ref, k_ref, v_ref, qb_ref, kb_ref, vb_ref, *rest, dils, window_tiles):
    tm = x_ref.shape[0]
    rc = min(tm, IN_ROW_CHUNK)
    tiles = ATTN_WIDTH // LANES
    for c in range(tm // rc):
        rows = slice(c * rc, (c + 1) * rc)
        h = _rms(x_ref[rows, :], g_ref[...]).astype(BF16)
        p = jnp.dot(h, w_ref[...], preferred_element_type=F32)
        u_ref[rows, :] = p[:, :SSM_WIDTH]
        q = p[:, SSM_WIDTH:SSM_WIDTH + ATTN_WIDTH] * (HEAD_DIM ** -0.5)
        k = p[:, SSM_WIDTH + ATTN_WIDTH:SSM_WIDTH + 2 * ATTN_WIDTH]
        v = p[:, SSM_WIDTH + 2 * ATTN_WIDTH:]
        if window_tiles is None:
            k_ref[rows, :] = k
            v_ref[rows, :] = v
        else:
            k_ref[:, rows] = k.T
            v_ref[:, rows] = v.T
        qb_ref[rows, :] = q.astype(BF16)
        kb_ref[rows, :] = k.astype(BF16)
        vb_ref[rows, :] = v.astype(BF16)
        if not dils:
            continue
        scr = rest[-1]
        for a, val in enumerate((q, k, v)):
            for ct in range(tiles):
                scr[c, a * tiles + ct] = val[:, ct * LANES:(ct + 1) * LANES]
        for di, d in enumerate(dils):
            out_rows = slice(c * rc // d, (c + 1) * rc // d)
            for a in range(3):
                out = rest[di * 3 + a]
                for r in range(d):
                    for ct in range(tiles):
                        piece = scr[c, a * tiles + ct, pl.ds(r, rc // d, stride=d), :]
                        c0 = r * ATTN_WIDTH + ct * LANES
                        out[out_rows, c0:c0 + LANES] = piece.astype(BF16)


def _in_proj(x2d, ln_mix, w_in_bf16, dils=(), seq_window=None):
    n = x2d.shape[0]
    tm = min(n, IN_TILE)
    row = lambda w: pl.BlockSpec((tm, w), lambda i: (i, 0))
    window_tiles = None
    kv_spec, kv_shape = row(ATTN_WIDTH), jax.ShapeDtypeStruct((n, ATTN_WIDTH), F32)
    if seq_window is not None:
        s_len, window = seq_window
        seq_tiles, first = s_len // tm, (s_len - window) // tm
        window_tiles = (seq_tiles, first)
        kv_spec = pl.BlockSpec(
            (None, ATTN_WIDTH, tm),
            lambda i: (i // seq_tiles, 0, jnp.maximum(i % seq_tiles - first, 0)))
        kv_shape = jax.ShapeDtypeStruct((n // s_len, ATTN_WIDTH, window), F32)
    out_specs = [row(SSM_WIDTH), kv_spec, kv_spec] + [row(ATTN_WIDTH)] * 3
    out_shape = ([jax.ShapeDtypeStruct((n, SSM_WIDTH), F32), kv_shape, kv_shape]
                 + [jax.ShapeDtypeStruct((n, ATTN_WIDTH), BF16)] * 3)
    for d in dils:
        out_specs += [pl.BlockSpec((tm // d, d * ATTN_WIDTH), lambda i: (i, 0))] * 3
        out_shape += [jax.ShapeDtypeStruct((n // d, d * ATTN_WIDTH), BF16)] * 3
    rc = min(tm, IN_ROW_CHUNK)
    scratch = [pltpu.VMEM((tm // rc, 3 * ATTN_WIDTH // LANES, rc, LANES), F32)] if dils else []
    return pl.pallas_call(
        functools.partial(_in_kernel, dils=tuple(dils), window_tiles=window_tiles),
        grid=(n // tm,),
        in_specs=[row(D_MODEL), _full((1, D_MODEL)), _full((D_MODEL, IN_WIDTH))],
        out_specs=out_specs,
        out_shape=out_shape,
        scratch_shapes=scratch,
        compiler_params=pltpu.CompilerParams(dimension_semantics=("arbitrary",),
                                             vmem_limit_bytes=VMEM_LIMIT),
        name="in_proj",
    )(x2d, ln_mix.reshape(1, D_MODEL), w_in_bf16)


def _s5_kernel(u_ref, bmat_ref, cmat_ref, are_ref, aim_ref, h0_ref, d_ref,
               y_ref, ht_ref, buf, hc, tmp, *, nb, tt, batched):
    rows = 2 * nb
    ntile = 2 * STATE_HALF // LANES
    half_tiles = ntile // 2

    def lane_tile(c):
        return slice(c * LANES, (c + 1) * LANES)

    @pl.when(pl.program_id(0) == 0)
    def _():
        hc[...] = h0_ref[...]

    if batched:
        u_all = u_ref[...].reshape(nb * tt, SSM_WIDTH)
        ub_all = u_all.astype(BF16)
        for hf in range(2):
            bu = jnp.dot(ub_all[:, hf * SSM_HALF:(hf + 1) * SSM_HALF], bmat_ref[hf],
                         preferred_element_type=F32)
            for c in range(ntile):
                tmp[c] = bu[:, lane_tile(c)]
            for c in range(ntile):
                for t in range(tt):
                    buf[c, t * rows + hf * nb:t * rows + (hf + 1) * nb, :] = tmp[c, pl.ds(t, nb, stride=tt), :]
    else:
        ub_all = u_ref[...].reshape(nb * tt, SSM_WIDTH).astype(BF16)
        for hf in range(2):
            bu = jnp.dot(ub_all[:, hf * SSM_HALF:(hf + 1) * SSM_HALF], bmat_ref[hf],
                         preferred_element_type=F32)
            for b in range(nb):
                for c in range(ntile):
                    buf[c, pl.ds(hf * nb + b, tt, stride=rows), :] = bu[b * tt:(b + 1) * tt, lane_tile(c)]

    group = 8
    for s in range(rows // SUBLANES):
        r0 = s * SUBLANES
        for c0 in range(0, half_tiles, group):
            ar = [are_ref[r0:r0 + SUBLANES, lane_tile(c0 + k)] for k in range(group)]
            ai = [aim_ref[r0:r0 + SUBLANES, lane_tile(c0 + k)] for k in range(group)]
            init = tuple(hc[r0:r0 + SUBLANES, lane_tile(c0 + k)] for k in range(group)) + tuple(
                hc[r0:r0 + SUBLANES, lane_tile(half_tiles + c0 + k)] for k in range(group))

            def step(t, carry, r0=r0, c0=c0, ar=ar, ai=ai):
                row = pl.multiple_of(t * rows + r0, SUBLANES)
                out_r, out_i = [], []
                for k in range(group):
                    hr, hi = carry[k], carry[group + k]
                    xr = buf[c0 + k, pl.ds(row, SUBLANES), :]
                    xi = buf[half_tiles + c0 + k, pl.ds(row, SUBLANES), :]
                    nr = ar[k] * hr - ai[k] * hi + xr
                    ni = ar[k] * hi + ai[k] * hr + xi
                    buf[c0 + k, pl.ds(row, SUBLANES), :] = nr
                    buf[half_tiles + c0 + k, pl.ds(row, SUBLANES), :] = ni
                    out_r.append(nr)
                    out_i.append(ni)
                return tuple(out_r) + tuple(out_i)

            fin = lax.fori_loop(0, tt, step, init, unroll=min(tt, 8))
            for k in range(group):
                hc[r0:r0 + SUBLANES, lane_tile(c0 + k)] = fin[k]
                hc[r0:r0 + SUBLANES, lane_tile(half_tiles + c0 + k)] = fin[group + k]

    if batched:
        parts = []
        for hf in range(2):
            for c in range(ntile):
                for t in range(tt):
                    tmp[c, pl.ds(t, nb, stride=tt), :] = buf[c, t * rows + hf * nb:t * rows + (hf + 1) * nb, :]
            hs = jnp.concatenate([tmp[c] for c in range(ntile)], axis=1).astype(BF16)
            parts.append(jnp.dot(hs, cmat_ref[hf], preferred_element_type=F32))
        y_all = jnp.concatenate(parts, axis=1) + d_ref[...] * u_all
        y_ref[...] = y_all.reshape(nb, tt, SSM_WIDTH)
    else:
        parts = []
        for hf in range(2):
            hs = jnp.concatenate(
                [jnp.concatenate([buf[c, pl.ds(hf * nb + b, tt, stride=rows), :] for c in range(ntile)],
                                 axis=1).astype(BF16) for b in range(nb)], axis=0)
            parts.append(jnp.dot(hs, cmat_ref[hf], preferred_element_type=F32))
        y_all = jnp.concatenate(parts, axis=1) + d_ref[...] * u_ref[...].reshape(nb * tt, SSM_WIDTH)
        y_ref[...] = y_all.reshape(nb, tt, SSM_WIDTH)

    ht_ref[...] = hc[...]


def _s5(u3, h0, bmat, cmat, a_re, a_im, d_skip):
    nb, t_len, _ = u3.shape
    tt = min(t_len, 256)
    rows = 2 * nb
    batched = tt < 16
    kern = functools.partial(_s5_kernel, nb=nb, tt=tt, batched=batched)
    ntile = 2 * STATE_HALF // LANES
    tmp_shape = (ntile, nb * tt, LANES) if batched else (1, SUBLANES, LANES)
    return pl.pallas_call(
        kern,
        grid=(t_len // tt,),
        in_specs=[pl.BlockSpec((nb, tt, SSM_WIDTH), lambda i: (0, i, 0)),
                  _full((2, SSM_HALF, 2 * STATE_HALF)),
                  _full((2, 2 * STATE_HALF, SSM_HALF)),
                  _full((rows, STATE_HALF)), _full((rows, STATE_HALF)),
                  _full((rows, 2 * STATE_HALF)), _full((1, SSM_WIDTH))],
        out_specs=[pl.BlockSpec((nb, tt, SSM_WIDTH), lambda i: (0, i, 0)),
                   _full((rows, 2 * STATE_HALF))],
        out_shape=[jax.ShapeDtypeStruct((nb, t_len, SSM_WIDTH), F32),
                   jax.ShapeDtypeStruct((rows, 2 * STATE_HALF), F32)],
        scratch_shapes=[pltpu.VMEM((ntile, tt * rows, LANES), F32),
                        pltpu.VMEM((rows, 2 * STATE_HALF), F32),
                        pltpu.VMEM(tmp_shape, F32)],
        compiler_params=pltpu.CompilerParams(dimension_semantics=("arbitrary",),
                                             vmem_limit_bytes=VMEM_LIMIT),
        name="s5_scan",
    )(u3, bmat, cmat, a_re, a_im, h0, d_skip.reshape(1, SSM_WIDTH))


def _s5_params(a_re, a_im, b_re, b_im, c_re, c_im, log_dt):
    dt = jnp.exp(log_dt)[:, None]
    mag = jnp.exp(dt * a_re)
    ang = dt * a_im
    ab_re, ab_im = mag * jnp.cos(ang), mag * jnp.sin(ang)
    den = a_re * a_re + a_im * a_im
    nr, ni = ab_re - 1.0, ab_im
    f_re = (nr * a_re + ni * a_im) / den
    f_im = (ni * a_re - nr * a_im) / den
    bb_re = f_re[..., None] * b_re - f_im[..., None] * b_im
    bb_im = f_re[..., None] * b_im + f_im[..., None] * b_re
    gh = SSM_GROUPS // 2
    eye = jnp.eye(gh, dtype=F32)

    def b_half(w):
        return jnp.einsum('gnc,gh->gchn', w, eye).reshape(gh * SSM_GROUP, gh * SSM_STATE)

    def c_half(w):
        return jnp.einsum('gcn,gh->gnhc', w, eye).reshape(gh * SSM_STATE, gh * SSM_GROUP)

    bmat = jnp.stack([jnp.concatenate([b_half(bb_re[h * gh:(h + 1) * gh]),
                                       b_half(bb_im[h * gh:(h + 1) * gh])], axis=1)
                      for h in range(2)]).astype(BF16)
    cmat = jnp.stack([jnp.concatenate([c_half(c_re[h * gh:(h + 1) * gh]),
                                       -c_half(c_im[h * gh:(h + 1) * gh])], axis=0)
                      for h in range(2)]).astype(BF16)
    return bmat, cmat, ab_re.reshape(2, 1, STATE_HALF), ab_im.reshape(2, 1, STATE_HALF)


def _state_to_rows(h_re, h_im):
    nb = h_re.shape[0]
    f = lambda h: h.reshape(nb, 2, STATE_HALF).transpose(1, 0, 2).reshape(2 * nb, STATE_HALF)
    return jnp.concatenate([f(h_re), f(h_im)], axis=1)


def _rows_to_state(ht, nb):
    f = lambda h: h.reshape(2, nb, STATE_HALF).transpose(1, 0, 2).reshape(nb, SSM_GROUPS, SSM_STATE)
    return f(ht[:, :STATE_HALF]), f(ht[:, STATE_HALF:])


def _attn_prompt_kernel(q_ref, kp_ref, kc_ref, vp_ref, vc_ref, o_ref, l_ref, bias_s, s_s, p_s, *, dil):
    blk = ATTN_BLOCK
    n = pl.program_id(2)

    @pl.when((pl.program_id(0) == 0) & (pl.program_id(1) == 0) & (n == 0))
    def _():
        i_idx = lax.broadcasted_iota(I32, (blk, 2 * blk), 0)
        j_idx = lax.broadcasted_iota(I32, (blk, 2 * blk), 1)
        delta = i_idx - j_idx + blk
        in_band = (delta >= 0) & (delta <= blk)
        dist = (delta * dil).astype(F32)
        for h in range(N_HEADS):
            biased = -ALIBI_SLOPES[h] * dist
            bias_s[0, h] = jnp.where(in_band & (j_idx >= blk), biased, MASK_VALUE)
            bias_s[1, h] = jnp.where(in_band, biased, MASK_VALUE)

    lane = lax.broadcasted_iota(I32, (blk, LANES), 1)
    nqb = q_ref.shape[0] // blk
    for qb in range(nqb):
        which = jnp.minimum(n, 1) if qb == 0 else 1
        rows = slice(qb * blk, (qb + 1) * blk)
        for hp in range(N_HEADS // 2):
            cols = slice(hp * LANES, (hp + 1) * LANES)
            q2 = q_ref[rows, cols]
            before = kp_ref[:, cols] if qb == 0 else kc_ref[(qb - 1) * blk:qb * blk, cols]
            kk = jnp.concatenate([before, kc_ref[rows, cols]], axis=0)
            for half in range(2):
                h = 2 * hp + half
                in_head = (lane >= half * HEAD_DIM) & (lane < (half + 1) * HEAD_DIM)
                qm = jnp.where(in_head, q2, jnp.zeros_like(q2))
                s = lax.dot_general(qm, kk, (((1,), (1,)), ((), ())), preferred_element_type=F32)
                s_s[qb * N_HEADS + h] = s + bias_s[which, h]
    for qb in range(nqb):
        lse_all = jnp.zeros((blk, LANES), F32)
        for h in range(N_HEADS):
            s = s_s[qb * N_HEADS + h]
            m = jnp.max(s, axis=1, keepdims=True)
            p = jnp.exp(s - m)
            l = jnp.sum(p, axis=1, keepdims=True)
            p_s[qb * N_HEADS + h] = (p * (1.0 / l)).astype(BF16)
            lse_all = jnp.where(lane == h, m + jnp.log(l), lse_all)
        l_ref[qb * blk:(qb + 1) * blk, :] = lse_all
    for qb in range(nqb):
        rows = slice(qb * blk, (qb + 1) * blk)
        for hp in range(N_HEADS // 2):
            cols = slice(hp * LANES, (hp + 1) * LANES)
            before = vp_ref[:, cols] if qb == 0 else vc_ref[(qb - 1) * blk:qb * blk, cols]
            vv = jnp.concatenate([before, vc_ref[rows, cols]], axis=0)
            outs = [jnp.dot(p_s[qb * N_HEADS + 2 * hp + half], vv, preferred_element_type=F32)
                    for half in range(2)]
            o_ref[rows, cols] = jnp.where(lane < HEAD_DIM, outs[0], outs[1])


def _attn_prompt_branch(qb, kb, vb, bsz, s_len, dil):
    sub = s_len // dil
    nqb = ATTN_STEP_BLOCKS
    step = nqb * ATTN_BLOCK
    nstep = sub // step
    view = lambda t: t.reshape(bsz, sub, dil * ATTN_WIDTH)
    cur = pl.BlockSpec((None, step, ATTN_WIDTH), lambda b, r, n: (b, n, r))
    prev = pl.BlockSpec((None, ATTN_BLOCK, ATTN_WIDTH),
                        lambda b, r, n: (b, jnp.maximum(n * nqb - 1, 0), r))
    o, lse = pl.pallas_call(
        functools.partial(_attn_prompt_kernel, dil=dil),
        grid=(bsz, dil, nstep),
        in_specs=[cur, prev, cur, prev, cur],
        out_specs=[cur, pl.BlockSpec((None, step, LANES), lambda b, r, n: (b, n, r))],
        out_shape=[jax.ShapeDtypeStruct((bsz, sub, dil * ATTN_WIDTH), F32),
                   jax.ShapeDtypeStruct((bsz, sub, dil * LANES), F32)],
        scratch_shapes=[pltpu.VMEM((2, N_HEADS, ATTN_BLOCK, 2 * ATTN_BLOCK), F32),
                        pltpu.VMEM((nqb * N_HEADS, ATTN_BLOCK, 2 * ATTN_BLOCK), F32),
                        pltpu.VMEM((nqb * N_HEADS, ATTN_BLOCK, 2 * ATTN_BLOCK), BF16)],
        compiler_params=pltpu.CompilerParams(
            dimension_semantics=("arbitrary", "arbitrary", "arbitrary"),
            vmem_limit_bytes=VMEM_LIMIT),
        name=f"attn_prompt_d{dil}",
    )(view(qb), view(kb), view(kb), view(vb), view(vb))
    return o.reshape(bsz * sub, dil * ATTN_WIDTH), lse.reshape(bsz * sub, dil * LANES)


def _attn_sample_kernel(q_ref, kn_ref, vn_ref, kc_ref, vc_ref, o_ref, bias_s, mult_s, *, t_len, wb):
    nrow = N_HEADS * t_len
    t_shift = t_len.bit_length() - 1
    d_shift = HEAD_DIM.bit_length() - 1
    nt = (((1,), (1,)), ((), ()))

    def branch_count(dist):
        mult = jnp.zeros(dist.shape, F32)
        for win, dil in zip(WINDOWS, DILATIONS):
            hit = (dist >= 0) & (dist <= win) & ((dist & (dil - 1)) == 0)
            mult = mult + jnp.where(hit, 1.0, 0.0)
        return mult

    def biased(dist, mult):
        head = lax.broadcasted_iota(I32, dist.shape, 0) >> t_shift
        slope = jnp.zeros(dist.shape, F32)
        for h in range(N_HEADS):
            slope = jnp.where(head == h, ALIBI_SLOPES[h], slope)
        return jnp.where(mult > 0.0, -slope * dist.astype(F32), MASK_VALUE)

    @pl.when(pl.program_id(0) == 0)
    def _():
        row = lax.broadcasted_iota(I32, (nrow, wb), 0)
        col = lax.broadcasted_iota(I32, (nrow, wb), 1)
        dist = wb + (row & (t_len - 1)) - col
        mult = branch_count(dist)
        mult_s[...] = mult
        bias_s[...] = biased(dist, mult)

    q = q_ref[...].astype(F32)
    qt = jnp.concatenate([q] * N_HEADS, axis=0)
    row_w = lax.broadcasted_iota(I32, (nrow, ATTN_WIDTH), 0)
    lane_w = lax.broadcasted_iota(I32, (nrow, ATTN_WIDTH), 1)
    qm = jnp.where((lane_w >> d_shift) == (row_w >> t_shift), qt, 0.0).astype(BF16)

    pad = LANES - t_len
    kn = jnp.concatenate([kn_ref[...].astype(F32), jnp.zeros((pad, ATTN_WIDTH), F32)], axis=0).astype(BF16)
    vn = jnp.concatenate([vn_ref[...].astype(F32), jnp.zeros((pad, ATTN_WIDTH), F32)], axis=0).astype(BF16)
    row_n = lax.broadcasted_iota(I32, (nrow, LANES), 0)
    col_n = lax.broadcasted_iota(I32, (nrow, LANES), 1)
    dist_n = jnp.where(col_n < t_len, (row_n & (t_len - 1)) - col_n, -1)
    mult_n = branch_count(dist_n)
    s_n = lax.dot_general(qm, kn, nt, preferred_element_type=F32) + biased(dist_n, mult_n)
    s_c = jnp.dot(qm, kc_ref[...].astype(BF16), preferred_element_type=F32) + bias_s[...]

    m = jnp.maximum(jnp.max(s_c, axis=1, keepdims=True), jnp.max(s_n, axis=1, keepdims=True))
    p_c = jnp.exp(s_c - m) * mult_s[...]
    p_n = jnp.exp(s_n - m) * mult_n
    l = jnp.sum(p_c, axis=1, keepdims=True) + jnp.sum(p_n, axis=1, keepdims=True)
    o = (lax.dot_general(p_c.astype(BF16), vc_ref[...].astype(BF16), nt, preferred_element_type=F32)
         + jnp.dot(p_n.astype(BF16), vn, preferred_element_type=F32)) / l
    lane_o = lax.broadcasted_iota(I32, (t_len, ATTN_WIDTH), 1) >> d_shift
    out = jnp.zeros((t_len, ATTN_WIDTH), F32)
    for h in range(N_HEADS):
        out = jnp.where(lane_o == h, o[h * t_len:(h + 1) * t_len], out)
    o_ref[...] = out


def _attn_sample(qb, kb, vb, cache_k, cache_v):
    bsz, t_len, _ = qb.shape
    wb = cache_k.shape[1]
    feature_major = lambda c: jnp.transpose(c, (0, 2, 3, 1)).reshape(bsz, ATTN_WIDTH, wb)
    new = pl.BlockSpec((None, t_len, ATTN_WIDTH), lambda b: (b, 0, 0))
    old = pl.BlockSpec((None, ATTN_WIDTH, wb), lambda b: (b, 0, 0))
    return pl.pallas_call(
        functools.partial(_attn_sample_kernel, t_len=t_len, wb=wb),
        grid=(bsz,),
        in_specs=[new, new, new, old, old],
        out_specs=new,
        out_shape=jax.ShapeDtypeStruct((bsz, t_len, ATTN_WIDTH), F32),
        scratch_shapes=[pltpu.VMEM((N_HEADS * t_len, wb), F32),
                        pltpu.VMEM((N_HEADS * t_len, wb), F32)],
        compiler_params=pltpu.CompilerParams(dimension_semantics=("arbitrary",),
                                             vmem_limit_bytes=VMEM_LIMIT),
        name="attn_sample",
    )(qb, kb, vb, feature_major(cache_k), feature_major(cache_v))


def _mid_kernel(*refs, dils):
    n_branch = max(len(dils), 1)
    x_ref, y_ref = refs[0], refs[1]
    o_refs = refs[2:2 + n_branch]
    pos = 2 + n_branch
    l_refs = refs[pos:pos + len(dils)]
    pos += len(l_refs)
    (wglu_ref, bglu_ref, lns_ref, lna_ref, wout_ref, lnm_ref, wrh_ref, wrl_ref, br_ref,
     expand_ref, tri_ref,
     x1_ref, hrow_ref, eidx_ref, epos_ref, egate_ref, cnt_ref, carry, nat) = refs[pos:]
    tm = x_ref.shape[0]
    o_tiles = ATTN_WIDTH // LANES

    rb = min(tm, MID_ROW_BLOCK)

    for bi, d in enumerate(dils):
        if d == 1:
            continue
        for r in range(d):
            spread = pl.ds(r, tm // d, stride=d)
            for ct in range(o_tiles):
                c0 = r * ATTN_WIDTH + ct * LANES
                nat[bi, ct, spread, :] = o_refs[bi][:, c0:c0 + LANES]
            nat[bi, o_tiles, spread, :] = l_refs[bi][:, r * LANES:(r + 1) * LANES]

    def natural(bi, d, rows):
        if d == 1:
            return o_refs[bi][rows, :], l_refs[bi][rows, :]
        return (jnp.concatenate([nat[bi, ct, rows, :] for ct in range(o_tiles)], axis=1),
                nat[bi, o_tiles, rows, :])

    @pl.when(pl.program_id(0) == 0)
    def _():
        carry[...] = jnp.zeros_like(carry)

    lane = lax.broadcasted_iota(I32, (rb, LANES), 1)
    lane_f = lane.astype(F32)
    run = carry[0:1, :]
    for blk in range(tm // rb):
        rows = slice(blk * rb, (blk + 1) * rb)
        y = y_ref[rows, :]
        z = y * (0.5 * (1.0 + jnp.tanh(math.sqrt(2.0 / math.pi) * (y + 0.044715 * (y * y * y)))))
        glu = z * _sigmoid(jnp.dot(z.astype(BF16), wglu_ref[...], preferred_element_type=F32)
                           + bglu_ref[...])
        n_ssm = _rms(glu, lns_ref[...])

        if not dils:
            attn = o_refs[0][rows, :]
        else:
            pairs = [natural(bi, d, rows) for bi, d in enumerate(dils)]
            lses = [p[1] for p in pairs]
            mx = functools.reduce(jnp.maximum, lses)
            es = [jnp.exp(l - mx) for l in lses]
            inv = 1.0 / functools.reduce(lambda a, b: a + b, es)
            attn = jnp.zeros((rb, ATTN_WIDTH), F32)
            for e, (o_nat, _) in zip(es, pairs):
                w = e * inv
                w_hi = w.astype(BF16)
                w_lo = (w - w_hi.astype(F32)).astype(BF16)
                wide = jnp.dot(jnp.concatenate([w_hi, w_lo], axis=1), expand_ref[...],
                               preferred_element_type=F32)
                attn = attn + wide * o_nat
        n_attn = _rms(attn, lna_ref[...])

        x1 = (x_ref[rows, :]
              + jnp.dot(n_ssm.astype(BF16), wout_ref[:SSM_WIDTH, :], preferred_element_type=F32)
              + jnp.dot(n_attn.astype(BF16), wout_ref[SSM_WIDTH:, :], preferred_element_type=F32))
        x1_ref[rows, :] = x1
        hm = _rms(x1, lnm_ref[...])
        for s in range(ROW_TILES):
            hrow_ref[pl.ds(blk * rb * ROW_TILES + s, rb, stride=ROW_TILES), :] = hm[:, s * LANES:(s + 1) * LANES]

        h_hi = hm.astype(BF16)
        h_lo = (hm - h_hi.astype(F32)).astype(BF16)
        both = jnp.dot(h_hi, wrl_ref[...], preferred_element_type=F32)
        logits = (both[:, :LANES] + both[:, LANES:]
                  + jnp.dot(h_lo, wrh_ref[...], preferred_element_type=F32)
                  + br_ref[...])
        work = jnp.where(lane < N_EXPERTS, logits, NEG_BIG)
        vals, idxs, hots = [], [], []
        for _ in range(TOP_K):
            m = jnp.max(work, axis=1, keepdims=True)
            idx = jnp.min(jnp.where(work == m, lane_f, float(LANES)), axis=1, keepdims=True)
            hot = lane_f == idx
            vals.append(m)
            idxs.append(idx)
            hots.append(hot)
            work = jnp.where(hot, NEG_BIG, work)
        exps = [jnp.exp(v - vals[0]) for v in vals]
        inv = 1.0 / functools.reduce(lambda a, b: a + b, exps)

        sel = functools.reduce(lambda a, b: a + b, [h.astype(F32) for h in hots])
        before = jnp.dot(tri_ref[...], sel.astype(BF16), preferred_element_type=F32) + run
        eidx = jnp.zeros((rb, LANES), I32)
        epos = jnp.zeros((rb, LANES), I32)
        egate = jnp.zeros((rb, LANES), F32)
        for k in range(TOP_K):
            pk = jnp.sum(jnp.where(hots[k], before, 0.0), axis=1, keepdims=True)
            eidx = jnp.where(lane == k, idxs[k].astype(I32), eidx)
            epos = jnp.where(lane == k, pk.astype(I32), epos)
            egate = jnp.where(lane == k, exps[k] * inv, egate)
        eidx_ref[rows, :] = eidx
        epos_ref[rows, :] = epos
        egate_ref[rows, :] = egate
        run = run + jnp.sum(sel, axis=0, keepdims=True)
    carry[...] = jnp.broadcast_to(run, carry.shape)
    cnt_ref[...] = jnp.broadcast_to(run, cnt_ref.shape).astype(I32)


def _mid(x2d, y2d, attn_o, attn_lse, dils, w):
    n = x2d.shape[0]
    tm = min(n, MID_TILE)
    rb = min(tm, MID_ROW_BLOCK)
    ti = jnp.arange(rb)
    tri = (ti[:, None] > ti[None, :]).astype(BF16)
    n_branch = len(attn_o)
    row = lambda width: pl.BlockSpec((tm, width), lambda i: (i, 0))
    packed = lambda d, width: pl.BlockSpec((tm // d, d * width), lambda i: (i, 0))
    attn_specs = ([packed(d, ATTN_WIDTH) for d in dils] + [packed(d, LANES) for d in dils]
                  if dils else [row(ATTN_WIDTH)])
    in_specs = ([row(D_MODEL), row(SSM_WIDTH)] + attn_specs
                + [_full((SSM_WIDTH, SSM_WIDTH)), _full((1, SSM_WIDTH)), _full((1, SSM_WIDTH)),
                   _full((1, ATTN_WIDTH)), _full((D_MODEL, D_MODEL)), _full((1, D_MODEL)),
                   _full((D_MODEL, LANES)), _full((D_MODEL, 2 * LANES)), _full((1, LANES)),
                   _full((2 * LANES, ATTN_WIDTH)), _full((rb, rb))])
    out_specs = [row(D_MODEL), pl.BlockSpec((tm * ROW_TILES, LANES), lambda i: (i, 0)),
                 row(LANES), row(LANES), row(LANES), _full((SUBLANES, LANES))]
    out_shape = [jax.ShapeDtypeStruct((n, D_MODEL), F32),
                 jax.ShapeDtypeStruct((n * ROW_TILES, LANES), F32),
                 jax.ShapeDtypeStruct((n, LANES), I32),
                 jax.ShapeDtypeStruct((n, LANES), I32),
                 jax.ShapeDtypeStruct((n, LANES), F32),
                 jax.ShapeDtypeStruct((SUBLANES, LANES), I32)]
    return pl.pallas_call(
        functools.partial(_mid_kernel, dils=tuple(dils)),
        grid=(n // tm,),
        in_specs=in_specs,
        out_specs=out_specs,
        out_shape=out_shape,
        scratch_shapes=[pltpu.VMEM((SUBLANES, LANES), F32),
                        pltpu.VMEM((n_branch, ATTN_WIDTH // LANES + 1, tm, LANES), F32)],
        compiler_params=pltpu.CompilerParams(dimension_semantics=("arbitrary",),
                                             vmem_limit_bytes=VMEM_LIMIT),
        name="mid",
    )(x2d, y2d, *attn_o, *attn_lse, w["w_glu"], w["b_glu"], w["ln_ssm_out"], w["ln_attn_out"],
      w["w_out"], w["ln_moe"], w["wr_hi"], w["wr_lo"], w["b_router"], w["expand"], tri)


def _dispatch_kernel(dest_ref, hp_ref, hs_ref, xs_hbm, sem, *, n_prompt_tiles):
    i = pl.program_id(0)
    npair = TOKEN_TILE * TOP_K

    def run(src_ref):
        def issue(t, c):
            src = src_ref.at[pl.ds(pl.multiple_of(t * ROW_TILES, ROW_TILES), ROW_TILES), :]
            for k in range(TOP_K):
                d = dest_ref[0, t * TOP_K + k]
                pltpu.make_async_copy(
                    src, xs_hbm.at[pl.ds(pl.multiple_of(d * ROW_TILES, ROW_TILES), ROW_TILES), :],
                    sem).start()
            return c

        lax.fori_loop(0, TOKEN_TILE, issue, 0, unroll=2)
        span = pl.ds(0, npair * ROW_TILES)
        pltpu.make_async_copy(xs_hbm.at[span, :], xs_hbm.at[span, :], sem).wait()

    @pl.when(i < n_prompt_tiles)
    def _():
        run(hp_ref)

    @pl.when(i >= n_prompt_tiles)
    def _():
        run(hs_ref)


def _dispatch(dest, hrow_p, hrow_s):
    n_p = hrow_p.shape[0] // ROW_TILES
    n_s = hrow_s.shape[0] // ROW_TILES
    npt = n_p // TOKEN_TILE
    ntile = (n_p + n_s) // TOKEN_TILE
    npair = TOKEN_TILE * TOP_K
    blk = (TOKEN_TILE * ROW_TILES, LANES)
    return pl.pallas_call(
        functools.partial(_dispatch_kernel, n_prompt_tiles=npt),
        grid=(ntile,),
        in_specs=[pl.BlockSpec((None, 1, npair), lambda i: (i, 0, 0), memory_space=pltpu.SMEM),
                  pl.BlockSpec(blk, lambda i: (jnp.minimum(i, npt - 1), 0)),
                  pl.BlockSpec(blk, lambda i: (jnp.maximum(i - npt, 0), 0))],
        out_specs=pl.BlockSpec(memory_space=pl.ANY),
        out_shape=jax.ShapeDtypeStruct(((n_p + n_s) * TOP_K * ROW_TILES, LANES), F32),
        scratch_shapes=[pltpu.SemaphoreType.DMA(())],
        compiler_params=pltpu.CompilerParams(dimension_semantics=("arbitrary",),
                                             vmem_limit_bytes=VMEM_LIMIT),
        name="moe_dispatch",
    )(dest.reshape(ntile, 1, npair), hrow_p, hrow_s)


def _expert_kernel(vt_ref, ve_ref, vok_ref, gs_ref, xs_ref, wu_ref, bu_ref, wd_ref, bd_ref,
                   out_ref, wu_s, wd_s, x_s):
    v = pl.program_id(0)
    e = ve_ref[v]
    j = vt_ref[v]
    vprev = jnp.maximum(v - 1, 0)
    new_e = (v == 0) | (e != ve_ref[vprev])
    new_j = (v == 0) | (j != vt_ref[vprev])
    tm = MOE_TILE

    @pl.when(new_e)
    def _():
        wu_s[...] = wu_ref[...].astype(BF16)
        wd_s[...] = wd_ref[...].astype(BF16)

    @pl.when(new_j)
    def _():
        out_ref[...] = jnp.zeros_like(out_ref)

    sub = x_s.shape[0]
    for part in range(tm // sub):
        base = part * sub * ROW_TILES
        first_row = j * tm + part * sub

        @pl.when((vok_ref[v] == 1) & (gs_ref[e] < first_row + sub) & (gs_ref[e + 1] > first_row))
        def _(part=part, base=base):
            for s in range(ROW_TILES):
                x_s[:, s * LANES:(s + 1) * LANES] = xs_ref[
                    pl.ds(base + s, sub, stride=ROW_TILES), :].astype(BF16)
            a = jnp.dot(x_s[...], wu_s[...], preferred_element_type=F32) + bu_ref[...]
            g = jnp.minimum(a[:, :EXPERT_FF], SWIGLU_LIMIT)
            lin = jnp.clip(a[:, EXPERT_FF:], -SWIGLU_LIMIT, SWIGLU_LIMIT)
            act = (lin + 1.0) * (g * _sigmoid(SWIGLU_ALPHA * g))
            y = jnp.dot(act.astype(BF16), wd_s[...], preferred_element_type=F32) + bd_ref[...]
            rows = j * tm + part * sub + lax.broadcasted_iota(I32, (sub, 1), 0)
            mine = (rows >= gs_ref[e]) & (rows < gs_ref[e + 1])
            for s in range(ROW_TILES):
                cur = out_ref[pl.ds(base + s, sub, stride=ROW_TILES), :]
                out_ref[pl.ds(base + s, sub, stride=ROW_TILES), :] = jnp.where(
                    mine, y[:, s * LANES:(s + 1) * LANES], cur)


def _experts(xs, vt, ve, vok, gstart, w_up, b_up, w_down, b_down):
    tm = MOE_TILE
    nvisit = vt.shape[0]
    rows = pl.BlockSpec((tm * ROW_TILES, LANES), lambda v, vt, ve, vok, gs: (vt[v], 0))
    per_e = lambda a, b: pl.BlockSpec((None, a, b), lambda v, vt, ve, vok, gs: (ve[v], 0, 0))
    grid_spec = pltpu.PrefetchScalarGridSpec(
        num_scalar_prefetch=4,
        grid=(nvisit,),
        in_specs=[rows, per_e(D_MODEL, 2 * EXPERT_FF), per_e(1, 2 * EXPERT_FF),
                  per_e(EXPERT_FF, D_MODEL), per_e(1, D_MODEL)],
        out_specs=rows,
        scratch_shapes=[pltpu.VMEM((D_MODEL, 2 * EXPERT_FF), BF16),
                        pltpu.VMEM((EXPERT_FF, D_MODEL), BF16),
                        pltpu.VMEM((MOE_SUBTILE, D_MODEL), BF16)],
    )
    return pl.pallas_call(
        _expert_kernel,
        grid_spec=grid_spec,
        out_shape=jax.ShapeDtypeStruct(xs.shape, F32),
        compiler_params=pltpu.CompilerParams(dimension_semantics=("arbitrary",),
                                             vmem_limit_bytes=VMEM_LIMIT),
        name="moe_experts",
    )(vt, ve, vok, gstart, xs, w_up, b_up.reshape(N_EXPERTS, 1, 2 * EXPERT_FF),
      w_down, b_down.reshape(N_EXPERTS, 1, D_MODEL))


def _routing(eidx_p, epos_p, cnt_p, eidx_s, epos_s, cnt_s, n_rows):
    cnt_p = cnt_p[0, :N_EXPERTS]
    cnt_s = cnt_s[0, :N_EXPERTS]
    cnt = cnt_p + cnt_s
    gend = jnp.cumsum(cnt)
    gstart = gend - cnt
    experts = jnp.arange(N_EXPERTS, dtype=I32)

    def lookup(table, idx):
        return jnp.sum(jnp.where(idx[..., None] == experts, table, 0), axis=-1)

    ep = eidx_p[:, :TOP_K]
    es = eidx_s[:, :TOP_K]
    dest_p = lookup(gstart, ep) + epos_p[:, :TOP_K]
    dest_s = lookup(gstart + cnt_p, es) + epos_s[:, :TOP_K]
    ntile = n_rows // MOE_TILE
    nvisit = ntile + N_EXPERTS
    first = gstart // MOE_TILE
    last = jnp.maximum(gend - 1, 0) // MOE_TILE
    nv = jnp.where(cnt > 0, last - first + 1, 0)
    vend = jnp.cumsum(nv)
    vstart = vend - nv
    total = vend[-1]
    v = jnp.arange(nvisit, dtype=I32)
    vc = jnp.minimum(v, total - 1)
    ve = jnp.sum((vend[None, :] <= vc[:, None]).astype(I32), axis=1)
    vt = (lookup(first - vstart, ve) + vc).astype(I32)
    vok = (v < total).astype(I32)
    gs = jnp.concatenate([gstart, gend[-1:]]).astype(I32)
    return dest_p.astype(I32), dest_s.astype(I32), vt, ve, vok, gs


def _out_kernel(dest_ref, next_ref, y_hbm, x1_ref, gate_ref, pe_ref, lnp_ref, wg_ref, bg_ref, wp_ref,
                lnf_ref, o_ref, buf, sem, *, ntile):
    tm = TOKEN_TILE
    npair = tm * TOP_K
    i = pl.program_id(0)
    slot = i % 2

    def gather(idx_ref, to):
        def issue(t, c):
            for k in range(TOP_K):
                d = idx_ref[0, t * TOP_K + k]
                pltpu.make_async_copy(
                    y_hbm.at[pl.ds(pl.multiple_of(d * ROW_TILES, ROW_TILES), ROW_TILES), :],
                    buf.at[to, pl.ds(pl.multiple_of((k * tm + t) * ROW_TILES, ROW_TILES), ROW_TILES), :],
                    sem.at[to]).start()
            return c

        lax.fori_loop(0, tm, issue, 0, unroll=2)

    @pl.when(i == 0)
    def _():
        gather(dest_ref, 0)

    if ntile > 1:
        @pl.when(i + 1 < ntile)
        def _():
            gather(next_ref, 1 - slot)

    pltpu.make_async_copy(y_hbm.at[pl.ds(0, npair * ROW_TILES), :], buf.at[slot], sem.at[slot]).wait()

    gates = gate_ref[...]
    parts = []
    for s in range(ROW_TILES):
        acc = jnp.zeros((tm, LANES), F32)
        for k in range(TOP_K):
            rows = buf[slot, pl.ds(k * tm * ROW_TILES + s, tm, stride=ROW_TILES), :]
            acc = acc + gates[:, k:k + 1] * rows
        parts.append(acc)
    x2 = x1_ref[...] + jnp.concatenate(parts, axis=1)
    gate = _sigmoid(jnp.dot(_rms(x2, lnp_ref[...]).astype(BF16), wg_ref[...],
                            preferred_element_type=F32) + bg_ref[...])
    x3 = x2 + gate * jnp.dot(pe_ref[...].astype(BF16), wp_ref[...], preferred_element_type=F32)
    o_ref[...] = _rms(x3, lnf_ref[...])


def _combine(dest, y_rows, x1, egate, pe, w):
    n = x1.shape[0]
    tm = TOKEN_TILE
    npair = tm * TOP_K
    row = lambda width: pl.BlockSpec((tm, width), lambda i: (i, 0))
    ntile = n // tm
    dest3 = dest.reshape(ntile, 1, npair)
    return pl.pallas_call(
        functools.partial(_out_kernel, ntile=ntile),
        grid=(ntile,),
        in_specs=[pl.BlockSpec((None, 1, npair), lambda i: (i, 0, 0), memory_space=pltpu.SMEM),
                  pl.BlockSpec((None, 1, npair), lambda i: (jnp.minimum(i + 1, ntile - 1), 0, 0),
                               memory_space=pltpu.SMEM),
                  pl.BlockSpec(memory_space=pl.ANY),
                  row(D_MODEL), row(LANES), row(PLE_DIM),
                  _full((1, D_MODEL)), _full((D_MODEL, D_MODEL)), _full((1, D_MODEL)),
                  _full((PLE_DIM, D_MODEL)), _full((1, D_MODEL))],
        out_specs=row(D_MODEL),
        out_shape=jax.ShapeDtypeStruct((n, D_MODEL), F32),
        scratch_shapes=[pltpu.VMEM((2, npair * ROW_TILES, LANES), F32), pltpu.SemaphoreType.DMA((2,))],
        compiler_params=pltpu.CompilerParams(dimension_semantics=("arbitrary",),
                                             vmem_limit_bytes=VMEM_LIMIT),
        name="combine_out",
    )(dest3, dest3, y_rows, x1, egate, pe, w["ln_ple"], w["w_ple_gate"],
      w["b_ple_gate"], w["w_ple_proj"], w["ln_final"])


def kernel(x_prompt, x_sample, cache_attn_k, cache_attn_v, state_ssm_re, state_ssm_im, p_prompt, p_sample, ln_mix, w_in, ssm_a_re, ssm_a_im, ssm_b_re, ssm_b_im, ssm_c_re, ssm_c_im, ssm_d, ssm_log_dt, w_glu, b_glu, ln_ssm_out, ln_attn_out, w_out, ln_moe, w_router, b_router, w_up, b_up, w_down, b_down, ln_ple, w_ple_gate, b_ple_gate, w_ple_proj, ln_final):
    bsz, s_len, _ = x_prompt.shape
    dbsz, dt_len, _ = x_sample.shape
    n_p, n_s = bsz * s_len, dbsz * dt_len
    wb = cache_attn_k.shape[2]
    wb_prompt = min(WINDOWS[-1], s_len)

    wr = jnp.pad(w_router[0], ((0, 0), (0, LANES - N_EXPERTS)))
    wr_hi = wr.astype(BF16)
    w = {
        "w_glu": w_glu[0].astype(BF16), "b_glu": b_glu[0].reshape(1, -1),
        "ln_ssm_out": ln_ssm_out[0].reshape(1, -1), "ln_attn_out": ln_attn_out[0].reshape(1, -1),
        "w_out": w_out[0].astype(BF16), "ln_moe": ln_moe[0].reshape(1, -1),
        "wr_hi": wr_hi,
        "wr_lo": jnp.concatenate([wr_hi, (wr - wr_hi.astype(F32)).astype(BF16)], axis=1),
        "b_router": jnp.pad(b_router[0], (0, LANES - N_EXPERTS)).reshape(1, -1),
        "expand": (jnp.arange(2 * LANES)[:, None] % LANES
                   == jnp.arange(ATTN_WIDTH)[None, :] // HEAD_DIM).astype(BF16),
        "ln_ple": ln_ple[0].reshape(1, -1), "w_ple_gate": w_ple_gate[0].astype(BF16),
        "b_ple_gate": b_ple_gate[0].reshape(1, -1), "w_ple_proj": w_ple_proj[0].astype(BF16),
        "ln_final": ln_final.reshape(1, -1),
    }
    w_in_b = w_in[0].astype(BF16)
    bmat, cmat, ab_re, ab_im = _s5_params(ssm_a_re[0], ssm_a_im[0], ssm_b_re[0], ssm_b_im[0],
                                          ssm_c_re[0], ssm_c_im[0], ssm_log_dt[0])

    def coeff(a, nb):
        return jnp.broadcast_to(a, (2, nb, STATE_HALF)).reshape(2 * nb, STATE_HALF)

    proj_p = _in_proj(x_prompt.reshape(n_p, D_MODEL), ln_mix[0], w_in_b, dils=DILATIONS[1:],
                      seq_window=(s_len, wb_prompt))
    u_p, k_p, v_p = proj_p[:3]
    qkv = [proj_p[3:6]] + [proj_p[6 + 3 * i:9 + 3 * i] for i in range(len(DILATIONS) - 1)]
    zeros_state = jnp.zeros((bsz, SSM_GROUPS, SSM_STATE), F32)
    y_p, ht_p = _s5(u_p.reshape(bsz, s_len, SSM_WIDTH), _state_to_rows(zeros_state, zeros_state),
                    bmat, cmat, coeff(ab_re, bsz), coeff(ab_im, bsz), ssm_d[0])
    branches = [_attn_prompt_branch(*qkv[i], bsz, s_len, d) for i, d in enumerate(DILATIONS)]
    x1_p, hrow_p, eidx_p, epos_p, egate_p, cnt_p = _mid(
        x_prompt.reshape(n_p, D_MODEL), y_p.reshape(n_p, SSM_WIDTH),
        [b[0] for b in branches], [b[1] for b in branches], DILATIONS, w)

    u_s, k_s, v_s, qb_s, kb_s, vb_s = _in_proj(x_sample.reshape(n_s, D_MODEL), ln_mix[0], w_in_b)
    y_s, ht_s = _s5(u_s.reshape(dbsz, dt_len, SSM_WIDTH), _state_to_rows(state_ssm_re[0], state_ssm_im[0]),
                    bmat, cmat, coeff(ab_re, dbsz), coeff(ab_im, dbsz), ssm_d[0])
    as3 = lambda t: t.reshape(dbsz, dt_len, ATTN_WIDTH)
    attn_s = _attn_sample(as3(qb_s), as3(kb_s), as3(vb_s), cache_attn_k[0], cache_attn_v[0])
    x1_s, hrow_s, eidx_s, epos_s, egate_s, cnt_s = _mid(
        x_sample.reshape(n_s, D_MODEL), y_s.reshape(n_s, SSM_WIDTH),
        [attn_s.reshape(n_s, ATTN_WIDTH)], [], (), w)

    n_rows = (n_p + n_s) * TOP_K
    dest_p, dest_s, vt, ve, vok, gs = _routing(eidx_p, epos_p, cnt_p, eidx_s, epos_s, cnt_s, n_rows)
    xs = _dispatch(jnp.concatenate([dest_p.reshape(-1), dest_s.reshape(-1)]), hrow_p, hrow_s)
    y_rows = _experts(xs, vt, ve, vok, gs, w_up[0], b_up[0], w_down[0], b_down[0])

    out_p = _combine(dest_p, y_rows, x1_p, egate_p, p_prompt[0].reshape(n_p, PLE_DIM), w)
    out_s = _combine(dest_s, y_rows, x1_s, egate_s, p_sample[0].reshape(n_s, PLE_DIM), w)

    hr_p, hi_p = _rows_to_state(ht_p, bsz)
    hr_s, hi_s = _rows_to_state(ht_s, dbsz)
    kv_p = lambda t: jnp.transpose(t.reshape(bsz, N_HEADS, HEAD_DIM, wb_prompt), (0, 3, 1, 2))[None]
    kv_s = lambda t: t.reshape(dbsz, dt_len, N_HEADS, HEAD_DIM)[None]
    return (out_p.reshape(bsz, s_len, D_MODEL), out_s.reshape(dbsz, dt_len, D_MODEL),
            kv_p(k_p), kv_p(v_p), hr_p[None], hi_p[None],
            kv_s(k_s), kv_s(v_s), hr_s[None], hi_s[None])
```

```python
import functools
import math

import jax
import jax.numpy as jnp
from jax import lax
from jax.experimental import pallas as pl
from jax.experimental.pallas import tpu as pltpu

F32 = jnp.float32
BF16 = jnp.bfloat16
I32 = jnp.int32

D_MODEL = 1024
SSM_WIDTH = 512
SSM_GROUP = 16
SSM_GROUPS = 32
SSM_STATE = 64
ATTN_WIDTH = 512
HEAD_DIM = 64
N_HEADS = 8
IN_WIDTH = SSM_WIDTH + 3 * ATTN_WIDTH
DILATIONS = (1, 4, 16)
WINDOWS = (128, 512, 2048)
ATTN_BLOCK = 128
ATTN_STEP_BLOCKS = 2
N_EXPERTS = 32
TOP_K = 4
EXPERT_FF = D_MODEL
SWIGLU_LIMIT = 7.0
SWIGLU_ALPHA = 1.702
PLE_DIM = 256
EPS = 1e-6
MASK_VALUE = -1e30
NEG_BIG = -3.0e38

LANES = 128
SUBLANES = 8
ROW_TILES = D_MODEL // LANES
TOKEN_TILE = 256
IN_TILE = 1024
IN_ROW_CHUNK = 512
MID_TILE = 512
MID_ROW_BLOCK = 512
MOE_TILE = 1024
MOE_SUBTILE = 512
SSM_HALF = SSM_WIDTH // 2
STATE_HALF = SSM_GROUPS * SSM_STATE // 2
ALIBI_SLOPES = tuple(2.0 ** (-8.0 * (h + 1) / N_HEADS) for h in range(N_HEADS))
VMEM_LIMIT = 56 * 1024 * 1024


def _rms(x, g):
    return x * lax.rsqrt(jnp.mean(x * x, axis=-1, keepdims=True) + EPS) * g


def _sigmoid(x):
    return 1.0 / (1.0 + jnp.exp(-x))


def _full(shape):
    n = len(shape)
    return pl.BlockSpec(shape, lambda *_: (0,) * n)


def _in_kernel(x_ref, g_ref, w_ref, u_ref, k_ref, v_ref, qb_ref, kb_ref, vb_ref, *rest, dils, window_tiles):
    tm = x_ref.shape[0]
    rc = min(tm, IN_ROW_CHUNK)
    tiles = ATTN_WIDTH // LANES
    for c in range(tm // rc):
        rows = slice(c * rc, (c + 1) * rc)
        h = _rms(x_ref[rows, :], g_ref[...]).astype(BF16)
        p = jnp.dot(h, w_ref[...], preferred_element_type=F32)
        u_ref[rows, :] = p[:, :SSM_WIDTH]
        q = p[:, SSM_WIDTH:SSM_WIDTH + ATTN_WIDTH] * (HEAD_DIM ** -0.5)
        k = p[:, SSM_WIDTH + ATTN_WIDTH:SSM_WIDTH + 2 * ATTN_WIDTH]
        v = p[:, SSM_WIDTH + 2 * ATTN_WIDTH:]
        if window_tiles is None:
            k_ref[rows, :] = k
            v_ref[rows, :] = v
        else:
            k_ref[:, rows] = k.T
            v_ref[:, rows] = v.T
        qb_ref[rows, :] = q.astype(BF16)
        kb_ref[rows, :] = k.astype(BF16)
        vb_ref[rows, :] = v.astype(BF16)
        if not dils:
            continue
        scr = rest[-1]
        for a, val in enumerate((q, k, v)):
            for ct in range(tiles):
                scr[c, a * tiles + ct] = val[:, ct * LANES:(ct + 1) * LANES]
        for di, d in enumerate(dils):
            out_rows = slice(c * rc // d, (c + 1) * rc // d)
            for a in range(3):
                out = rest[di * 3 + a]
                for r in range(d):
                    for ct in range(tiles):
                        piece = scr[c, a * tiles + ct, pl.ds(r, rc // d, stride=d), :]
                        c0 = r * ATTN_WIDTH + ct * LANES
                        out[out_rows, c0:c0 + LANES] = piece.astype(BF16)


def _in_proj(x2d, ln_mix, w_in_bf16, dils=(), seq_window=None):
    n = x2d.shape[0]
    tm = min(n, IN_TILE)
    row = lambda w: pl.BlockSpec((tm, w), lambda i: (i, 0))
    window_tiles = None
    kv_spec, kv_shape = row(ATTN_WIDTH), jax.ShapeDtypeStruct((n, ATTN_WIDTH), F32)
    if seq_window is not None:
        s_len, window = seq_window
        seq_tiles, first = s_len // tm, (s_len - window) // tm
        window_tiles = (seq_tiles, first)
        kv_spec = pl.BlockSpec(
            (None, ATTN_WIDTH, tm),
            lambda i: (i // seq_tiles, 0, jnp.maximum(i % seq_tiles - first, 0)))
        kv_shape = jax.ShapeDtypeStruct((n // s_len, ATTN_WIDTH, window), F32)
    out_specs = [row(SSM_WIDTH), kv_spec, kv_spec] + [row(ATTN_WIDTH)] * 3
    out_shape = ([jax.ShapeDtypeStruct((n, SSM_WIDTH), F32), kv_shape, kv_shape]
                 + [jax.ShapeDtypeStruct((n, ATTN_WIDTH), BF16)] * 3)
    for d in dils:
        out_specs += [pl.BlockSpec((tm // d, d * ATTN_WIDTH), lambda i: (i, 0))] * 3
        out_shape += [jax.ShapeDtypeStruct((n // d, d * ATTN_WIDTH), BF16)] * 3
    rc = min(tm, IN_ROW_CHUNK)
    scratch = [pltpu.VMEM((tm // rc, 3 * ATTN_WIDTH // LANES, rc, LANES), F32)] if dils else []
    return pl.pallas_call(
        functools.partial(_in_kernel, dils=tuple(dils), window_tiles=window_tiles),
        grid=(n // tm,),
        in_specs=[row(D_MODEL), _full((1, D_MODEL)), _full((D_MODEL, IN_WIDTH))],
        out_specs=out_specs,
        out_shape=out_shape,
        scratch_shapes=scratch,
        compiler_params=pltpu.CompilerParams(dimension_semantics=("arbitrary",),
                                             vmem_limit_bytes=VMEM_LIMIT),
        name="in_proj",
    )(x2d, ln_mix.reshape(1, D_MODEL), w_in_bf16)


def _s5_kernel(u_ref, bmat_ref, cmat_ref, are_ref, aim_ref, h0_ref, d_ref,
               y_ref, ht_ref, buf, hc, tmp, *, nb, tt, batched):
    rows = 2 * nb
    ntile = 2 * STATE_HALF // LANES
    half_tiles = ntile // 2

    def lane_tile(c):
        return slice(c * LANES, (c + 1) * LANES)

    @pl.when(pl.program_id(0) == 0)
    def _():
        hc[...] = h0_ref[...]

    if batched:
        u_all = u_ref[...].reshape(nb * tt, SSM_WIDTH)
        ub_all = u_all.astype(BF16)
        for hf in range(2):
            bu = jnp.dot(ub_all[:, hf * SSM_HALF:(hf + 1) * SSM_HALF], bmat_ref[hf],
                         preferred_element_type=F32)
            for c in range(ntile):
                tmp[c] = bu[:, lane_tile(c)]
            for c in range(ntile):
                for t in range(tt):
                    buf[c, t * rows + hf * nb:t * rows + (hf + 1) * nb, :] = tmp[c, pl.ds(t, nb, stride=tt), :]
    else:
        ub_all = u_ref[...].reshape(nb * tt, SSM_WIDTH).astype(BF16)
        for hf in range(2):
            bu = jnp.dot(ub_all[:, hf * SSM_HALF:(hf + 1) * SSM_HALF], bmat_ref[hf],
                         preferred_element_type=F32)
            for b in range(nb):
                for c in range(ntile):
                    buf[c, pl.ds(hf * nb + b, tt, stride=rows), :] = bu[b * tt:(b + 1) * tt, lane_tile(c)]

    group = 8
    for s in range(rows // SUBLANES):
        r0 = s * SUBLANES
        for c0 in range(0, half_tiles, group):
            ar = [are_ref[r0:r0 + SUBLANES, lane_tile(c0 + k)] for k in range(group)]
            ai = [aim_ref[r0:r0 + SUBLANES, lane_tile(c0 + k)] for k in range(group)]
            init = tuple(hc[r0:r0 + SUBLANES, lane_tile(c0 + k)] for k in range(group)) + tuple(
                hc[r0:r0 + SUBLANES, lane_tile(half_tiles + c0 + k)] for k in range(group))

            def step(t, carry, r0=r0, c0=c0, ar=ar, ai=ai):
                row = pl.multiple_of(t * rows + r0, SUBLANES)
                out_r, out_i = [], []
                for k in range(group):
                    hr, hi = carry[k], carry[group + k]
                    xr = buf[c0 + k, pl.ds(row, SUBLANES), :]
                    xi = buf[half_tiles + c0 + k, pl.ds(row, SUBLANES), :]
                    nr = ar[k] * hr - ai[k] * hi + xr
                    ni = ar[k] * hi + ai[k] * hr + xi
                    buf[c0 + k, pl.ds(row, SUBLANES), :] = nr
                    buf[half_tiles + c0 + k, pl.ds(row, SUBLANES), :] = ni
                    out_r.append(nr)
                    out_i.append(ni)
                return tuple(out_r) + tuple(out_i)

            fin = lax.fori_loop(0, tt, step, init, unroll=min(tt, 8))
            for k in range(group):
                hc[r0:r0 + SUBLANES, lane_tile(c0 + k)] = fin[k]
                hc[r0:r0 + SUBLANES, lane_tile(half_tiles + c0 + k)] = fin[group + k]

    if batched:
        parts = []
        for hf in range(2):
            for c in range(ntile):
                for t in range(tt):
                    tmp[c, pl.ds(t, nb, stride=tt), :] = buf[c, t * rows + hf * nb:t * rows + (hf + 1) * nb, :]
            hs = jnp.concatenate([tmp[c] for c in range(ntile)], axis=1).astype(BF16)
            parts.append(jnp.dot(hs, cmat_ref[hf], preferred_element_type=F32))
        y_all = jnp.concatenate(parts, axis=1) + d_ref[...] * u_all
        y_ref[...] = y_all.reshape(nb, tt, SSM_WIDTH)
    else:
        parts = []
        for hf in range(2):
            hs = jnp.concatenate(
                [jnp.concatenate([buf[c, pl.ds(hf * nb + b, tt, stride=rows), :] for c in range(ntile)],
                                 axis=1).astype(BF16) for b in range(nb)], axis=0)
            parts.append(jnp.dot(hs, cmat_ref[hf], preferred_element_type=F32))
        y_all = jnp.concatenate(parts, axis=1) + d_ref[...] * u_ref[...].reshape(nb * tt, SSM_WIDTH)
        y_ref[...] = y_all.reshape(nb, tt, SSM_WIDTH)

    ht_ref[...] = hc[...]


def _s5(u3, h0, bmat, cmat, a_re, a_im, d_skip):
    nb, t_len, _ = u3.shape
    tt = min(t_len, 256)
    rows = 2 * nb
    batched = tt < 16
    kern = functools.partial(_s5_kernel, nb=nb, tt=tt, batched=batched)
    ntile = 2 * STATE_HALF // LANES
    tmp_shape = (ntile, nb * tt, LANES) if batched else (1, SUBLANES, LANES)
    return pl.pallas_call(
        kern,
        grid=(t_len // tt,),
        in_specs=[pl.BlockSpec((nb, tt, SSM_WIDTH), lambda i: (0, i, 0)),
                  _full((2, SSM_HALF, 2 * STATE_HALF)),
                  _full((2, 2 * STATE_HALF, SSM_HALF)),
                  _full((rows, STATE_HALF)), _full((rows, STATE_HALF)),
                  _full((rows, 2 * STATE_HALF)), _full((1, SSM_WIDTH))],
        out_specs=[pl.BlockSpec((nb, tt, SSM_WIDTH), lambda i: (0, i, 0)),
                   _full((rows, 2 * STATE_HALF))],
        out_shape=[jax.ShapeDtypeStruct((nb, t_len, SSM_WIDTH), F32),
                   jax.ShapeDtypeStruct((rows, 2 * STATE_HALF), F32)],
        scratch_shapes=[pltpu.VMEM((ntile, tt * rows, LANES), F32),
                        pltpu.VMEM((rows, 2 * STATE_HALF), F32),
                        pltpu.VMEM(tmp_shape, F32)],
        compiler_params=pltpu.CompilerParams(dimension_semantics=("arbitrary",),
                                             vmem_limit_bytes=VMEM_LIMIT),
        name="s5_scan",
    )(u3, bmat, cmat, a_re, a_im, h0, d_skip.reshape(1, SSM_WIDTH))


def _s5_params(a_re, a_im, b_re, b_im, c_re, c_im, log_dt):
    dt = jnp.exp(log_dt)[:, None]
    mag = jnp.exp(dt * a_re)
    ang = dt * a_im
    ab_re, ab_im = mag * jnp.cos(ang), mag * jnp.sin(ang)
    den = a_re * a_re + a_im * a_im
    nr, ni = ab_re - 1.0, ab_im
    f_re = (nr * a_re + ni * a_im) / den
    f_im = (ni * a_re - nr * a_im) / den
    bb_re = f_re[..., None] * b_re - f_im[..., None] * b_im
    bb_im = f_re[..., None] * b_im + f_im[..., None] * b_re
    gh = SSM_GROUPS // 2
    eye = jnp.eye(gh, dtype=F32)

    def b_half(w):
        return jnp.einsum('gnc,gh->gchn', w, eye).reshape(gh * SSM_GROUP, gh * SSM_STATE)

    def c_half(w):
        return jnp.einsum('gcn,gh->gnhc', w, eye).reshape(gh * SSM_STATE, gh * SSM_GROUP)

    bmat = jnp.stack([jnp.concatenate([b_half(bb_re[h * gh:(h + 1) * gh]),
                                       b_half(bb_im[h * gh:(h + 1) * gh])], axis=1)
                      for h in range(2)]).astype(BF16)
    cmat = jnp.stack([jnp.concatenate([c_half(c_re[h * gh:(h + 1) * gh]),
                                       -c_half(c_im[h * gh:(h + 1) * gh])], axis=0)
                      for h in range(2)]).astype(BF16)
    return bmat, cmat, ab_re.reshape(2, 1, STATE_HALF), ab_im.reshape(2, 1, STATE_HALF)


def _state_to_rows(h_re, h_im):
    nb = h_re.shape[0]
    f = lambda h: h.reshape(nb, 2, STATE_HALF).transpose(1, 0, 2).reshape(2 * nb, STATE_HALF)
    return jnp.concatenate([f(h_re), f(h_im)], axis=1)


def _rows_to_state(ht, nb):
    f = lambda h: h.reshape(2, nb, STATE_HALF).transpose(1, 0, 2).reshape(nb, SSM_GROUPS, SSM_STATE)
    return f(ht[:, :STATE_HALF]), f(ht[:, STATE_HALF:])


def _attn_prompt_kernel(q_ref, kp_ref, kc_ref, vp_ref, vc_ref, o_ref, l_ref, bias_s, s_s, p_s, *, dil):
    blk = ATTN_BLOCK
    n = pl.program_id(2)

    @pl.when((pl.program_id(0) == 0) & (pl.program_id(1) == 0) & (n == 0))
    def _():
        i_idx = lax.broadcasted_iota(I32, (blk, 2 * blk), 0)
        j_idx = lax.broadcasted_iota(I32, (blk, 2 * blk), 1)
        delta = i_idx - j_idx + blk
        in_band = (delta >= 0) & (delta <= blk)
        dist = (delta * dil).astype(F32)
        for h in range(N_HEADS):
            biased = -ALIBI_SLOPES[h] * dist
            bias_s[0, h] = jnp.where(in_band & (j_idx >= blk), biased, MASK_VALUE)
            bias_s[1, h] = jnp.where(in_band, biased, MASK_VALUE)

    lane = lax.broadcasted_iota(I32, (blk, LANES), 1)
    nqb = q_ref.shape[0] // blk
    for qb in range(nqb):
        which = jnp.minimum(n, 1) if qb == 0 else 1
        rows = slice(qb * blk, (qb + 1) * blk)
        for hp in range(N_HEADS // 2):
            cols = slice(hp * LANES, (hp + 1) * LANES)
            q2 = q_ref[rows, cols]
            before = kp_ref[:, cols] if qb == 0 else kc_ref[(qb - 1) * blk:qb * blk, cols]
            kk = jnp.concatenate([before, kc_ref[rows, cols]], axis=0)
            for half in range(2):
                h = 2 * hp + half
                in_head = (lane >= half * HEAD_DIM) & (lane < (half + 1) * HEAD_DIM)
                qm = jnp.where(in_head, q2, jnp.zeros_like(q2))
                s = lax.dot_general(qm, kk, (((1,), (1,)), ((), ())), preferred_element_type=F32)
                s_s[qb * N_HEADS + h] = s + bias_s[which, h]
    for qb in range(nqb):
        lse_all = jnp.zeros((blk, LANES), F32)
        for h in range(N_HEADS):
            s = s_s[qb * N_HEADS + h]
            m = jnp.max(s, axis=1, keepdims=True)
            p = jnp.exp(s - m)
            l = jnp.sum(p, axis=1, keepdims=True)
            p_s[qb * N_HEADS + h] = (p * (1.0 / l)).astype(BF16)
            lse_all = jnp.where(lane == h, m + jnp.log(l), lse_all)
        l_ref[qb * blk:(qb + 1) * blk, :] = lse_all
    for qb in range(nqb):
        rows = slice(qb * blk, (qb + 1) * blk)
        for hp in range(N_HEADS // 2):
            cols = slice(hp * LANES, (hp + 1) * LANES)
            before = vp_ref[:, cols] if qb == 0 else vc_ref[(qb - 1) * blk:qb * blk, cols]
            vv = jnp.concatenate([before, vc_ref[rows, cols]], axis=0)
            outs = [jnp.dot(p_s[qb * N_HEADS + 2 * hp + half], vv, preferred_element_type=F32)
                    for half in range(2)]
            o_ref[rows, cols] = jnp.where(lane < HEAD_DIM, outs[0], outs[1])


def _attn_prompt_branch(qb, kb, vb, bsz, s_len, dil):
    sub = s_len // dil
    nqb = ATTN_STEP_BLOCKS
    step = nqb * ATTN_BLOCK
    nstep = sub // step
    view = lambda t: t.reshape(bsz, sub, dil * ATTN_WIDTH)
    cur = pl.BlockSpec((None, step, ATTN_WIDTH), lambda b, r, n: (b, n, r))
    prev = pl.BlockSpec((None, ATTN_BLOCK, ATTN_WIDTH),
                        lambda b, r, n: (b, jnp.maximum(n * nqb - 1, 0), r))
    o, lse = pl.pallas_call(
        functools.partial(_attn_prompt_kernel, dil=dil),
        grid=(bsz, dil, nstep),
        in_specs=[cur, prev, cur, prev, cur],
        out_specs=[cur, pl.BlockSpec((None, step, LANES), lambda b, r, n: (b, n, r))],
        out_shape=[jax.ShapeDtypeStruct((bsz, sub, dil * ATTN_WIDTH), F32),
                   jax.ShapeDtypeStruct((bsz, sub, dil * LANES), F32)],
        scratch_shapes=[pltpu.VMEM((2, N_HEADS, ATTN_BLOCK, 2 * ATTN_BLOCK), F32),
                        pltpu.VMEM((nqb * N_HEADS, ATTN_BLOCK, 2 * ATTN_BLOCK), F32),
                        pltpu.VMEM((nqb * N_HEADS, ATTN_BLOCK, 2 * ATTN_BLOCK), BF16)],
        compiler_params=pltpu.CompilerParams(
            dimension_semantics=("arbitrary", "arbitrary", "arbitrary"),
            vmem_limit_bytes=VMEM_LIMIT),
        name=f"attn_prompt_d{dil}",
    )(view(qb), view(kb), view(kb), view(vb), view(vb))
    return o.reshape(bsz * sub, dil * ATTN_WIDTH), lse.reshape(bsz * sub, dil * LANES)


def _attn_sample_kernel(q_ref, kn_ref, vn_ref, kc_ref, vc_ref, o_ref, bias_s, mult_s, *, t_len, wb):
    nrow = N_HEADS * t_len
    t_shift = t_len.bit_length() - 1
    d_shift = HEAD_DIM.bit_length() - 1
    nt = (((1,), (1,)), ((), ()))

    def branch_count(dist):
        mult = jnp.zeros(dist.shape, F32)
        for win, dil in zip(WINDOWS, DILATIONS):
            hit = (dist >= 0) & (dist <= win) & ((dist & (dil - 1)) == 0)
            mult = mult + jnp.where(hit, 1.0, 0.0)
        return mult

    def biased(dist, mult):
        head = lax.broadcasted_iota(I32, dist.shape, 0) >> t_shift
        slope = jnp.zeros(dist.shape, F32)
        for h in range(N_HEADS):
            slope = jnp.where(head == h, ALIBI_SLOPES[h], slope)
        return jnp.where(mult > 0.0, -slope * dist.astype(F32), MASK_VALUE)

    @pl.when(pl.program_id(0) == 0)
    def _():
        row = lax.broadcasted_iota(I32, (nrow, wb), 0)
        col = lax.broadcasted_iota(I32, (nrow, wb), 1)
        dist = wb + (row & (t_len - 1)) - col
        mult = branch_count(dist)
        mult_s[...] = mult
        bias_s[...] = biased(dist, mult)

    q = q_ref[...].astype(F32)
    qt = jnp.concatenate([q] * N_HEADS, axis=0)
    row_w = lax.broadcasted_iota(I32, (nrow, ATTN_WIDTH), 0)
    lane_w = lax.broadcasted_iota(I32, (nrow, ATTN_WIDTH), 1)
    qm = jnp.where((lane_w >> d_shift) == (row_w >> t_shift), qt, 0.0).astype(BF16)

    pad = LANES - t_len
    kn = jnp.concatenate([kn_ref[...].astype(F32), jnp.zeros((pad, ATTN_WIDTH), F32)], axis=0).astype(BF16)
    vn = jnp.concatenate([vn_ref[...].astype(F32), jnp.zeros((pad, ATTN_WIDTH), F32)], axis=0).astype(BF16)
    row_n = lax.broadcasted_iota(I32, (nrow, LANES), 0)
    col_n = lax.broadcasted_iota(I32, (nrow, LANES), 1)
    dist_n = jnp.where(col_n < t_len, (row_n & (t_len - 1)) - col_n, -1)
    mult_n = branch_count(dist_n)
    s_n = lax.dot_general(qm, kn, nt, preferred_element_type=F32) + biased(dist_n, mult_n)
    s_c = jnp.dot(qm, kc_ref[...].astype(BF16), preferred_element_type=F32) + bias_s[...]

    m = jnp.maximum(jnp.max(s_c, axis=1, keepdims=True), jnp.max(s_n, axis=1, keepdims=True))
    p_c = jnp.exp(s_c - m) * mult_s[...]
    p_n = jnp.exp(s_n - m) * mult_n
    l = jnp.sum(p_c, axis=1, keepdims=True) + jnp.sum(p_n, axis=1, keepdims=True)
    o = (lax.dot_general(p_c.astype(BF16), vc_ref[...].astype(BF16), nt, preferred_element_type=F32)
         + jnp.dot(p_n.astype(BF16), vn, preferred_element_type=F32)) / l
    lane_o = lax.broadcasted_iota(I32, (t_len, ATTN_WIDTH), 1) >> d_shift
    out = jnp.zeros((t_len, ATTN_WIDTH), F32)
    for h in range(N_HEADS):
        out = jnp.where(lane_o == h, o[h * t_len:(h + 1) * t_len], out)
    o_ref[...] = out


def _attn_sample(qb, kb, vb, cache_k, cache_v):
    bsz, t_len, _ = qb.shape
    wb = cache_k.shape[1]
    feature_major = lambda c: jnp.transpose(c, (0, 2, 3, 1)).reshape(bsz, ATTN_WIDTH, wb)
    new = pl.BlockSpec((None, t_len, ATTN_WIDTH), lambda b: (b, 0, 0))
    old = pl.BlockSpec((None, ATTN_WIDTH, wb), lambda b: (b, 0, 0))
    return pl.pallas_call(
        functools.partial(_attn_sample_kernel, t_len=t_len, wb=wb),
        grid=(bsz,),
        in_specs=[new, new, new, old, old],
        out_specs=new,
        out_shape=jax.ShapeDtypeStruct((bsz, t_len, ATTN_WIDTH), F32),
        scratch_shapes=[pltpu.VMEM((N_HEADS * t_len, wb), F32),
                        pltpu.VMEM((N_HEADS * t_len, wb), F32)],
        compiler_params=pltpu.CompilerParams(dimension_semantics=("arbitrary",),
                                             vmem_limit_bytes=VMEM_LIMIT),
        name="attn_sample",
    )(qb, kb, vb, feature_major(cache_k), feature_major(cache_v))


def _mid_kernel(*refs, dils):
    n_branch = max(len(dils), 1)
    x_ref, y_ref = refs[0], refs[1]
    o_refs = refs[2:2 + n_branch]
    pos = 2 + n_branch
    l_refs = refs[pos:pos + len(dils)]
    pos += len(l_refs)
    (wglu_ref, bglu_ref, lns_ref, lna_ref, wout_ref, lnm_ref, wrh_ref, wrl_ref, br_ref,
     expand_ref, tri_ref,
     x1_ref, hrow_ref, eidx_ref, epos_ref, egate_ref, cnt_ref, carry, nat) = refs[pos:]
    tm = x_ref.shape[0]
    o_tiles = ATTN_WIDTH // LANES

    rb = min(tm, MID_ROW_BLOCK)

    for bi, d in enumerate(dils):
        if d == 1:
            continue
        for r in range(d):
            spread = pl.ds(r, tm // d, stride=d)
            for ct in range(o_tiles):
                c0 = r * ATTN_WIDTH + ct * LANES
                nat[bi, ct, spread, :] = o_refs[bi][:, c0:c0 + LANES]
            nat[bi, o_tiles, spread, :] = l_refs[bi][:, r * LANES:(r + 1) * LANES]

    def natural(bi, d, rows):
        if d == 1:
            return o_refs[bi][rows, :], l_refs[bi][rows, :]
        return (jnp.concatenate([nat[bi, ct, rows, :] for ct in range(o_tiles)], axis=1),
                nat[bi, o_tiles, rows, :])

    @pl.when(pl.program_id(0) == 0)
    def _():
        carry[...] = jnp.zeros_like(carry)

    lane = lax.broadcasted_iota(I32, (rb, LANES), 1)
    lane_f = lane.astype(F32)
    run = carry[0:1, :]
    for blk in range(tm // rb):
        rows = slice(blk * rb, (blk + 1) * rb)
        y = y_ref[rows, :]
        z = y * (0.5 * (1.0 + jnp.tanh(math.sqrt(2.0 / math.pi) * (y + 0.044715 * (y * y * y)))))
        glu = z * _sigmoid(jnp.dot(z.astype(BF16), wglu_ref[...], preferred_element_type=F32)
                           + bglu_ref[...])
        n_ssm = _rms(glu, lns_ref[...])

        if not dils:
            attn = o_refs[0][rows, :]
        else:
            pairs = [natural(bi, d, rows) for bi, d in enumerate(dils)]
            lses = [p[1] for p in pairs]
            mx = functools.reduce(jnp.maximum, lses)
            es = [jnp.exp(l - mx) for l in lses]
            inv = 1.0 / functools.reduce(lambda a, b: a + b, es)
            attn = jnp.zeros((rb, ATTN_WIDTH), F32)
            for e, (o_nat, _) in zip(es, pairs):
                w = e * inv
                w_hi = w.astype(BF16)
                w_lo = (w - w_hi.astype(F32)).astype(BF16)
                wide = jnp.dot(jnp.concatenate([w_hi, w_lo], axis=1), expand_ref[...],
                               preferred_element_type=F32)
                attn = attn + wide * o_nat
        n_attn = _rms(attn, lna_ref[...])

        x1 = (x_ref[rows, :]
              + jnp.dot(n_ssm.astype(BF16), wout_ref[:SSM_WIDTH, :], preferred_element_type=F32)
              + jnp.dot(n_attn.astype(BF16), wout_ref[SSM_WIDTH:, :], preferred_element_type=F32))
        x1_ref[rows, :] = x1
        hm = _rms(x1, lnm_ref[...])
        for s in range(ROW_TILES):
            hrow_ref[pl.ds(blk * rb * ROW_TILES + s, rb, stride=ROW_TILES), :] = hm[:, s * LANES:(s + 1) * LANES]

        h_hi = hm.astype(BF16)
        h_lo = (hm - h_hi.astype(F32)).astype(BF16)
        both = jnp.dot(h_hi, wrl_ref[...], preferred_element_type=F32)
        logits = (both[:, :LANES] + both[:, LANES:]
                  + jnp.dot(h_lo, wrh_ref[...], preferred_element_type=F32)
                  + br_ref[...])
        work = jnp.where(lane < N_EXPERTS, logits, NEG_BIG)
        vals, idxs, hots = [], [], []
        for _ in range(TOP_K):
            m = jnp.max(work, axis=1, keepdims=True)
            idx = jnp.min(jnp.where(work == m, lane_f, float(LANES)), axis=1, keepdims=True)
            hot = lane_f == idx
            vals.append(m)
            idxs.append(idx)
            hots.append(hot)
            work = jnp.where(hot, NEG_BIG, work)
        exps = [jnp.exp(v - vals[0]) for v in vals]
        inv = 1.0 / functools.reduce(lambda a, b: a + b, exps)

        sel = functools.reduce(lambda a, b: a + b, [h.astype(F32) for h in hots])
        before = jnp.dot(tri_ref[...], sel.astype(BF16), preferred_element_type=F32) + run
        eidx = jnp.zeros((rb, LANES), I32)
        epos = jnp.zeros((rb, LANES), I32)
        egate = jnp.zeros((rb, LANES), F32)
        for k in range(TOP_K):
            pk = jnp.sum(jnp.where(hots[k], before, 0.0), axis=1, keepdims=True)
            eidx = jnp.where(lane == k, idxs[k].astype(I32), eidx)
            epos = jnp.where(lane == k, pk.astype(I32), epos)
            egate = jnp.where(lane == k, exps[k] * inv, egate)
        eidx_ref[rows, :] = eidx
        epos_ref[rows, :] = epos
        egate_ref[rows, :] = egate
        run = run + jnp.sum(sel, axis=0, keepdims=True)
    carry[...] = jnp.broadcast_to(run, carry.shape)
    cnt_ref[...] = jnp.broadcast_to(run, cnt_ref.shape).astype(I32)


def _mid(x2d, y2d, attn_o, attn_lse, dils, w):
    n = x2d.shape[0]
    tm = min(n, MID_TILE)
    rb = min(tm, MID_ROW_BLOCK)
    ti = jnp.arange(rb)
    tri = (ti[:, None] > ti[None, :]).astype(BF16)
    n_branch = len(attn_o)
    row = lambda width: pl.BlockSpec((tm, width), lambda i: (i, 0))
    packed = lambda d, width: pl.BlockSpec((tm // d, d * width), lambda i: (i, 0))
    attn_specs = ([packed(d, ATTN_WIDTH) for d in dils] + [packed(d, LANES) for d in dils]
                  if dils else [row(ATTN_WIDTH)])
    in_specs = ([row(D_MODEL), row(SSM_WIDTH)] + attn_specs
                + [_full((SSM_WIDTH, SSM_WIDTH)), _full((1, SSM_WIDTH)), _full((1, SSM_WIDTH)),
                   _full((1, ATTN_WIDTH)), _full((D_MODEL, D_MODEL)), _full((1, D_MODEL)),
                   _full((D_MODEL, LANES)), _full((D_MODEL, 2 * LANES)), _full((1, LANES)),
                   _full((2 * LANES, ATTN_WIDTH)), _full((rb, rb))])
    out_specs = [row(D_MODEL), pl.BlockSpec((tm * ROW_TILES, LANES), lambda i: (i, 0)),
                 row(LANES), row(LANES), row(LANES), _full((SUBLANES, LANES))]
    out_shape = [jax.ShapeDtypeStruct((n, D_MODEL), F32),
                 jax.ShapeDtypeStruct((n * ROW_TILES, LANES), F32),
                 jax.ShapeDtypeStruct((n, LANES), I32),
                 jax.ShapeDtypeStruct((n, LANES), I32),
                 jax.ShapeDtypeStruct((n, LANES), F32),
                 jax.ShapeDtypeStruct((SUBLANES, LANES), I32)]
    return pl.pallas_call(
        functools.partial(_mid_kernel, dils=tuple(dils)),
        grid=(n // tm,),
        in_specs=in_specs,
        out_specs=out_specs,
        out_shape=out_shape,
        scratch_shapes=[pltpu.VMEM((SUBLANES, LANES), F32),
                        pltpu.VMEM((n_branch, ATTN_WIDTH // LANES + 1, tm, LANES), F32)],
        compiler_params=pltpu.CompilerParams(dimension_semantics=("arbitrary",),
                                             vmem_limit_bytes=VMEM_LIMIT),
        name="mid",
    )(x2d, y2d, *attn_o, *attn_lse, w["w_glu"], w["b_glu"], w["ln_ssm_out"], w["ln_attn_out"],
      w["w_out"], w["ln_moe"], w["wr_hi"], w["wr_lo"], w["b_router"], w["expand"], tri)


def _dispatch_kernel(dest_ref, hp_ref, hs_ref, xs_hbm, sem, *, n_prompt_tiles):
    i = pl.program_id(0)
    npair = TOKEN_TILE * TOP_K

    def run(src_ref):
        def issue(t, c):
            src = src_ref.at[pl.ds(pl.multiple_of(t * ROW_TILES, ROW_TILES), ROW_TILES), :]
            for k in range(TOP_K):
                d = dest_ref[0, t * TOP_K + k]
                pltpu.make_async_copy(
                    src, xs_hbm.at[pl.ds(pl.multiple_of(d * ROW_TILES, ROW_TILES), ROW_TILES), :],
                    sem).start()
            return c

        lax.fori_loop(0, TOKEN_TILE, issue, 0, unroll=2)
        span = pl.ds(0, npair * ROW_TILES)
        pltpu.make_async_copy(xs_hbm.at[span, :], xs_hbm.at[span, :], sem).wait()

    @pl.when(i < n_prompt_tiles)
    def _():
        run(hp_ref)

    @pl.when(i >= n_prompt_tiles)
    def _():
        run(hs_ref)


def _dispatch(dest, hrow_p, hrow_s):
    n_p = hrow_p.shape[0] // ROW_TILES
    n_s = hrow_s.shape[0] // ROW_TILES
    npt = n_p // TOKEN_TILE
    ntile = (n_p + n_s) // TOKEN_TILE
    npair = TOKEN_TILE * TOP_K
    blk = (TOKEN_TILE * ROW_TILES, LANES)
    return pl.pallas_call(
        functools.partial(_dispatch_kernel, n_prompt_tiles=npt),
        grid=(ntile,),
        in_specs=[pl.BlockSpec((None, 1, npair), lambda i: (i, 0, 0), memory_space=pltpu.SMEM),
                  pl.BlockSpec(blk, lambda i: (jnp.minimum(i, npt - 1), 0)),
                  pl.BlockSpec(blk, lambda i: (jnp.maximum(i - npt, 0), 0))],
        out_specs=pl.BlockSpec(memory_space=pl.ANY),
        out_shape=jax.ShapeDtypeStruct(((n_p + n_s) * TOP_K * ROW_TILES, LANES), F32),
        scratch_shapes=[pltpu.SemaphoreType.DMA(())],
        compiler_params=pltpu.CompilerParams(dimension_semantics=("arbitrary",),
                                             vmem_limit_bytes=VMEM_LIMIT),
        name="moe_dispatch",
    )(dest.reshape(ntile, 1, npair), hrow_p, hrow_s)


def _expert_kernel(vt_ref, ve_ref, vok_ref, gs_ref, xs_ref, wu_ref, bu_ref, wd_ref, bd_ref,
                   out_ref, wu_s, wd_s, x_s):
    v = pl.program_id(0)
    e = ve_ref[v]
    j = vt_ref[v]
    vprev = jnp.maximum(v - 1, 0)
    new_e = (v == 0) | (e != ve_ref[vprev])
    new_j = (v == 0) | (j != vt_ref[vprev])
    tm = MOE_TILE

    @pl.when(new_e)
    def _():
        wu_s[...] = wu_ref[...].astype(BF16)
        wd_s[...] = wd_ref[...].astype(BF16)

    @pl.when(new_j)
    def _():
        out_ref[...] = jnp.zeros_like(out_ref)

    lo, hi = gs_ref[e], gs_ref[e + 1]

    def run_pass(row0, nrows):
        base = row0 * ROW_TILES
        for s in range(ROW_TILES):
            x_s[:nrows, s * LANES:(s + 1) * LANES] = xs_ref[
                pl.ds(base + s, nrows, stride=ROW_TILES), :].astype(BF16)
        a = jnp.dot(x_s[:nrows, :], wu_s[...], preferred_element_type=F32) + bu_ref[...]
        g = jnp.minimum(a[:, :EXPERT_FF], SWIGLU_LIMIT)
        lin = jnp.clip(a[:, EXPERT_FF:], -SWIGLU_LIMIT, SWIGLU_LIMIT)
        act = (lin + 1.0) * (g * _sigmoid(SWIGLU_ALPHA * g))
        y = jnp.dot(act.astype(BF16), wd_s[...], preferred_element_type=F32) + bd_ref[...]
        rows = j * tm + row0 + lax.broadcasted_iota(I32, (nrows, 1), 0)
        mine = (rows >= lo) & (rows < hi)
        for s in range(ROW_TILES):
            cur = out_ref[pl.ds(base + s, nrows, stride=ROW_TILES), :]
            out_ref[pl.ds(base + s, nrows, stride=ROW_TILES), :] = jnp.where(
                mine, y[:, s * LANES:(s + 1) * LANES], cur)

    sub = x_s.shape[0]
    half = sub // 2
    live = vok_ref[v] == 1
    for part in range(tm // sub):
        row0 = part * sub
        start = j * tm + row0
        in_first = live & (lo < start + half) & (hi > start)
        in_second = live & (lo < start + sub) & (hi > start + half)
        pl.when(in_first & in_second)(functools.partial(run_pass, row0, sub))
        pl.when(in_first & jnp.logical_not(in_second))(functools.partial(run_pass, row0, half))
        pl.when(in_second & jnp.logical_not(in_first))(functools.partial(run_pass, row0 + half, half))


def _experts(xs, vt, ve, vok, gstart, w_up, b_up, w_down, b_down):
    tm = MOE_TILE
    nvisit = vt.shape[0]
    rows = pl.BlockSpec((tm * ROW_TILES, LANES), lambda v, vt, ve, vok, gs: (vt[v], 0))
    per_e = lambda a, b: pl.BlockSpec((None, a, b), lambda v, vt, ve, vok, gs: (ve[v], 0, 0))
    grid_spec = pltpu.PrefetchScalarGridSpec(
        num_scalar_prefetch=4,
        grid=(nvisit,),
        in_specs=[rows, per_e(D_MODEL, 2 * EXPERT_FF), per_e(1, 2 * EXPERT_FF),
                  per_e(EXPERT_FF, D_MODEL), per_e(1, D_MODEL)],
        out_specs=rows,
        scratch_shapes=[pltpu.VMEM((D_MODEL, 2 * EXPERT_FF), BF16),
                        pltpu.VMEM((EXPERT_FF, D_MODEL), BF16),
                        pltpu.VMEM((MOE_SUBTILE, D_MODEL), BF16)],
    )
    return pl.pallas_call(
        _expert_kernel,
        grid_spec=grid_spec,
        out_shape=jax.ShapeDtypeStruct(xs.shape, F32),
        compiler_params=pltpu.CompilerParams(dimension_semantics=("arbitrary",),
                                             vmem_limit_bytes=VMEM_LIMIT),
        name="moe_experts",
    )(vt, ve, vok, gstart, xs, w_up, b_up.reshape(N_EXPERTS, 1, 2 * EXPERT_FF),
      w_down, b_down.reshape(N_EXPERTS, 1, D_MODEL))


def _routing(eidx_p, epos_p, cnt_p, eidx_s, epos_s, cnt_s, n_rows):
    cnt_p = cnt_p[0, :N_EXPERTS]
    cnt_s = cnt_s[0, :N_EXPERTS]
    cnt = cnt_p + cnt_s
    gend = jnp.cumsum(cnt)
    gstart = gend - cnt
    experts = jnp.arange(N_EXPERTS, dtype=I32)

    def lookup(table, idx):
        return jnp.sum(jnp.where(idx[..., None] == experts, table, 0), axis=-1)

    ep = eidx_p[:, :TOP_K]
    es = eidx_s[:, :TOP_K]
    dest_p = lookup(gstart, ep) + epos_p[:, :TOP_K]
    dest_s = lookup(gstart + cnt_p, es) + epos_s[:, :TOP_K]
    ntile = n_rows // MOE_TILE
    nvisit = ntile + N_EXPERTS
    first = gstart // MOE_TILE
    last = jnp.maximum(gend - 1, 0) // MOE_TILE
    nv = jnp.where(cnt > 0, last - first + 1, 0)
    vend = jnp.cumsum(nv)
    vstart = vend - nv
    total = vend[-1]
    v = jnp.arange(nvisit, dtype=I32)
    vc = jnp.minimum(v, total - 1)
    ve = jnp.sum((vend[None, :] <= vc[:, None]).astype(I32), axis=1)
    vt = (lookup(first - vstart, ve) + vc).astype(I32)
    vok = (v < total).astype(I32)
    gs = jnp.concatenate([gstart, gend[-1:]]).astype(I32)
    return dest_p.astype(I32), dest_s.astype(I32), vt, ve, vok, gs


def _out_kernel(dest_ref, next_ref, y_hbm, x1_ref, gate_ref, pe_ref, lnp_ref, wg_ref, bg_ref, wp_ref,
                lnf_ref, o_ref, buf, sem, *, ntile):
    tm = TOKEN_TILE
    npair = tm * TOP_K
    i = pl.program_id(0)
    slot = i % 2

    def gather(idx_ref, to):
        def issue(t, c):
            for k in range(TOP_K):
                d = idx_ref[0, t * TOP_K + k]
                pltpu.make_async_copy(
                    y_hbm.at[pl.ds(pl.multiple_of(d * ROW_TILES, ROW_TILES), ROW_TILES), :],
                    buf.at[to, pl.ds(pl.multiple_of((k * tm + t) * ROW_TILES, ROW_TILES), ROW_TILES), :],
                    sem.at[to]).start()
            return c

        lax.fori_loop(0, tm, issue, 0, unroll=2)

    @pl.when(i == 0)
    def _():
        gather(dest_ref, 0)

    if ntile > 1:
        @pl.when(i + 1 < ntile)
        def _():
            gather(next_ref, 1 - slot)

    pltpu.make_async_copy(y_hbm.at[pl.ds(0, npair * ROW_TILES), :], buf.at[slot], sem.at[slot]).wait()

    gates = gate_ref[...]
    parts = []
    for s in range(ROW_TILES):
        acc = jnp.zeros((tm, LANES), F32)
        for k in range(TOP_K):
            rows = buf[slot, pl.ds(k * tm * ROW_TILES + s, tm, stride=ROW_TILES), :]
            acc = acc + gates[:, k:k + 1] * rows
        parts.append(acc)
    x2 = x1_ref[...] + jnp.concatenate(parts, axis=1)
    gate = _sigmoid(jnp.dot(_rms(x2, lnp_ref[...]).astype(BF16), wg_ref[...],
                            preferred_element_type=F32) + bg_ref[...])
    x3 = x2 + gate * jnp.dot(pe_ref[...].astype(BF16), wp_ref[...], preferred_element_type=F32)
    o_ref[...] = _rms(x3, lnf_ref[...])


def _combine(dest, y_rows, x1, egate, pe, w):
    n = x1.shape[0]
    tm = TOKEN_TILE
    npair = tm * TOP_K
    row = lambda width: pl.BlockSpec((tm, width), lambda i: (i, 0))
    ntile = n // tm
    dest3 = dest.reshape(ntile, 1, npair)
    return pl.pallas_call(
        functools.partial(_out_kernel, ntile=ntile),
        grid=(ntile,),
        in_specs=[pl.BlockSpec((None, 1, npair), lambda i: (i, 0, 0), memory_space=pltpu.SMEM),
                  pl.BlockSpec((None, 1, npair), lambda i: (jnp.minimum(i + 1, ntile - 1), 0, 0),
                               memory_space=pltpu.SMEM),
                  pl.BlockSpec(memory_space=pl.ANY),
                  row(D_MODEL), row(LANES), row(PLE_DIM),
                  _full((1, D_MODEL)), _full((D_MODEL, D_MODEL)), _full((1, D_MODEL)),
                  _full((PLE_DIM, D_MODEL)), _full((1, D_MODEL))],
        out_specs=row(D_MODEL),
        out_shape=jax.ShapeDtypeStruct((n, D_MODEL), F32),
        scratch_shapes=[pltpu.VMEM((2, npair * ROW_TILES, LANES), F32), pltpu.SemaphoreType.DMA((2,))],
        compiler_params=pltpu.CompilerParams(dimension_semantics=("arbitrary",),
                                             vmem_limit_bytes=VMEM_LIMIT),
        name="combine_out",
    )(dest3, dest3, y_rows, x1, egate, pe, w["ln_ple"], w["w_ple_gate"],
      w["b_ple_gate"], w["w_ple_proj"], w["ln_final"])


def kernel(x_prompt, x_sample, cache_attn_k, cache_attn_v, state_ssm_re, state_ssm_im, p_prompt, p_sample, ln_mix, w_in, ssm_a_re, ssm_a_im, ssm_b_re, ssm_b_im, ssm_c_re, ssm_c_im, ssm_d, ssm_log_dt, w_glu, b_glu, ln_ssm_out, ln_attn_out, w_out, ln_moe, w_router, b_router, w_up, b_up, w_down, b_down, ln_ple, w_ple_gate, b_ple_gate, w_ple_proj, ln_final):
    bsz, s_len, _ = x_prompt.shape
    dbsz, dt_len, _ = x_sample.shape
    n_p, n_s = bsz * s_len, dbsz * dt_len
    wb = cache_attn_k.shape[2]
    wb_prompt = min(WINDOWS[-1], s_len)

    wr = jnp.pad(w_router[0], ((0, 0), (0, LANES - N_EXPERTS)))
    wr_hi = wr.astype(BF16)
    w = {
        "w_glu": w_glu[0].astype(BF16), "b_glu": b_glu[0].reshape(1, -1),
        "ln_ssm_out": ln_ssm_out[0].reshape(1, -1), "ln_attn_out": ln_attn_out[0].reshape(1, -1),
        "w_out": w_out[0].astype(BF16), "ln_moe": ln_moe[0].reshape(1, -1),
        "wr_hi": wr_hi,
        "wr_lo": jnp.concatenate([wr_hi, (wr - wr_hi.astype(F32)).astype(BF16)], axis=1),
        "b_router": jnp.pad(b_router[0], (0, LANES - N_EXPERTS)).reshape(1, -1),
        "expand": (jnp.arange(2 * LANES)[:, None] % LANES
                   == jnp.arange(ATTN_WIDTH)[None, :] // HEAD_DIM).astype(BF16),
        "ln_ple": ln_ple[0].reshape(1, -1), "w_ple_gate": w_ple_gate[0].astype(BF16),
        "b_ple_gate": b_ple_gate[0].reshape(1, -1), "w_ple_proj": w_ple_proj[0].astype(BF16),
        "ln_final": ln_final.reshape(1, -1),
    }
    w_in_b = w_in[0].astype(BF16)
    bmat, cmat, ab_re, ab_im = _s5_params(ssm_a_re[0], ssm_a_im[0], ssm_b_re[0], ssm_b_im[0],
                                          ssm_c_re[0], ssm_c_im[0], ssm_log_dt[0])

    def coeff(a, nb):
        return jnp.broadcast_to(a, (2, nb, STATE_HALF)).reshape(2 * nb, STATE_HALF)

    proj_p = _in_proj(x_prompt.reshape(n_p, D_MODEL), ln_mix[0], w_in_b, dils=DILATIONS[1:],
                      seq_window=(s_len, wb_prompt))
    u_p, k_p, v_p = proj_p[:3]
    qkv = [proj_p[3:6]] + [proj_p[6 + 3 * i:9 + 3 * i] for i in range(len(DILATIONS) - 1)]
    zeros_state = jnp.zeros((bsz, SSM_GROUPS, SSM_STATE), F32)
    y_p, ht_p = _s5(u_p.reshape(bsz, s_len, SSM_WIDTH), _state_to_rows(zeros_state, zeros_state),
                    bmat, cmat, coeff(ab_re, bsz), coeff(ab_im, bsz), ssm_d[0])
    branches = [_attn_prompt_branch(*qkv[i], bsz, s_len, d) for i, d in enumerate(DILATIONS)]
    x1_p, hrow_p, eidx_p, epos_p, egate_p, cnt_p = _mid(
        x_prompt.reshape(n_p, D_MODEL), y_p.reshape(n_p, SSM_WIDTH),
        [b[0] for b in branches], [b[1] for b in branches], DILATIONS, w)

    u_s, k_s, v_s, qb_s, kb_s, vb_s = _in_proj(x_sample.reshape(n_s, D_MODEL), ln_mix[0], w_in_b)
    y_s, ht_s = _s5(u_s.reshape(dbsz, dt_len, SSM_WIDTH), _state_to_rows(state_ssm_re[0], state_ssm_im[0]),
                    bmat, cmat, coeff(ab_re, dbsz), coeff(ab_im, dbsz), ssm_d[0])
    as3 = lambda t: t.reshape(dbsz, dt_len, ATTN_WIDTH)
    attn_s = _attn_sample(as3(qb_s), as3(kb_s), as3(vb_s), cache_attn_k[0], cache_attn_v[0])
    x1_s, hrow_s, eidx_s, epos_s, egate_s, cnt_s = _mid(
        x_sample.reshape(n_s, D_MODEL), y_s.reshape(n_s, SSM_WIDTH),
        [attn_s.reshape(n_s, ATTN_WIDTH)], [], (), w)

    n_rows = (n_p + n_s) * TOP_K
    dest_p, dest_s, vt, ve, vok, gs = _routing(eidx_p, epos_p, cnt_p, eidx_s, epos_s, cnt_s, n_rows)
    xs = _dispatch(jnp.concatenate([dest_p.reshape(-1), dest_s.reshape(-1)]), hrow_p, hrow_s)
    y_rows = _experts(xs, vt, ve, vok, gs, w_up[0], b_up[0], w_down[0], b_down[0])

    out_p = _combine(dest_p, y_rows, x1_p, egate_p, p_prompt[0].reshape(n_p, PLE_DIM), w)
    out_s = _combine(dest_s, y_rows, x1_s, egate_s, p_sample[0].reshape(n_s, PLE_DIM), w)

    hr_p, hi_p = _rows_to_state(ht_p, bsz)
    hr_s, hi_s = _rows_to_state(ht_s, dbsz)
    kv_p = lambda t: jnp.transpose(t.reshape(bsz, N_HEADS, HEAD_DIM, wb_prompt), (0, 3, 1, 2))[None]
    kv_s = lambda t: t.reshape(dbsz, dt_len, N_HEADS, HEAD_DIM)[None]
    return (out_p.reshape(bsz, s_len, D_MODEL), out_s.reshape(dbsz, dt_len, D_MODEL),
            kv_p(k_p), kv_p(v_p), hr_p[None], hi_p[None],
            kv_s(k_s), kv_s(v_s), hr_s[None], hi_s[None])
```

```python
import functools
import math

import jax
import jax.numpy as jnp
from jax import lax
from jax.experimental import pallas as pl
from jax.experimental.pallas import tpu as pltpu

F32 = jnp.float32
BF16 = jnp.bfloat16
I32 = jnp.int32

D_MODEL = 1024
SSM_WIDTH = 512
SSM_GROUP = 16
SSM_GROUPS = 32
SSM_STATE = 64
ATTN_WIDTH = 512
HEAD_DIM = 64
N_HEADS = 8
IN_WIDTH = SSM_WIDTH + 3 * ATTN_WIDTH
DILATIONS = (1, 4, 16)
WINDOWS = (128, 512, 2048)
ATTN_BLOCK = 128
ATTN_STEP_BLOCKS = 4
N_EXPERTS = 32
TOP_K = 4
EXPERT_FF = D_MODEL
SWIGLU_LIMIT = 7.0
SWIGLU_ALPHA = 1.702
PLE_DIM = 256
EPS = 1e-6
MASK_VALUE = -1e30
NEG_BIG = -3.0e38

LANES = 128
SUBLANES = 8
ROW_TILES = D_MODEL // LANES
TOKEN_TILE = 256
IN_TILE = 1024
IN_ROW_CHUNK = 512
MID_TILE = 512
MOE_TILE = 1024
MOE_SUBTILE = 512
SSM_HALF = SSM_WIDTH // 2
STATE_HALF = SSM_GROUPS * SSM_STATE // 2
ALIBI_SLOPES = tuple(2.0 ** (-8.0 * (h + 1) / N_HEADS) for h in range(N_HEADS))
VMEM_LIMIT = 56 * 1024 * 1024


def _rms(x, g):
    return x * lax.rsqrt(jnp.mean(x * x, axis=-1, keepdims=True) + EPS) * g


def _sigmoid(x):
    return 1.0 / (1.0 + jnp.exp(-x))


def _full(shape):
    n = len(shape)
    return pl.BlockSpec(shape, lambda *_: (0,) * n)


def _in_kernel(x_ref, g_ref, w_ref, u_ref, k_ref, v_ref, qb_ref, kb_ref, vb_ref, *rest, dils, window_tiles):
    tm = x_ref.shape[0]
    rc = min(tm, IN_ROW_CHUNK)
    tiles = ATTN_WIDTH // LANES
    for c in range(tm // rc):
        rows = slice(c * rc, (c + 1) * rc)
        h = _rms(x_ref[rows, :], g_ref[...]).astype(BF16)
        p = jnp.dot(h, w_ref[...], preferred_element_type=F32)
        u_ref[rows, :] = p[:, :SSM_WIDTH]
        q = p[:, SSM_WIDTH:SSM_WIDTH + ATTN_WIDTH] * (HEAD_DIM ** -0.5)
        k = p[:, SSM_WIDTH + ATTN_WIDTH:SSM_WIDTH + 2 * ATTN_WIDTH]
        v = p[:, SSM_WIDTH + 2 * ATTN_WIDTH:]
        if window_tiles is None:
            k_ref[rows, :] = k
            v_ref[rows, :] = v
        else:
            k_ref[:, rows] = k.T
            v_ref[:, rows] = v.T
        qb_ref[rows, :] = q.astype(BF16)
        kb_ref[rows, :] = k.astype(BF16)
        vb_ref[rows, :] = v.astype(BF16)
        if not dils:
            continue
        scr = rest[-1]
        for a, val in enumerate((q, k, v)):
            for ct in range(tiles):
                scr[c, a * tiles + ct] = val[:, ct * LANES:(ct + 1) * LANES]
        for di, d in enumerate(dils):
            out_rows = slice(c * rc // d, (c + 1) * rc // d)
            for a in range(3):
                out = rest[di * 3 + a]
                for r in range(d):
                    for ct in range(tiles):
                        piece = scr[c, a * tiles + ct, pl.ds(r, rc // d, stride=d), :]
                        c0 = r * ATTN_WIDTH + ct * LANES
                        out[out_rows, c0:c0 + LANES] = piece.astype(BF16)


def _in_proj(x2d, ln_mix, w_in_bf16, dils=(), seq_window=None):
    n = x2d.shape[0]
    tm = min(n, IN_TILE)
    row = lambda w: pl.BlockSpec((tm, w), lambda i: (i, 0))
    window_tiles = None
    kv_spec, kv_shape = row(ATTN_WIDTH), jax.ShapeDtypeStruct((n, ATTN_WIDTH), F32)
    if seq_window is not None:
        s_len, window = seq_window
        seq_tiles, first = s_len // tm, (s_len - window) // tm
        window_tiles = (seq_tiles, first)
        kv_spec = pl.BlockSpec(
            (None, ATTN_WIDTH, tm),
            lambda i: (i // seq_tiles, 0, jnp.maximum(i % seq_tiles - first, 0)))
        kv_shape = jax.ShapeDtypeStruct((n // s_len, ATTN_WIDTH, window), F32)
    out_specs = [row(SSM_WIDTH), kv_spec, kv_spec] + [row(ATTN_WIDTH)] * 3
    out_shape = ([jax.ShapeDtypeStruct((n, SSM_WIDTH), F32), kv_shape, kv_shape]
                 + [jax.ShapeDtypeStruct((n, ATTN_WIDTH), BF16)] * 3)
    for d in dils:
        out_specs += [pl.BlockSpec((tm // d, d * ATTN_WIDTH), lambda i: (i, 0))] * 3
        out_shape += [jax.ShapeDtypeStruct((n // d, d * ATTN_WIDTH), BF16)] * 3
    rc = min(tm, IN_ROW_CHUNK)
    scratch = [pltpu.VMEM((tm // rc, 3 * ATTN_WIDTH // LANES, rc, LANES), F32)] if dils else []
    return pl.pallas_call(
        functools.partial(_in_kernel, dils=tuple(dils), window_tiles=window_tiles),
        grid=(n // tm,),
        in_specs=[row(D_MODEL), _full((1, D_MODEL)), _full((D_MODEL, IN_WIDTH))],
        out_specs=out_specs,
        out_shape=out_shape,
        scratch_shapes=scratch,
        compiler_params=pltpu.CompilerParams(dimension_semantics=("arbitrary",),
                                             vmem_limit_bytes=VMEM_LIMIT),
        name="in_proj",
    )(x2d, ln_mix.reshape(1, D_MODEL), w_in_bf16)


def _s5_kernel(u_ref, bmat_ref, cmat_ref, are_ref, aim_ref, h0_ref, d_ref,
               y_ref, ht_ref, buf, hc, tmp, *, nb, tt, batched):
    rows = 2 * nb
    ntile = 2 * STATE_HALF // LANES
    half_tiles = ntile // 2

    def lane_tile(c):
        return slice(c * LANES, (c + 1) * LANES)

    @pl.when(pl.program_id(0) == 0)
    def _():
        hc[...] = h0_ref[...]

    if batched:
        u_all = u_ref[...].reshape(nb * tt, SSM_WIDTH)
        ub_all = u_all.astype(BF16)
        for hf in range(2):
            bu = jnp.dot(ub_all[:, hf * SSM_HALF:(hf + 1) * SSM_HALF], bmat_ref[hf],
                         preferred_element_type=F32)
            for c in range(ntile):
                tmp[c] = bu[:, lane_tile(c)]
            for c in range(ntile):
                for t in range(tt):
                    buf[c, t * rows + hf * nb:t * rows + (hf + 1) * nb, :] = tmp[c, pl.ds(t, nb, stride=tt), :]
    else:
        ub_all = u_ref[...].reshape(nb * tt, SSM_WIDTH).astype(BF16)
        for hf in range(2):
            bu = jnp.dot(ub_all[:, hf * SSM_HALF:(hf + 1) * SSM_HALF], bmat_ref[hf],
                         preferred_element_type=F32)
            for b in range(nb):
                for c in range(ntile):
                    buf[c, pl.ds(hf * nb + b, tt, stride=rows), :] = bu[b * tt:(b + 1) * tt, lane_tile(c)]

    group = 8
    for s in range(rows // SUBLANES):
        r0 = s * SUBLANES
        for c0 in range(0, half_tiles, group):
            ar = [are_ref[r0:r0 + SUBLANES, lane_tile(c0 + k)] for k in range(group)]
            ai = [aim_ref[r0:r0 + SUBLANES, lane_tile(c0 + k)] for k in range(group)]
            init = tuple(hc[r0:r0 + SUBLANES, lane_tile(c0 + k)] for k in range(group)) + tuple(
                hc[r0:r0 + SUBLANES, lane_tile(half_tiles + c0 + k)] for k in range(group))

            def step(t, carry, r0=r0, c0=c0, ar=ar, ai=ai):
                row = pl.multiple_of(t * rows + r0, SUBLANES)
                out_r, out_i = [], []
                for k in range(group):
                    hr, hi = carry[k], carry[group + k]
                    xr = buf[c0 + k, pl.ds(row, SUBLANES), :]
                    xi = buf[half_tiles + c0 + k, pl.ds(row, SUBLANES), :]
                    nr = ar[k] * hr - ai[k] * hi + xr
                    ni = ar[k] * hi + ai[k] * hr + xi
                    buf[c0 + k, pl.ds(row, SUBLANES), :] = nr
                    buf[half_tiles + c0 + k, pl.ds(row, SUBLANES), :] = ni
                    out_r.append(nr)
                    out_i.append(ni)
                return tuple(out_r) + tuple(out_i)

            fin = lax.fori_loop(0, tt, step, init, unroll=min(tt, 8))
            for k in range(group):
                hc[r0:r0 + SUBLANES, lane_tile(c0 + k)] = fin[k]
                hc[r0:r0 + SUBLANES, lane_tile(half_tiles + c0 + k)] = fin[group + k]

    if batched:
        parts = []
        for hf in range(2):
            for c in range(ntile):
                for t in range(tt):
                    tmp[c, pl.ds(t, nb, stride=tt), :] = buf[c, t * rows + hf * nb:t * rows + (hf + 1) * nb, :]
            hs = jnp.concatenate([tmp[c] for c in range(ntile)], axis=1).astype(BF16)
            parts.append(jnp.dot(hs, cmat_ref[hf], preferred_element_type=F32))
        y_all = jnp.concatenate(parts, axis=1) + d_ref[...] * u_all
        y_ref[...] = y_all.reshape(nb, tt, SSM_WIDTH)
    else:
        parts = []
        for hf in range(2):
            hs = jnp.concatenate(
                [jnp.concatenate([buf[c, pl.ds(hf * nb + b, tt, stride=rows), :] for c in range(ntile)],
                                 axis=1).astype(BF16) for b in range(nb)], axis=0)
            parts.append(jnp.dot(hs, cmat_ref[hf], preferred_element_type=F32))
        y_all = jnp.concatenate(parts, axis=1) + d_ref[...] * u_ref[...].reshape(nb * tt, SSM_WIDTH)
        y_ref[...] = y_all.reshape(nb, tt, SSM_WIDTH)

    ht_ref[...] = hc[...]


def _s5(u3, h0, bmat, cmat, a_re, a_im, d_skip):
    nb, t_len, _ = u3.shape
    tt = min(t_len, 256)
    rows = 2 * nb
    batched = tt < 16
    kern = functools.partial(_s5_kernel, nb=nb, tt=tt, batched=batched)
    ntile = 2 * STATE_HALF // LANES
    tmp_shape = (ntile, nb * tt, LANES) if batched else (1, SUBLANES, LANES)
    return pl.pallas_call(
        kern,
        grid=(t_len // tt,),
        in_specs=[pl.BlockSpec((nb, tt, SSM_WIDTH), lambda i: (0, i, 0)),
                  _full((2, SSM_HALF, 2 * STATE_HALF)),
                  _full((2, 2 * STATE_HALF, SSM_HALF)),
                  _full((rows, STATE_HALF)), _full((rows, STATE_HALF)),
                  _full((rows, 2 * STATE_HALF)), _full((1, SSM_WIDTH))],
        out_specs=[pl.BlockSpec((nb, tt, SSM_WIDTH), lambda i: (0, i, 0)),
                   _full((rows, 2 * STATE_HALF))],
        out_shape=[jax.ShapeDtypeStruct((nb, t_len, SSM_WIDTH), F32),
                   jax.ShapeDtypeStruct((rows, 2 * STATE_HALF), F32)],
        scratch_shapes=[pltpu.VMEM((ntile, tt * rows, LANES), F32),
                        pltpu.VMEM((rows, 2 * STATE_HALF), F32),
                        pltpu.VMEM(tmp_shape, F32)],
        compiler_params=pltpu.CompilerParams(dimension_semantics=("arbitrary",),
                                             vmem_limit_bytes=VMEM_LIMIT),
        name="s5_scan",
    )(u3, bmat, cmat, a_re, a_im, h0, d_skip.reshape(1, SSM_WIDTH))


def _s5_params(a_re, a_im, b_re, b_im, c_re, c_im, log_dt):
    dt = jnp.exp(log_dt)[:, None]
    mag = jnp.exp(dt * a_re)
    ang = dt * a_im
    ab_re, ab_im = mag * jnp.cos(ang), mag * jnp.sin(ang)
    den = a_re * a_re + a_im * a_im
    nr, ni = ab_re - 1.0, ab_im
    f_re = (nr * a_re + ni * a_im) / den
    f_im = (ni * a_re - nr * a_im) / den
    bb_re = f_re[..., None] * b_re - f_im[..., None] * b_im
    bb_im = f_re[..., None] * b_im + f_im[..., None] * b_re
    gh = SSM_GROUPS // 2
    eye = jnp.eye(gh, dtype=F32)

    def b_half(w):
        return jnp.einsum('gnc,gh->gchn', w, eye).reshape(gh * SSM_GROUP, gh * SSM_STATE)

    def c_half(w):
        return jnp.einsum('gcn,gh->gnhc', w, eye).reshape(gh * SSM_STATE, gh * SSM_GROUP)

    bmat = jnp.stack([jnp.concatenate([b_half(bb_re[h * gh:(h + 1) * gh]),
                                       b_half(bb_im[h * gh:(h + 1) * gh])], axis=1)
                      for h in range(2)]).astype(BF16)
    cmat = jnp.stack([jnp.concatenate([c_half(c_re[h * gh:(h + 1) * gh]),
                                       -c_half(c_im[h * gh:(h + 1) * gh])], axis=0)
                      for h in range(2)]).astype(BF16)
    return bmat, cmat, ab_re.reshape(2, 1, STATE_HALF), ab_im.reshape(2, 1, STATE_HALF)


def _state_to_rows(h_re, h_im):
    nb = h_re.shape[0]
    f = lambda h: h.reshape(nb, 2, STATE_HALF).transpose(1, 0, 2).reshape(2 * nb, STATE_HALF)
    return jnp.concatenate([f(h_re), f(h_im)], axis=1)


def _rows_to_state(ht, nb):
    f = lambda h: h.reshape(2, nb, STATE_HALF).transpose(1, 0, 2).reshape(nb, SSM_GROUPS, SSM_STATE)
    return f(ht[:, :STATE_HALF]), f(ht[:, STATE_HALF:])


def _attn_prompt_kernel(q_ref, kp_ref, kc_ref, vp_ref, vc_ref, o_ref, l_ref, bias_s, s_s, p_s, *, dil):
    blk = ATTN_BLOCK
    n = pl.program_id(2)

    @pl.when((pl.program_id(0) == 0) & (pl.program_id(1) == 0) & (n == 0))
    def _():
        i_idx = lax.broadcasted_iota(I32, (blk, 2 * blk), 0)
        j_idx = lax.broadcasted_iota(I32, (blk, 2 * blk), 1)
        delta = i_idx - j_idx + blk
        in_band = (delta >= 0) & (delta <= blk)
        dist = (delta * dil).astype(F32)
        for h in range(N_HEADS):
            biased = -ALIBI_SLOPES[h] * dist
            bias_s[0, h] = jnp.where(in_band & (j_idx >= blk), biased, MASK_VALUE)
            bias_s[1, h] = jnp.where(in_band, biased, MASK_VALUE)

    lane = lax.broadcasted_iota(I32, (blk, LANES), 1)
    nqb = q_ref.shape[0] // blk
    for qb in range(nqb):
        which = jnp.minimum(n, 1) if qb == 0 else 1
        rows = slice(qb * blk, (qb + 1) * blk)
        for hp in range(N_HEADS // 2):
            cols = slice(hp * LANES, (hp + 1) * LANES)
            q2 = q_ref[rows, cols]
            before = kp_ref[:, cols] if qb == 0 else kc_ref[(qb - 1) * blk:qb * blk, cols]
            kk = jnp.concatenate([before, kc_ref[rows, cols]], axis=0)
            for half in range(2):
                h = 2 * hp + half
                in_head = (lane >= half * HEAD_DIM) & (lane < (half + 1) * HEAD_DIM)
                qm = jnp.where(in_head, q2, jnp.zeros_like(q2))
                s = lax.dot_general(qm, kk, (((1,), (1,)), ((), ())), preferred_element_type=F32)
                s_s[qb * N_HEADS + h] = s + bias_s[which, h]
    for qb in range(nqb):
        lse_all = jnp.zeros((blk, LANES), F32)
        for h in range(N_HEADS):
            s = s_s[qb * N_HEADS + h]
            m = jnp.max(s, axis=1, keepdims=True)
            p = jnp.exp(s - m)
            l = jnp.sum(p, axis=1, keepdims=True)
            p_s[qb * N_HEADS + h] = (p * (1.0 / l)).astype(BF16)
            lse_all = jnp.where(lane == h, m + jnp.log(l), lse_all)
        l_ref[qb * blk:(qb + 1) * blk, :] = lse_all
    for qb in range(nqb):
        rows = slice(qb * blk, (qb + 1) * blk)
        for hp in range(N_HEADS // 2):
            cols = slice(hp * LANES, (hp + 1) * LANES)
            before = vp_ref[:, cols] if qb == 0 else vc_ref[(qb - 1) * blk:qb * blk, cols]
            vv = jnp.concatenate([before, vc_ref[rows, cols]], axis=0)
            outs = [jnp.dot(p_s[qb * N_HEADS + 2 * hp + half], vv, preferred_element_type=F32)
                    for half in range(2)]
            o_ref[rows, cols] = jnp.where(lane < HEAD_DIM, outs[0], outs[1])


def _attn_prompt_branch(qb, kb, vb, bsz, s_len, dil):
    sub = s_len // dil
    nqb = min(ATTN_STEP_BLOCKS, sub // ATTN_BLOCK)
    step = nqb * ATTN_BLOCK
    nstep = sub // step
    view = lambda t: t.reshape(bsz, sub, dil * ATTN_WIDTH)
    cur = pl.BlockSpec((None, step, ATTN_WIDTH), lambda b, r, n: (b, n, r))
    prev = pl.BlockSpec((None, ATTN_BLOCK, ATTN_WIDTH),
                        lambda b, r, n: (b, jnp.maximum(n * nqb - 1, 0), r))
    o, lse = pl.pallas_call(
        functools.partial(_attn_prompt_kernel, dil=dil),
        grid=(bsz, dil, nstep),
        in_specs=[cur, prev, cur, prev, cur],
        out_specs=[cur, pl.BlockSpec((None, step, LANES), lambda b, r, n: (b, n, r))],
        out_shape=[jax.ShapeDtypeStruct((bsz, sub, dil * ATTN_WIDTH), F32),
                   jax.ShapeDtypeStruct((bsz, sub, dil * LANES), F32)],
        scratch_shapes=[pltpu.VMEM((2, N_HEADS, ATTN_BLOCK, 2 * ATTN_BLOCK), F32),
                        pltpu.VMEM((nqb * N_HEADS, ATTN_BLOCK, 2 * ATTN_BLOCK), F32),
                        pltpu.VMEM((nqb * N_HEADS, ATTN_BLOCK, 2 * ATTN_BLOCK), BF16)],
        compiler_params=pltpu.CompilerParams(
            dimension_semantics=("arbitrary", "arbitrary", "arbitrary"),
            vmem_limit_bytes=VMEM_LIMIT),
        name=f"attn_prompt_d{dil}",
    )(view(qb), view(kb), view(kb), view(vb), view(vb))
    return o.reshape(bsz * sub, dil * ATTN_WIDTH), lse.reshape(bsz * sub, dil * LANES)


def _attn_sample_kernel(q_ref, kn_ref, vn_ref, kc_ref, vc_ref, o_ref, bias_s, mult_s, *, t_len, wb):
    nrow = N_HEADS * t_len
    t_shift = t_len.bit_length() - 1
    d_shift = HEAD_DIM.bit_length() - 1
    nt = (((1,), (1,)), ((), ()))

    def branch_count(dist):
        mult = jnp.zeros(dist.shape, F32)
        for win, dil in zip(WINDOWS, DILATIONS):
            hit = (dist >= 0) & (dist <= win) & ((dist & (dil - 1)) == 0)
            mult = mult + jnp.where(hit, 1.0, 0.0)
        return mult

    def biased(dist, mult):
        head = lax.broadcasted_iota(I32, dist.shape, 0) >> t_shift
        slope = jnp.zeros(dist.shape, F32)
        for h in range(N_HEADS):
            slope = jnp.where(head == h, ALIBI_SLOPES[h], slope)
        return jnp.where(mult > 0.0, -slope * dist.astype(F32), MASK_VALUE)

    @pl.when(pl.program_id(0) == 0)
    def _():
        row = lax.broadcasted_iota(I32, (nrow, wb), 0)
        col = lax.broadcasted_iota(I32, (nrow, wb), 1)
        dist = wb + (row & (t_len - 1)) - col
        mult = branch_count(dist)
        mult_s[...] = mult
        bias_s[...] = biased(dist, mult)

    q = q_ref[...].astype(F32)
    qt = jnp.concatenate([q] * N_HEADS, axis=0)
    row_w = lax.broadcasted_iota(I32, (nrow, ATTN_WIDTH), 0)
    lane_w = lax.broadcasted_iota(I32, (nrow, ATTN_WIDTH), 1)
    qm = jnp.where((lane_w >> d_shift) == (row_w >> t_shift), qt, 0.0).astype(BF16)

    pad = LANES - t_len
    kn = jnp.concatenate([kn_ref[...].astype(F32), jnp.zeros((pad, ATTN_WIDTH), F32)], axis=0).astype(BF16)
    vn = jnp.concatenate([vn_ref[...].astype(F32), jnp.zeros((pad, ATTN_WIDTH), F32)], axis=0).astype(BF16)
    row_n = lax.broadcasted_iota(I32, (nrow, LANES), 0)
    col_n = lax.broadcasted_iota(I32, (nrow, LANES), 1)
    dist_n = jnp.where(col_n < t_len, (row_n & (t_len - 1)) - col_n, -1)
    mult_n = branch_count(dist_n)
    s_n = lax.dot_general(qm, kn, nt, preferred_element_type=F32) + biased(dist_n, mult_n)
    s_c = jnp.dot(qm, kc_ref[...].astype(BF16), preferred_element_type=F32) + bias_s[...]

    m = jnp.maximum(jnp.max(s_c, axis=1, keepdims=True), jnp.max(s_n, axis=1, keepdims=True))
    p_c = jnp.exp(s_c - m) * mult_s[...]
    p_n = jnp.exp(s_n - m) * mult_n
    l = jnp.sum(p_c, axis=1, keepdims=True) + jnp.sum(p_n, axis=1, keepdims=True)
    o = (lax.dot_general(p_c.astype(BF16), vc_ref[...].astype(BF16), nt, preferred_element_type=F32)
         + jnp.dot(p_n.astype(BF16), vn, preferred_element_type=F32)) / l
    lane_o = lax.broadcasted_iota(I32, (t_len, ATTN_WIDTH), 1) >> d_shift
    out = jnp.zeros((t_len, ATTN_WIDTH), F32)
    for h in range(N_HEADS):
        out = jnp.where(lane_o == h, o[h * t_len:(h + 1) * t_len], out)
    o_ref[...] = out


def _attn_sample(qb, kb, vb, cache_k, cache_v):
    bsz, t_len, _ = qb.shape
    wb = cache_k.shape[1]
    feature_major = lambda c: jnp.transpose(c, (0, 2, 3, 1)).reshape(bsz, ATTN_WIDTH, wb)
    new = pl.BlockSpec((None, t_len, ATTN_WIDTH), lambda b: (b, 0, 0))
    old = pl.BlockSpec((None, ATTN_WIDTH, wb), lambda b: (b, 0, 0))
    return pl.pallas_call(
        functools.partial(_attn_sample_kernel, t_len=t_len, wb=wb),
        grid=(bsz,),
        in_specs=[new, new, new, old, old],
        out_specs=new,
        out_shape=jax.ShapeDtypeStruct((bsz, t_len, ATTN_WIDTH), F32),
        scratch_shapes=[pltpu.VMEM((N_HEADS * t_len, wb), F32),
                        pltpu.VMEM((N_HEADS * t_len, wb), F32)],
        compiler_params=pltpu.CompilerParams(dimension_semantics=("arbitrary",),
                                             vmem_limit_bytes=VMEM_LIMIT),
        name="attn_sample",
    )(qb, kb, vb, feature_major(cache_k), feature_major(cache_v))


def _mid_kernel(*refs, dils):
    n_branch = max(len(dils), 1)
    x_ref, y_ref = refs[0], refs[1]
    o_refs = refs[2:2 + n_branch]
    pos = 2 + n_branch
    l_refs = refs[pos:pos + len(dils)]
    pos += len(l_refs)
    (wglu_ref, bglu_ref, lns_ref, lna_ref, wout_ref, lnm_ref, wrh_ref, wrl_ref, br_ref,
     expand_ref, tri_ref,
     x1_ref, hrow_ref, eidx_ref, epos_ref, egate_ref, cnt_ref, carry, nat) = refs[pos:]
    tm = x_ref.shape[0]
    o_tiles = ATTN_WIDTH // LANES

    rb = tm

    for bi, d in enumerate(dils):
        if d == 1:
            continue
        for r in range(d):
            spread = pl.ds(r, tm // d, stride=d)
            for ct in range(o_tiles):
                c0 = r * ATTN_WIDTH + ct * LANES
                nat[bi, ct, spread, :] = o_refs[bi][:, c0:c0 + LANES]
            nat[bi, o_tiles, spread, :] = l_refs[bi][:, r * LANES:(r + 1) * LANES]

    def natural(bi, d, rows):
        if d == 1:
            return o_refs[bi][rows, :], l_refs[bi][rows, :]
        return (jnp.concatenate([nat[bi, ct, rows, :] for ct in range(o_tiles)], axis=1),
                nat[bi, o_tiles, rows, :])

    @pl.when(pl.program_id(0) == 0)
    def _():
        carry[...] = jnp.zeros_like(carry)

    lane = lax.broadcasted_iota(I32, (rb, LANES), 1)
    lane_f = lane.astype(F32)
    run = carry[0:1, :]
    for blk in range(tm // rb):
        rows = slice(blk * rb, (blk + 1) * rb)
        y = y_ref[rows, :]
        z = y * (0.5 * (1.0 + jnp.tanh(math.sqrt(2.0 / math.pi) * (y + 0.044715 * (y * y * y)))))
        glu = z * _sigmoid(jnp.dot(z.astype(BF16), wglu_ref[...], preferred_element_type=F32)
                           + bglu_ref[...])
        n_ssm = _rms(glu, lns_ref[...])

        if not dils:
            attn = o_refs[0][rows, :]
        else:
            pairs = [natural(bi, d, rows) for bi, d in enumerate(dils)]
            lses = [p[1] for p in pairs]
            mx = functools.reduce(jnp.maximum, lses)
            es = [jnp.exp(l - mx) for l in lses]
            inv = 1.0 / functools.reduce(lambda a, b: a + b, es)
            attn = jnp.zeros((rb, ATTN_WIDTH), F32)
            for e, (o_nat, _) in zip(es, pairs):
                w = e * inv
                w_hi = w.astype(BF16)
                w_lo = (w - w_hi.astype(F32)).astype(BF16)
                wide = jnp.dot(jnp.concatenate([w_hi, w_lo], axis=1), expand_ref[...],
                               preferred_element_type=F32)
                attn = attn + wide * o_nat
        n_attn = _rms(attn, lna_ref[...])

        x1 = (x_ref[rows, :]
              + jnp.dot(n_ssm.astype(BF16), wout_ref[:SSM_WIDTH, :], preferred_element_type=F32)
              + jnp.dot(n_attn.astype(BF16), wout_ref[SSM_WIDTH:, :], preferred_element_type=F32))
        x1_ref[rows, :] = x1
        hm = _rms(x1, lnm_ref[...])
        for s in range(ROW_TILES):
            hrow_ref[pl.ds(blk * rb * ROW_TILES + s, rb, stride=ROW_TILES), :] = hm[:, s * LANES:(s + 1) * LANES]

        h_hi = hm.astype(BF16)
        h_lo = (hm - h_hi.astype(F32)).astype(BF16)
        both = jnp.dot(h_hi, wrl_ref[...], preferred_element_type=F32)
        logits = (both[:, :LANES] + both[:, LANES:]
                  + jnp.dot(h_lo, wrh_ref[...], preferred_element_type=F32)
                  + br_ref[...])
        work = jnp.where(lane < N_EXPERTS, logits, NEG_BIG)
        vals, idxs, hots = [], [], []
        for _ in range(TOP_K):
            m = jnp.max(work, axis=1, keepdims=True)
            idx = jnp.min(jnp.where(work == m, lane_f, float(LANES)), axis=1, keepdims=True)
            hot = lane_f == idx
            vals.append(m)
            idxs.append(idx)
            hots.append(hot)
            work = jnp.where(hot, NEG_BIG, work)
        exps = [jnp.exp(v - vals[0]) for v in vals]
        inv = 1.0 / functools.reduce(lambda a, b: a + b, exps)

        sel = functools.reduce(lambda a, b: a + b, [h.astype(F32) for h in hots])
        before = jnp.dot(tri_ref[...], sel.astype(BF16), preferred_element_type=F32) + run
        eidx = jnp.zeros((rb, LANES), I32)
        epos = jnp.zeros((rb, LANES), I32)
        egate = jnp.zeros((rb, LANES), F32)
        for k in range(TOP_K):
            pk = jnp.sum(jnp.where(hots[k], before, 0.0), axis=1, keepdims=True)
            eidx = jnp.where(lane == k, idxs[k].astype(I32), eidx)
            epos = jnp.where(lane == k, pk.astype(I32), epos)
            egate = jnp.where(lane == k, exps[k] * inv, egate)
        eidx_ref[rows, :] = eidx
        epos_ref[rows, :] = epos
        egate_ref[rows, :] = egate
        run = run + jnp.sum(sel, axis=0, keepdims=True)
    carry[...] = jnp.broadcast_to(run, carry.shape)
    cnt_ref[...] = jnp.broadcast_to(run, cnt_ref.shape).astype(I32)


def _mid(x2d, y2d, attn_o, attn_lse, dils, w):
    n = x2d.shape[0]
    tm = min(n, MID_TILE)
    rb = tm
    ti = jnp.arange(rb)
    tri = (ti[:, None] > ti[None, :]).astype(BF16)
    n_branch = len(attn_o)
    row = lambda width: pl.BlockSpec((tm, width), lambda i: (i, 0))
    packed = lambda d, width: pl.BlockSpec((tm // d, d * width), lambda i: (i, 0))
    attn_specs = ([packed(d, ATTN_WIDTH) for d in dils] + [packed(d, LANES) for d in dils]
                  if dils else [row(ATTN_WIDTH)])
    in_specs = ([row(D_MODEL), row(SSM_WIDTH)] + attn_specs
                + [_full((SSM_WIDTH, SSM_WIDTH)), _full((1, SSM_WIDTH)), _full((1, SSM_WIDTH)),
                   _full((1, ATTN_WIDTH)), _full((D_MODEL, D_MODEL)), _full((1, D_MODEL)),
                   _full((D_MODEL, LANES)), _full((D_MODEL, 2 * LANES)), _full((1, LANES)),
                   _full((2 * LANES, ATTN_WIDTH)), _full((rb, rb))])
    out_specs = [row(D_MODEL), pl.BlockSpec((tm * ROW_TILES, LANES), lambda i: (i, 0)),
                 row(LANES), row(LANES), row(LANES), _full((SUBLANES, LANES))]
    out_shape = [jax.ShapeDtypeStruct((n, D_MODEL), F32),
                 jax.ShapeDtypeStruct((n * ROW_TILES, LANES), F32),
                 jax.ShapeDtypeStruct((n, LANES), I32),
                 jax.ShapeDtypeStruct((n, LANES), I32),
                 jax.ShapeDtypeStruct((n, LANES), F32),
                 jax.ShapeDtypeStruct((SUBLANES, LANES), I32)]
    return pl.pallas_call(
        functools.partial(_mid_kernel, dils=tuple(dils)),
        grid=(n // tm,),
        in_specs=in_specs,
        out_specs=out_specs,
        out_shape=out_shape,
        scratch_shapes=[pltpu.VMEM((SUBLANES, LANES), F32),
                        pltpu.VMEM((n_branch, ATTN_WIDTH // LANES + 1, tm, LANES), F32)],
        compiler_params=pltpu.CompilerParams(dimension_semantics=("arbitrary",),
                                             vmem_limit_bytes=VMEM_LIMIT),
        name="mid",
    )(x2d, y2d, *attn_o, *attn_lse, w["w_glu"], w["b_glu"], w["ln_ssm_out"], w["ln_attn_out"],
      w["w_out"], w["ln_moe"], w["wr_hi"], w["wr_lo"], w["b_router"], w["expand"], tri)


def _dispatch_kernel(dest_ref, hp_ref, hs_ref, xs_hbm, sem, *, n_prompt_tiles):
    i = pl.program_id(0)
    npair = TOKEN_TILE * TOP_K

    def run(src_ref):
        def issue(t, c):
            src = src_ref.at[pl.ds(pl.multiple_of(t * ROW_TILES, ROW_TILES), ROW_TILES), :]
            for k in range(TOP_K):
                d = dest_ref[0, t * TOP_K + k]
                pltpu.make_async_copy(
                    src, xs_hbm.at[pl.ds(pl.multiple_of(d * ROW_TILES, ROW_TILES), ROW_TILES), :],
                    sem).start()
            return c

        lax.fori_loop(0, TOKEN_TILE, issue, 0, unroll=2)
        span = pl.ds(0, npair * ROW_TILES)
        pltpu.make_async_copy(xs_hbm.at[span, :], xs_hbm.at[span, :], sem).wait()

    @pl.when(i < n_prompt_tiles)
    def _():
        run(hp_ref)

    @pl.when(i >= n_prompt_tiles)
    def _():
        run(hs_ref)


def _dispatch(dest, hrow_p, hrow_s):
    n_p = hrow_p.shape[0] // ROW_TILES
    n_s = hrow_s.shape[0] // ROW_TILES
    npt = n_p // TOKEN_TILE
    ntile = (n_p + n_s) // TOKEN_TILE
    npair = TOKEN_TILE * TOP_K
    blk = (TOKEN_TILE * ROW_TILES, LANES)
    return pl.pallas_call(
        functools.partial(_dispatch_kernel, n_prompt_tiles=npt),
        grid=(ntile,),
        in_specs=[pl.BlockSpec((None, 1, npair), lambda i: (i, 0, 0), memory_space=pltpu.SMEM),
                  pl.BlockSpec(blk, lambda i: (jnp.minimum(i, npt - 1), 0)),
                  pl.BlockSpec(blk, lambda i: (jnp.maximum(i - npt, 0), 0))],
        out_specs=pl.BlockSpec(memory_space=pl.ANY),
        out_shape=jax.ShapeDtypeStruct(((n_p + n_s) * TOP_K * ROW_TILES, LANES), F32),
        scratch_shapes=[pltpu.SemaphoreType.DMA(())],
        compiler_params=pltpu.CompilerParams(dimension_semantics=("arbitrary",),
                                             vmem_limit_bytes=VMEM_LIMIT),
        name="moe_dispatch",
    )(dest.reshape(ntile, 1, npair), hrow_p, hrow_s)


def _expert_kernel(vt_ref, ve_ref, vok_ref, gs_ref, xs_ref, wu_ref, bu_ref, wd_ref, bd_ref,
                   out_ref, wu_s, wd_s, x_s):
    v = pl.program_id(0)
    e = ve_ref[v]
    j = vt_ref[v]
    vprev = jnp.maximum(v - 1, 0)
    new_e = (v == 0) | (e != ve_ref[vprev])
    new_j = (v == 0) | (j != vt_ref[vprev])
    tm = MOE_TILE

    @pl.when(new_e)
    def _():
        wu_s[...] = wu_ref[...].astype(BF16)
        wd_s[...] = wd_ref[...].astype(BF16)

    @pl.when(new_j)
    def _():
        out_ref[...] = jnp.zeros_like(out_ref)

    lo, hi = gs_ref[e], gs_ref[e + 1]

    def run_pass(row0, nrows):
        base = row0 * ROW_TILES
        for s in range(ROW_TILES):
            x_s[:nrows, s * LANES:(s + 1) * LANES] = xs_ref[
                pl.ds(base + s, nrows, stride=ROW_TILES), :].astype(BF16)
        a = jnp.dot(x_s[:nrows, :], wu_s[...], preferred_element_type=F32) + bu_ref[...]
        g = jnp.minimum(a[:, :EXPERT_FF], SWIGLU_LIMIT)
        lin = jnp.clip(a[:, EXPERT_FF:], -SWIGLU_LIMIT, SWIGLU_LIMIT)
        act = (lin + 1.0) * (g * _sigmoid(SWIGLU_ALPHA * g))
        y = jnp.dot(act.astype(BF16), wd_s[...], preferred_element_type=F32) + bd_ref[...]
        rows = j * tm + row0 + lax.broadcasted_iota(I32, (nrows, 1), 0)
        mine = (rows >= lo) & (rows < hi)
        for s in range(ROW_TILES):
            cur = out_ref[pl.ds(base + s, nrows, stride=ROW_TILES), :]
            out_ref[pl.ds(base + s, nrows, stride=ROW_TILES), :] = jnp.where(
                mine, y[:, s * LANES:(s + 1) * LANES], cur)

    sub = x_s.shape[0]
    half = sub // 2
    live = vok_ref[v] == 1
    for part in range(tm // sub):
        row0 = part * sub
        start = j * tm + row0
        in_first = live & (lo < start + half) & (hi > start)
        in_second = live & (lo < start + sub) & (hi > start + half)
        pl.when(in_first & in_second)(functools.partial(run_pass, row0, sub))
        pl.when(in_first & jnp.logical_not(in_second))(functools.partial(run_pass, row0, half))
        pl.when(in_second & jnp.logical_not(in_first))(functools.partial(run_pass, row0 + half, half))


def _experts(xs, vt, ve, vok, gstart, w_up, b_up, w_down, b_down):
    tm = MOE_TILE
    nvisit = vt.shape[0]
    rows = pl.BlockSpec((tm * ROW_TILES, LANES), lambda v, vt, ve, vok, gs: (vt[v], 0))
    per_e = lambda a, b: pl.BlockSpec((None, a, b), lambda v, vt, ve, vok, gs: (ve[v], 0, 0))
    grid_spec = pltpu.PrefetchScalarGridSpec(
        num_scalar_prefetch=4,
        grid=(nvisit,),
        in_specs=[rows, per_e(D_MODEL, 2 * EXPERT_FF), per_e(1, 2 * EXPERT_FF),
                  per_e(EXPERT_FF, D_MODEL), per_e(1, D_MODEL)],
        out_specs=rows,
        scratch_shapes=[pltpu.VMEM((D_MODEL, 2 * EXPERT_FF), BF16),
                        pltpu.VMEM((EXPERT_FF, D_MODEL), BF16),
                        pltpu.VMEM((MOE_SUBTILE, D_MODEL), BF16)],
    )
    return pl.pallas_call(
        _expert_kernel,
        grid_spec=grid_spec,
        out_shape=jax.ShapeDtypeStruct(xs.shape, F32),
        compiler_params=pltpu.CompilerParams(dimension_semantics=("arbitrary",),
                                             vmem_limit_bytes=VMEM_LIMIT),
        name="moe_experts",
    )(vt, ve, vok, gstart, xs, w_up, b_up.reshape(N_EXPERTS, 1, 2 * EXPERT_FF),
      w_down, b_down.reshape(N_EXPERTS, 1, D_MODEL))


def _routing(eidx_p, epos_p, cnt_p, eidx_s, epos_s, cnt_s, n_rows):
    cnt_p = cnt_p[0, :N_EXPERTS]
    cnt_s = cnt_s[0, :N_EXPERTS]
    cnt = cnt_p + cnt_s
    gend = jnp.cumsum(cnt)
    gstart = gend - cnt
    experts = jnp.arange(N_EXPERTS, dtype=I32)

    def lookup(table, idx):
        return jnp.sum(jnp.where(idx[..., None] == experts, table, 0), axis=-1)

    ep = eidx_p[:, :TOP_K]
    es = eidx_s[:, :TOP_K]
    dest_p = lookup(gstart, ep) + epos_p[:, :TOP_K]
    dest_s = lookup(gstart + cnt_p, es) + epos_s[:, :TOP_K]
    ntile = n_rows // MOE_TILE
    nvisit = ntile + N_EXPERTS
    first = gstart // MOE_TILE
    last = jnp.maximum(gend - 1, 0) // MOE_TILE
    nv = jnp.where(cnt > 0, last - first + 1, 0)
    vend = jnp.cumsum(nv)
    vstart = vend - nv
    total = vend[-1]
    v = jnp.arange(nvisit, dtype=I32)
    vc = jnp.minimum(v, total - 1)
    ve = jnp.sum((vend[None, :] <= vc[:, None]).astype(I32), axis=1)
    vt = (lookup(first - vstart, ve) + vc).astype(I32)
    vok = (v < total).astype(I32)
    gs = jnp.concatenate([gstart, gend[-1:]]).astype(I32)
    return dest_p.astype(I32), dest_s.astype(I32), vt, ve, vok, gs


def _out_kernel(dest_ref, next_ref, y_hbm, x1_ref, gate_ref, pe_ref, lnp_ref, wg_ref, bg_ref, wp_ref,
                lnf_ref, o_ref, buf, sem, *, ntile):
    tm = TOKEN_TILE
    npair = tm * TOP_K
    i = pl.program_id(0)
    slot = i % 2

    def gather(idx_ref, to):
        def issue(t, c):
            for k in range(TOP_K):
                d = idx_ref[0, t * TOP_K + k]
                pltpu.make_async_copy(
                    y_hbm.at[pl.ds(pl.multiple_of(d * ROW_TILES, ROW_TILES), ROW_TILES), :],
                    buf.at[to, pl.ds(pl.multiple_of((k * tm + t) * ROW_TILES, ROW_TILES), ROW_TILES), :],
                    sem.at[to]).start()
            return c

        lax.fori_loop(0, tm, issue, 0, unroll=2)

    @pl.when(i == 0)
    def _():
        gather(dest_ref, 0)

    if ntile > 1:
        @pl.when(i + 1 < ntile)
        def _():
            gather(next_ref, 1 - slot)

    pltpu.make_async_copy(y_hbm.at[pl.ds(0, npair * ROW_TILES), :], buf.at[slot], sem.at[slot]).wait()

    gates = gate_ref[...]
    parts = []
    for s in range(ROW_TILES):
        acc = jnp.zeros((tm, LANES), F32)
        for k in range(TOP_K):
            rows = buf[slot, pl.ds(k * tm * ROW_TILES + s, tm, stride=ROW_TILES), :]
            acc = acc + gates[:, k:k + 1] * rows
        parts.append(acc)
    x2 = x1_ref[...] + jnp.concatenate(parts, axis=1)
    gate = _sigmoid(jnp.dot(_rms(x2, lnp_ref[...]).astype(BF16), wg_ref[...],
                            preferred_element_type=F32) + bg_ref[...])
    x3 = x2 + gate * jnp.dot(pe_ref[...].astype(BF16), wp_ref[...], preferred_element_type=F32)
    o_ref[...] = _rms(x3, lnf_ref[...])


def _combine(dest, y_rows, x1, egate, pe, w):
    n = x1.shape[0]
    tm = TOKEN_TILE
    npair = tm * TOP_K
    row = lambda width: pl.BlockSpec((tm, width), lambda i: (i, 0))
    ntile = n // tm
    dest3 = dest.reshape(ntile, 1, npair)
    return pl.pallas_call(
        functools.partial(_out_kernel, ntile=ntile),
        grid=(ntile,),
        in_specs=[pl.BlockSpec((None, 1, npair), lambda i: (i, 0, 0), memory_space=pltpu.SMEM),
                  pl.BlockSpec((None, 1, npair), lambda i: (jnp.minimum(i + 1, ntile - 1), 0, 0),
                               memory_space=pltpu.SMEM),
                  pl.BlockSpec(memory_space=pl.ANY),
                  row(D_MODEL), row(LANES), row(PLE_DIM),
                  _full((1, D_MODEL)), _full((D_MODEL, D_MODEL)), _full((1, D_MODEL)),
                  _full((PLE_DIM, D_MODEL)), _full((1, D_MODEL))],
        out_specs=row(D_MODEL),
        out_shape=jax.ShapeDtypeStruct((n, D_MODEL), F32),
        scratch_shapes=[pltpu.VMEM((2, npair * ROW_TILES, LANES), F32), pltpu.SemaphoreType.DMA((2,))],
        compiler_params=pltpu.CompilerParams(dimension_semantics=("arbitrary",),
                                             vmem_limit_bytes=VMEM_LIMIT),
        name="combine_out",
    )(dest3, dest3, y_rows, x1, egate, pe, w["ln_ple"], w["w_ple_gate"],
      w["b_ple_gate"], w["w_ple_proj"], w["ln_final"])


def kernel(x_prompt, x_sample, cache_attn_k, cache_attn_v, state_ssm_re, state_ssm_im, p_prompt, p_sample, ln_mix, w_in, ssm_a_re, ssm_a_im, ssm_b_re, ssm_b_im, ssm_c_re, ssm_c_im, ssm_d, ssm_log_dt, w_glu, b_glu, ln_ssm_out, ln_attn_out, w_out, ln_moe, w_router, b_router, w_up, b_up, w_down, b_down, ln_ple, w_ple_gate, b_ple_gate, w_ple_proj, ln_final):
    bsz, s_len, _ = x_prompt.shape
    dbsz, dt_len, _ = x_sample.shape
    n_p, n_s = bsz * s_len, dbsz * dt_len
    wb = cache_attn_k.shape[2]
    wb_prompt = min(WINDOWS[-1], s_len)

    wr = jnp.pad(w_router[0], ((0, 0), (0, LANES - N_EXPERTS)))
    wr_hi = wr.astype(BF16)
    w = {
        "w_glu": w_glu[0].astype(BF16), "b_glu": b_glu[0].reshape(1, -1),
        "ln_ssm_out": ln_ssm_out[0].reshape(1, -1), "ln_attn_out": ln_attn_out[0].reshape(1, -1),
        "w_out": w_out[0].astype(BF16), "ln_moe": ln_moe[0].reshape(1, -1),
        "wr_hi": wr_hi,
        "wr_lo": jnp.concatenate([wr_hi, (wr - wr_hi.astype(F32)).astype(BF16)], axis=1),
        "b_router": jnp.pad(b_router[0], (0, LANES - N_EXPERTS)).reshape(1, -1),
        "expand": (jnp.arange(2 * LANES)[:, None] % LANES
                   == jnp.arange(ATTN_WIDTH)[None, :] // HEAD_DIM).astype(BF16),
        "ln_ple": ln_ple[0].reshape(1, -1), "w_ple_gate": w_ple_gate[0].astype(BF16),
        "b_ple_gate": b_ple_gate[0].reshape(1, -1), "w_ple_proj": w_ple_proj[0].astype(BF16),
        "ln_final": ln_final.reshape(1, -1),
    }
    w_in_b = w_in[0].astype(BF16)
    bmat, cmat, ab_re, ab_im = _s5_params(ssm_a_re[0], ssm_a_im[0], ssm_b_re[0], ssm_b_im[0],
                                          ssm_c_re[0], ssm_c_im[0], ssm_log_dt[0])

    def coeff(a, nb):
        return jnp.broadcast_to(a, (2, nb, STATE_HALF)).reshape(2 * nb, STATE_HALF)

    proj_p = _in_proj(x_prompt.reshape(n_p, D_MODEL), ln_mix[0], w_in_b, dils=DILATIONS[1:],
                      seq_window=(s_len, wb_prompt))
    u_p, k_p, v_p = proj_p[:3]
    qkv = [proj_p[3:6]] + [proj_p[6 + 3 * i:9 + 3 * i] for i in range(len(DILATIONS) - 1)]
    zeros_state = jnp.zeros((bsz, SSM_GROUPS, SSM_STATE), F32)
    y_p, ht_p = _s5(u_p.reshape(bsz, s_len, SSM_WIDTH), _state_to_rows(zeros_state, zeros_state),
                    bmat, cmat, coeff(ab_re, bsz), coeff(ab_im, bsz), ssm_d[0])
    branches = [_attn_prompt_branch(*qkv[i], bsz, s_len, d) for i, d in enumerate(DILATIONS)]
    x1_p, hrow_p, eidx_p, epos_p, egate_p, cnt_p = _mid(
        x_prompt.reshape(n_p, D_MODEL), y_p.reshape(n_p, SSM_WIDTH),
        [b[0] for b in branches], [b[1] for b in branches], DILATIONS, w)

    u_s, k_s, v_s, qb_s, kb_s, vb_s = _in_proj(x_sample.reshape(n_s, D_MODEL), ln_mix[0], w_in_b)
    y_s, ht_s = _s5(u_s.reshape(dbsz, dt_len, SSM_WIDTH), _state_to_rows(state_ssm_re[0], state_ssm_im[0]),
                    bmat, cmat, coeff(ab_re, dbsz), coeff(ab_im, dbsz), ssm_d[0])
    as3 = lambda t: t.reshape(dbsz, dt_len, ATTN_WIDTH)
    attn_s = _attn_sample(as3(qb_s), as3(kb_s), as3(vb_s), cache_attn_k[0], cache_attn_v[0])
    x1_s, hrow_s, eidx_s, epos_s, egate_s, cnt_s = _mid(
        x_sample.reshape(n_s, D_MODEL), y_s.reshape(n_s, SSM_WIDTH),
        [attn_s.reshape(n_s, ATTN_WIDTH)], [], (), w)

    n_rows = (n_p + n_s) * TOP_K
    dest_p, dest_s, vt, ve, vok, gs = _routing(eidx_p, epos_p, cnt_p, eidx_s, epos_s, cnt_s, n_rows)
    xs = _dispatch(jnp.concatenate([dest_p.reshape(-1), dest_s.reshape(-1)]), hrow_p, hrow_s)
    y_rows = _experts(xs, vt, ve, vok, gs, w_up[0], b_up[0], w_down[0], b_down[0])

    out_p = _combine(dest_p, y_rows, x1_p, egate_p, p_prompt[0].reshape(n_p, PLE_DIM), w)
    out_s = _combine(dest_s, y_rows, x1_s, egate_s, p_sample[0].reshape(n_s, PLE_DIM), w)

    hr_p, hi_p = _rows_to_state(ht_p, bsz)
    hr_s, hi_s = _rows_to_state(ht_s, dbsz)
    kv_p = lambda t: jnp.transpose(t.reshape(bsz, N_HEADS, HEAD_DIM, wb_prompt), (0, 3, 1, 2))[None]
    kv_s = lambda t: t.reshape(dbsz, dt_len, N_HEADS, HEAD_DIM)[None]
    return (out_p.reshape(bsz, s_len, D_MODEL), out_s.reshape(dbsz, dt_len, D_MODEL),
            kv_p(k_p), kv_p(v_p), hr_p[None], hi_p[None],
            kv_s(k_s), kv_s(v_s), hr_s[None], hi_s[None])
```

```python
import functools
import math

import jax
import jax.numpy as jnp
from jax import lax
from jax.experimental import pallas as pl
from jax.experimental.pallas import tpu as pltpu

F32 = jnp.float32
BF16 = jnp.bfloat16
I32 = jnp.int32

D_MODEL = 1024
SSM_WIDTH = 512
SSM_GROUP = 16
SSM_GROUPS = 32
SSM_STATE = 64
ATTN_WIDTH = 512
HEAD_DIM = 64
N_HEADS = 8
IN_WIDTH = SSM_WIDTH + 3 * ATTN_WIDTH
DILATIONS = (1, 4, 16)
WINDOWS = (128, 512, 2048)
ATTN_BLOCK = 128
ATTN_STEP_BLOCKS = 4
N_EXPERTS = 32
TOP_K = 4
EXPERT_FF = D_MODEL
SWIGLU_LIMIT = 7.0
SWIGLU_ALPHA = 1.702
PLE_DIM = 256
EPS = 1e-6
MASK_VALUE = -1e30
NEG_BIG = -3.0e38

LANES = 128
SUBLANES = 8
ROW_TILES = D_MODEL // LANES
TOKEN_TILE = 256
IN_TILE = 1024
IN_ROW_CHUNK = 512
MID_TILE = 512
MOE_TILE = 1024
MOE_SUBTILE = 512
SSM_HALF = SSM_WIDTH // 2
STATE_HALF = SSM_GROUPS * SSM_STATE // 2
ALIBI_SLOPES = tuple(2.0 ** (-8.0 * (h + 1) / N_HEADS) for h in range(N_HEADS))
VMEM_LIMIT = 56 * 1024 * 1024


def _rms(x, g):
    return x * lax.rsqrt(jnp.mean(x * x, axis=-1, keepdims=True) + EPS) * g


def _sigmoid(x):
    return 1.0 / (1.0 + jnp.exp(-x))


def _full(shape):
    n = len(shape)
    return pl.BlockSpec(shape, lambda *_: (0,) * n)


def _in_kernel(x_ref, g_ref, w_ref, u_ref, k_ref, v_ref, qb_ref, kb_ref, vb_ref, *rest, dils, window_tiles):
    tm = x_ref.shape[0]
    rc = min(tm, IN_ROW_CHUNK)
    tiles = ATTN_WIDTH // LANES
    for c in range(tm // rc):
        rows = slice(c * rc, (c + 1) * rc)
        h = _rms(x_ref[rows, :], g_ref[...]).astype(BF16)
        p = jnp.dot(h, w_ref[...], preferred_element_type=F32)
        u_ref[rows, :] = p[:, :SSM_WIDTH]
        q = p[:, SSM_WIDTH:SSM_WIDTH + ATTN_WIDTH] * (HEAD_DIM ** -0.5)
        k = p[:, SSM_WIDTH + ATTN_WIDTH:SSM_WIDTH + 2 * ATTN_WIDTH]
        v = p[:, SSM_WIDTH + 2 * ATTN_WIDTH:]
        if window_tiles is None:
            k_ref[rows, :] = k
            v_ref[rows, :] = v
        else:
            k_ref[:, rows] = k.T
            v_ref[:, rows] = v.T
        qb_ref[rows, :] = q.astype(BF16)
        kb_ref[rows, :] = k.astype(BF16)
        vb_ref[rows, :] = v.astype(BF16)
        if not dils:
            continue
        scr = rest[-1]
        for a, val in enumerate((q, k, v)):
            for ct in range(tiles):
                scr[c, a * tiles + ct] = val[:, ct * LANES:(ct + 1) * LANES]
        for di, d in enumerate(dils):
            out_rows = slice(c * rc // d, (c + 1) * rc // d)
            for a in range(3):
                out = rest[di * 3 + a]
                for r in range(d):
                    for ct in range(tiles):
                        piece = scr[c, a * tiles + ct, pl.ds(r, rc // d, stride=d), :]
                        c0 = r * ATTN_WIDTH + ct * LANES
                        out[out_rows, c0:c0 + LANES] = piece.astype(BF16)


def _in_proj(x2d, ln_mix, w_in_bf16, dils=(), seq_window=None):
    n = x2d.shape[0]
    tm = min(n, IN_TILE)
    row = lambda w: pl.BlockSpec((tm, w), lambda i: (i, 0))
    window_tiles = None
    kv_spec, kv_shape = row(ATTN_WIDTH), jax.ShapeDtypeStruct((n, ATTN_WIDTH), F32)
    if seq_window is not None:
        s_len, window = seq_window
        seq_tiles, first = s_len // tm, (s_len - window) // tm
        window_tiles = (seq_tiles, first)
        kv_spec = pl.BlockSpec(
            (None, ATTN_WIDTH, tm),
            lambda i: (i // seq_tiles, 0, jnp.maximum(i % seq_tiles - first, 0)))
        kv_shape = jax.ShapeDtypeStruct((n // s_len, ATTN_WIDTH, window), F32)
    out_specs = [row(SSM_WIDTH), kv_spec, kv_spec] + [row(ATTN_WIDTH)] * 3
    out_shape = ([jax.ShapeDtypeStruct((n, SSM_WIDTH), F32), kv_shape, kv_shape]
                 + [jax.ShapeDtypeStruct((n, ATTN_WIDTH), BF16)] * 3)
    for d in dils:
        out_specs += [pl.BlockSpec((tm // d, d * ATTN_WIDTH), lambda i: (i, 0))] * 3
        out_shape += [jax.ShapeDtypeStruct((n // d, d * ATTN_WIDTH), BF16)] * 3
    rc = min(tm, IN_ROW_CHUNK)
    scratch = [pltpu.VMEM((tm // rc, 3 * ATTN_WIDTH // LANES, rc, LANES), F32)] if dils else []
    return pl.pallas_call(
        functools.partial(_in_kernel, dils=tuple(dils), window_tiles=window_tiles),
        grid=(n // tm,),
        in_specs=[row(D_MODEL), _full((1, D_MODEL)), _full((D_MODEL, IN_WIDTH))],
        out_specs=out_specs,
        out_shape=out_shape,
        scratch_shapes=scratch,
        compiler_params=pltpu.CompilerParams(dimension_semantics=("arbitrary",),
                                             vmem_limit_bytes=VMEM_LIMIT),
        name="in_proj",
    )(x2d, ln_mix.reshape(1, D_MODEL), w_in_bf16)


def _s5_kernel(u_ref, bmat_ref, cmat_ref, are_ref, aim_ref, h0_ref, d_ref,
               y_ref, ht_ref, buf, hc, tmp, *, nb, tt, batched):
    rows = 2 * nb
    ntile = 2 * STATE_HALF // LANES
    half_tiles = ntile // 2

    def lane_tile(c):
        return slice(c * LANES, (c + 1) * LANES)

    @pl.when(pl.program_id(0) == 0)
    def _():
        hc[...] = h0_ref[...]

    if batched:
        u_all = u_ref[...].reshape(nb * tt, SSM_WIDTH)
        ub_all = u_all.astype(BF16)
        for hf in range(2):
            bu = jnp.dot(ub_all[:, hf * SSM_HALF:(hf + 1) * SSM_HALF], bmat_ref[hf],
                         preferred_element_type=F32)
            for c in range(ntile):
                tmp[c] = bu[:, lane_tile(c)]
            for c in range(ntile):
                for t in range(tt):
                    buf[c, t * rows + hf * nb:t * rows + (hf + 1) * nb, :] = tmp[c, pl.ds(t, nb, stride=tt), :]
    else:
        ub_all = u_ref[...].reshape(nb * tt, SSM_WIDTH).astype(BF16)
        for hf in range(2):
            bu = jnp.dot(ub_all[:, hf * SSM_HALF:(hf + 1) * SSM_HALF], bmat_ref[hf],
                         preferred_element_type=F32)
            for b in range(nb):
                for c in range(ntile):
                    buf[c, pl.ds(hf * nb + b, tt, stride=rows), :] = bu[b * tt:(b + 1) * tt, lane_tile(c)]

    group = 8
    for s in range(rows // SUBLANES):
        r0 = s * SUBLANES
        for c0 in range(0, half_tiles, group):
            ar = [are_ref[r0:r0 + SUBLANES, lane_tile(c0 + k)] for k in range(group)]
            ai = [aim_ref[r0:r0 + SUBLANES, lane_tile(c0 + k)] for k in range(group)]
            init = tuple(hc[r0:r0 + SUBLANES, lane_tile(c0 + k)] for k in range(group)) + tuple(
                hc[r0:r0 + SUBLANES, lane_tile(half_tiles + c0 + k)] for k in range(group))

            def step(t, carry, r0=r0, c0=c0, ar=ar, ai=ai):
                row = pl.multiple_of(t * rows + r0, SUBLANES)
                out_r, out_i = [], []
                for k in range(group):
                    hr, hi = carry[k], carry[group + k]
                    xr = buf[c0 + k, pl.ds(row, SUBLANES), :]
                    xi = buf[half_tiles + c0 + k, pl.ds(row, SUBLANES), :]
                    nr = ar[k] * hr - ai[k] * hi + xr
                    ni = ar[k] * hi + ai[k] * hr + xi
                    buf[c0 + k, pl.ds(row, SUBLANES), :] = nr
                    buf[half_tiles + c0 + k, pl.ds(row, SUBLANES), :] = ni
                    out_r.append(nr)
                    out_i.append(ni)
                return tuple(out_r) + tuple(out_i)

            fin = lax.fori_loop(0, tt, step, init, unroll=min(tt, 8))
            for k in range(group):
                hc[r0:r0 + SUBLANES, lane_tile(c0 + k)] = fin[k]
                hc[r0:r0 + SUBLANES, lane_tile(half_tiles + c0 + k)] = fin[group + k]

    if batched:
        parts = []
        for hf in range(2):
            for c in range(ntile):
                for t in range(tt):
                    tmp[c, pl.ds(t, nb, stride=tt), :] = buf[c, t * rows + hf * nb:t * rows + (hf + 1) * nb, :]
            hs = jnp.concatenate([tmp[c] for c in range(ntile)], axis=1).astype(BF16)
            parts.append(jnp.dot(hs, cmat_ref[hf], preferred_element_type=F32))
        y_all = jnp.concatenate(parts, axis=1) + d_ref[...] * u_all
        y_ref[...] = y_all.reshape(nb, tt, SSM_WIDTH)
    else:
        parts = []
        for hf in range(2):
            hs = jnp.concatenate(
                [jnp.concatenate([buf[c, pl.ds(hf * nb + b, tt, stride=rows), :] for c in range(ntile)],
                                 axis=1).astype(BF16) for b in range(nb)], axis=0)
            parts.append(jnp.dot(hs, cmat_ref[hf], preferred_element_type=F32))
        y_all = jnp.concatenate(parts, axis=1) + d_ref[...] * u_ref[...].reshape(nb * tt, SSM_WIDTH)
        y_ref[...] = y_all.reshape(nb, tt, SSM_WIDTH)

    ht_ref[...] = hc[...]


def _s5(u3, h0, bmat, cmat, a_re, a_im, d_skip):
    nb, t_len, _ = u3.shape
    tt = min(t_len, 256)
    rows = 2 * nb
    batched = tt < 16
    kern = functools.partial(_s5_kernel, nb=nb, tt=tt, batched=batched)
    ntile = 2 * STATE_HALF // LANES
    tmp_shape = (ntile, nb * tt, LANES) if batched else (1, SUBLANES, LANES)
    return pl.pallas_call(
        kern,
        grid=(t_len // tt,),
        in_specs=[pl.BlockSpec((nb, tt, SSM_WIDTH), lambda i: (0, i, 0)),
                  _full((2, SSM_HALF, 2 * STATE_HALF)),
                  _full((2, 2 * STATE_HALF, SSM_HALF)),
                  _full((rows, STATE_HALF)), _full((rows, STATE_HALF)),
                  _full((rows, 2 * STATE_HALF)), _full((1, SSM_WIDTH))],
        out_specs=[pl.BlockSpec((nb, tt, SSM_WIDTH), lambda i: (0, i, 0)),
                   _full((rows, 2 * STATE_HALF))],
        out_shape=[jax.ShapeDtypeStruct((nb, t_len, SSM_WIDTH), F32),
                   jax.ShapeDtypeStruct((rows, 2 * STATE_HALF), F32)],
        scratch_shapes=[pltpu.VMEM((ntile, tt * rows, LANES), F32),
                        pltpu.VMEM((rows, 2 * STATE_HALF), F32),
                        pltpu.VMEM(tmp_shape, F32)],
        compiler_params=pltpu.CompilerParams(dimension_semantics=("arbitrary",),
                                             vmem_limit_bytes=VMEM_LIMIT),
        name="s5_scan",
    )(u3, bmat, cmat, a_re, a_im, h0, d_skip.reshape(1, SSM_WIDTH))


def _s5_params(a_re, a_im, b_re, b_im, c_re, c_im, log_dt):
    dt = jnp.exp(log_dt)[:, None]
    mag = jnp.exp(dt * a_re)
    ang = dt * a_im
    ab_re, ab_im = mag * jnp.cos(ang), mag * jnp.sin(ang)
    den = a_re * a_re + a_im * a_im
    nr, ni = ab_re - 1.0, ab_im
    f_re = (nr * a_re + ni * a_im) / den
    f_im = (ni * a_re - nr * a_im) / den
    bb_re = f_re[..., None] * b_re - f_im[..., None] * b_im
    bb_im = f_re[..., None] * b_im + f_im[..., None] * b_re
    gh = SSM_GROUPS // 2
    eye = jnp.eye(gh, dtype=F32)

    def b_half(w):
        return jnp.einsum('gnc,gh->gchn', w, eye).reshape(gh * SSM_GROUP, gh * SSM_STATE)

    def c_half(w):
        return jnp.einsum('gcn,gh->gnhc', w, eye).reshape(gh * SSM_STATE, gh * SSM_GROUP)

    bmat = jnp.stack([jnp.concatenate([b_half(bb_re[h * gh:(h + 1) * gh]),
                                       b_half(bb_im[h * gh:(h + 1) * gh])], axis=1)
                      for h in range(2)]).astype(BF16)
    cmat = jnp.stack([jnp.concatenate([c_half(c_re[h * gh:(h + 1) * gh]),
                                       -c_half(c_im[h * gh:(h + 1) * gh])], axis=0)
                      for h in range(2)]).astype(BF16)
    return bmat, cmat, ab_re.reshape(2, 1, STATE_HALF), ab_im.reshape(2, 1, STATE_HALF)


def _state_to_rows(h_re, h_im):
    nb = h_re.shape[0]
    f = lambda h: h.reshape(nb, 2, STATE_HALF).transpose(1, 0, 2).reshape(2 * nb, STATE_HALF)
    return jnp.concatenate([f(h_re), f(h_im)], axis=1)


def _rows_to_state(ht, nb):
    f = lambda h: h.reshape(2, nb, STATE_HALF).transpose(1, 0, 2).reshape(nb, SSM_GROUPS, SSM_STATE)
    return f(ht[:, :STATE_HALF]), f(ht[:, STATE_HALF:])


def _attn_prompt_kernel(q_ref, kp_ref, kc_ref, vp_ref, vc_ref, o_ref, l_ref, bias_s, s_s, p_s, *, dil):
    blk = ATTN_BLOCK
    n = pl.program_id(2)

    @pl.when((pl.program_id(0) == 0) & (pl.program_id(1) == 0) & (n == 0))
    def _():
        i_idx = lax.broadcasted_iota(I32, (blk, 2 * blk), 0)
        j_idx = lax.broadcasted_iota(I32, (blk, 2 * blk), 1)
        delta = i_idx - j_idx + blk
        in_band = (delta >= 0) & (delta <= blk)
        dist = (delta * dil).astype(F32)
        for h in range(N_HEADS):
            biased = -ALIBI_SLOPES[h] * dist
            bias_s[0, h] = jnp.where(in_band & (j_idx >= blk), biased, MASK_VALUE)
            bias_s[1, h] = jnp.where(in_band, biased, MASK_VALUE)

    lane = lax.broadcasted_iota(I32, (blk, LANES), 1)
    nqb = q_ref.shape[0] // blk
    for qb in range(nqb):
        which = jnp.minimum(n, 1) if qb == 0 else 1
        rows = slice(qb * blk, (qb + 1) * blk)
        for hp in range(N_HEADS // 2):
            cols = slice(hp * LANES, (hp + 1) * LANES)
            q2 = q_ref[rows, cols]
            before = kp_ref[:, cols] if qb == 0 else kc_ref[(qb - 1) * blk:qb * blk, cols]
            kk = jnp.concatenate([before, kc_ref[rows, cols]], axis=0)
            for half in range(2):
                h = 2 * hp + half
                in_head = (lane >= half * HEAD_DIM) & (lane < (half + 1) * HEAD_DIM)
                qm = jnp.where(in_head, q2, jnp.zeros_like(q2))
                s = lax.dot_general(qm, kk, (((1,), (1,)), ((), ())), preferred_element_type=F32)
                s_s[qb * N_HEADS + h] = s + bias_s[which, h]
    for qb in range(nqb):
        lse_all = jnp.zeros((blk, LANES), F32)
        for h in range(N_HEADS):
            s = s_s[qb * N_HEADS + h]
            m = jnp.max(s, axis=1, keepdims=True)
            p = jnp.exp(s - m)
            l = jnp.sum(p, axis=1, keepdims=True)
            p_s[qb * N_HEADS + h] = (p * (1.0 / l)).astype(BF16)
            lse_all = jnp.where(lane == h, m + jnp.log(l), lse_all)
        l_ref[qb * blk:(qb + 1) * blk, :] = lse_all
    for qb in range(nqb):
        rows = slice(qb * blk, (qb + 1) * blk)
        for hp in range(N_HEADS // 2):
            cols = slice(hp * LANES, (hp + 1) * LANES)
            before = vp_ref[:, cols] if qb == 0 else vc_ref[(qb - 1) * blk:qb * blk, cols]
            vv = jnp.concatenate([before, vc_ref[rows, cols]], axis=0)
            outs = [jnp.dot(p_s[qb * N_HEADS + 2 * hp + half], vv, preferred_element_type=F32)
                    for half in range(2)]
            o_ref[rows, cols] = jnp.where(lane < HEAD_DIM, outs[0], outs[1])


def _attn_prompt_branch(qb, kb, vb, bsz, s_len, dil):
    sub = s_len // dil
    nqb = min(ATTN_STEP_BLOCKS, sub // ATTN_BLOCK)
    step = nqb * ATTN_BLOCK
    nstep = sub // step
    view = lambda t: t.reshape(bsz, sub, dil * ATTN_WIDTH)
    cur = pl.BlockSpec((None, step, ATTN_WIDTH), lambda b, r, n: (b, n, r))
    prev = pl.BlockSpec((None, ATTN_BLOCK, ATTN_WIDTH),
                        lambda b, r, n: (b, jnp.maximum(n * nqb - 1, 0), r))
    o, lse = pl.pallas_call(
        functools.partial(_attn_prompt_kernel, dil=dil),
        grid=(bsz, dil, nstep),
        in_specs=[cur, prev, cur, prev, cur],
        out_specs=[cur, pl.BlockSpec((None, step, LANES), lambda b, r, n: (b, n, r))],
        out_shape=[jax.ShapeDtypeStruct((bsz, sub, dil * ATTN_WIDTH), F32),
                   jax.ShapeDtypeStruct((bsz, sub, dil * LANES), F32)],
        scratch_shapes=[pltpu.VMEM((2, N_HEADS, ATTN_BLOCK, 2 * ATTN_BLOCK), F32),
                        pltpu.VMEM((nqb * N_HEADS, ATTN_BLOCK, 2 * ATTN_BLOCK), F32),
                        pltpu.VMEM((nqb * N_HEADS, ATTN_BLOCK, 2 * ATTN_BLOCK), BF16)],
        compiler_params=pltpu.CompilerParams(
            dimension_semantics=("arbitrary", "arbitrary", "arbitrary"),
            vmem_limit_bytes=VMEM_LIMIT),
        name=f"attn_prompt_d{dil}",
    )(view(qb), view(kb), view(kb), view(vb), view(vb))
    return o.reshape(bsz * sub, dil * ATTN_WIDTH), lse.reshape(bsz * sub, dil * LANES)


def _attn_sample_kernel(q_ref, kn_ref, vn_ref, kc_ref, vc_ref, o_ref, bias_s, mult_s, *, t_len, wb):
    nrow = N_HEADS * t_len
    t_shift = t_len.bit_length() - 1
    d_shift = HEAD_DIM.bit_length() - 1
    nt = (((1,), (1,)), ((), ()))

    def branch_count(dist):
        mult = jnp.zeros(dist.shape, F32)
        for win, dil in zip(WINDOWS, DILATIONS):
            hit = (dist >= 0) & (dist <= win) & ((dist & (dil - 1)) == 0)
            mult = mult + jnp.where(hit, 1.0, 0.0)
        return mult

    def biased(dist, mult):
        head = lax.broadcasted_iota(I32, dist.shape, 0) >> t_shift
        slope = jnp.zeros(dist.shape, F32)
        for h in range(N_HEADS):
            slope = jnp.where(head == h, ALIBI_SLOPES[h], slope)
        return jnp.where(mult > 0.0, -slope * dist.astype(F32), MASK_VALUE)

    @pl.when(pl.program_id(0) == 0)
    def _():
        row = lax.broadcasted_iota(I32, (nrow, wb), 0)
        col = lax.broadcasted_iota(I32, (nrow, wb), 1)
        dist = wb + (row & (t_len - 1)) - col
        mult = branch_count(dist)
        mult_s[...] = mult
        bias_s[...] = biased(dist, mult)

    q = q_ref[...].astype(F32)
    qt = jnp.concatenate([q] * N_HEADS, axis=0)
    row_w = lax.broadcasted_iota(I32, (nrow, ATTN_WIDTH), 0)
    lane_w = lax.broadcasted_iota(I32, (nrow, ATTN_WIDTH), 1)
    qm = jnp.where((lane_w >> d_shift) == (row_w >> t_shift), qt, 0.0).astype(BF16)

    pad = LANES - t_len
    kn = jnp.concatenate([kn_ref[...].astype(F32), jnp.zeros((pad, ATTN_WIDTH), F32)], axis=0).astype(BF16)
    vn = jnp.concatenate([vn_ref[...].astype(F32), jnp.zeros((pad, ATTN_WIDTH), F32)], axis=0).astype(BF16)
    row_n = lax.broadcasted_iota(I32, (nrow, LANES), 0)
    col_n = lax.broadcasted_iota(I32, (nrow, LANES), 1)
    dist_n = jnp.where(col_n < t_len, (row_n & (t_len - 1)) - col_n, -1)
    mult_n = branch_count(dist_n)
    s_n = lax.dot_general(qm, kn, nt, preferred_element_type=F32) + biased(dist_n, mult_n)
    s_c = jnp.dot(qm, kc_ref[...].astype(BF16), preferred_element_type=F32) + bias_s[...]

    m = jnp.maximum(jnp.max(s_c, axis=1, keepdims=True), jnp.max(s_n, axis=1, keepdims=True))
    p_c = jnp.exp(s_c - m) * mult_s[...]
    p_n = jnp.exp(s_n - m) * mult_n
    l = jnp.sum(p_c, axis=1, keepdims=True) + jnp.sum(p_n, axis=1, keepdims=True)
    o = (lax.dot_general(p_c.astype(BF16), vc_ref[...].astype(BF16), nt, preferred_element_type=F32)
         + jnp.dot(p_n.astype(BF16), vn, preferred_element_type=F32)) / l
    lane_o = lax.broadcasted_iota(I32, (t_len, ATTN_WIDTH), 1) >> d_shift
    out = jnp.zeros((t_len, ATTN_WIDTH), F32)
    for h in range(N_HEADS):
        out = jnp.where(lane_o == h, o[h * t_len:(h + 1) * t_len], out)
    o_ref[...] = out


def _attn_sample(qb, kb, vb, cache_k, cache_v):
    bsz, t_len, _ = qb.shape
    wb = cache_k.shape[1]
    feature_major = lambda c: jnp.transpose(c, (0, 2, 3, 1)).reshape(bsz, ATTN_WIDTH, wb)
    new = pl.BlockSpec((None, t_len, ATTN_WIDTH), lambda b: (b, 0, 0))
    old = pl.BlockSpec((None, ATTN_WIDTH, wb), lambda b: (b, 0, 0))
    return pl.pallas_call(
        functools.partial(_attn_sample_kernel, t_len=t_len, wb=wb),
        grid=(bsz,),
        in_specs=[new, new, new, old, old],
        out_specs=new,
        out_shape=jax.ShapeDtypeStruct((bsz, t_len, ATTN_WIDTH), F32),
        scratch_shapes=[pltpu.VMEM((N_HEADS * t_len, wb), F32),
                        pltpu.VMEM((N_HEADS * t_len, wb), F32)],
        compiler_params=pltpu.CompilerParams(dimension_semantics=("arbitrary",),
                                             vmem_limit_bytes=VMEM_LIMIT),
        name="attn_sample",
    )(qb, kb, vb, feature_major(cache_k), feature_major(cache_v))


def _mid_kernel(*refs, dils):
    n_branch = max(len(dils), 1)
    x_ref, y_ref = refs[0], refs[1]
    o_refs = refs[2:2 + n_branch]
    pos = 2 + n_branch
    l_refs = refs[pos:pos + len(dils)]
    pos += len(l_refs)
    (wglu_ref, bglu_ref, lns_ref, lna_ref, wout_ref, lnm_ref, wrh_ref, wrl_ref, br_ref,
     expand_ref, tri_ref,
     x1_ref, hrow_ref, eidx_ref, epos_ref, egate_ref, cnt_ref, carry, nat) = refs[pos:]
    tm = x_ref.shape[0]
    o_tiles = ATTN_WIDTH // LANES

    rb = tm

    for bi, d in enumerate(dils):
        if d == 1:
            continue
        for r in range(d):
            spread = pl.ds(r, tm // d, stride=d)
            for ct in range(o_tiles):
                c0 = r * ATTN_WIDTH + ct * LANES
                nat[bi, ct, spread, :] = o_refs[bi][:, c0:c0 + LANES]
            nat[bi, o_tiles, spread, :] = l_refs[bi][:, r * LANES:(r + 1) * LANES]

    def natural(bi, d, rows):
        if d == 1:
            return o_refs[bi][rows, :], l_refs[bi][rows, :]
        return (jnp.concatenate([nat[bi, ct, rows, :] for ct in range(o_tiles)], axis=1),
                nat[bi, o_tiles, rows, :])

    @pl.when(pl.program_id(0) == 0)
    def _():
        carry[...] = jnp.zeros_like(carry)

    lane = lax.broadcasted_iota(I32, (rb, LANES), 1)
    lane_f = lane.astype(F32)
    run = carry[0:1, :]
    for blk in range(tm // rb):
        rows = slice(blk * rb, (blk + 1) * rb)
        y = y_ref[rows, :]
        z = y * (0.5 * (1.0 + jnp.tanh(math.sqrt(2.0 / math.pi) * (y + 0.044715 * (y * y * y)))))
        glu = z * _sigmoid(jnp.dot(z.astype(BF16), wglu_ref[...], preferred_element_type=F32)
                           + bglu_ref[...])
        n_ssm = _rms(glu, lns_ref[...])

        if not dils:
            attn = o_refs[0][rows, :]
        else:
            pairs = [natural(bi, d, rows) for bi, d in enumerate(dils)]
            lses = [p[1] for p in pairs]
            mx = functools.reduce(jnp.maximum, lses)
            es = [jnp.exp(l - mx) for l in lses]
            inv = 1.0 / functools.reduce(lambda a, b: a + b, es)
            attn = jnp.zeros((rb, ATTN_WIDTH), F32)
            for e, (o_nat, _) in zip(es, pairs):
                w = e * inv
                w_hi = w.astype(BF16)
                w_lo = (w - w_hi.astype(F32)).astype(BF16)
                wide = jnp.dot(jnp.concatenate([w_hi, w_lo], axis=1), expand_ref[...],
                               preferred_element_type=F32)
                attn = attn + wide * o_nat
        n_attn = _rms(attn, lna_ref[...])

        x1 = (x_ref[rows, :]
              + jnp.dot(n_ssm.astype(BF16), wout_ref[:SSM_WIDTH, :], preferred_element_type=F32)
              + jnp.dot(n_attn.astype(BF16), wout_ref[SSM_WIDTH:, :], preferred_element_type=F32))
        x1_ref[rows, :] = x1
        hm = _rms(x1, lnm_ref[...])
        for s in range(ROW_TILES):
            hrow_ref[pl.ds(blk * rb * ROW_TILES + s, rb, stride=ROW_TILES), :] = hm[:, s * LANES:(s + 1) * LANES]

        h_hi = hm.astype(BF16)
        h_lo = (hm - h_hi.astype(F32)).astype(BF16)
        both = jnp.dot(h_hi, wrl_ref[...], preferred_element_type=F32)
        logits = (both[:, :LANES] + both[:, LANES:]
                  + jnp.dot(h_lo, wrh_ref[...], preferred_element_type=F32)
                  + br_ref[...])
        work = jnp.where(lane < N_EXPERTS, logits, NEG_BIG)
        vals, idxs, hots = [], [], []
        for _ in range(TOP_K):
            m = jnp.max(work, axis=1, keepdims=True)
            idx = jnp.min(jnp.where(work == m, lane_f, float(LANES)), axis=1, keepdims=True)
            hot = lane_f == idx
            vals.append(m)
            idxs.append(idx)
            hots.append(hot)
            work = jnp.where(hot, NEG_BIG, work)
        exps = [jnp.exp(v - vals[0]) for v in vals]
        inv = 1.0 / functools.reduce(lambda a, b: a + b, exps)

        sel = functools.reduce(lambda a, b: a + b, [h.astype(F32) for h in hots])
        before = jnp.dot(tri_ref[...], sel.astype(BF16), preferred_element_type=F32) + run
        eidx = jnp.zeros((rb, LANES), I32)
        epos = jnp.zeros((rb, LANES), I32)
        egate = jnp.zeros((rb, LANES), F32)
        for k in range(TOP_K):
            pk = jnp.sum(jnp.where(hots[k], before, 0.0), axis=1, keepdims=True)
            eidx = jnp.where(lane == k, idxs[k].astype(I32), eidx)
            epos = jnp.where(lane == k, pk.astype(I32), epos)
            egate = jnp.where(lane == k, exps[k] * inv, egate)
        eidx_ref[rows, :] = eidx
        epos_ref[rows, :] = epos
        egate_ref[rows, :] = egate
        run = run + jnp.sum(sel, axis=0, keepdims=True)
    carry[...] = jnp.broadcast_to(run, carry.shape)
    cnt_ref[...] = jnp.broadcast_to(run, cnt_ref.shape).astype(I32)


def _mid(x2d, y2d, attn_o, attn_lse, dils, w):
    n = x2d.shape[0]
    tm = min(n, MID_TILE)
    rb = tm
    ti = jnp.arange(rb)
    tri = (ti[:, None] > ti[None, :]).astype(BF16)
    n_branch = len(attn_o)
    row = lambda width: pl.BlockSpec((tm, width), lambda i: (i, 0))
    packed = lambda d, width: pl.BlockSpec((tm // d, d * width), lambda i: (i, 0))
    attn_specs = ([packed(d, ATTN_WIDTH) for d in dils] + [packed(d, LANES) for d in dils]
                  if dils else [row(ATTN_WIDTH)])
    in_specs = ([row(D_MODEL), row(SSM_WIDTH)] + attn_specs
                + [_full((SSM_WIDTH, SSM_WIDTH)), _full((1, SSM_WIDTH)), _full((1, SSM_WIDTH)),
                   _full((1, ATTN_WIDTH)), _full((D_MODEL, D_MODEL)), _full((1, D_MODEL)),
                   _full((D_MODEL, LANES)), _full((D_MODEL, 2 * LANES)), _full((1, LANES)),
                   _full((2 * LANES, ATTN_WIDTH)), _full((rb, rb))])
    out_specs = [row(D_MODEL), pl.BlockSpec((tm * ROW_TILES, LANES), lambda i: (i, 0)),
                 row(LANES), row(LANES), row(LANES), _full((SUBLANES, LANES))]
    out_shape = [jax.ShapeDtypeStruct((n, D_MODEL), F32),
                 jax.ShapeDtypeStruct((n * ROW_TILES, LANES), F32),
                 jax.ShapeDtypeStruct((n, LANES), I32),
                 jax.ShapeDtypeStruct((n, LANES), I32),
                 jax.ShapeDtypeStruct((n, LANES), F32),
                 jax.ShapeDtypeStruct((SUBLANES, LANES), I32)]
    return pl.pallas_call(
        functools.partial(_mid_kernel, dils=tuple(dils)),
        grid=(n // tm,),
        in_specs=in_specs,
        out_specs=out_specs,
        out_shape=out_shape,
        scratch_shapes=[pltpu.VMEM((SUBLANES, LANES), F32),
                        pltpu.VMEM((n_branch, ATTN_WIDTH // LANES + 1, tm, LANES), F32)],
        compiler_params=pltpu.CompilerParams(dimension_semantics=("arbitrary",),
                                             vmem_limit_bytes=VMEM_LIMIT),
        name="mid",
    )(x2d, y2d, *attn_o, *attn_lse, w["w_glu"], w["b_glu"], w["ln_ssm_out"], w["ln_attn_out"],
      w["w_out"], w["ln_moe"], w["wr_hi"], w["wr_lo"], w["b_router"], w["expand"], tri)


def _dispatch_kernel(dest_ref, hp_ref, hs_ref, xs_hbm, sem, *, n_prompt_tiles):
    i = pl.program_id(0)
    npair = TOKEN_TILE * TOP_K

    def run(src_ref):
        def issue(t, c):
            src = src_ref.at[pl.ds(pl.multiple_of(t * ROW_TILES, ROW_TILES), ROW_TILES), :]
            for k in range(TOP_K):
                d = dest_ref[0, t * TOP_K + k]
                pltpu.make_async_copy(
                    src, xs_hbm.at[pl.ds(pl.multiple_of(d * ROW_TILES, ROW_TILES), ROW_TILES), :],
                    sem).start()
            return c

        lax.fori_loop(0, TOKEN_TILE, issue, 0, unroll=2)
        span = pl.ds(0, npair * ROW_TILES)
        pltpu.make_async_copy(xs_hbm.at[span, :], xs_hbm.at[span, :], sem).wait()

    @pl.when(i < n_prompt_tiles)
    def _():
        run(hp_ref)

    @pl.when(i >= n_prompt_tiles)
    def _():
        run(hs_ref)


def _dispatch(dest, hrow_p, hrow_s):
    n_p = hrow_p.shape[0] // ROW_TILES
    n_s = hrow_s.shape[0] // ROW_TILES
    npt = n_p // TOKEN_TILE
    ntile = (n_p + n_s) // TOKEN_TILE
    npair = TOKEN_TILE * TOP_K
    blk = (TOKEN_TILE * ROW_TILES, LANES)
    return pl.pallas_call(
        functools.partial(_dispatch_kernel, n_prompt_tiles=npt),
        grid=(ntile,),
        in_specs=[pl.BlockSpec((None, 1, npair), lambda i: (i, 0, 0), memory_space=pltpu.SMEM),
                  pl.BlockSpec(blk, lambda i: (jnp.minimum(i, npt - 1), 0)),
                  pl.BlockSpec(blk, lambda i: (jnp.maximum(i - npt, 0), 0))],
        out_specs=pl.BlockSpec(memory_space=pl.ANY),
        out_shape=jax.ShapeDtypeStruct(((n_p + n_s) * TOP_K * ROW_TILES, LANES), F32),
        scratch_shapes=[pltpu.SemaphoreType.DMA(())],
        compiler_params=pltpu.CompilerParams(dimension_semantics=("arbitrary",),
                                             vmem_limit_bytes=VMEM_LIMIT),
        name="moe_dispatch",
    )(dest.reshape(ntile, 1, npair), hrow_p, hrow_s)


def _expert_kernel(vt_ref, ve_ref, vok_ref, vnext_ref, vmore_ref, vslot_ref, gs_ref,
                   xs_ref, wu_hbm, bu_ref, wd_hbm, bd_ref,
                   out_ref, wu_s, wd_s, x_s, wu_f, wd_f, wsem):
    v = pl.program_id(0)
    e = ve_ref[v]
    j = vt_ref[v]
    vprev = jnp.maximum(v - 1, 0)
    new_e = (v == 0) | (e != ve_ref[vprev])
    new_j = (v == 0) | (j != vt_ref[vprev])
    tm = MOE_TILE
    slot = vslot_ref[v]

    def weight_copies(expert, to):
        return (pltpu.make_async_copy(wu_hbm.at[expert], wu_f.at[to], wsem.at[0, to]),
                pltpu.make_async_copy(wd_hbm.at[expert], wd_f.at[to], wsem.at[1, to]))

    @pl.when(v == 0)
    def _():
        for c in weight_copies(e, slot):
            c.start()

    @pl.when(new_e)
    def _():
        for c in weight_copies(e, slot):
            c.wait()

        @pl.when(vmore_ref[v] == 1)
        def _():
            for c in weight_copies(vnext_ref[v], 1 - slot):
                c.start()

        wu_s[...] = wu_f[slot].astype(BF16)
        wd_s[...] = wd_f[slot].astype(BF16)

    @pl.when(new_j)
    def _():
        out_ref[...] = jnp.zeros_like(out_ref)

    lo, hi = gs_ref[e], gs_ref[e + 1]

    def run_pass(row0, nrows):
        base = row0 * ROW_TILES
        for s in range(ROW_TILES):
            x_s[:nrows, s * LANES:(s + 1) * LANES] = xs_ref[
                pl.ds(base + s, nrows, stride=ROW_TILES), :].astype(BF16)
        a = jnp.dot(x_s[:nrows, :], wu_s[...], preferred_element_type=F32) + bu_ref[...]
        g = jnp.minimum(a[:, :EXPERT_FF], SWIGLU_LIMIT)
        lin = jnp.clip(a[:, EXPERT_FF:], -SWIGLU_LIMIT, SWIGLU_LIMIT)
        act = (lin + 1.0) * (g * _sigmoid(SWIGLU_ALPHA * g))
        y = jnp.dot(act.astype(BF16), wd_s[...], preferred_element_type=F32) + bd_ref[...]
        rows = j * tm + row0 + lax.broadcasted_iota(I32, (nrows, 1), 0)
        mine = (rows >= lo) & (rows < hi)
        for s in range(ROW_TILES):
            cur = out_ref[pl.ds(base + s, nrows, stride=ROW_TILES), :]
            out_ref[pl.ds(base + s, nrows, stride=ROW_TILES), :] = jnp.where(
                mine, y[:, s * LANES:(s + 1) * LANES], cur)

    sub = x_s.shape[0]
    half = sub // 2
    live = vok_ref[v] == 1
    for part in range(tm // sub):
        row0 = part * sub
        start = j * tm + row0
        in_first = live & (lo < start + half) & (hi > start)
        in_second = live & (lo < start + sub) & (hi > start + half)
        pl.when(in_first & in_second)(functools.partial(run_pass, row0, sub))
        pl.when(in_first & jnp.logical_not(in_second))(functools.partial(run_pass, row0, half))
        pl.when(in_second & jnp.logical_not(in_first))(functools.partial(run_pass, row0 + half, half))


def _experts(xs, visits, gstart, w_up, b_up, w_down, b_down):
    tm = MOE_TILE
    nvisit = visits[0].shape[0]
    rows = pl.BlockSpec((tm * ROW_TILES, LANES), lambda v, vt, *_: (vt[v], 0))
    per_e = lambda a, b: pl.BlockSpec((None, a, b), lambda v, vt, ve, *_: (ve[v], 0, 0))
    grid_spec = pltpu.PrefetchScalarGridSpec(
        num_scalar_prefetch=7,
        grid=(nvisit,),
        in_specs=[rows, pl.BlockSpec(memory_space=pl.ANY), per_e(1, 2 * EXPERT_FF),
                  pl.BlockSpec(memory_space=pl.ANY), per_e(1, D_MODEL)],
        out_specs=rows,
        scratch_shapes=[pltpu.VMEM((D_MODEL, 2 * EXPERT_FF), BF16),
                        pltpu.VMEM((EXPERT_FF, D_MODEL), BF16),
                        pltpu.VMEM((MOE_SUBTILE, D_MODEL), BF16),
                        pltpu.VMEM((2, D_MODEL, 2 * EXPERT_FF), F32),
                        pltpu.VMEM((2, EXPERT_FF, D_MODEL), F32),
                        pltpu.SemaphoreType.DMA((2, 2))],
    )
    return pl.pallas_call(
        _expert_kernel,
        grid_spec=grid_spec,
        out_shape=jax.ShapeDtypeStruct(xs.shape, F32),
        compiler_params=pltpu.CompilerParams(dimension_semantics=("arbitrary",),
                                             vmem_limit_bytes=VMEM_LIMIT),
        name="moe_experts",
    )(*visits, gstart, xs, w_up, b_up.reshape(N_EXPERTS, 1, 2 * EXPERT_FF),
      w_down, b_down.reshape(N_EXPERTS, 1, D_MODEL))


def _routing(eidx_p, epos_p, cnt_p, eidx_s, epos_s, cnt_s, n_rows):
    cnt_p = cnt_p[0, :N_EXPERTS]
    cnt_s = cnt_s[0, :N_EXPERTS]
    cnt = cnt_p + cnt_s
    gend = jnp.cumsum(cnt)
    gstart = gend - cnt
    experts = jnp.arange(N_EXPERTS, dtype=I32)

    def lookup(table, idx):
        return jnp.sum(jnp.where(idx[..., None] == experts, table, 0), axis=-1)

    ep = eidx_p[:, :TOP_K]
    es = eidx_s[:, :TOP_K]
    dest_p = lookup(gstart, ep) + epos_p[:, :TOP_K]
    dest_s = lookup(gstart + cnt_p, es) + epos_s[:, :TOP_K]
    ntile = n_rows // MOE_TILE
    nvisit = ntile + N_EXPERTS
    first = gstart // MOE_TILE
    last = jnp.maximum(gend - 1, 0) // MOE_TILE
    nv = jnp.where(cnt > 0, last - first + 1, 0)
    vend = jnp.cumsum(nv)
    vstart = vend - nv
    total = vend[-1]
    v = jnp.arange(nvisit, dtype=I32)
    vc = jnp.minimum(v, total - 1)
    ve = jnp.sum((vend[None, :] <= vc[:, None]).astype(I32), axis=1)
    vt = (lookup(first - vstart, ve) + vc).astype(I32)
    vok = (v < total).astype(I32)
    has_rows = cnt > 0
    later = (experts[None, :] > experts[:, None]) & has_rows[None, :]
    nxt = jnp.min(jnp.where(later, experts[None, :], N_EXPERTS), axis=1)
    more = nxt < N_EXPERTS
    slot = (jnp.cumsum(has_rows.astype(I32)) - 1) % 2
    visits = (vt, ve, vok, lookup(jnp.where(more, nxt, 0), ve).astype(I32),
              lookup(more.astype(I32), ve).astype(I32), lookup(slot, ve).astype(I32))
    gs = jnp.concatenate([gstart, gend[-1:]]).astype(I32)
    return dest_p.astype(I32), dest_s.astype(I32), visits, gs


def _out_kernel(dest_ref, next_ref, y_hbm, x1_ref, gate_ref, pe_ref, lnp_ref, wg_ref, bg_ref, wp_ref,
                lnf_ref, o_ref, buf, sem, *, ntile):
    tm = TOKEN_TILE
    npair = tm * TOP_K
    i = pl.program_id(0)
    slot = i % 2

    def gather(idx_ref, to):
        def issue(t, c):
            for k in range(TOP_K):
                d = idx_ref[0, t * TOP_K + k]
                pltpu.make_async_copy(
                    y_hbm.at[pl.ds(pl.multiple_of(d * ROW_TILES, ROW_TILES), ROW_TILES), :],
                    buf.at[to, pl.ds(pl.multiple_of((k * tm + t) * ROW_TILES, ROW_TILES), ROW_TILES), :],
                    sem.at[to]).start()
            return c

        lax.fori_loop(0, tm, issue, 0, unroll=2)

    @pl.when(i == 0)
    def _():
        gather(dest_ref, 0)

    if ntile > 1:
        @pl.when(i + 1 < ntile)
        def _():
            gather(next_ref, 1 - slot)

    pltpu.make_async_copy(y_hbm.at[pl.ds(0, npair * ROW_TILES), :], buf.at[slot], sem.at[slot]).wait()

    gates = gate_ref[...]
    parts = []
    for s in range(ROW_TILES):
        acc = jnp.zeros((tm, LANES), F32)
        for k in range(TOP_K):
            rows = buf[slot, pl.ds(k * tm * ROW_TILES + s, tm, stride=ROW_TILES), :]
            acc = acc + gates[:, k:k + 1] * rows
        parts.append(acc)
    x2 = x1_ref[...] + jnp.concatenate(parts, axis=1)
    gate = _sigmoid(jnp.dot(_rms(x2, lnp_ref[...]).astype(BF16), wg_ref[...],
                            preferred_element_type=F32) + bg_ref[...])
    x3 = x2 + gate * jnp.dot(pe_ref[...].astype(BF16), wp_ref[...], preferred_element_type=F32)
    o_ref[...] = _rms(x3, lnf_ref[...])


def _combine(dest, y_rows, x1, egate, pe, w):
    n = x1.shape[0]
    tm = TOKEN_TILE
    npair = tm * TOP_K
    row = lambda width: pl.BlockSpec((tm, width), lambda i: (i, 0))
    ntile = n // tm
    dest3 = dest.reshape(ntile, 1, npair)
    return pl.pallas_call(
        functools.partial(_out_kernel, ntile=ntile),
        grid=(ntile,),
        in_specs=[pl.BlockSpec((None, 1, npair), lambda i: (i, 0, 0), memory_space=pltpu.SMEM),
                  pl.BlockSpec((None, 1, npair), lambda i: (jnp.minimum(i + 1, ntile - 1), 0, 0),
                               memory_space=pltpu.SMEM),
                  pl.BlockSpec(memory_space=pl.ANY),
                  row(D_MODEL), row(LANES), row(PLE_DIM),
                  _full((1, D_MODEL)), _full((D_MODEL, D_MODEL)), _full((1, D_MODEL)),
                  _full((PLE_DIM, D_MODEL)), _full((1, D_MODEL))],
        out_specs=row(D_MODEL),
        out_shape=jax.ShapeDtypeStruct((n, D_MODEL), F32),
        scratch_shapes=[pltpu.VMEM((2, npair * ROW_TILES, LANES), F32), pltpu.SemaphoreType.DMA((2,))],
        compiler_params=pltpu.CompilerParams(dimension_semantics=("arbitrary",),
                                             vmem_limit_bytes=VMEM_LIMIT),
        name="combine_out",
    )(dest3, dest3, y_rows, x1, egate, pe, w["ln_ple"], w["w_ple_gate"],
      w["b_ple_gate"], w["w_ple_proj"], w["ln_final"])


def kernel(x_prompt, x_sample, cache_attn_k, cache_attn_v, state_ssm_re, state_ssm_im, p_prompt, p_sample, ln_mix, w_in, ssm_a_re, ssm_a_im, ssm_b_re, ssm_b_im, ssm_c_re, ssm_c_im, ssm_d, ssm_log_dt, w_glu, b_glu, ln_ssm_out, ln_attn_out, w_out, ln_moe, w_router, b_router, w_up, b_up, w_down, b_down, ln_ple, w_ple_gate, b_ple_gate, w_ple_proj, ln_final):
    bsz, s_len, _ = x_prompt.shape
    dbsz, dt_len, _ = x_sample.shape
    n_p, n_s = bsz * s_len, dbsz * dt_len
    wb = cache_attn_k.shape[2]
    wb_prompt = min(WINDOWS[-1], s_len)

    wr = jnp.pad(w_router[0], ((0, 0), (0, LANES - N_EXPERTS)))
    wr_hi = wr.astype(BF16)
    w = {
        "w_glu": w_glu[0].astype(BF16), "b_glu": b_glu[0].reshape(1, -1),
        "ln_ssm_out": ln_ssm_out[0].reshape(1, -1), "ln_attn_out": ln_attn_out[0].reshape(1, -1),
        "w_out": w_out[0].astype(BF16), "ln_moe": ln_moe[0].reshape(1, -1),
        "wr_hi": wr_hi,
        "wr_lo": jnp.concatenate([wr_hi, (wr - wr_hi.astype(F32)).astype(BF16)], axis=1),
        "b_router": jnp.pad(b_router[0], (0, LANES - N_EXPERTS)).reshape(1, -1),
        "expand": (jnp.arange(2 * LANES)[:, None] % LANES
                   == jnp.arange(ATTN_WIDTH)[None, :] // HEAD_DIM).astype(BF16),
        "ln_ple": ln_ple[0].reshape(1, -1), "w_ple_gate": w_ple_gate[0].astype(BF16),
        "b_ple_gate": b_ple_gate[0].reshape(1, -1), "w_ple_proj": w_ple_proj[0].astype(BF16),
        "ln_final": ln_final.reshape(1, -1),
    }
    w_in_b = w_in[0].astype(BF16)
    bmat, cmat, ab_re, ab_im = _s5_params(ssm_a_re[0], ssm_a_im[0], ssm_b_re[0], ssm_b_im[0],
                                          ssm_c_re[0], ssm_c_im[0], ssm_log_dt[0])

    def coeff(a, nb):
        return jnp.broadcast_to(a, (2, nb, STATE_HALF)).reshape(2 * nb, STATE_HALF)

    proj_p = _in_proj(x_prompt.reshape(n_p, D_MODEL), ln_mix[0], w_in_b, dils=DILATIONS[1:],
                      seq_window=(s_len, wb_prompt))
    u_p, k_p, v_p = proj_p[:3]
    qkv = [proj_p[3:6]] + [proj_p[6 + 3 * i:9 + 3 * i] for i in range(len(DILATIONS) - 1)]
    zeros_state = jnp.zeros((bsz, SSM_GROUPS, SSM_STATE), F32)
    y_p, ht_p = _s5(u_p.reshape(bsz, s_len, SSM_WIDTH), _state_to_rows(zeros_state, zeros_state),
                    bmat, cmat, coeff(ab_re, bsz), coeff(ab_im, bsz), ssm_d[0])
    branches = [_attn_prompt_branch(*qkv[i], bsz, s_len, d) for i, d in enumerate(DILATIONS)]
    x1_p, hrow_p, eidx_p, epos_p, egate_p, cnt_p = _mid(
        x_prompt.reshape(n_p, D_MODEL), y_p.reshape(n_p, SSM_WIDTH),
        [b[0] for b in branches], [b[1] for b in branches], DILATIONS, w)

    u_s, k_s, v_s, qb_s, kb_s, vb_s = _in_proj(x_sample.reshape(n_s, D_MODEL), ln_mix[0], w_in_b)
    y_s, ht_s = _s5(u_s.reshape(dbsz, dt_len, SSM_WIDTH), _state_to_rows(state_ssm_re[0], state_ssm_im[0]),
                    bmat, cmat, coeff(ab_re, dbsz), coeff(ab_im, dbsz), ssm_d[0])
    as3 = lambda t: t.reshape(dbsz, dt_len, ATTN_WIDTH)
    attn_s = _attn_sample(as3(qb_s), as3(kb_s), as3(vb_s), cache_attn_k[0], cache_attn_v[0])
    x1_s, hrow_s, eidx_s, epos_s, egate_s, cnt_s = _mid(
        x_sample.reshape(n_s, D_MODEL), y_s.reshape(n_s, SSM_WIDTH),
        [attn_s.reshape(n_s, ATTN_WIDTH)], [], (), w)

    n_rows = (n_p + n_s) * TOP_K
    dest_p, dest_s, visits, gs = _routing(eidx_p, epos_p, cnt_p, eidx_s, epos_s, cnt_s, n_rows)
    xs = _dispatch(jnp.concatenate([dest_p.reshape(-1), dest_s.reshape(-1)]), hrow_p, hrow_s)
    y_rows = _experts(xs, visits, gs, w_up[0], b_up[0], w_down[0], b_down[0])

    out_p = _combine(dest_p, y_rows, x1_p, egate_p, p_prompt[0].reshape(n_p, PLE_DIM), w)
    out_s = _combine(dest_s, y_rows, x1_s, egate_s, p_sample[0].reshape(n_s, PLE_DIM), w)

    hr_p, hi_p = _rows_to_state(ht_p, bsz)
    hr_s, hi_s = _rows_to_state(ht_s, dbsz)
    kv_p = lambda t: jnp.transpose(t.reshape(bsz, N_HEADS, HEAD_DIM, wb_prompt), (0, 3, 1, 2))[None]
    kv_s = lambda t: t.reshape(dbsz, dt_len, N_HEADS, HEAD_DIM)[None]
    return (out_p.reshape(bsz, s_len, D_MODEL), out_s.reshape(dbsz, dt_len, D_MODEL),
            kv_p(k_p), kv_p(v_p), hr_p[None], hi_p[None],
            kv_s(k_s), kv_s(v_s), hr_s[None], hi_s[None])
```

```python
import functools
import math

import jax
import jax.numpy as jnp
from jax import lax
from jax.experimental import pallas as pl
from jax.experimental.pallas import tpu as pltpu

F32 = jnp.float32
BF16 = jnp.bfloat16
I32 = jnp.int32

D_MODEL = 1024
SSM_WIDTH = 512
SSM_GROUP = 16
SSM_GROUPS = 32
SSM_STATE = 64
ATTN_WIDTH = 512
HEAD_DIM = 64
N_HEADS = 8
IN_WIDTH = SSM_WIDTH + 3 * ATTN_WIDTH
DILATIONS = (1, 4, 16)
WINDOWS = (128, 512, 2048)
ATTN_BLOCK = 128
ATTN_STEP_BLOCKS = 4
N_EXPERTS = 32
TOP_K = 4
EXPERT_FF = D_MODEL
SWIGLU_LIMIT = 7.0
SWIGLU_ALPHA = 1.702
PLE_DIM = 256
EPS = 1e-6
MASK_VALUE = -1e30
NEG_BIG = -3.0e38

LANES = 128
SUBLANES = 8
ROW_TILES = D_MODEL // LANES
TOKEN_TILE = 256
IN_TILE = 1024
IN_ROW_CHUNK = 512
MID_TILE = 512
MOE_TILE = 1024
MOE_SUBTILE = 512
SSM_HALF = SSM_WIDTH // 2
STATE_HALF = SSM_GROUPS * SSM_STATE // 2
ALIBI_SLOPES = tuple(2.0 ** (-8.0 * (h + 1) / N_HEADS) for h in range(N_HEADS))
VMEM_LIMIT = 56 * 1024 * 1024


def _rms(x, g):
    return x * lax.rsqrt(jnp.mean(x * x, axis=-1, keepdims=True) + EPS) * g


def _sigmoid(x):
    return 1.0 / (1.0 + jnp.exp(-x))


def _full(shape):
    n = len(shape)
    return pl.BlockSpec(shape, lambda *_: (0,) * n)


def _in_kernel(x_ref, g_ref, w_ref, u_ref, k_ref, v_ref, qb_ref, kb_ref, vb_ref, *rest, dils, window_tiles):
    tm = x_ref.shape[0]
    rc = min(tm, IN_ROW_CHUNK)
    tiles = ATTN_WIDTH // LANES
    for c in range(tm // rc):
        rows = slice(c * rc, (c + 1) * rc)
        h = _rms(x_ref[rows, :], g_ref[...]).astype(BF16)
        p = jnp.dot(h, w_ref[...], preferred_element_type=F32)
        u_ref[rows, :] = p[:, :SSM_WIDTH]
        q = p[:, SSM_WIDTH:SSM_WIDTH + ATTN_WIDTH] * (HEAD_DIM ** -0.5)
        k = p[:, SSM_WIDTH + ATTN_WIDTH:SSM_WIDTH + 2 * ATTN_WIDTH]
        v = p[:, SSM_WIDTH + 2 * ATTN_WIDTH:]
        if window_tiles is None:
            k_ref[rows, :] = k
            v_ref[rows, :] = v
        else:
            k_ref[:, rows] = k.T
            v_ref[:, rows] = v.T
        qb_ref[rows, :] = q.astype(BF16)
        kb_ref[rows, :] = k.astype(BF16)
        vb_ref[rows, :] = v.astype(BF16)
        if not dils:
            continue
        scr = rest[-1]
        for a, val in enumerate((q, k, v)):
            for ct in range(tiles):
                scr[c, a * tiles + ct] = val[:, ct * LANES:(ct + 1) * LANES]
        for di, d in enumerate(dils):
            out_rows = slice(c * rc // d, (c + 1) * rc // d)
            for a in range(3):
                out = rest[di * 3 + a]
                for r in range(d):
                    for ct in range(tiles):
                        piece = scr[c, a * tiles + ct, pl.ds(r, rc // d, stride=d), :]
                        c0 = r * ATTN_WIDTH + ct * LANES
                        out[out_rows, c0:c0 + LANES] = piece.astype(BF16)


def _in_proj(x2d, ln_mix, w_in_bf16, dils=(), seq_window=None):
    n = x2d.shape[0]
    tm = min(n, IN_TILE)
    row = lambda w: pl.BlockSpec((tm, w), lambda i: (i, 0))
    window_tiles = None
    kv_spec, kv_shape = row(ATTN_WIDTH), jax.ShapeDtypeStruct((n, ATTN_WIDTH), F32)
    if seq_window is not None:
        s_len, window = seq_window
        seq_tiles, first = s_len // tm, (s_len - window) // tm
        window_tiles = (seq_tiles, first)
        kv_spec = pl.BlockSpec(
            (None, ATTN_WIDTH, tm),
            lambda i: (i // seq_tiles, 0, jnp.maximum(i % seq_tiles - first, 0)))
        kv_shape = jax.ShapeDtypeStruct((n // s_len, ATTN_WIDTH, window), F32)
    out_specs = [row(SSM_WIDTH), kv_spec, kv_spec] + [row(ATTN_WIDTH)] * 3
    out_shape = ([jax.ShapeDtypeStruct((n, SSM_WIDTH), F32), kv_shape, kv_shape]
                 + [jax.ShapeDtypeStruct((n, ATTN_WIDTH), BF16)] * 3)
    for d in dils:
        out_specs += [pl.BlockSpec((tm // d, d * ATTN_WIDTH), lambda i: (i, 0))] * 3
        out_shape += [jax.ShapeDtypeStruct((n // d, d * ATTN_WIDTH), BF16)] * 3
    rc = min(tm, IN_ROW_CHUNK)
    scratch = [pltpu.VMEM((tm // rc, 3 * ATTN_WIDTH // LANES, rc, LANES), F32)] if dils else []
    return pl.pallas_call(
        functools.partial(_in_kernel, dils=tuple(dils), window_tiles=window_tiles),
        grid=(n // tm,),
        in_specs=[row(D_MODEL), _full((1, D_MODEL)), _full((D_MODEL, IN_WIDTH))],
        out_specs=out_specs,
        out_shape=out_shape,
        scratch_shapes=scratch,
        compiler_params=pltpu.CompilerParams(dimension_semantics=("arbitrary",),
                                             vmem_limit_bytes=VMEM_LIMIT),
        name="in_proj",
    )(x2d, ln_mix.reshape(1, D_MODEL), w_in_bf16)


def _s5_kernel(u_ref, bmat_ref, cmat_ref, are_ref, aim_ref, h0_ref, d_ref,
               y_ref, ht_ref, buf, hc, tmp, *, nb, tt, batched):
    rows = 2 * nb
    ntile = 2 * STATE_HALF // LANES
    half_tiles = ntile // 2

    def lane_tile(c):
        return slice(c * LANES, (c + 1) * LANES)

    @pl.when(pl.program_id(0) == 0)
    def _():
        hc[...] = h0_ref[...]

    if batched:
        u_all = u_ref[...].reshape(nb * tt, SSM_WIDTH)
        ub_all = u_all.astype(BF16)
        for hf in range(2):
            bu = jnp.dot(ub_all[:, hf * SSM_HALF:(hf + 1) * SSM_HALF], bmat_ref[hf],
                         preferred_element_type=F32)
            for c in range(ntile):
                tmp[c] = bu[:, lane_tile(c)]
            for c in range(ntile):
                for t in range(tt):
                    buf[c, t * rows + hf * nb:t * rows + (hf + 1) * nb, :] = tmp[c, pl.ds(t, nb, stride=tt), :]
    else:
        ub_all = u_ref[...].reshape(nb * tt, SSM_WIDTH).astype(BF16)
        for hf in range(2):
            bu = jnp.dot(ub_all[:, hf * SSM_HALF:(hf + 1) * SSM_HALF], bmat_ref[hf],
                         preferred_element_type=F32)
            for b in range(nb):
                for c in range(ntile):
                    buf[c, pl.ds(hf * nb + b, tt, stride=rows), :] = bu[b * tt:(b + 1) * tt, lane_tile(c)]

    group = 8
    for s in range(rows // SUBLANES):
        r0 = s * SUBLANES
        for c0 in range(0, half_tiles, group):
            ar = [are_ref[r0:r0 + SUBLANES, lane_tile(c0 + k)] for k in range(group)]
            ai = [aim_ref[r0:r0 + SUBLANES, lane_tile(c0 + k)] for k in range(group)]
            init = tuple(hc[r0:r0 + SUBLANES, lane_tile(c0 + k)] for k in range(group)) + tuple(
                hc[r0:r0 + SUBLANES, lane_tile(half_tiles + c0 + k)] for k in range(group))

            def step(t, carry, r0=r0, c0=c0, ar=ar, ai=ai):
                row = pl.multiple_of(t * rows + r0, SUBLANES)
                out_r, out_i = [], []
                for k in range(group):
                    hr, hi = carry[k], carry[group + k]
                    xr = buf[c0 + k, pl.ds(row, SUBLANES), :]
                    xi = buf[half_tiles + c0 + k, pl.ds(row, SUBLANES), :]
                    nr = ar[k] * hr - ai[k] * hi + xr
                    ni = ar[k] * hi + ai[k] * hr + xi
                    buf[c0 + k, pl.ds(row, SUBLANES), :] = nr
                    buf[half_tiles + c0 + k, pl.ds(row, SUBLANES), :] = ni
                    out_r.append(nr)
                    out_i.append(ni)
                return tuple(out_r) + tuple(out_i)

            fin = lax.fori_loop(0, tt, step, init, unroll=min(tt, 8))
            for k in range(group):
                hc[r0:r0 + SUBLANES, lane_tile(c0 + k)] = fin[k]
                hc[r0:r0 + SUBLANES, lane_tile(half_tiles + c0 + k)] = fin[group + k]

    if batched:
        parts = []
        for hf in range(2):
            for c in range(ntile):
                for t in range(tt):
                    tmp[c, pl.ds(t, nb, stride=tt), :] = buf[c, t * rows + hf * nb:t * rows + (hf + 1) * nb, :]
            hs = jnp.concatenate([tmp[c] for c in range(ntile)], axis=1).astype(BF16)
            parts.append(jnp.dot(hs, cmat_ref[hf], preferred_element_type=F32))
        y_all = jnp.concatenate(parts, axis=1) + d_ref[...] * u_all
        y_ref[...] = y_all.reshape(nb, tt, SSM_WIDTH)
    else:
        parts = []
        for hf in range(2):
            hs = jnp.concatenate(
                [jnp.concatenate([buf[c, pl.ds(hf * nb + b, tt, stride=rows), :] for c in range(ntile)],
                                 axis=1).astype(BF16) for b in range(nb)], axis=0)
            parts.append(jnp.dot(hs, cmat_ref[hf], preferred_element_type=F32))
        y_all = jnp.concatenate(parts, axis=1) + d_ref[...] * u_ref[...].reshape(nb * tt, SSM_WIDTH)
        y_ref[...] = y_all.reshape(nb, tt, SSM_WIDTH)

    ht_ref[...] = hc[...]


def _s5(u3, h0, bmat, cmat, a_re, a_im, d_skip):
    nb, t_len, _ = u3.shape
    tt = min(t_len, 256)
    rows = 2 * nb
    batched = tt < 16
    kern = functools.partial(_s5_kernel, nb=nb, tt=tt, batched=batched)
    ntile = 2 * STATE_HALF // LANES
    tmp_shape = (ntile, nb * tt, LANES) if batched else (1, SUBLANES, LANES)
    return pl.pallas_call(
        kern,
        grid=(t_len // tt,),
        in_specs=[pl.BlockSpec((nb, tt, SSM_WIDTH), lambda i: (0, i, 0)),
                  _full((2, SSM_HALF, 2 * STATE_HALF)),
                  _full((2, 2 * STATE_HALF, SSM_HALF)),
                  _full((rows, STATE_HALF)), _full((rows, STATE_HALF)),
                  _full((rows, 2 * STATE_HALF)), _full((1, SSM_WIDTH))],
        out_specs=[pl.BlockSpec((nb, tt, SSM_WIDTH), lambda i: (0, i, 0)),
                   _full((rows, 2 * STATE_HALF))],
        out_shape=[jax.ShapeDtypeStruct((nb, t_len, SSM_WIDTH), F32),
                   jax.ShapeDtypeStruct((rows, 2 * STATE_HALF), F32)],
        scratch_shapes=[pltpu.VMEM((ntile, tt * rows, LANES), F32),
                        pltpu.VMEM((rows, 2 * STATE_HALF), F32),
                        pltpu.VMEM(tmp_shape, F32)],
        compiler_params=pltpu.CompilerParams(dimension_semantics=("arbitrary",),
                                             vmem_limit_bytes=VMEM_LIMIT),
        name="s5_scan",
    )(u3, bmat, cmat, a_re, a_im, h0, d_skip.reshape(1, SSM_WIDTH))


def _s5_params(a_re, a_im, b_re, b_im, c_re, c_im, log_dt):
    dt = jnp.exp(log_dt)[:, None]
    mag = jnp.exp(dt * a_re)
    ang = dt * a_im
    ab_re, ab_im = mag * jnp.cos(ang), mag * jnp.sin(ang)
    den = a_re * a_re + a_im * a_im
    nr, ni = ab_re - 1.0, ab_im
    f_re = (nr * a_re + ni * a_im) / den
    f_im = (ni * a_re - nr * a_im) / den
    bb_re = f_re[..., None] * b_re - f_im[..., None] * b_im
    bb_im = f_re[..., None] * b_im + f_im[..., None] * b_re
    gh = SSM_GROUPS // 2
    eye = jnp.eye(gh, dtype=F32)

    def b_half(w):
        return jnp.einsum('gnc,gh->gchn', w, eye).reshape(gh * SSM_GROUP, gh * SSM_STATE)

    def c_half(w):
        return jnp.einsum('gcn,gh->gnhc', w, eye).reshape(gh * SSM_STATE, gh * SSM_GROUP)

    bmat = jnp.stack([jnp.concatenate([b_half(bb_re[h * gh:(h + 1) * gh]),
                                       b_half(bb_im[h * gh:(h + 1) * gh])], axis=1)
                      for h in range(2)]).astype(BF16)
    cmat = jnp.stack([jnp.concatenate([c_half(c_re[h * gh:(h + 1) * gh]),
                                       -c_half(c_im[h * gh:(h + 1) * gh])], axis=0)
                      for h in range(2)]).astype(BF16)
    return bmat, cmat, ab_re.reshape(2, 1, STATE_HALF), ab_im.reshape(2, 1, STATE_HALF)


def _state_to_rows(h_re, h_im):
    nb = h_re.shape[0]
    f = lambda h: h.reshape(nb, 2, STATE_HALF).transpose(1, 0, 2).reshape(2 * nb, STATE_HALF)
    return jnp.concatenate([f(h_re), f(h_im)], axis=1)


def _rows_to_state(ht, nb):
    f = lambda h: h.reshape(2, nb, STATE_HALF).transpose(1, 0, 2).reshape(nb, SSM_GROUPS, SSM_STATE)
    return f(ht[:, :STATE_HALF]), f(ht[:, STATE_HALF:])


def _attn_prompt_kernel(q_ref, kp_ref, kc_ref, vp_ref, vc_ref, o_ref, l_ref, bias_s, s_s, p_s, *, dil):
    blk = ATTN_BLOCK
    n = pl.program_id(2)

    @pl.when((pl.program_id(0) == 0) & (pl.program_id(1) == 0) & (n == 0))
    def _():
        i_idx = lax.broadcasted_iota(I32, (blk, 2 * blk), 0)
        j_idx = lax.broadcasted_iota(I32, (blk, 2 * blk), 1)
        delta = i_idx - j_idx + blk
        in_band = (delta >= 0) & (delta <= blk)
        dist = (delta * dil).astype(F32)
        for h in range(N_HEADS):
            biased = -ALIBI_SLOPES[h] * dist
            bias_s[0, h] = jnp.where(in_band & (j_idx >= blk), biased, MASK_VALUE)
            bias_s[1, h] = jnp.where(in_band, biased, MASK_VALUE)

    lane = lax.broadcasted_iota(I32, (blk, LANES), 1)
    nqb = q_ref.shape[0] // blk
    nres = q_ref.shape[1] // ATTN_WIDTH
    units = [(res, qb) for res in range(nres) for qb in range(nqb)]
    for u, (res, qb) in enumerate(units):
        which = jnp.minimum(n, 1) if qb == 0 else 1
        rows = slice(qb * blk, (qb + 1) * blk)
        for hp in range(N_HEADS // 2):
            cols = slice(res * ATTN_WIDTH + hp * LANES, res * ATTN_WIDTH + (hp + 1) * LANES)
            q2 = q_ref[rows, cols]
            before = kp_ref[:, cols] if qb == 0 else kc_ref[(qb - 1) * blk:qb * blk, cols]
            kk = jnp.concatenate([before, kc_ref[rows, cols]], axis=0)
            for half in range(2):
                h = 2 * hp + half
                in_head = (lane >= half * HEAD_DIM) & (lane < (half + 1) * HEAD_DIM)
                qm = jnp.where(in_head, q2, jnp.zeros_like(q2))
                s = lax.dot_general(qm, kk, (((1,), (1,)), ((), ())), preferred_element_type=F32)
                s_s[u * N_HEADS + h] = s + bias_s[which, h]
    for u, (res, qb) in enumerate(units):
        lse_all = jnp.zeros((blk, LANES), F32)
        for h in range(N_HEADS):
            s = s_s[u * N_HEADS + h]
            m = jnp.max(s, axis=1, keepdims=True)
            p = jnp.exp(s - m)
            l = jnp.sum(p, axis=1, keepdims=True)
            p_s[u * N_HEADS + h] = (p * (1.0 / l)).astype(BF16)
            lse_all = jnp.where(lane == h, m + jnp.log(l), lse_all)
        l_ref[qb * blk:(qb + 1) * blk, res * LANES:(res + 1) * LANES] = lse_all
    for u, (res, qb) in enumerate(units):
        rows = slice(qb * blk, (qb + 1) * blk)
        for hp in range(N_HEADS // 2):
            cols = slice(res * ATTN_WIDTH + hp * LANES, res * ATTN_WIDTH + (hp + 1) * LANES)
            before = vp_ref[:, cols] if qb == 0 else vc_ref[(qb - 1) * blk:qb * blk, cols]
            vv = jnp.concatenate([before, vc_ref[rows, cols]], axis=0)
            outs = [jnp.dot(p_s[u * N_HEADS + 2 * hp + half], vv, preferred_element_type=F32)
                    for half in range(2)]
            o_ref[rows, cols] = jnp.where(lane < HEAD_DIM, outs[0], outs[1])


def _attn_prompt_branch(qb, kb, vb, bsz, s_len, dil):
    sub = s_len // dil
    nqb = min(ATTN_STEP_BLOCKS, sub // ATTN_BLOCK)
    nres = min(ATTN_STEP_BLOCKS // nqb, dil)
    step = nqb * ATTN_BLOCK
    nstep = sub // step
    nunit = nres * nqb
    view = lambda t: t.reshape(bsz, sub, dil * ATTN_WIDTH)
    cur = pl.BlockSpec((None, step, nres * ATTN_WIDTH), lambda b, r, n: (b, n, r))
    prev = pl.BlockSpec((None, ATTN_BLOCK, nres * ATTN_WIDTH),
                        lambda b, r, n: (b, jnp.maximum(n * nqb - 1, 0), r))
    o, lse = pl.pallas_call(
        functools.partial(_attn_prompt_kernel, dil=dil),
        grid=(bsz, dil // nres, nstep),
        in_specs=[cur, prev, cur, prev, cur],
        out_specs=[cur, pl.BlockSpec((None, step, nres * LANES), lambda b, r, n: (b, n, r))],
        out_shape=[jax.ShapeDtypeStruct((bsz, sub, dil * ATTN_WIDTH), F32),
                   jax.ShapeDtypeStruct((bsz, sub, dil * LANES), F32)],
        scratch_shapes=[pltpu.VMEM((2, N_HEADS, ATTN_BLOCK, 2 * ATTN_BLOCK), F32),
                        pltpu.VMEM((nunit * N_HEADS, ATTN_BLOCK, 2 * ATTN_BLOCK), F32),
                        pltpu.VMEM((nunit * N_HEADS, ATTN_BLOCK, 2 * ATTN_BLOCK), BF16)],
        compiler_params=pltpu.CompilerParams(
            dimension_semantics=("arbitrary", "arbitrary", "arbitrary"),
            vmem_limit_bytes=VMEM_LIMIT),
        name=f"attn_prompt_d{dil}",
    )(view(qb), view(kb), view(kb), view(vb), view(vb))
    return o.reshape(bsz * sub, dil * ATTN_WIDTH), lse.reshape(bsz * sub, dil * LANES)


def _attn_sample_kernel(q_ref, kn_ref, vn_ref, kc_ref, vc_ref, o_ref, bias_s, mult_s, *, t_len, wb):
    nrow = N_HEADS * t_len
    t_shift = t_len.bit_length() - 1
    d_shift = HEAD_DIM.bit_length() - 1
    nt = (((1,), (1,)), ((), ()))

    def branch_count(dist):
        mult = jnp.zeros(dist.shape, F32)
        for win, dil in zip(WINDOWS, DILATIONS):
            hit = (dist >= 0) & (dist <= win) & ((dist & (dil - 1)) == 0)
            mult = mult + jnp.where(hit, 1.0, 0.0)
        return mult

    def biased(dist, mult):
        head = lax.broadcasted_iota(I32, dist.shape, 0) >> t_shift
        slope = jnp.zeros(dist.shape, F32)
        for h in range(N_HEADS):
            slope = jnp.where(head == h, ALIBI_SLOPES[h], slope)
        return jnp.where(mult > 0.0, -slope * dist.astype(F32), MASK_VALUE)

    @pl.when(pl.program_id(0) == 0)
    def _():
        row = lax.broadcasted_iota(I32, (nrow, wb), 0)
        col = lax.broadcasted_iota(I32, (nrow, wb), 1)
        dist = wb + (row & (t_len - 1)) - col
        mult = branch_count(dist)
        mult_s[...] = mult
        bias_s[...] = biased(dist, mult)

    q = q_ref[...].astype(F32)
    qt = jnp.concatenate([q] * N_HEADS, axis=0)
    row_w = lax.broadcasted_iota(I32, (nrow, ATTN_WIDTH), 0)
    lane_w = lax.broadcasted_iota(I32, (nrow, ATTN_WIDTH), 1)
    qm = jnp.where((lane_w >> d_shift) == (row_w >> t_shift), qt, 0.0).astype(BF16)

    pad = LANES - t_len
    kn = jnp.concatenate([kn_ref[...].astype(F32), jnp.zeros((pad, ATTN_WIDTH), F32)], axis=0).astype(BF16)
    vn = jnp.concatenate([vn_ref[...].astype(F32), jnp.zeros((pad, ATTN_WIDTH), F32)], axis=0).astype(BF16)
    row_n = lax.broadcasted_iota(I32, (nrow, LANES), 0)
    col_n = lax.broadcasted_iota(I32, (nrow, LANES), 1)
    dist_n = jnp.where(col_n < t_len, (row_n & (t_len - 1)) - col_n, -1)
    mult_n = branch_count(dist_n)
    s_n = lax.dot_general(qm, kn, nt, preferred_element_type=F32) + biased(dist_n, mult_n)
    s_c = jnp.dot(qm, kc_ref[...].astype(BF16), preferred_element_type=F32) + bias_s[...]

    m = jnp.maximum(jnp.max(s_c, axis=1, keepdims=True), jnp.max(s_n, axis=1, keepdims=True))
    p_c = jnp.exp(s_c - m) * mult_s[...]
    p_n = jnp.exp(s_n - m) * mult_n
    l = jnp.sum(p_c, axis=1, keepdims=True) + jnp.sum(p_n, axis=1, keepdims=True)
    o = (lax.dot_general(p_c.astype(BF16), vc_ref[...].astype(BF16), nt, preferred_element_type=F32)
         + jnp.dot(p_n.astype(BF16), vn, preferred_element_type=F32)) / l
    lane_o = lax.broadcasted_iota(I32, (t_len, ATTN_WIDTH), 1) >> d_shift
    out = jnp.zeros((t_len, ATTN_WIDTH), F32)
    for h in range(N_HEADS):
        out = jnp.where(lane_o == h, o[h * t_len:(h + 1) * t_len], out)
    o_ref[...] = out


def _attn_sample(qb, kb, vb, cache_k, cache_v):
    bsz, t_len, _ = qb.shape
    wb = cache_k.shape[1]
    feature_major = lambda c: jnp.transpose(c, (0, 2, 3, 1)).reshape(bsz, ATTN_WIDTH, wb)
    new = pl.BlockSpec((None, t_len, ATTN_WIDTH), lambda b: (b, 0, 0))
    old = pl.BlockSpec((None, ATTN_WIDTH, wb), lambda b: (b, 0, 0))
    return pl.pallas_call(
        functools.partial(_attn_sample_kernel, t_len=t_len, wb=wb),
        grid=(bsz,),
        in_specs=[new, new, new, old, old],
        out_specs=new,
        out_shape=jax.ShapeDtypeStruct((bsz, t_len, ATTN_WIDTH), F32),
        scratch_shapes=[pltpu.VMEM((N_HEADS * t_len, wb), F32),
                        pltpu.VMEM((N_HEADS * t_len, wb), F32)],
        compiler_params=pltpu.CompilerParams(dimension_semantics=("arbitrary",),
                                             vmem_limit_bytes=VMEM_LIMIT),
        name="attn_sample",
    )(qb, kb, vb, feature_major(cache_k), feature_major(cache_v))


def _mid_kernel(*refs, dils):
    n_branch = max(len(dils), 1)
    x_ref, y_ref = refs[0], refs[1]
    o_refs = refs[2:2 + n_branch]
    pos = 2 + n_branch
    l_refs = refs[pos:pos + len(dils)]
    pos += len(l_refs)
    (wglu_ref, bglu_ref, lns_ref, lna_ref, wout_ref, lnm_ref, wrh_ref, wrl_ref, br_ref,
     expand_ref, tri_ref,
     x1_ref, hrow_ref, eidx_ref, epos_ref, egate_ref, cnt_ref, carry, nat) = refs[pos:]
    tm = x_ref.shape[0]
    o_tiles = ATTN_WIDTH // LANES

    rb = tm

    for bi, d in enumerate(dils):
        if d == 1:
            continue
        for r in range(d):
            spread = pl.ds(r, tm // d, stride=d)
            for ct in range(o_tiles):
                c0 = r * ATTN_WIDTH + ct * LANES
                nat[bi, ct, spread, :] = o_refs[bi][:, c0:c0 + LANES]
            nat[bi, o_tiles, spread, :] = l_refs[bi][:, r * LANES:(r + 1) * LANES]

    def natural(bi, d, rows):
        if d == 1:
            return o_refs[bi][rows, :], l_refs[bi][rows, :]
        return (jnp.concatenate([nat[bi, ct, rows, :] for ct in range(o_tiles)], axis=1),
                nat[bi, o_tiles, rows, :])

    @pl.when(pl.program_id(0) == 0)
    def _():
        carry[...] = jnp.zeros_like(carry)

    lane = lax.broadcasted_iota(I32, (rb, LANES), 1)
    lane_f = lane.astype(F32)
    run = carry[0:1, :]
    for blk in range(tm // rb):
        rows = slice(blk * rb, (blk + 1) * rb)
        y = y_ref[rows, :]
        z = y * (0.5 * (1.0 + jnp.tanh(math.sqrt(2.0 / math.pi) * (y + 0.044715 * (y * y * y)))))
        glu = z * _sigmoid(jnp.dot(z.astype(BF16), wglu_ref[...], preferred_element_type=F32)
                           + bglu_ref[...])
        n_ssm = _rms(glu, lns_ref[...])

        if not dils:
            attn = o_refs[0][rows, :]
        else:
            pairs = [natural(bi, d, rows) for bi, d in enumerate(dils)]
            lses = [p[1] for p in pairs]
            mx = functools.reduce(jnp.maximum, lses)
            es = [jnp.exp(l - mx) for l in lses]
            inv = 1.0 / functools.reduce(lambda a, b: a + b, es)
            attn = jnp.zeros((rb, ATTN_WIDTH), F32)
            for e, (o_nat, _) in zip(es, pairs):
                w = e * inv
                w_hi = w.astype(BF16)
                w_lo = (w - w_hi.astype(F32)).astype(BF16)
                wide = jnp.dot(jnp.concatenate([w_hi, w_lo], axis=1), expand_ref[...],
                               preferred_element_type=F32)
                attn = attn + wide * o_nat
        n_attn = _rms(attn, lna_ref[...])

        x1 = (x_ref[rows, :]
              + jnp.dot(n_ssm.astype(BF16), wout_ref[:SSM_WIDTH, :], preferred_element_type=F32)
              + jnp.dot(n_attn.astype(BF16), wout_ref[SSM_WIDTH:, :], preferred_element_type=F32))
        x1_ref[rows, :] = x1
        hm = _rms(x1, lnm_ref[...])
        for s in range(ROW_TILES):
            hrow_ref[pl.ds(blk * rb * ROW_TILES + s, rb, stride=ROW_TILES), :] = hm[:, s * LANES:(s + 1) * LANES]

        h_hi = hm.astype(BF16)
        h_lo = (hm - h_hi.astype(F32)).astype(BF16)
        both = jnp.dot(h_hi, wrl_ref[...], preferred_element_type=F32)
        logits = (both[:, :LANES] + both[:, LANES:]
                  + jnp.dot(h_lo, wrh_ref[...], preferred_element_type=F32)
                  + br_ref[...])
        work = jnp.where(lane < N_EXPERTS, logits, NEG_BIG)
        vals, idxs, hots = [], [], []
        for _ in range(TOP_K):
            m = jnp.max(work, axis=1, keepdims=True)
            idx = jnp.min(jnp.where(work == m, lane_f, float(LANES)), axis=1, keepdims=True)
            hot = lane_f == idx
            vals.append(m)
            idxs.append(idx)
            hots.append(hot)
            work = jnp.where(hot, NEG_BIG, work)
        exps = [jnp.exp(v - vals[0]) for v in vals]
        inv = 1.0 / functools.reduce(lambda a, b: a + b, exps)

        sel = functools.reduce(lambda a, b: a + b, [h.astype(F32) for h in hots])
        before = jnp.dot(tri_ref[...], sel.astype(BF16), preferred_element_type=F32) + run
        eidx = jnp.zeros((rb, LANES), I32)
        epos = jnp.zeros((rb, LANES), I32)
        egate = jnp.zeros((rb, LANES), F32)
        for k in range(TOP_K):
            pk = jnp.sum(jnp.where(hots[k], before, 0.0), axis=1, keepdims=True)
            eidx = jnp.where(lane == k, idxs[k].astype(I32), eidx)
            epos = jnp.where(lane == k, pk.astype(I32), epos)
            egate = jnp.where(lane == k, exps[k] * inv, egate)
        eidx_ref[rows, :] = eidx
        epos_ref[rows, :] = epos
        egate_ref[rows, :] = egate
        run = run + jnp.sum(sel, axis=0, keepdims=True)
    carry[...] = jnp.broadcast_to(run, carry.shape)
    cnt_ref[...] = jnp.broadcast_to(run, cnt_ref.shape).astype(I32)


def _mid(x2d, y2d, attn_o, attn_lse, dils, w):
    n = x2d.shape[0]
    tm = min(n, MID_TILE)
    rb = tm
    ti = jnp.arange(rb)
    tri = (ti[:, None] > ti[None, :]).astype(BF16)
    n_branch = len(attn_o)
    row = lambda width: pl.BlockSpec((tm, width), lambda i: (i, 0))
    packed = lambda d, width: pl.BlockSpec((tm // d, d * width), lambda i: (i, 0))
    attn_specs = ([packed(d, ATTN_WIDTH) for d in dils] + [packed(d, LANES) for d in dils]
                  if dils else [row(ATTN_WIDTH)])
    in_specs = ([row(D_MODEL), row(SSM_WIDTH)] + attn_specs
                + [_full((SSM_WIDTH, SSM_WIDTH)), _full((1, SSM_WIDTH)), _full((1, SSM_WIDTH)),
                   _full((1, ATTN_WIDTH)), _full((D_MODEL, D_MODEL)), _full((1, D_MODEL)),
                   _full((D_MODEL, LANES)), _full((D_MODEL, 2 * LANES)), _full((1, LANES)),
                   _full((2 * LANES, ATTN_WIDTH)), _full((rb, rb))])
    out_specs = [row(D_MODEL), pl.BlockSpec((tm * ROW_TILES, LANES), lambda i: (i, 0)),
                 row(LANES), row(LANES), row(LANES), _full((SUBLANES, LANES))]
    out_shape = [jax.ShapeDtypeStruct((n, D_MODEL), F32),
                 jax.ShapeDtypeStruct((n * ROW_TILES, LANES), F32),
                 jax.ShapeDtypeStruct((n, LANES), I32),
                 jax.ShapeDtypeStruct((n, LANES), I32),
                 jax.ShapeDtypeStruct((n, LANES), F32),
                 jax.ShapeDtypeStruct((SUBLANES, LANES), I32)]
    return pl.pallas_call(
        functools.partial(_mid_kernel, dils=tuple(dils)),
        grid=(n // tm,),
        in_specs=in_specs,
        out_specs=out_specs,
        out_shape=out_shape,
        scratch_shapes=[pltpu.VMEM((SUBLANES, LANES), F32),
                        pltpu.VMEM((n_branch, ATTN_WIDTH // LANES + 1, tm, LANES), F32)],
        compiler_params=pltpu.CompilerParams(dimension_semantics=("arbitrary",),
                                             vmem_limit_bytes=VMEM_LIMIT),
        name="mid",
    )(x2d, y2d, *attn_o, *attn_lse, w["w_glu"], w["b_glu"], w["ln_ssm_out"], w["ln_attn_out"],
      w["w_out"], w["ln_moe"], w["wr_hi"], w["wr_lo"], w["b_router"], w["expand"], tri)


def _dispatch_kernel(dest_ref, hp_ref, hs_ref, xs_hbm, sem, *, n_prompt_tiles):
    i = pl.program_id(0)
    npair = TOKEN_TILE * TOP_K

    def run(src_ref):
        def issue(t, c):
            src = src_ref.at[pl.ds(pl.multiple_of(t * ROW_TILES, ROW_TILES), ROW_TILES), :]
            for k in range(TOP_K):
                d = dest_ref[0, t * TOP_K + k]
                pltpu.make_async_copy(
                    src, xs_hbm.at[pl.ds(pl.multiple_of(d * ROW_TILES, ROW_TILES), ROW_TILES), :],
                    sem).start(priority=k % 2)
            return c

        lax.fori_loop(0, TOKEN_TILE, issue, 0, unroll=2)
        span = pl.ds(0, npair * ROW_TILES)
        pltpu.make_async_copy(xs_hbm.at[span, :], xs_hbm.at[span, :], sem).wait()

    @pl.when(i < n_prompt_tiles)
    def _():
        run(hp_ref)

    @pl.when(i >= n_prompt_tiles)
    def _():
        run(hs_ref)


def _dispatch(dest, hrow_p, hrow_s):
    n_p = hrow_p.shape[0] // ROW_TILES
    n_s = hrow_s.shape[0] // ROW_TILES
    npt = n_p // TOKEN_TILE
    ntile = (n_p + n_s) // TOKEN_TILE
    npair = TOKEN_TILE * TOP_K
    blk = (TOKEN_TILE * ROW_TILES, LANES)
    return pl.pallas_call(
        functools.partial(_dispatch_kernel, n_prompt_tiles=npt),
        grid=(ntile,),
        in_specs=[pl.BlockSpec((None, 1, npair), lambda i: (i, 0, 0), memory_space=pltpu.SMEM),
                  pl.BlockSpec(blk, lambda i: (jnp.minimum(i, npt - 1), 0)),
                  pl.BlockSpec(blk, lambda i: (jnp.maximum(i - npt, 0), 0))],
        out_specs=pl.BlockSpec(memory_space=pl.ANY),
        out_shape=jax.ShapeDtypeStruct(((n_p + n_s) * TOP_K * ROW_TILES, LANES), F32),
        scratch_shapes=[pltpu.SemaphoreType.DMA(())],
        compiler_params=pltpu.CompilerParams(dimension_semantics=("arbitrary",),
                                             vmem_limit_bytes=VMEM_LIMIT),
        name="moe_dispatch",
    )(dest.reshape(ntile, 1, npair), hrow_p, hrow_s)


def _expert_kernel(vt_ref, ve_ref, vok_ref, vnext_ref, vmore_ref, vslot_ref, gs_ref,
                   xs_ref, wu_hbm, bu_ref, wd_hbm, bd_ref,
                   out_ref, wu_s, wd_s, x_s, wu_f, wd_f, wsem):
    v = pl.program_id(0)
    e = ve_ref[v]
    j = vt_ref[v]
    vprev = jnp.maximum(v - 1, 0)
    new_e = (v == 0) | (e != ve_ref[vprev])
    new_j = (v == 0) | (j != vt_ref[vprev])
    tm = MOE_TILE
    slot = vslot_ref[v]

    def weight_copies(expert, to):
        return (pltpu.make_async_copy(wu_hbm.at[expert], wu_f.at[to], wsem.at[0, to]),
                pltpu.make_async_copy(wd_hbm.at[expert], wd_f.at[to], wsem.at[1, to]))

    @pl.when(v == 0)
    def _():
        for c in weight_copies(e, slot):
            c.start()

    @pl.when(new_e)
    def _():
        for c in weight_copies(e, slot):
            c.wait()

        @pl.when(vmore_ref[v] == 1)
        def _():
            for c in weight_copies(vnext_ref[v], 1 - slot):
                c.start()

        wu_s[...] = wu_f[slot].astype(BF16)
        wd_s[...] = wd_f[slot].astype(BF16)

    @pl.when(new_j)
    def _():
        out_ref[...] = jnp.zeros_like(out_ref)

    lo, hi = gs_ref[e], gs_ref[e + 1]

    def run_pass(row0, nrows):
        base = row0 * ROW_TILES
        for s in range(ROW_TILES):
            x_s[:nrows, s * LANES:(s + 1) * LANES] = xs_ref[
                pl.ds(base + s, nrows, stride=ROW_TILES), :].astype(BF16)
        a = jnp.dot(x_s[:nrows, :], wu_s[...], preferred_element_type=F32) + bu_ref[...]
        g = jnp.minimum(a[:, :EXPERT_FF], SWIGLU_LIMIT)
        lin = jnp.clip(a[:, EXPERT_FF:], -SWIGLU_LIMIT, SWIGLU_LIMIT)
        act = (lin + 1.0) * (g * _sigmoid(SWIGLU_ALPHA * g))
        y = jnp.dot(act.astype(BF16), wd_s[...], preferred_element_type=F32) + bd_ref[...]
        rows = j * tm + row0 + lax.broadcasted_iota(I32, (nrows, 1), 0)
        mine = (rows >= lo) & (rows < hi)
        for s in range(ROW_TILES):
            cur = out_ref[pl.ds(base + s, nrows, stride=ROW_TILES), :]
            out_ref[pl.ds(base + s, nrows, stride=ROW_TILES), :] = jnp.where(
                mine, y[:, s * LANES:(s + 1) * LANES], cur)

    sub = x_s.shape[0]
    half = sub // 2
    live = vok_ref[v] == 1
    for part in range(tm // sub):
        row0 = part * sub
        start = j * tm + row0
        in_first = live & (lo < start + half) & (hi > start)
        in_second = live & (lo < start + sub) & (hi > start + half)
        pl.when(in_first & in_second)(functools.partial(run_pass, row0, sub))
        pl.when(in_first & jnp.logical_not(in_second))(functools.partial(run_pass, row0, half))
        pl.when(in_second & jnp.logical_not(in_first))(functools.partial(run_pass, row0 + half, half))


def _experts(xs, visits, gstart, w_up, b_up, w_down, b_down):
    tm = MOE_TILE
    nvisit = visits[0].shape[0]
    rows = pl.BlockSpec((tm * ROW_TILES, LANES), lambda v, vt, *_: (vt[v], 0))
    per_e = lambda a, b: pl.BlockSpec((None, a, b), lambda v, vt, ve, *_: (ve[v], 0, 0))
    grid_spec = pltpu.PrefetchScalarGridSpec(
        num_scalar_prefetch=7,
        grid=(nvisit,),
        in_specs=[rows, pl.BlockSpec(memory_space=pl.ANY), per_e(1, 2 * EXPERT_FF),
                  pl.BlockSpec(memory_space=pl.ANY), per_e(1, D_MODEL)],
        out_specs=rows,
        scratch_shapes=[pltpu.VMEM((D_MODEL, 2 * EXPERT_FF), BF16),
                        pltpu.VMEM((EXPERT_FF, D_MODEL), BF16),
                        pltpu.VMEM((MOE_SUBTILE, D_MODEL), BF16),
                        pltpu.VMEM((2, D_MODEL, 2 * EXPERT_FF), F32),
                        pltpu.VMEM((2, EXPERT_FF, D_MODEL), F32),
                        pltpu.SemaphoreType.DMA((2, 2))],
    )
    return pl.pallas_call(
        _expert_kernel,
        grid_spec=grid_spec,
        out_shape=jax.ShapeDtypeStruct(xs.shape, F32),
        compiler_params=pltpu.CompilerParams(dimension_semantics=("arbitrary",),
                                             vmem_limit_bytes=VMEM_LIMIT),
        name="moe_experts",
    )(*visits, gstart, xs, w_up, b_up.reshape(N_EXPERTS, 1, 2 * EXPERT_FF),
      w_down, b_down.reshape(N_EXPERTS, 1, D_MODEL))


def _routing(eidx_p, epos_p, cnt_p, eidx_s, epos_s, cnt_s, n_rows):
    cnt_p = cnt_p[0, :N_EXPERTS]
    cnt_s = cnt_s[0, :N_EXPERTS]
    cnt = cnt_p + cnt_s
    gend = jnp.cumsum(cnt)
    gstart = gend - cnt
    experts = jnp.arange(N_EXPERTS, dtype=I32)

    def lookup(table, idx):
        return jnp.sum(jnp.where(idx[..., None] == experts, table, 0), axis=-1)

    ep = eidx_p[:, :TOP_K]
    es = eidx_s[:, :TOP_K]
    dest_p = lookup(gstart, ep) + epos_p[:, :TOP_K]
    dest_s = lookup(gstart + cnt_p, es) + epos_s[:, :TOP_K]
    ntile = n_rows // MOE_TILE
    nvisit = ntile + N_EXPERTS
    first = gstart // MOE_TILE
    last = jnp.maximum(gend - 1, 0) // MOE_TILE
    nv = jnp.where(cnt > 0, last - first + 1, 0)
    vend = jnp.cumsum(nv)
    vstart = vend - nv
    total = vend[-1]
    v = jnp.arange(nvisit, dtype=I32)
    vc = jnp.minimum(v, total - 1)
    ve = jnp.sum((vend[None, :] <= vc[:, None]).astype(I32), axis=1)
    vt = (lookup(first - vstart, ve) + vc).astype(I32)
    vok = (v < total).astype(I32)
    has_rows = cnt > 0
    later = (experts[None, :] > experts[:, None]) & has_rows[None, :]
    nxt = jnp.min(jnp.where(later, experts[None, :], N_EXPERTS), axis=1)
    more = nxt < N_EXPERTS
    slot = (jnp.cumsum(has_rows.astype(I32)) - 1) % 2
    visits = (vt, ve, vok, lookup(jnp.where(more, nxt, 0), ve).astype(I32),
              lookup(more.astype(I32), ve).astype(I32), lookup(slot, ve).astype(I32))
    gs = jnp.concatenate([gstart, gend[-1:]]).astype(I32)
    return dest_p.astype(I32), dest_s.astype(I32), visits, gs


def _out_kernel(dest_ref, next_ref, y_hbm, x1_ref, gate_ref, pe_ref, lnp_ref, wg_ref, bg_ref, wp_ref,
                lnf_ref, o_ref, buf, sem, *, ntile):
    tm = TOKEN_TILE
    npair = tm * TOP_K
    i = pl.program_id(0)
    slot = i % 2

    def gather(idx_ref, to):
        def issue(t, c):
            for k in range(TOP_K):
                d = idx_ref[0, t * TOP_K + k]
                pltpu.make_async_copy(
                    y_hbm.at[pl.ds(pl.multiple_of(d * ROW_TILES, ROW_TILES), ROW_TILES), :],
                    buf.at[to, pl.ds(pl.multiple_of((k * tm + t) * ROW_TILES, ROW_TILES), ROW_TILES), :],
                    sem.at[to]).start(priority=k % 2)
            return c

        lax.fori_loop(0, tm, issue, 0, unroll=2)

    @pl.when(i == 0)
    def _():
        gather(dest_ref, 0)

    if ntile > 1:
        @pl.when(i + 1 < ntile)
        def _():
            gather(next_ref, 1 - slot)

    pltpu.make_async_copy(y_hbm.at[pl.ds(0, npair * ROW_TILES), :], buf.at[slot], sem.at[slot]).wait()

    gates = gate_ref[...]
    parts = []
    for s in range(ROW_TILES):
        acc = jnp.zeros((tm, LANES), F32)
        for k in range(TOP_K):
            rows = buf[slot, pl.ds(k * tm * ROW_TILES + s, tm, stride=ROW_TILES), :]
            acc = acc + gates[:, k:k + 1] * rows
        parts.append(acc)
    x2 = x1_ref[...] + jnp.concatenate(parts, axis=1)
    gate = _sigmoid(jnp.dot(_rms(x2, lnp_ref[...]).astype(BF16), wg_ref[...],
                            preferred_element_type=F32) + bg_ref[...])
    x3 = x2 + gate * jnp.dot(pe_ref[...].astype(BF16), wp_ref[...], preferred_element_type=F32)
    o_ref[...] = _rms(x3, lnf_ref[...])


def _combine(dest, y_rows, x1, egate, pe, w):
    n = x1.shape[0]
    tm = TOKEN_TILE
    npair = tm * TOP_K
    row = lambda width: pl.BlockSpec((tm, width), lambda i: (i, 0))
    ntile = n // tm
    dest3 = dest.reshape(ntile, 1, npair)
    return pl.pallas_call(
        functools.partial(_out_kernel, ntile=ntile),
        grid=(ntile,),
        in_specs=[pl.BlockSpec((None, 1, npair), lambda i: (i, 0, 0), memory_space=pltpu.SMEM),
                  pl.BlockSpec((None, 1, npair), lambda i: (jnp.minimum(i + 1, ntile - 1), 0, 0),
                               memory_space=pltpu.SMEM),
                  pl.BlockSpec(memory_space=pl.ANY),
                  row(D_MODEL), row(LANES), row(PLE_DIM),
                  _full((1, D_MODEL)), _full((D_MODEL, D_MODEL)), _full((1, D_MODEL)),
                  _full((PLE_DIM, D_MODEL)), _full((1, D_MODEL))],
        out_specs=row(D_MODEL),
        out_shape=jax.ShapeDtypeStruct((n, D_MODEL), F32),
        scratch_shapes=[pltpu.VMEM((2, npair * ROW_TILES, LANES), F32), pltpu.SemaphoreType.DMA((2,))],
        compiler_params=pltpu.CompilerParams(dimension_semantics=("arbitrary",),
                                             vmem_limit_bytes=VMEM_LIMIT),
        name="combine_out",
    )(dest3, dest3, y_rows, x1, egate, pe, w["ln_ple"], w["w_ple_gate"],
      w["b_ple_gate"], w["w_ple_proj"], w["ln_final"])


def kernel(x_prompt, x_sample, cache_attn_k, cache_attn_v, state_ssm_re, state_ssm_im, p_prompt, p_sample, ln_mix, w_in, ssm_a_re, ssm_a_im, ssm_b_re, ssm_b_im, ssm_c_re, ssm_c_im, ssm_d, ssm_log_dt, w_glu, b_glu, ln_ssm_out, ln_attn_out, w_out, ln_moe, w_router, b_router, w_up, b_up, w_down, b_down, ln_ple, w_ple_gate, b_ple_gate, w_ple_proj, ln_final):
    bsz, s_len, _ = x_prompt.shape
    dbsz, dt_len, _ = x_sample.shape
    n_p, n_s = bsz * s_len, dbsz * dt_len
    wb = cache_attn_k.shape[2]
    wb_prompt = min(WINDOWS[-1], s_len)

    wr = jnp.pad(w_router[0], ((0, 0), (0, LANES - N_EXPERTS)))
    wr_hi = wr.astype(BF16)
    w = {
        "w_glu": w_glu[0].astype(BF16), "b_glu": b_glu[0].reshape(1, -1),
        "ln_ssm_out": ln_ssm_out[0].reshape(1, -1), "ln_attn_out": ln_attn_out[0].reshape(1, -1),
        "w_out": w_out[0].astype(BF16), "ln_moe": ln_moe[0].reshape(1, -1),
        "wr_hi": wr_hi,
        "wr_lo": jnp.concatenate([wr_hi, (wr - wr_hi.astype(F32)).astype(BF16)], axis=1),
        "b_router": jnp.pad(b_router[0], (0, LANES - N_EXPERTS)).reshape(1, -1),
        "expand": (jnp.arange(2 * LANES)[:, None] % LANES
                   == jnp.arange(ATTN_WIDTH)[None, :] // HEAD_DIM).astype(BF16),
        "ln_ple": ln_ple[0].reshape(1, -1), "w_ple_gate": w_ple_gate[0].astype(BF16),
        "b_ple_gate": b_ple_gate[0].reshape(1, -1), "w_ple_proj": w_ple_proj[0].astype(BF16),
        "ln_final": ln_final.reshape(1, -1),
    }
    w_in_b = w_in[0].astype(BF16)
    bmat, cmat, ab_re, ab_im = _s5_params(ssm_a_re[0], ssm_a_im[0], ssm_b_re[0], ssm_b_im[0],
                                          ssm_c_re[0], ssm_c_im[0], ssm_log_dt[0])

    def coeff(a, nb):
        return jnp.broadcast_to(a, (2, nb, STATE_HALF)).reshape(2 * nb, STATE_HALF)

    proj_p = _in_proj(x_prompt.reshape(n_p, D_MODEL), ln_mix[0], w_in_b, dils=DILATIONS[1:],
                      seq_window=(s_len, wb_prompt))
    u_p, k_p, v_p = proj_p[:3]
    qkv = [proj_p[3:6]] + [proj_p[6 + 3 * i:9 + 3 * i] for i in range(len(DILATIONS) - 1)]
    zeros_state = jnp.zeros((bsz, SSM_GROUPS, SSM_STATE), F32)
    y_p, ht_p = _s5(u_p.reshape(bsz, s_len, SSM_WIDTH), _state_to_rows(zeros_state, zeros_state),
                    bmat, cmat, coeff(ab_re, bsz), coeff(ab_im, bsz), ssm_d[0])
    branches = [_attn_prompt_branch(*qkv[i], bsz, s_len, d) for i, d in enumerate(DILATIONS)]
    x1_p, hrow_p, eidx_p, epos_p, egate_p, cnt_p = _mid(
        x_prompt.reshape(n_p, D_MODEL), y_p.reshape(n_p, SSM_WIDTH),
        [b[0] for b in branches], [b[1] for b in branches], DILATIONS, w)

    u_s, k_s, v_s, qb_s, kb_s, vb_s = _in_proj(x_sample.reshape(n_s, D_MODEL), ln_mix[0], w_in_b)
    y_s, ht_s = _s5(u_s.reshape(dbsz, dt_len, SSM_WIDTH), _state_to_rows(state_ssm_re[0], state_ssm_im[0]),
                    bmat, cmat, coeff(ab_re, dbsz), coeff(ab_im, dbsz), ssm_d[0])
    as3 = lambda t: t.reshape(dbsz, dt_len, ATTN_WIDTH)
    attn_s = _attn_sample(as3(qb_s), as3(kb_s), as3(vb_s), cache_attn_k[0], cache_attn_v[0])
    x1_s, hrow_s, eidx_s, epos_s, egate_s, cnt_s = _mid(
        x_sample.reshape(n_s, D_MODEL), y_s.reshape(n_s, SSM_WIDTH),
        [attn_s.reshape(n_s, ATTN_WIDTH)], [], (), w)

    n_rows = (n_p + n_s) * TOP_K
    dest_p, dest_s, visits, gs = _routing(eidx_p, epos_p, cnt_p, eidx_s, epos_s, cnt_s, n_rows)
    xs = _dispatch(jnp.concatenate([dest_p.reshape(-1), dest_s.reshape(-1)]), hrow_p, hrow_s)
    y_rows = _experts(xs, visits, gs, w_up[0], b_up[0], w_down[0], b_down[0])

    out_p = _combine(dest_p, y_rows, x1_p, egate_p, p_prompt[0].reshape(n_p, PLE_DIM), w)
    out_s = _combine(dest_s, y_rows, x1_s, egate_s, p_sample[0].reshape(n_s, PLE_DIM), w)

    hr_p, hi_p = _rows_to_state(ht_p, bsz)
    hr_s, hi_s = _rows_to_state(ht_s, dbsz)
    kv_p = lambda t: jnp.transpose(t.reshape(bsz, N_HEADS, HEAD_DIM, wb_prompt), (0, 3, 1, 2))[None]
    kv_s = lambda t: t.reshape(dbsz, dt_len, N_HEADS, HEAD_DIM)[None]
    return (out_p.reshape(bsz, s_len, D_MODEL), out_s.reshape(dbsz, dt_len, D_MODEL),
            kv_p(k_p), kv_p(v_p), hr_p[None], hi_p[None],
            kv_s(k_s), kv_s(v_s), hr_s[None], hi_s[None])
```

```python
import functools
import math

import jax
import jax.numpy as jnp
from jax import lax
from jax.experimental import pallas as pl
from jax.experimental.pallas import tpu as pltpu

F32 = jnp.float32
BF16 = jnp.bfloat16
I32 = jnp.int32

D_MODEL = 1024
SSM_WIDTH = 512
SSM_GROUP = 16
SSM_GROUPS = 32
SSM_STATE = 64
ATTN_WIDTH = 512
HEAD_DIM = 64
N_HEADS = 8
IN_WIDTH = SSM_WIDTH + 3 * ATTN_WIDTH
DILATIONS = (1, 4, 16)
WINDOWS = (128, 512, 2048)
ATTN_BLOCK = 128
ATTN_STEP_BLOCKS = 4
N_EXPERTS = 32
TOP_K = 4
EXPERT_FF = D_MODEL
SWIGLU_LIMIT = 7.0
SWIGLU_ALPHA = 1.702
PLE_DIM = 256
EPS = 1e-6
MASK_VALUE = -1e30
NEG_BIG = -3.0e38

LANES = 128
SUBLANES = 8
ROW_TILES = D_MODEL // LANES
TOKEN_TILE = 256
IN_TILE = 1024
IN_ROW_CHUNK = 512
MID_TILE = 512
MOE_TILE = 1024
MOE_SUBTILE = 512
SSM_HALF = SSM_WIDTH // 2
STATE_HALF = SSM_GROUPS * SSM_STATE // 2
ALIBI_SLOPES = tuple(2.0 ** (-8.0 * (h + 1) / N_HEADS) for h in range(N_HEADS))
VMEM_LIMIT = 56 * 1024 * 1024


def _rms(x, g):
    return x * lax.rsqrt(jnp.mean(x * x, axis=-1, keepdims=True) + EPS) * g


def _sigmoid(x):
    return 1.0 / (1.0 + jnp.exp(-x))


def _full(shape):
    n = len(shape)
    return pl.BlockSpec(shape, lambda *_: (0,) * n)


def _in_kernel(x_ref, g_ref, w_ref, u_ref, k_ref, v_ref, qb_ref, kb_ref, vb_ref, *rest, dils, window_tiles):
    tm = x_ref.shape[0]
    rc = min(tm, IN_ROW_CHUNK)
    tiles = ATTN_WIDTH // LANES
    for c in range(tm // rc):
        rows = slice(c * rc, (c + 1) * rc)
        h = _rms(x_ref[rows, :], g_ref[...]).astype(BF16)
        p = jnp.dot(h, w_ref[...], preferred_element_type=F32)
        u_ref[rows, :] = p[:, :SSM_WIDTH]
        q = p[:, SSM_WIDTH:SSM_WIDTH + ATTN_WIDTH] * (HEAD_DIM ** -0.5)
        k = p[:, SSM_WIDTH + ATTN_WIDTH:SSM_WIDTH + 2 * ATTN_WIDTH]
        v = p[:, SSM_WIDTH + 2 * ATTN_WIDTH:]
        if window_tiles is None:
            k_ref[rows, :] = k
            v_ref[rows, :] = v
        else:
            k_ref[:, rows] = k.T
            v_ref[:, rows] = v.T
        qb_ref[rows, :] = q.astype(BF16)
        kb_ref[rows, :] = k.astype(BF16)
        vb_ref[rows, :] = v.astype(BF16)
        if not dils:
            continue
        scr = rest[-1]
        for a, val in enumerate((q, k, v)):
            for ct in range(tiles):
                scr[c, a * tiles + ct] = val[:, ct * LANES:(ct + 1) * LANES]
        for di, d in enumerate(dils):
            out_rows = slice(c * rc // d, (c + 1) * rc // d)
            for a in range(3):
                out = rest[di * 3 + a]
                for r in range(d):
                    for ct in range(tiles):
                        piece = scr[c, a * tiles + ct, pl.ds(r, rc // d, stride=d), :]
                        c0 = r * ATTN_WIDTH + ct * LANES
                        out[out_rows, c0:c0 + LANES] = piece.astype(BF16)


def _in_proj(x2d, ln_mix, w_in_bf16, dils=(), seq_window=None):
    n = x2d.shape[0]
    tm = min(n, IN_TILE)
    row = lambda w: pl.BlockSpec((tm, w), lambda i: (i, 0))
    window_tiles = None
    kv_spec, kv_shape = row(ATTN_WIDTH), jax.ShapeDtypeStruct((n, ATTN_WIDTH), F32)
    if seq_window is not None:
        s_len, window = seq_window
        seq_tiles, first = s_len // tm, (s_len - window) // tm
        window_tiles = (seq_tiles, first)
        kv_spec = pl.BlockSpec(
            (None, ATTN_WIDTH, tm),
            lambda i: (i // seq_tiles, 0, jnp.maximum(i % seq_tiles - first, 0)))
        kv_shape = jax.ShapeDtypeStruct((n // s_len, ATTN_WIDTH, window), F32)
    out_specs = [row(SSM_WIDTH), kv_spec, kv_spec] + [row(ATTN_WIDTH)] * 3
    out_shape = ([jax.ShapeDtypeStruct((n, SSM_WIDTH), F32), kv_shape, kv_shape]
                 + [jax.ShapeDtypeStruct((n, ATTN_WIDTH), BF16)] * 3)
    for d in dils:
        out_specs += [pl.BlockSpec((tm // d, d * ATTN_WIDTH), lambda i: (i, 0))] * 3
        out_shape += [jax.ShapeDtypeStruct((n // d, d * ATTN_WIDTH), BF16)] * 3
    rc = min(tm, IN_ROW_CHUNK)
    scratch = [pltpu.VMEM((tm // rc, 3 * ATTN_WIDTH // LANES, rc, LANES), F32)] if dils else []
    return pl.pallas_call(
        functools.partial(_in_kernel, dils=tuple(dils), window_tiles=window_tiles),
        grid=(n // tm,),
        in_specs=[row(D_MODEL), _full((1, D_MODEL)), _full((D_MODEL, IN_WIDTH))],
        out_specs=out_specs,
        out_shape=out_shape,
        scratch_shapes=scratch,
        compiler_params=pltpu.CompilerParams(dimension_semantics=("arbitrary",),
                                             vmem_limit_bytes=VMEM_LIMIT),
        name="in_proj",
    )(x2d, ln_mix.reshape(1, D_MODEL), w_in_bf16)


def _s5_kernel(u_ref, bmat_ref, cmat_ref, are_ref, aim_ref, h0_ref, d_ref,
               y_ref, ht_ref, buf, hc, tmp, *, nb, tt, batched):
    rows = 2 * nb
    ntile = 2 * STATE_HALF // LANES
    half_tiles = ntile // 2

    def lane_tile(c):
        return slice(c * LANES, (c + 1) * LANES)

    @pl.when(pl.program_id(0) == 0)
    def _():
        hc[...] = h0_ref[...]

    if batched:
        u_all = u_ref[...].reshape(nb * tt, SSM_WIDTH)
        ub_all = u_all.astype(BF16)
        for hf in range(2):
            bu = jnp.dot(ub_all[:, hf * SSM_HALF:(hf + 1) * SSM_HALF], bmat_ref[hf],
                         preferred_element_type=F32)
            for c in range(ntile):
                tmp[c] = bu[:, lane_tile(c)]
            for c in range(ntile):
                for t in range(tt):
                    buf[c, t * rows + hf * nb:t * rows + (hf + 1) * nb, :] = tmp[c, pl.ds(t, nb, stride=tt), :]
    else:
        ub_all = u_ref[...].reshape(nb * tt, SSM_WIDTH).astype(BF16)
        for hf in range(2):
            bu = jnp.dot(ub_all[:, hf * SSM_HALF:(hf + 1) * SSM_HALF], bmat_ref[hf],
                         preferred_element_type=F32)
            for b in range(nb):
                for c in range(ntile):
                    buf[c, pl.ds(hf * nb + b, tt, stride=rows), :] = bu[b * tt:(b + 1) * tt, lane_tile(c)]

    group = 8
    for s in range(rows // SUBLANES):
        r0 = s * SUBLANES
        for c0 in range(0, half_tiles, group):
            ar = [are_ref[r0:r0 + SUBLANES, lane_tile(c0 + k)] for k in range(group)]
            ai = [aim_ref[r0:r0 + SUBLANES, lane_tile(c0 + k)] for k in range(group)]
            init = tuple(hc[r0:r0 + SUBLANES, lane_tile(c0 + k)] for k in range(group)) + tuple(
                hc[r0:r0 + SUBLANES, lane_tile(half_tiles + c0 + k)] for k in range(group))

            def step(t, carry, r0=r0, c0=c0, ar=ar, ai=ai):
                row = pl.multiple_of(t * rows + r0, SUBLANES)
                out_r, out_i = [], []
                for k in range(group):
                    hr, hi = carry[k], carry[group + k]
                    xr = buf[c0 + k, pl.ds(row, SUBLANES), :]
                    xi = buf[half_tiles + c0 + k, pl.ds(row, SUBLANES), :]
                    nr = ar[k] * hr - ai[k] * hi + xr
                    ni = ar[k] * hi + ai[k] * hr + xi
                    buf[c0 + k, pl.ds(row, SUBLANES), :] = nr
                    buf[half_tiles + c0 + k, pl.ds(row, SUBLANES), :] = ni
                    out_r.append(nr)
                    out_i.append(ni)
                return tuple(out_r) + tuple(out_i)

            fin = lax.fori_loop(0, tt, step, init, unroll=min(tt, 8))
            for k in range(group):
                hc[r0:r0 + SUBLANES, lane_tile(c0 + k)] = fin[k]
                hc[r0:r0 + SUBLANES, lane_tile(half_tiles + c0 + k)] = fin[group + k]

    if batched:
        parts = []
        for hf in range(2):
            for c in range(ntile):
                for t in range(tt):
                    tmp[c, pl.ds(t, nb, stride=tt), :] = buf[c, t * rows + hf * nb:t * rows + (hf + 1) * nb, :]
            hs = jnp.concatenate([tmp[c] for c in range(ntile)], axis=1).astype(BF16)
            parts.append(jnp.dot(hs, cmat_ref[hf], preferred_element_type=F32))
        y_all = jnp.concatenate(parts, axis=1) + d_ref[...] * u_all
        y_ref[...] = y_all.reshape(nb, tt, SSM_WIDTH)
    else:
        parts = []
        for hf in range(2):
            hs = jnp.concatenate(
                [jnp.concatenate([buf[c, pl.ds(hf * nb + b, tt, stride=rows), :] for c in range(ntile)],
                                 axis=1).astype(BF16) for b in range(nb)], axis=0)
            parts.append(jnp.dot(hs, cmat_ref[hf], preferred_element_type=F32))
        y_all = jnp.concatenate(parts, axis=1) + d_ref[...] * u_ref[...].reshape(nb * tt, SSM_WIDTH)
        y_ref[...] = y_all.reshape(nb, tt, SSM_WIDTH)

    ht_ref[...] = hc[...]


def _s5(u3, h0, bmat, cmat, a_re, a_im, d_skip):
    nb, t_len, _ = u3.shape
    tt = min(t_len, 256)
    rows = 2 * nb
    batched = tt < 16
    kern = functools.partial(_s5_kernel, nb=nb, tt=tt, batched=batched)
    ntile = 2 * STATE_HALF // LANES
    tmp_shape = (ntile, nb * tt, LANES) if batched else (1, SUBLANES, LANES)
    return pl.pallas_call(
        kern,
        grid=(t_len // tt,),
        in_specs=[pl.BlockSpec((nb, tt, SSM_WIDTH), lambda i: (0, i, 0)),
                  _full((2, SSM_HALF, 2 * STATE_HALF)),
                  _full((2, 2 * STATE_HALF, SSM_HALF)),
                  _full((rows, STATE_HALF)), _full((rows, STATE_HALF)),
                  _full((rows, 2 * STATE_HALF)), _full((1, SSM_WIDTH))],
        out_specs=[pl.BlockSpec((nb, tt, SSM_WIDTH), lambda i: (0, i, 0)),
                   _full((rows, 2 * STATE_HALF))],
        out_shape=[jax.ShapeDtypeStruct((nb, t_len, SSM_WIDTH), F32),
                   jax.ShapeDtypeStruct((rows, 2 * STATE_HALF), F32)],
        scratch_shapes=[pltpu.VMEM((ntile, tt * rows, LANES), F32),
                        pltpu.VMEM((rows, 2 * STATE_HALF), F32),
                        pltpu.VMEM(tmp_shape, F32)],
        compiler_params=pltpu.CompilerParams(dimension_semantics=("arbitrary",),
                                             vmem_limit_bytes=VMEM_LIMIT),
        name="s5_scan",
    )(u3, bmat, cmat, a_re, a_im, h0, d_skip.reshape(1, SSM_WIDTH))


def _s5_params(a_re, a_im, b_re, b_im, c_re, c_im, log_dt):
    dt = jnp.exp(log_dt)[:, None]
    mag = jnp.exp(dt * a_re)
    ang = dt * a_im
    ab_re, ab_im = mag * jnp.cos(ang), mag * jnp.sin(ang)
    den = a_re * a_re + a_im * a_im
    nr, ni = ab_re - 1.0, ab_im
    f_re = (nr * a_re + ni * a_im) / den
    f_im = (ni * a_re - nr * a_im) / den
    bb_re = f_re[..., None] * b_re - f_im[..., None] * b_im
    bb_im = f_re[..., None] * b_im + f_im[..., None] * b_re
    gh = SSM_GROUPS // 2
    eye = jnp.eye(gh, dtype=F32)

    def b_half(w):
        return jnp.einsum('gnc,gh->gchn', w, eye).reshape(gh * SSM_GROUP, gh * SSM_STATE)

    def c_half(w):
        return jnp.einsum('gcn,gh->gnhc', w, eye).reshape(gh * SSM_STATE, gh * SSM_GROUP)

    bmat = jnp.stack([jnp.concatenate([b_half(bb_re[h * gh:(h + 1) * gh]),
                                       b_half(bb_im[h * gh:(h + 1) * gh])], axis=1)
                      for h in range(2)]).astype(BF16)
    cmat = jnp.stack([jnp.concatenate([c_half(c_re[h * gh:(h + 1) * gh]),
                                       -c_half(c_im[h * gh:(h + 1) * gh])], axis=0)
                      for h in range(2)]).astype(BF16)
    return bmat, cmat, ab_re.reshape(2, 1, STATE_HALF), ab_im.reshape(2, 1, STATE_HALF)


def _state_to_rows(h_re, h_im):
    nb = h_re.shape[0]
    f = lambda h: h.reshape(nb, 2, STATE_HALF).transpose(1, 0, 2).reshape(2 * nb, STATE_HALF)
    return jnp.concatenate([f(h_re), f(h_im)], axis=1)


def _rows_to_state(ht, nb):
    f = lambda h: h.reshape(2, nb, STATE_HALF).transpose(1, 0, 2).reshape(nb, SSM_GROUPS, SSM_STATE)
    return f(ht[:, :STATE_HALF]), f(ht[:, STATE_HALF:])


def _attn_prompt_kernel(q_ref, kp_ref, kc_ref, vp_ref, vc_ref, o_ref, l_ref, bias_s, s_s, p_s, *, dil):
    blk = ATTN_BLOCK
    n = pl.program_id(2)

    @pl.when((pl.program_id(0) == 0) & (pl.program_id(1) == 0) & (n == 0))
    def _():
        i_idx = lax.broadcasted_iota(I32, (blk, 2 * blk), 0)
        j_idx = lax.broadcasted_iota(I32, (blk, 2 * blk), 1)
        delta = i_idx - j_idx + blk
        in_band = (delta >= 0) & (delta <= blk)
        dist = (delta * dil).astype(F32)
        for h in range(N_HEADS):
            biased = -ALIBI_SLOPES[h] * dist
            bias_s[0, h] = jnp.where(in_band & (j_idx >= blk), biased, MASK_VALUE)
            bias_s[1, h] = jnp.where(in_band, biased, MASK_VALUE)

    lane = lax.broadcasted_iota(I32, (blk, LANES), 1)
    nqb = q_ref.shape[0] // blk
    nres = q_ref.shape[1] // ATTN_WIDTH
    units = [(res, qb) for res in range(nres) for qb in range(nqb)]
    for u, (res, qb) in enumerate(units):
        which = jnp.minimum(n, 1) if qb == 0 else 1
        rows = slice(qb * blk, (qb + 1) * blk)
        for hp in range(N_HEADS // 2):
            cols = slice(res * ATTN_WIDTH + hp * LANES, res * ATTN_WIDTH + (hp + 1) * LANES)
            q2 = q_ref[rows, cols]
            before = kp_ref[:, cols] if qb == 0 else kc_ref[(qb - 1) * blk:qb * blk, cols]
            kk = jnp.concatenate([before, kc_ref[rows, cols]], axis=0)
            for half in range(2):
                h = 2 * hp + half
                in_head = (lane >= half * HEAD_DIM) & (lane < (half + 1) * HEAD_DIM)
                qm = jnp.where(in_head, q2, jnp.zeros_like(q2))
                s = lax.dot_general(qm, kk, (((1,), (1,)), ((), ())), preferred_element_type=F32)
                s_s[u * N_HEADS + h] = s + bias_s[which, h]
    for u, (res, qb) in enumerate(units):
        lse_all = jnp.zeros((blk, LANES), F32)
        for h in range(N_HEADS):
            s = s_s[u * N_HEADS + h]
            m = jnp.max(s, axis=1, keepdims=True)
            p = jnp.exp(s - m)
            l = jnp.sum(p, axis=1, keepdims=True)
            p_s[u * N_HEADS + h] = (p * (1.0 / l)).astype(BF16)
            lse_all = jnp.where(lane == h, m + jnp.log(l), lse_all)
        l_ref[qb * blk:(qb + 1) * blk, res * LANES:(res + 1) * LANES] = lse_all
    for u, (res, qb) in enumerate(units):
        rows = slice(qb * blk, (qb + 1) * blk)
        for hp in range(N_HEADS // 2):
            cols = slice(res * ATTN_WIDTH + hp * LANES, res * ATTN_WIDTH + (hp + 1) * LANES)
            before = vp_ref[:, cols] if qb == 0 else vc_ref[(qb - 1) * blk:qb * blk, cols]
            vv = jnp.concatenate([before, vc_ref[rows, cols]], axis=0)
            outs = [jnp.dot(p_s[u * N_HEADS + 2 * hp + half], vv, preferred_element_type=F32)
                    for half in range(2)]
            o_ref[rows, cols] = jnp.where(lane < HEAD_DIM, outs[0], outs[1])


def _attn_prompt_branch(qb, kb, vb, bsz, s_len, dil):
    sub = s_len // dil
    nqb = min(ATTN_STEP_BLOCKS, sub // ATTN_BLOCK)
    nres = min(ATTN_STEP_BLOCKS // nqb, dil)
    step = nqb * ATTN_BLOCK
    nstep = sub // step
    nunit = nres * nqb
    view = lambda t: t.reshape(bsz, sub, dil * ATTN_WIDTH)
    cur = pl.BlockSpec((None, step, nres * ATTN_WIDTH), lambda b, r, n: (b, n, r))
    prev = pl.BlockSpec((None, ATTN_BLOCK, nres * ATTN_WIDTH),
                        lambda b, r, n: (b, jnp.maximum(n * nqb - 1, 0), r))
    o, lse = pl.pallas_call(
        functools.partial(_attn_prompt_kernel, dil=dil),
        grid=(bsz, dil // nres, nstep),
        in_specs=[cur, prev, cur, prev, cur],
        out_specs=[cur, pl.BlockSpec((None, step, nres * LANES), lambda b, r, n: (b, n, r))],
        out_shape=[jax.ShapeDtypeStruct((bsz, sub, dil * ATTN_WIDTH), F32),
                   jax.ShapeDtypeStruct((bsz, sub, dil * LANES), F32)],
        scratch_shapes=[pltpu.VMEM((2, N_HEADS, ATTN_BLOCK, 2 * ATTN_BLOCK), F32),
                        pltpu.VMEM((nunit * N_HEADS, ATTN_BLOCK, 2 * ATTN_BLOCK), F32),
                        pltpu.VMEM((nunit * N_HEADS, ATTN_BLOCK, 2 * ATTN_BLOCK), BF16)],
        compiler_params=pltpu.CompilerParams(
            dimension_semantics=("arbitrary", "arbitrary", "arbitrary"),
            vmem_limit_bytes=VMEM_LIMIT),
        name=f"attn_prompt_d{dil}",
    )(view(qb), view(kb), view(kb), view(vb), view(vb))
    return o.reshape(bsz * sub, dil * ATTN_WIDTH), lse.reshape(bsz * sub, dil * LANES)


def _attn_sample_kernel(q_ref, kn_ref, vn_ref, kc_ref, vc_ref, o_ref, bias_s, mult_s, *, t_len, wb):
    nrow = N_HEADS * t_len
    t_shift = t_len.bit_length() - 1
    d_shift = HEAD_DIM.bit_length() - 1
    nt = (((1,), (1,)), ((), ()))

    def branch_count(dist):
        mult = jnp.zeros(dist.shape, F32)
        for win, dil in zip(WINDOWS, DILATIONS):
            hit = (dist >= 0) & (dist <= win) & ((dist & (dil - 1)) == 0)
            mult = mult + jnp.where(hit, 1.0, 0.0)
        return mult

    def biased(dist, mult):
        head = lax.broadcasted_iota(I32, dist.shape, 0) >> t_shift
        slope = jnp.zeros(dist.shape, F32)
        for h in range(N_HEADS):
            slope = jnp.where(head == h, ALIBI_SLOPES[h], slope)
        return jnp.where(mult > 0.0, -slope * dist.astype(F32), MASK_VALUE)

    @pl.when(pl.program_id(0) == 0)
    def _():
        row = lax.broadcasted_iota(I32, (nrow, wb), 0)
        col = lax.broadcasted_iota(I32, (nrow, wb), 1)
        dist = wb + (row & (t_len - 1)) - col
        mult = branch_count(dist)
        mult_s[...] = mult
        bias_s[...] = biased(dist, mult)

    q = q_ref[...].astype(F32)
    qt = jnp.concatenate([q] * N_HEADS, axis=0)
    row_w = lax.broadcasted_iota(I32, (nrow, ATTN_WIDTH), 0)
    lane_w = lax.broadcasted_iota(I32, (nrow, ATTN_WIDTH), 1)
    qm = jnp.where((lane_w >> d_shift) == (row_w >> t_shift), qt, 0.0).astype(BF16)

    pad = LANES - t_len
    kn = jnp.concatenate([kn_ref[...].astype(F32), jnp.zeros((pad, ATTN_WIDTH), F32)], axis=0).astype(BF16)
    vn = jnp.concatenate([vn_ref[...].astype(F32), jnp.zeros((pad, ATTN_WIDTH), F32)], axis=0).astype(BF16)
    row_n = lax.broadcasted_iota(I32, (nrow, LANES), 0)
    col_n = lax.broadcasted_iota(I32, (nrow, LANES), 1)
    dist_n = jnp.where(col_n < t_len, (row_n & (t_len - 1)) - col_n, -1)
    mult_n = branch_count(dist_n)
    s_n = lax.dot_general(qm, kn, nt, preferred_element_type=F32) + biased(dist_n, mult_n)
    s_c = jnp.dot(qm, kc_ref[...].astype(BF16), preferred_element_type=F32) + bias_s[...]

    m = jnp.maximum(jnp.max(s_c, axis=1, keepdims=True), jnp.max(s_n, axis=1, keepdims=True))
    p_c = jnp.exp(s_c - m) * mult_s[...]
    p_n = jnp.exp(s_n - m) * mult_n
    l = jnp.sum(p_c, axis=1, keepdims=True) + jnp.sum(p_n, axis=1, keepdims=True)
    o = (lax.dot_general(p_c.astype(BF16), vc_ref[...].astype(BF16), nt, preferred_element_type=F32)
         + jnp.dot(p_n.astype(BF16), vn, preferred_element_type=F32)) / l
    lane_o = lax.broadcasted_iota(I32, (t_len, ATTN_WIDTH), 1) >> d_shift
    out = jnp.zeros((t_len, ATTN_WIDTH), F32)
    for h in range(N_HEADS):
        out = jnp.where(lane_o == h, o[h * t_len:(h + 1) * t_len], out)
    o_ref[...] = out


def _attn_sample(qb, kb, vb, cache_k, cache_v):
    bsz, t_len, _ = qb.shape
    wb = cache_k.shape[1]
    feature_major = lambda c: jnp.transpose(c, (0, 2, 3, 1)).reshape(bsz, ATTN_WIDTH, wb)
    new = pl.BlockSpec((None, t_len, ATTN_WIDTH), lambda b: (b, 0, 0))
    old = pl.BlockSpec((None, ATTN_WIDTH, wb), lambda b: (b, 0, 0))
    return pl.pallas_call(
        functools.partial(_attn_sample_kernel, t_len=t_len, wb=wb),
        grid=(bsz,),
        in_specs=[new, new, new, old, old],
        out_specs=new,
        out_shape=jax.ShapeDtypeStruct((bsz, t_len, ATTN_WIDTH), F32),
        scratch_shapes=[pltpu.VMEM((N_HEADS * t_len, wb), F32),
                        pltpu.VMEM((N_HEADS * t_len, wb), F32)],
        compiler_params=pltpu.CompilerParams(dimension_semantics=("arbitrary",),
                                             vmem_limit_bytes=VMEM_LIMIT),
        name="attn_sample",
    )(qb, kb, vb, feature_major(cache_k), feature_major(cache_v))


def _mid_kernel(*refs, dils):
    n_branch = max(len(dils), 1)
    x_ref, y_ref = refs[0], refs[1]
    o_refs = refs[2:2 + n_branch]
    pos = 2 + n_branch
    l_refs = refs[pos:pos + len(dils)]
    pos += len(l_refs)
    (wglu_ref, bglu_ref, lns_ref, lna_ref, wout_ref, lnm_ref, wrh_ref, wrl_ref, br_ref,
     expand_ref, tri_ref,
     x1_ref, hrow_ref, eidx_ref, epos_ref, egate_ref, cnt_ref, carry, nat) = refs[pos:]
    tm = x_ref.shape[0]
    o_tiles = ATTN_WIDTH // LANES

    rb = tm

    for bi, d in enumerate(dils):
        if d == 1:
            continue
        for r in range(d):
            spread = pl.ds(r, tm // d, stride=d)
            for ct in range(o_tiles):
                c0 = r * ATTN_WIDTH + ct * LANES
                nat[bi, ct, spread, :] = o_refs[bi][:, c0:c0 + LANES]
            nat[bi, o_tiles, spread, :] = l_refs[bi][:, r * LANES:(r + 1) * LANES]

    def natural(bi, d, rows):
        if d == 1:
            return o_refs[bi][rows, :], l_refs[bi][rows, :]
        return (jnp.concatenate([nat[bi, ct, rows, :] for ct in range(o_tiles)], axis=1),
                nat[bi, o_tiles, rows, :])

    @pl.when(pl.program_id(0) == 0)
    def _():
        carry[...] = jnp.zeros_like(carry)

    lane = lax.broadcasted_iota(I32, (rb, LANES), 1)
    lane_f = lane.astype(F32)
    run = carry[0:1, :]
    for blk in range(tm // rb):
        rows = slice(blk * rb, (blk + 1) * rb)
        y = y_ref[rows, :]
        z = y * (0.5 * (1.0 + jnp.tanh(math.sqrt(2.0 / math.pi) * (y + 0.044715 * (y * y * y)))))
        glu = z * _sigmoid(jnp.dot(z.astype(BF16), wglu_ref[...], preferred_element_type=F32)
                           + bglu_ref[...])
        n_ssm = _rms(glu, lns_ref[...])

        if not dils:
            attn = o_refs[0][rows, :]
        else:
            pairs = [natural(bi, d, rows) for bi, d in enumerate(dils)]
            lses = [p[1] for p in pairs]
            mx = functools.reduce(jnp.maximum, lses)
            es = [jnp.exp(l - mx) for l in lses]
            inv = 1.0 / functools.reduce(lambda a, b: a + b, es)
            attn = jnp.zeros((rb, ATTN_WIDTH), F32)
            for e, (o_nat, _) in zip(es, pairs):
                w = e * inv
                w_hi = w.astype(BF16)
                w_lo = (w - w_hi.astype(F32)).astype(BF16)
                wide = jnp.dot(jnp.concatenate([w_hi, w_lo], axis=1), expand_ref[...],
                               preferred_element_type=F32)
                attn = attn + wide * o_nat
        n_attn = _rms(attn, lna_ref[...])

        x1 = (x_ref[rows, :]
              + jnp.dot(n_ssm.astype(BF16), wout_ref[:SSM_WIDTH, :], preferred_element_type=F32)
              + jnp.dot(n_attn.astype(BF16), wout_ref[SSM_WIDTH:, :], preferred_element_type=F32))
        x1_ref[rows, :] = x1
        hm = _rms(x1, lnm_ref[...])
        for s in range(ROW_TILES):
            hrow_ref[pl.ds(blk * rb * ROW_TILES + s, rb, stride=ROW_TILES), :] = hm[:, s * LANES:(s + 1) * LANES]

        h_hi = hm.astype(BF16)
        h_lo = (hm - h_hi.astype(F32)).astype(BF16)
        both = jnp.dot(h_hi, wrl_ref[...], preferred_element_type=F32)
        logits = (both[:, :LANES] + both[:, LANES:]
                  + jnp.dot(h_lo, wrh_ref[...], preferred_element_type=F32)
                  + br_ref[...])
        work = jnp.where(lane < N_EXPERTS, logits, NEG_BIG)
        vals, idxs, hots = [], [], []
        for _ in range(TOP_K):
            m = jnp.max(work, axis=1, keepdims=True)
            idx = jnp.min(jnp.where(work == m, lane_f, float(LANES)), axis=1, keepdims=True)
            hot = lane_f == idx
            vals.append(m)
            idxs.append(idx)
            hots.append(hot)
            work = jnp.where(hot, NEG_BIG, work)
        exps = [jnp.exp(v - vals[0]) for v in vals]
        inv = 1.0 / functools.reduce(lambda a, b: a + b, exps)

        sel = functools.reduce(lambda a, b: a + b, [h.astype(F32) for h in hots])
        before = jnp.dot(tri_ref[...], sel.astype(BF16), preferred_element_type=F32) + run
        eidx = jnp.zeros((rb, LANES), I32)
        epos = jnp.zeros((rb, LANES), I32)
        egate = jnp.zeros((rb, LANES), F32)
        for k in range(TOP_K):
            pk = jnp.sum(jnp.where(hots[k], before, 0.0), axis=1, keepdims=True)
            eidx = jnp.where(lane == k, idxs[k].astype(I32), eidx)
            epos = jnp.where(lane == k, pk.astype(I32), epos)
            egate = jnp.where(lane == k, exps[k] * inv, egate)
        eidx_ref[rows, :] = eidx
        epos_ref[rows, :] = epos
        egate_ref[rows, :] = egate
        run = run + jnp.sum(sel, axis=0, keepdims=True)
    carry[...] = jnp.broadcast_to(run, carry.shape)
    cnt_ref[...] = jnp.broadcast_to(run, cnt_ref.shape).astype(I32)


def _mid(x2d, y2d, attn_o, attn_lse, dils, w):
    n = x2d.shape[0]
    tm = min(n, MID_TILE)
    rb = tm
    ti = jnp.arange(rb)
    tri = (ti[:, None] > ti[None, :]).astype(BF16)
    n_branch = len(attn_o)
    row = lambda width: pl.BlockSpec((tm, width), lambda i: (i, 0))
    packed = lambda d, width: pl.BlockSpec((tm // d, d * width), lambda i: (i, 0))
    attn_specs = ([packed(d, ATTN_WIDTH) for d in dils] + [packed(d, LANES) for d in dils]
                  if dils else [row(ATTN_WIDTH)])
    in_specs = ([row(D_MODEL), row(SSM_WIDTH)] + attn_specs
                + [_full((SSM_WIDTH, SSM_WIDTH)), _full((1, SSM_WIDTH)), _full((1, SSM_WIDTH)),
                   _full((1, ATTN_WIDTH)), _full((D_MODEL, D_MODEL)), _full((1, D_MODEL)),
                   _full((D_MODEL, LANES)), _full((D_MODEL, 2 * LANES)), _full((1, LANES)),
                   _full((2 * LANES, ATTN_WIDTH)), _full((rb, rb))])
    out_specs = [row(D_MODEL), pl.BlockSpec((tm * ROW_TILES, LANES), lambda i: (i, 0)),
                 row(LANES), row(LANES), row(LANES), _full((SUBLANES, LANES))]
    out_shape = [jax.ShapeDtypeStruct((n, D_MODEL), F32),
                 jax.ShapeDtypeStruct((n * ROW_TILES, LANES), F32),
                 jax.ShapeDtypeStruct((n, LANES), I32),
                 jax.ShapeDtypeStruct((n, LANES), I32),
                 jax.ShapeDtypeStruct((n, LANES), F32),
                 jax.ShapeDtypeStruct((SUBLANES, LANES), I32)]
    return pl.pallas_call(
        functools.partial(_mid_kernel, dils=tuple(dils)),
        grid=(n // tm,),
        in_specs=in_specs,
        out_specs=out_specs,
        out_shape=out_shape,
        scratch_shapes=[pltpu.VMEM((SUBLANES, LANES), F32),
                        pltpu.VMEM((n_branch, ATTN_WIDTH // LANES + 1, tm, LANES), F32)],
        compiler_params=pltpu.CompilerParams(dimension_semantics=("arbitrary",),
                                             vmem_limit_bytes=VMEM_LIMIT),
        name="mid",
    )(x2d, y2d, *attn_o, *attn_lse, w["w_glu"], w["b_glu"], w["ln_ssm_out"], w["ln_attn_out"],
      w["w_out"], w["ln_moe"], w["wr_hi"], w["wr_lo"], w["b_router"], w["expand"], tri)


def _dispatch_kernel(dest_ref, hp_ref, hs_ref, xs_hbm, sem, *, n_prompt_tiles):
    i = pl.program_id(0)
    npair = TOKEN_TILE * TOP_K

    def run(src_ref):
        def issue(t, c):
            src = src_ref.at[pl.ds(pl.multiple_of(t * ROW_TILES, ROW_TILES), ROW_TILES), :]
            for k in range(TOP_K):
                d = dest_ref[0, t * TOP_K + k]
                pltpu.make_async_copy(
                    src, xs_hbm.at[pl.ds(pl.multiple_of(d * ROW_TILES, ROW_TILES), ROW_TILES), :],
                    sem).start(priority=k % 2)
            return c

        lax.fori_loop(0, TOKEN_TILE, issue, 0, unroll=2)
        span = pl.ds(0, npair * ROW_TILES)
        pltpu.make_async_copy(xs_hbm.at[span, :], xs_hbm.at[span, :], sem).wait()

    @pl.when(i < n_prompt_tiles)
    def _():
        run(hp_ref)

    @pl.when(i >= n_prompt_tiles)
    def _():
        run(hs_ref)


def _dispatch(dest, hrow_p, hrow_s):
    n_p = hrow_p.shape[0] // ROW_TILES
    n_s = hrow_s.shape[0] // ROW_TILES
    npt = n_p // TOKEN_TILE
    ntile = (n_p + n_s) // TOKEN_TILE
    npair = TOKEN_TILE * TOP_K
    blk = (TOKEN_TILE * ROW_TILES, LANES)
    return pl.pallas_call(
        functools.partial(_dispatch_kernel, n_prompt_tiles=npt),
        grid=(ntile,),
        in_specs=[pl.BlockSpec((None, 1, npair), lambda i: (i, 0, 0), memory_space=pltpu.SMEM),
                  pl.BlockSpec(blk, lambda i: (jnp.minimum(i, npt - 1), 0)),
                  pl.BlockSpec(blk, lambda i: (jnp.maximum(i - npt, 0), 0))],
        out_specs=pl.BlockSpec(memory_space=pl.ANY),
        out_shape=jax.ShapeDtypeStruct(((n_p + n_s) * TOP_K * ROW_TILES, LANES), F32),
        scratch_shapes=[pltpu.SemaphoreType.DMA(())],
        compiler_params=pltpu.CompilerParams(dimension_semantics=("arbitrary",),
                                             vmem_limit_bytes=VMEM_LIMIT),
        name="moe_dispatch",
    )(dest.reshape(ntile, 1, npair), hrow_p, hrow_s)


def _expert_kernel(vt_ref, ve_ref, vok_ref, vnext_ref, vmore_ref, vslot_ref, gs_ref,
                   xs_ref, wu_hbm, bu_ref, wd_hbm, bd_ref,
                   out_ref, wu_s, wd_s, x_s, wu_f, wd_f, wsem):
    v = pl.program_id(0)
    e = ve_ref[v]
    j = vt_ref[v]
    vprev = jnp.maximum(v - 1, 0)
    new_e = (v == 0) | (e != ve_ref[vprev])
    new_j = (v == 0) | (j != vt_ref[vprev])
    tm = MOE_TILE
    slot = vslot_ref[v]

    def weight_copies(expert, to):
        return (pltpu.make_async_copy(wu_hbm.at[expert], wu_f.at[to], wsem.at[0, to]),
                pltpu.make_async_copy(wd_hbm.at[expert], wd_f.at[to], wsem.at[1, to]))

    @pl.when(v == 0)
    def _():
        for c in weight_copies(e, slot):
            c.start()

    @pl.when(new_e)
    def _():
        for c in weight_copies(e, slot):
            c.wait()

        @pl.when(vmore_ref[v] == 1)
        def _():
            for c in weight_copies(vnext_ref[v], 1 - slot):
                c.start()

        wu_s[...] = wu_f[slot].astype(BF16)
        wd_s[...] = wd_f[slot].astype(BF16)

    @pl.when(new_j)
    def _():
        out_ref[...] = jnp.zeros_like(out_ref)

    lo, hi = gs_ref[e], gs_ref[e + 1]

    def run_pass(row0, nrows):
        base = row0 * ROW_TILES
        for s in range(ROW_TILES):
            x_s[:nrows, s * LANES:(s + 1) * LANES] = xs_ref[
                pl.ds(base + s, nrows, stride=ROW_TILES), :].astype(BF16)
        a = jnp.dot(x_s[:nrows, :], wu_s[...], preferred_element_type=F32) + bu_ref[...]
        g = jnp.minimum(a[:, :EXPERT_FF], SWIGLU_LIMIT)
        lin = jnp.clip(a[:, EXPERT_FF:], -SWIGLU_LIMIT, SWIGLU_LIMIT)
        act = (lin + 1.0) * (g * _sigmoid(SWIGLU_ALPHA * g))
        y = jnp.dot(act.astype(BF16), wd_s[...], preferred_element_type=F32) + bd_ref[...]
        rows = j * tm + row0 + lax.broadcasted_iota(I32, (nrows, 1), 0)
        mine = (rows >= lo) & (rows < hi)
        for s in range(ROW_TILES):
            cur = out_ref[pl.ds(base + s, nrows, stride=ROW_TILES), :]
            out_ref[pl.ds(base + s, nrows, stride=ROW_TILES), :] = jnp.where(
                mine, y[:, s * LANES:(s + 1) * LANES], cur)

    sub = x_s.shape[0]
    half = sub // 2
    live = vok_ref[v] == 1
    for part in range(tm // sub):
        row0 = part * sub
        start = j * tm + row0
        in_first = live & (lo < start + half) & (hi > start)
        in_second = live & (lo < start + sub) & (hi > start + half)
        pl.when(in_first & in_second)(functools.partial(run_pass, row0, sub))
        pl.when(in_first & jnp.logical_not(in_second))(functools.partial(run_pass, row0, half))
        pl.when(in_second & jnp.logical_not(in_first))(functools.partial(run_pass, row0 + half, half))


def _experts(xs, visits, gstart, w_up, b_up, w_down, b_down):
    tm = MOE_TILE
    nvisit = visits[0].shape[0]
    rows = pl.BlockSpec((tm * ROW_TILES, LANES), lambda v, vt, *_: (vt[v], 0))
    per_e = lambda a, b: pl.BlockSpec((None, a, b), lambda v, vt, ve, *_: (ve[v], 0, 0))
    grid_spec = pltpu.PrefetchScalarGridSpec(
        num_scalar_prefetch=7,
        grid=(nvisit,),
        in_specs=[rows, pl.BlockSpec(memory_space=pl.ANY), per_e(1, 2 * EXPERT_FF),
                  pl.BlockSpec(memory_space=pl.ANY), per_e(1, D_MODEL)],
        out_specs=rows,
        scratch_shapes=[pltpu.VMEM((D_MODEL, 2 * EXPERT_FF), BF16),
                        pltpu.VMEM((EXPERT_FF, D_MODEL), BF16),
                        pltpu.VMEM((MOE_SUBTILE, D_MODEL), BF16),
                        pltpu.VMEM((2, D_MODEL, 2 * EXPERT_FF), F32),
                        pltpu.VMEM((2, EXPERT_FF, D_MODEL), F32),
                        pltpu.SemaphoreType.DMA((2, 2))],
    )
    return pl.pallas_call(
        _expert_kernel,
        grid_spec=grid_spec,
        out_shape=jax.ShapeDtypeStruct(xs.shape, F32),
        compiler_params=pltpu.CompilerParams(dimension_semantics=("arbitrary",),
                                             vmem_limit_bytes=VMEM_LIMIT),
        name="moe_experts",
    )(*visits, gstart, xs, w_up, b_up.reshape(N_EXPERTS, 1, 2 * EXPERT_FF),
      w_down, b_down.reshape(N_EXPERTS, 1, D_MODEL))


def _routing(eidx_p, epos_p, cnt_p, eidx_s, epos_s, cnt_s, n_rows):
    cnt_p = cnt_p[0, :N_EXPERTS]
    cnt_s = cnt_s[0, :N_EXPERTS]
    cnt = cnt_p + cnt_s
    gend = jnp.cumsum(cnt)
    gstart = gend - cnt
    experts = jnp.arange(N_EXPERTS, dtype=I32)

    def lookup(table, idx):
        return jnp.sum(jnp.where(idx[..., None] == experts, table, 0), axis=-1)

    ep = eidx_p[:, :TOP_K]
    es = eidx_s[:, :TOP_K]
    dest_p = lookup(gstart, ep) + epos_p[:, :TOP_K]
    dest_s = lookup(gstart + cnt_p, es) + epos_s[:, :TOP_K]
    ntile = n_rows // MOE_TILE
    nvisit = ntile + N_EXPERTS
    first = gstart // MOE_TILE
    last = jnp.maximum(gend - 1, 0) // MOE_TILE
    nv = jnp.where(cnt > 0, last - first + 1, 0)
    vend = jnp.cumsum(nv)
    vstart = vend - nv
    total = vend[-1]
    v = jnp.arange(nvisit, dtype=I32)
    vc = jnp.minimum(v, total - 1)
    ve = jnp.sum((vend[None, :] <= vc[:, None]).astype(I32), axis=1)
    vt = (lookup(first - vstart, ve) + vc).astype(I32)
    vok = (v < total).astype(I32)
    has_rows = cnt > 0
    later = (experts[None, :] > experts[:, None]) & has_rows[None, :]
    nxt = jnp.min(jnp.where(later, experts[None, :], N_EXPERTS), axis=1)
    more = nxt < N_EXPERTS
    slot = (jnp.cumsum(has_rows.astype(I32)) - 1) % 2
    visits = (vt, ve, vok, lookup(jnp.where(more, nxt, 0), ve).astype(I32),
              lookup(more.astype(I32), ve).astype(I32), lookup(slot, ve).astype(I32))
    gs = jnp.concatenate([gstart, gend[-1:]]).astype(I32)
    return dest_p.astype(I32), dest_s.astype(I32), visits, gs


def _out_kernel(dest_ref, next_ref, y_hbm, x1_ref, gate_ref, pe_ref, lnp_ref, wg_ref, bg_ref, wp_ref,
                lnf_ref, o_ref, buf, sem, *, ntile):
    tm = TOKEN_TILE
    npair = tm * TOP_K
    i = pl.program_id(0)
    slot = i % 2

    def gather(idx_ref, to):
        def issue(t, c):
            for k in range(TOP_K):
                d = idx_ref[0, t * TOP_K + k]
                pltpu.make_async_copy(
                    y_hbm.at[pl.ds(pl.multiple_of(d * ROW_TILES, ROW_TILES), ROW_TILES), :],
                    buf.at[to, pl.ds(pl.multiple_of((k * tm + t) * ROW_TILES, ROW_TILES), ROW_TILES), :],
                    sem.at[to]).start(priority=1)
            return c

        lax.fori_loop(0, tm, issue, 0, unroll=2)

    @pl.when(i == 0)
    def _():
        gather(dest_ref, 0)

    if ntile > 1:
        @pl.when(i + 1 < ntile)
        def _():
            gather(next_ref, 1 - slot)

    pltpu.make_async_copy(y_hbm.at[pl.ds(0, npair * ROW_TILES), :], buf.at[slot], sem.at[slot]).wait()

    gates = gate_ref[...]
    parts = []
    for s in range(ROW_TILES):
        acc = jnp.zeros((tm, LANES), F32)
        for k in range(TOP_K):
            rows = buf[slot, pl.ds(k * tm * ROW_TILES + s, tm, stride=ROW_TILES), :]
            acc = acc + gates[:, k:k + 1] * rows
        parts.append(acc)
    x2 = x1_ref[...] + jnp.concatenate(parts, axis=1)
    gate = _sigmoid(jnp.dot(_rms(x2, lnp_ref[...]).astype(BF16), wg_ref[...],
                            preferred_element_type=F32) + bg_ref[...])
    x3 = x2 + gate * jnp.dot(pe_ref[...].astype(BF16), wp_ref[...], preferred_element_type=F32)
    o_ref[...] = _rms(x3, lnf_ref[...])


def _combine(dest, y_rows, x1, egate, pe, w):
    n = x1.shape[0]
    tm = TOKEN_TILE
    npair = tm * TOP_K
    row = lambda width: pl.BlockSpec((tm, width), lambda i: (i, 0))
    ntile = n // tm
    dest3 = dest.reshape(ntile, 1, npair)
    return pl.pallas_call(
        functools.partial(_out_kernel, ntile=ntile),
        grid=(ntile,),
        in_specs=[pl.BlockSpec((None, 1, npair), lambda i: (i, 0, 0), memory_space=pltpu.SMEM),
                  pl.BlockSpec((None, 1, npair), lambda i: (jnp.minimum(i + 1, ntile - 1), 0, 0),
                               memory_space=pltpu.SMEM),
                  pl.BlockSpec(memory_space=pl.ANY),
                  row(D_MODEL), row(LANES), row(PLE_DIM),
                  _full((1, D_MODEL)), _full((D_MODEL, D_MODEL)), _full((1, D_MODEL)),
                  _full((PLE_DIM, D_MODEL)), _full((1, D_MODEL))],
        out_specs=row(D_MODEL),
        out_shape=jax.ShapeDtypeStruct((n, D_MODEL), F32),
        scratch_shapes=[pltpu.VMEM((2, npair * ROW_TILES, LANES), F32), pltpu.SemaphoreType.DMA((2,))],
        compiler_params=pltpu.CompilerParams(dimension_semantics=("arbitrary",),
                                             vmem_limit_bytes=VMEM_LIMIT),
        name="combine_out",
    )(dest3, dest3, y_rows, x1, egate, pe, w["ln_ple"], w["w_ple_gate"],
      w["b_ple_gate"], w["w_ple_proj"], w["ln_final"])


def kernel(x_prompt, x_sample, cache_attn_k, cache_attn_v, state_ssm_re, state_ssm_im, p_prompt, p_sample, ln_mix, w_in, ssm_a_re, ssm_a_im, ssm_b_re, ssm_b_im, ssm_c_re, ssm_c_im, ssm_d, ssm_log_dt, w_glu, b_glu, ln_ssm_out, ln_attn_out, w_out, ln_moe, w_router, b_router, w_up, b_up, w_down, b_down, ln_ple, w_ple_gate, b_ple_gate, w_ple_proj, ln_final):
    bsz, s_len, _ = x_prompt.shape
    dbsz, dt_len, _ = x_sample.shape
    n_p, n_s = bsz * s_len, dbsz * dt_len
    wb = cache_attn_k.shape[2]
    wb_prompt = min(WINDOWS[-1], s_len)

    wr = jnp.pad(w_router[0], ((0, 0), (0, LANES - N_EXPERTS)))
    wr_hi = wr.astype(BF16)
    w = {
        "w_glu": w_glu[0].astype(BF16), "b_glu": b_glu[0].reshape(1, -1),
        "ln_ssm_out": ln_ssm_out[0].reshape(1, -1), "ln_attn_out": ln_attn_out[0].reshape(1, -1),
        "w_out": w_out[0].astype(BF16), "ln_moe": ln_moe[0].reshape(1, -1),
        "wr_hi": wr_hi,
        "wr_lo": jnp.concatenate([wr_hi, (wr - wr_hi.astype(F32)).astype(BF16)], axis=1),
        "b_router": jnp.pad(b_router[0], (0, LANES - N_EXPERTS)).reshape(1, -1),
        "expand": (jnp.arange(2 * LANES)[:, None] % LANES
                   == jnp.arange(ATTN_WIDTH)[None, :] // HEAD_DIM).astype(BF16),
        "ln_ple": ln_ple[0].reshape(1, -1), "w_ple_gate": w_ple_gate[0].astype(BF16),
        "b_ple_gate": b_ple_gate[0].reshape(1, -1), "w_ple_proj": w_ple_proj[0].astype(BF16),
        "ln_final": ln_final.reshape(1, -1),
    }
    w_in_b = w_in[0].astype(BF16)
    bmat, cmat, ab_re, ab_im = _s5_params(ssm_a_re[0], ssm_a_im[0], ssm_b_re[0], ssm_b_im[0],
                                          ssm_c_re[0], ssm_c_im[0], ssm_log_dt[0])

    def coeff(a, nb):
        return jnp.broadcast_to(a, (2, nb, STATE_HALF)).reshape(2 * nb, STATE_HALF)

    proj_p = _in_proj(x_prompt.reshape(n_p, D_MODEL), ln_mix[0], w_in_b, dils=DILATIONS[1:],
                      seq_window=(s_len, wb_prompt))
    u_p, k_p, v_p = proj_p[:3]
    qkv = [proj_p[3:6]] + [proj_p[6 + 3 * i:9 + 3 * i] for i in range(len(DILATIONS) - 1)]
    zeros_state = jnp.zeros((bsz, SSM_GROUPS, SSM_STATE), F32)
    y_p, ht_p = _s5(u_p.reshape(bsz, s_len, SSM_WIDTH), _state_to_rows(zeros_state, zeros_state),
                    bmat, cmat, coeff(ab_re, bsz), coeff(ab_im, bsz), ssm_d[0])
    branches = [_attn_prompt_branch(*qkv[i], bsz, s_len, d) for i, d in enumerate(DILATIONS)]
    x1_p, hrow_p, eidx_p, epos_p, egate_p, cnt_p = _mid(
        x_prompt.reshape(n_p, D_MODEL), y_p.reshape(n_p, SSM_WIDTH),
        [b[0] for b in branches], [b[1] for b in branches], DILATIONS, w)

    u_s, k_s, v_s, qb_s, kb_s, vb_s = _in_proj(x_sample.reshape(n_s, D_MODEL), ln_mix[0], w_in_b)
    y_s, ht_s = _s5(u_s.reshape(dbsz, dt_len, SSM_WIDTH), _state_to_rows(state_ssm_re[0], state_ssm_im[0]),
                    bmat, cmat, coeff(ab_re, dbsz), coeff(ab_im, dbsz), ssm_d[0])
    as3 = lambda t: t.reshape(dbsz, dt_len, ATTN_WIDTH)
    attn_s = _attn_sample(as3(qb_s), as3(kb_s), as3(vb_s), cache_attn_k[0], cache_attn_v[0])
    x1_s, hrow_s, eidx_s, epos_s, egate_s, cnt_s = _mid(
        x_sample.reshape(n_s, D_MODEL), y_s.reshape(n_s, SSM_WIDTH),
        [attn_s.reshape(n_s, ATTN_WIDTH)], [], (), w)

    n_rows = (n_p + n_s) * TOP_K
    dest_p, dest_s, visits, gs = _routing(eidx_p, epos_p, cnt_p, eidx_s, epos_s, cnt_s, n_rows)
    xs = _dispatch(jnp.concatenate([dest_p.reshape(-1), dest_s.reshape(-1)]), hrow_p, hrow_s)
    y_rows = _experts(xs, visits, gs, w_up[0], b_up[0], w_down[0], b_down[0])

    out_p = _combine(dest_p, y_rows, x1_p, egate_p, p_prompt[0].reshape(n_p, PLE_DIM), w)
    out_s = _combine(dest_s, y_rows, x1_s, egate_s, p_sample[0].reshape(n_s, PLE_DIM), w)

    hr_p, hi_p = _rows_to_state(ht_p, bsz)
    hr_s, hi_s = _rows_to_state(ht_s, dbsz)
    kv_p = lambda t: jnp.transpose(t.reshape(bsz, N_HEADS, HEAD_DIM, wb_prompt), (0, 3, 1, 2))[None]
    kv_s = lambda t: t.reshape(dbsz, dt_len, N_HEADS, HEAD_DIM)[None]
    return (out_p.reshape(bsz, s_len, D_MODEL), out_s.reshape(dbsz, dt_len, D_MODEL),
            kv_p(k_p), kv_p(v_p), hr_p[None], hi_p[None],
            kv_s(k_s), kv_s(v_s), hr_s[None], hi_s[None])
```
